```python
import jax, jax.numpy as jnp
from jax import lax
import numpy as np

D_MODEL = 1024
BATCH = 4
SEQ = 8192
DEPTH = 2

D_MIX = D_MODEL
D_CONV = D_MIX // 2
CONV_HEADS = 8
CONV_WIDTH = 3
D_POOL = D_MIX - D_CONV
POOL_WINDOWS = (2, 4, 8, 16)
N_POOL_GROUPS = len(POOL_WINDOWS)
D_POOL_GROUP = D_POOL // N_POOL_GROUPS
D_IN_PROJ = 3 * D_CONV + D_POOL
N_EXPERT_GROUPS = 4
EXPERTS_PER_GROUP = 8
N_EXPERTS = N_EXPERT_GROUPS * EXPERTS_PER_GROUP
TOP_K = 2
D_EXPERT = D_MODEL // 2
EXPERT_BLOCK = 128
D_PLE = 256
DEEPNORM_ALPHA = (2 * DEPTH) ** 0.25
DEEPNORM_BETA = (8 * DEPTH) ** -0.25
LN_EPS = 1e-5

kernel_name = "hybrid_conv_pool_hmoe_encoder"


def _layernorm(x, g, b):
    xf = x.astype(jnp.float32)
    mu = jnp.mean(xf, axis=-1, keepdims=True)
    var = jnp.mean(jnp.square(xf - mu), axis=-1, keepdims=True)
    y = (xf - mu) * lax.rsqrt(var + LN_EPS)
    return (y * g.astype(jnp.float32) + b.astype(jnp.float32)).astype(x.dtype)


def _centred_mean_minus_self(u, window):
    S = u.shape[1]
    left = window // 2
    right = window - 1 - left
    cs = jnp.cumsum(u, axis=1)
    cs0 = jnp.pad(cs, ((0, 0), (1, 0), (0, 0)))
    cs_ext = jnp.pad(cs0, ((0, 0), (left, right), (0, 0)), mode="edge")
    hi = cs_ext[:, window:window + S]
    lo = cs_ext[:, :S]
    t = jnp.arange(S)
    cnt = (jnp.minimum(t + right + 1, S) - jnp.maximum(t - left, 0)).astype(jnp.float32)
    return (hi - lo) / cnt[None, :, None] - u


def _hybrid_mixer(xn, w_in, b_in, conv_w, pool_w, pool_scale, w_o):
    B, S, _ = xn.shape
    z = xn @ w_in + b_in
    h = z[..., :D_CONV]
    gate_b = z[..., D_CONV:2 * D_CONV]
    gate_c = z[..., 2 * D_CONV:3 * D_CONV]
    u = z[..., 3 * D_CONV:]
    v = gate_c * h
    v = lax.conv_general_dilated(
        v, conv_w[:, None, :], window_strides=(1,), padding=((1, 1),),
        dimension_numbers=("NWC", "WIO", "NWC"), feature_group_count=D_CONV)
    y_conv = gate_b * v
    ug = u.astype(jnp.float32).reshape(B, S, N_POOL_GROUPS, D_POOL_GROUP)
    pooled = jnp.stack(
        [_centred_mean_minus_self(ug[:, :, g], w) for g, w in enumerate(POOL_WINDOWS)],
        axis=2).astype(xn.dtype)
    y_pool = jnp.einsum("bsgc,gcd->bsgd", pooled, pool_w).reshape(B, S, D_POOL) * pool_scale
    return jnp.concatenate([y_conv, y_pool], axis=-1) @ w_o


def _hierarchical_moe(xn, w_rg, b_rg, w_re, b_re, w1, w3, w2):
    B, S, D = xn.shape
    N = B * S
    xf = xn.reshape(N, D)
    x32 = xf.astype(jnp.float32)
    g_probs = jax.nn.softmax(x32 @ w_rg.astype(jnp.float32) + b_rg.astype(jnp.float32), axis=-1)
    g_sel = jnp.argmax(g_probs, axis=-1)
    g_w = jnp.take_along_axis(g_probs, g_sel[:, None], axis=1)[:, 0]
    e_logits = jnp.einsum("nd,gde->nge", x32, w_re.astype(jnp.float32)) + b_re.astype(jnp.float32)
    e_logits = jnp.take_along_axis(e_logits, g_sel[:, None, None], axis=1)[:, 0]
    top_val, top_idx = lax.top_k(e_logits, TOP_K)
    top_w = jax.nn.softmax(top_val, axis=-1) * g_w[:, None]
    expert_id = (g_sel[:, None] * EXPERTS_PER_GROUP + top_idx).reshape(-1).astype(jnp.int32)
    token_id = jnp.repeat(jnp.arange(N, dtype=jnp.int32), TOP_K)
    gate_w = top_w.reshape(-1)
    A = N * TOP_K
    n_slots = -(-(A + N_EXPERTS * (EXPERT_BLOCK - 1)) // EXPERT_BLOCK) * EXPERT_BLOCK
    n_blocks = n_slots // EXPERT_BLOCK
    order = jnp.argsort(expert_id)
    e_sorted = expert_id[order]
    counts = jnp.bincount(expert_id, length=N_EXPERTS).astype(jnp.int32)
    start = jnp.cumsum(counts) - counts
    padded = (counts + EXPERT_BLOCK - 1) // EXPERT_BLOCK * EXPERT_BLOCK
    padded_end = jnp.cumsum(padded)
    padded_start = padded_end - padded
    rank = jnp.arange(A, dtype=jnp.int32) - start[e_sorted]
    dest = padded_start[e_sorted] + rank
    slot_token = jnp.full((n_slots,), N, jnp.int32).at[dest].set(token_id[order])
    slot_w = jnp.zeros((n_slots,), jnp.float32).at[dest].set(gate_w[order])
    block_starts = jnp.arange(n_blocks, dtype=jnp.int32) * EXPERT_BLOCK
    block_expert = jnp.minimum(jnp.searchsorted(padded_end, block_starts, side="right"),
                               N_EXPERTS - 1).astype(jnp.int32)
    x_pad = jnp.concatenate([xf, jnp.zeros((1, D), xf.dtype)], axis=0)
    xs = x_pad[slot_token].reshape(n_blocks, EXPERT_BLOCK, D)

    def expert_block(args):
        xb, e = args
        hb = jax.nn.silu(xb @ w1[e]) * (xb @ w3[e])
        return hb @ w2[e]

    ys = lax.map(expert_block, (xs, block_expert)).reshape(n_slots, D)
    ys = ys * slot_w.astype(ys.dtype)[:, None]
    out = jax.ops.segment_sum(ys, slot_token, num_segments=N + 1)[:N]
    return out.reshape(B, S, D)


def setup_inputs(seed: int = 0) -> dict:
    key = jax.random.key(seed)
    ks = jax.random.split(key, 32)
    L, D = DEPTH, D_MODEL

    def nrm(k, shape, scale):
        return jax.random.normal(k, shape, jnp.float32) * scale

    return {
        "x": nrm(ks[0], (BATCH, SEQ, D), 1.0),
        "p": nrm(ks[1], (DEPTH, BATCH, SEQ, D_PLE), 1.0),
        "ln0_g": 1.0 + nrm(ks[2], (D,), 0.02),
        "ln0_b": nrm(ks[3], (D,), 0.02),
        "w_in": nrm(ks[4], (L, D, D_IN_PROJ), D ** -0.5),
        "b_in": nrm(ks[5], (L, D_IN_PROJ), 0.02),
        "conv_w": nrm(ks[6], (L, CONV_WIDTH, D_CONV), CONV_WIDTH ** -0.5),
        "pool_w": nrm(ks[7], (L, N_POOL_GROUPS, D_POOL_GROUP, D_POOL_GROUP), D_POOL_GROUP ** -0.5),
        "pool_scale": 1.0 + nrm(ks[8], (L, D_POOL), 0.02),
        "w_o": nrm(ks[9], (L, D_MIX, D), DEEPNORM_BETA * D_MIX ** -0.5),
        "ln1_g": 1.0 + nrm(ks[10], (L, D), 0.02),
        "ln1_b": nrm(ks[11], (L, D), 0.02),
        "w_router_group": nrm(ks[12], (L, D, N_EXPERT_GROUPS), D ** -0.5),
        "b_router_group": nrm(ks[13], (L, N_EXPERT_GROUPS), 0.01),
        "w_router_expert": nrm(ks[14], (L, N_EXPERT_GROUPS, D, EXPERTS_PER_GROUP), D ** -0.5),
        "b_router_expert": nrm(ks[15], (L, N_EXPERT_GROUPS, EXPERTS_PER_GROUP), 0.01),
        "w1": nrm(ks[16], (L, N_EXPERTS, D, D_EXPERT), D ** -0.5),
        "w3": nrm(ks[17], (L, N_EXPERTS, D, D_EXPERT), D ** -0.5),
        "w2": nrm(ks[18], (L, N_EXPERTS, D_EXPERT, D), DEEPNORM_BETA * D_EXPERT ** -0.5),
        "w_ple_gate": nrm(ks[19], (L, D, D), D ** -0.5),
        "b_ple_gate": nrm(ks[20], (L, D), 0.02),
        "w_ple_proj": nrm(ks[21], (L, D_PLE, D), DEEPNORM_BETA * D_PLE ** -0.5),
        "ln2_g": 1.0 + nrm(ks[22], (L, D), 0.02),
        "ln2_b": nrm(ks[23], (L, D), 0.02),
    }


def reference(x, p, ln0_g, ln0_b, w_in, b_in, conv_w, pool_w, pool_scale, w_o,
              ln1_g, ln1_b, w_router_group, b_router_group, w_router_expert,
              b_router_expert, w1, w3, w2, w_ple_gate, b_ple_gate, w_ple_proj,
              ln2_g, ln2_b):
    x = _layernorm(x, ln0_g, ln0_b)
    for i in range(DEPTH):
        mix = _hybrid_mixer(x, w_in[i], b_in[i], conv_w[i], pool_w[i], pool_scale[i], w_o[i])
        x = _layernorm(DEEPNORM_ALPHA * x + mix, ln1_g[i], ln1_b[i])
        ffn = _hierarchical_moe(x, w_router_group[i], b_router_group[i], w_router_expert[i],
                                b_router_expert[i], w1[i], w3[i], w2[i])
        ple = jax.nn.sigmoid(x @ w_ple_gate[i] + b_ple_gate[i]) * (p[i] @ w_ple_proj[i])
        x = _layernorm(DEEPNORM_ALPHA * x + ffn + ple, ln2_g[i], ln2_b[i])
    return x
```

```python
import functools

import jax
import jax.numpy as jnp
from jax import lax
from jax.experimental import pallas as pl
from jax.experimental.pallas import tpu as pltpu

POOL_WINDOWS = (2, 4, 8, 16)
CONV_WIDTH = 3
TOP_K = 2
LN_EPS = 1e-5
HALO = 8
LANES = 128
SLOT_BLOCK = 256
VMEM_LIMIT = 56 * 1024 * 1024

F32 = jnp.float32
BF16 = jnp.bfloat16
I32 = jnp.int32
U32 = jnp.uint32


def _layernorm(v, g, b):
    mu = jnp.mean(v, axis=-1, keepdims=True)
    c = v - mu
    var = jnp.mean(c * c, axis=-1, keepdims=True)
    return c * lax.rsqrt(var + LN_EPS) * g + b


def _dot(a, b):
    return jnp.dot(a, b, preferred_element_type=F32)


def _pack_bf16_pairs(v):
    c = v.shape[1] // 2
    hi = pltpu.bitcast(v[:, :c].astype(BF16).astype(F32), U32)
    lo = pltpu.bitcast(v[:, c:].astype(BF16).astype(F32), U32)
    return hi | (lo >> 16)


def _unpack_bf16_pairs(u):
    hi = pltpu.bitcast(u & jnp.uint32(0xFFFF0000), F32)
    lo = pltpu.bitcast(u << 16, F32)
    return hi, lo


def _mixer_body(x_ref, xp_ref, xn_ref, g0_ref, b0_ref, win_ref, bin_ref, cw_ref, pw_ref, ps_ref,
                wo_ref, g1_ref, b1_ref, wr_ref, br_ref,
                x1_ref, ri_ref, wcol_ref, cnt_ref, *, apply_ln0, alpha, n_groups, per_group, seq):
    i = pl.program_id(1)
    nt = pl.num_programs(1)
    tm = x_ref.shape[1]
    rows = tm + 2 * HALO
    d_conv = cw_ref.shape[1]
    dg = pw_ref.shape[1]

    xe = jnp.concatenate([xp_ref[0], x_ref[0], xn_ref[0]], axis=0)
    if apply_ln0:
        xe = _layernorm(xe, g0_ref[...], b0_ref[...])
    z = _dot(xe.astype(BF16), win_ref[...]) + bin_ref[...]

    r = lax.broadcasted_iota(I32, (rows, 1), 0)
    valid = jnp.logical_and(jnp.logical_or(r >= HALO, i > 0),
                            jnp.logical_or(r < tm + HALO, i < nt - 1))

    h = z[:, :d_conv]
    gate_b = z[HALO:HALO + tm, d_conv:2 * d_conv]
    gate_c = z[:, 2 * d_conv:3 * d_conv]
    v = jnp.where(valid, gate_c * h, 0.0)
    cw = cw_ref[...]
    conv = (cw[0:1] * pltpu.roll(v, 1, 0)[HALO:HALO + tm]
            + cw[1:2] * v[HALO:HALO + tm]
            + cw[2:3] * pltpu.roll(v, rows - 1, 0)[HALO:HALO + tm])
    parts = [gate_b * conv]

    pos = i * tm + r[HALO:HALO + tm] - HALO
    ps = ps_ref[...]
    for g, w in enumerate(POOL_WINDOWS):
        left = w // 2
        right = w - 1 - left
        c0 = 3 * d_conv + g * dg
        ug = jnp.where(valid, z[:, c0:c0 + dg], 0.0)
        acc = ug
        k = 1
        while k < w:
            acc = acc + pltpu.roll(acc, k, 0)
            k *= 2
        if right:
            acc = pltpu.roll(acc, rows - right, 0)
        cnt = (jnp.minimum(pos + right + 1, seq) - jnp.maximum(pos - left, 0)).astype(F32)
        pooled = acc[HALO:HALO + tm] / cnt - ug[HALO:HALO + tm]
        parts.append(_dot(pooled.astype(BF16), pw_ref[g]) * ps[:, g * dg:(g + 1) * dg])
    cat = jnp.concatenate(parts, axis=1)
    mix = _dot(cat.astype(BF16), wo_ref[...])
    x1 = _layernorm(alpha * xe[HALO:HALO + tm] + mix, g1_ref[...], b1_ref[...])
    x1_ref[0] = x1

    xh = x1.astype(BF16)
    xl = (x1 - xh.astype(F32)).astype(BF16)
    l_hi = _dot(xh, wr_ref[...])
    l_lo = _dot(xl, wr_ref[:, :LANES])
    logits = l_hi[:, :LANES] + l_hi[:, LANES:] + l_lo + br_ref[...]
    lt = logits.T

    row8 = lax.broadcasted_iota(I32, (8, tm), 0)
    neg = jnp.float32(-jnp.inf)
    lg = jnp.where(row8 < n_groups, lt[0:8], neg)
    mg = jnp.max(lg, axis=0, keepdims=True)
    g_w = 1.0 / jnp.sum(jnp.exp(lg - mg), axis=0, keepdims=True)
    g_sel = jnp.min(jnp.where(lg == mg, row8, 8), axis=0, keepdims=True)
    le = lt[8:16]
    for g in range(1, n_groups):
        le = jnp.where(g_sel == g, lt[8 + 8 * g:16 + 8 * g], le)
    le = jnp.where(row8 < per_group, le, neg)
    v1 = jnp.max(le, axis=0, keepdims=True)
    i1 = jnp.min(jnp.where(le == v1, row8, 8), axis=0, keepdims=True)
    le2 = jnp.where(row8 == i1, neg, le)
    v2 = jnp.max(le2, axis=0, keepdims=True)
    i2 = jnp.min(jnp.where(le2 == v2, row8, 8), axis=0, keepdims=True)
    e21 = jnp.exp(v2 - v1)
    w_first = 1.0 / (1.0 + e21)
    gw0 = w_first * g_w
    gw1 = (e21 * w_first) * g_w
    eid0 = g_sel * per_group + i1
    eid1 = g_sel * per_group + i2

    rowl = lax.broadcasted_iota(I32, (LANES, tm), 0)
    oh0 = rowl == eid0
    oh1 = rowl == eid1
    a_idx = lax.broadcasted_iota(I32, (tm, tm), 0)
    b_idx = lax.broadcasted_iota(I32, (tm, tm), 1)
    before = (a_idx < b_idx).astype(BF16)
    r0 = _dot(oh0.astype(BF16), before)
    r1 = _dot(oh1.astype(BF16), before)
    cnt0 = jnp.sum(oh0.astype(F32), axis=1, keepdims=True)
    rank0 = jnp.sum(jnp.where(oh0, r0, 0.0), axis=0, keepdims=True)
    rank1 = jnp.sum(jnp.where(oh1, r1 + cnt0, 0.0), axis=0, keepdims=True)

    ri_ref[0] = jnp.zeros((8, tm), I32)
    ri_ref[0, 0:1, :] = eid0
    ri_ref[0, 1:2, :] = eid1
    ri_ref[0, 2:3, :] = rank0.astype(I32)
    ri_ref[0, 3:4, :] = rank1.astype(I32)

    wrows = jnp.where(rowl == 0, gw0, jnp.where(rowl == 1, gw1, 0.0))
    wcol_ref[0] = wrows.T

    both = jnp.logical_or(oh0, oh1).astype(BF16)
    ones = jnp.ones((8, tm), BF16)
    cnt_ref[0, 0] = lax.dot_general(ones, both, (((1,), (1,)), ((), ())),
                                    preferred_element_type=F32)


def _mixer(x, g0, b0, w_in, b_in, conv_w, pool_w, pool_scale, w_o, g1, b1, w_r, b_r, *,
           apply_ln0, alpha, n_groups, per_group, tm):
    bsz, seq, d = x.shape
    nt = seq // tm
    hb = tm // HALO
    nhb = seq // HALO
    d_in = w_in.shape[1]
    d_conv = conv_w.shape[1]
    full = lambda *shape: pl.BlockSpec(shape, lambda b, i: (0,) * len(shape))
    body = functools.partial(_mixer_body, apply_ln0=apply_ln0, alpha=alpha, n_groups=n_groups,
                             per_group=per_group, seq=seq)
    return pl.pallas_call(
        body,
        grid=(bsz, nt),
        in_specs=[
            pl.BlockSpec((1, tm, d), lambda b, i: (b, i, 0)),
            pl.BlockSpec((1, HALO, d), lambda b, i: (b, jnp.maximum(i * hb - 1, 0), 0)),
            pl.BlockSpec((1, HALO, d), lambda b, i: (b, jnp.minimum((i + 1) * hb, nhb - 1), 0)),
            full(1, d), full(1, d),
            full(d, d_in), full(1, d_in),
            full(CONV_WIDTH, d_conv),
            full(*pool_w.shape), full(1, pool_scale.shape[1]),
            full(d, d), full(1, d), full(1, d),
            full(d, 2 * LANES), full(1, LANES),
        ],
        out_specs=[
            pl.BlockSpec((1, tm, d), lambda b, i: (b, i, 0)),
            pl.BlockSpec((1, 8, tm), lambda b, i: (b, 0, i)),
            pl.BlockSpec((1, tm, LANES), lambda b, i: (b, i, 0)),
            pl.BlockSpec((1, 1, 8, LANES), lambda b, i: (b, i, 0, 0)),
        ],
        out_shape=[
            jax.ShapeDtypeStruct((bsz, seq, d), F32),
            jax.ShapeDtypeStruct((bsz, 8, seq), I32),
            jax.ShapeDtypeStruct((bsz, seq, LANES), F32),
            jax.ShapeDtypeStruct((bsz, nt, 8, LANES), F32),
        ],
        compiler_params=pltpu.CompilerParams(
            dimension_semantics=("arbitrary", "arbitrary"), vmem_limit_bytes=VMEM_LIMIT),
        name="mixer",
    )(x, x, x, g0, b0, w_in, b_in, conv_w, pool_w, pool_scale, w_o, g1, b1, w_r, b_r)


def _positions_body(cnt_ref, ri_ref, pos_ref, *, nt):
    t = pl.program_id(0) * nt + pl.program_id(1)
    tm = ri_ref.shape[2]
    c = cnt_ref[...]
    tot = jnp.sum(c, axis=0, keepdims=True)
    lane = lax.broadcasted_iota(I32, (8, LANES), 1)
    incl = jnp.broadcast_to(tot, (8, LANES))
    k = 1
    while k < LANES:
        incl = incl + jnp.where(lane >= k, pltpu.roll(incl, k, 1), 0.0)
        k *= 2
    tile = lax.broadcasted_iota(I32, c.shape, 0)
    prev = jnp.sum(jnp.where(tile < t, c, 0.0), axis=0, keepdims=True)
    base = (incl - tot + prev).astype(I32)

    row8 = lax.broadcasted_iota(I32, (8, LANES), 0)
    pieces = jnp.where(row8 == 0, base & 255,
                       jnp.where(row8 == 1, (base >> 8) & 255,
                                 jnp.where(row8 == 2, base >> 16, 0))).astype(F32).astype(BF16)
    rowl = lax.broadcasted_iota(I32, (LANES, tm), 0)
    ri = ri_ref[0]
    out = []
    for kk in range(TOP_K):
        oh = (rowl == ri[kk:kk + 1]).astype(BF16)
        pb = _dot(pieces, oh)
        off = pb[0:1] + 256.0 * pb[1:2] + 65536.0 * pb[2:3]
        out.append(off.astype(I32) + ri[TOP_K + kk:TOP_K + kk + 1])
    pos_ref[0] = jnp.zeros((8, tm), I32)
    pos_ref[0, 0:1, :] = out[0]
    pos_ref[0, 1:2, :] = out[1]


def _positions(cnt, ri, *, tm):
    bsz, _, seq = ri.shape
    nt = seq // tm
    return pl.pallas_call(
        functools.partial(_positions_body, nt=nt),
        grid=(bsz, nt),
        in_specs=[pl.BlockSpec(cnt.shape, lambda b, i: (0, 0)),
                  pl.BlockSpec((1, 8, tm), lambda b, i: (b, 0, i))],
        out_specs=pl.BlockSpec((1, 8, tm), lambda b, i: (b, 0, i)),
        out_shape=jax.ShapeDtypeStruct((bsz, 8, seq), I32),
        compiler_params=pltpu.CompilerParams(dimension_semantics=("arbitrary", "arbitrary")),
        name="positions",
    )(cnt, ri)


def _start_row_copies(pos_ref, n_rows, make_copy):
    def issue(j, carry):
        for kk in range(TOP_K):
            make_copy(j, kk, pos_ref[0, kk, j]).start()
        return carry
    lax.fori_loop(0, n_rows, issue, 0)


def _wait_row_copies(n_rows, make_copy):
    def drain(j, carry):
        for kk in range(TOP_K):
            make_copy(j, kk, 0).wait()
        return carry
    lax.fori_loop(0, n_rows, drain, 0)


def _ple_body(pos_ref, x1_ref, p_ref, wg_ref, bg_ref, wp_ref, ple_ref, xs_ref, buf_ref, sem):
    x1 = x1_ref[0]
    tm = x1.shape[0]
    buf_ref[...] = _pack_bf16_pairs(x1)

    def make_copy(j, kk, slot):
        return pltpu.make_async_copy(buf_ref.at[pl.ds(j, 1)], xs_ref.at[pl.ds(slot, 1)], sem)

    _start_row_copies(pos_ref, tm, make_copy)
    gate = jax.nn.sigmoid(_dot(x1.astype(BF16), wg_ref[...]) + bg_ref[...])
    ple_ref[0] = gate * _dot(p_ref[0].astype(BF16), wp_ref[...])
    _wait_row_copies(tm, make_copy)


def _ple_dispatch(pos, x1, p, w_gate, b_gate, w_proj, *, tm, n_slots):
    bsz, seq, d = x1.shape
    nt = seq // tm
    d_ple = p.shape[-1]
    full = lambda *shape: pl.BlockSpec(shape, lambda b, i: (0,) * len(shape))
    return pl.pallas_call(
        _ple_body,
        grid=(bsz, nt),
        in_specs=[
            pl.BlockSpec((1, 8, tm), lambda b, i: (b, 0, i), memory_space=pltpu.SMEM),
            pl.BlockSpec((1, tm, d), lambda b, i: (b, i, 0)),
            pl.BlockSpec((1, tm, d_ple), lambda b, i: (b, i, 0)),
            full(d, d), full(1, d), full(d_ple, d),
        ],
        out_specs=[
            pl.BlockSpec((1, tm, d), lambda b, i: (b, i, 0)),
            pl.BlockSpec(memory_space=pl.ANY),
        ],
        out_shape=[
            jax.ShapeDtypeStruct((bsz, seq, d), F32),
            jax.ShapeDtypeStruct((n_slots, d // 2), U32),
        ],
        scratch_shapes=[pltpu.VMEM((tm, d // 2), U32), pltpu.SemaphoreType.DMA(())],
        compiler_params=pltpu.CompilerParams(
            dimension_semantics=("arbitrary", "arbitrary"), vmem_limit_bytes=VMEM_LIMIT),
        name="ple_dispatch",
    )(pos, x1, p, w_gate, b_gate, w_proj)


def _experts_body(blk_ref, exp_ref, lo_ref, xs_ref, w1_ref, w3_ref, w2_ref, ys_ref):
    lo = lo_ref[pl.program_id(0)]
    nrow = xs_ref.shape[0]

    @pl.when(lo < nrow)
    def _():
        hi_half, lo_half = _unpack_bf16_pairs(xs_ref[...])
        xb = jnp.concatenate([hi_half.astype(BF16), lo_half.astype(BF16)], axis=1)
        h = jax.nn.silu(_dot(xb, w1_ref[...])) * _dot(xb, w3_ref[...])
        y = _pack_bf16_pairs(_dot(h.astype(BF16), w2_ref[...]))

        @pl.when(lo == 0)
        def _():
            ys_ref[...] = y

        @pl.when(lo > 0)
        def _():
            rows = lax.broadcasted_iota(I32, y.shape, 0)
            ys_ref[...] = jnp.where(rows >= lo, y, ys_ref[...])


def _experts(item_block, item_expert, item_lo, xs, w1, w3, w2):
    n_slots, half = xs.shape
    _, d, d_e = w1.shape
    n_items = item_block.shape[0]
    grid_spec = pltpu.PrefetchScalarGridSpec(
        num_scalar_prefetch=3,
        grid=(n_items,),
        in_specs=[
            pl.BlockSpec((SLOT_BLOCK, half), lambda i, blk, exp, lo: (blk[i], 0)),
            pl.BlockSpec((None, d, d_e), lambda i, blk, exp, lo: (exp[i], 0, 0)),
            pl.BlockSpec((None, d, d_e), lambda i, blk, exp, lo: (exp[i], 0, 0)),
            pl.BlockSpec((None, d_e, d), lambda i, blk, exp, lo: (exp[i], 0, 0)),
        ],
        out_specs=pl.BlockSpec((SLOT_BLOCK, half), lambda i, blk, exp, lo: (blk[i], 0)),
    )
    return pl.pallas_call(
        _experts_body,
        grid_spec=grid_spec,
        out_shape=jax.ShapeDtypeStruct((n_slots, half), U32),
        compiler_params=pltpu.CompilerParams(
            dimension_semantics=("arbitrary",), vmem_limit_bytes=VMEM_LIMIT),
        name="experts",
    )(item_block, item_expert, item_lo, xs, w1, w3, w2)


def _expert_items(cnt, n_experts, n_slots):
    n_blocks = n_slots // SLOT_BLOCK
    n_items = n_blocks + n_experts - 1
    tot = jnp.sum(cnt, axis=0)[:n_experts].astype(I32)
    end = jnp.cumsum(tot)
    starts = jnp.concatenate([jnp.arange(n_blocks, dtype=I32) * SLOT_BLOCK, (end - tot)[1:]])
    starts = jnp.sort(starts)
    dup = jnp.concatenate([jnp.zeros((1,), bool), starts[1:] == starts[:-1]])
    starts = jnp.sort(jnp.where(dup, n_slots, starts))
    live = starts < n_slots
    item_block = jnp.where(live, starts // SLOT_BLOCK, n_blocks - 1).astype(I32)
    item_lo = jnp.where(live, starts % SLOT_BLOCK, SLOT_BLOCK).astype(I32)
    item_expert = jnp.minimum(jnp.searchsorted(end, starts, side="right"), n_experts - 1).astype(I32)
    assert item_block.shape == (n_items,)
    return item_block, item_expert, item_lo


def _combine_body(pos_ref, x1_ref, ple_ref, wcol_ref, ys_ref, g2_ref, b2_ref, out_ref,
                  y0_ref, y1_ref, sem, *, alpha):
    tm = x1_ref.shape[1]
    bufs = (y0_ref, y1_ref)

    def make_copy(j, kk, slot):
        return pltpu.make_async_copy(ys_ref.at[pl.ds(slot, 1)], bufs[kk].at[pl.ds(j, 1)], sem)

    _start_row_copies(pos_ref, tm, make_copy)
    _wait_row_copies(tm, make_copy)

    wcol = wcol_ref[0]
    ffn = None
    for kk in range(TOP_K):
        hi_half, lo_half = _unpack_bf16_pairs(bufs[kk][...])
        term = jnp.concatenate([hi_half, lo_half], axis=1) * wcol[:, kk:kk + 1]
        ffn = term if ffn is None else ffn + term
    out_ref[0] = _layernorm(alpha * x1_ref[0] + ffn + ple_ref[0], g2_ref[...], b2_ref[...])


def _combine(pos, x1, ple, wcol, ys, g2, b2, *, alpha, tm):
    bsz, seq, d = x1.shape
    nt = seq // tm
    full = lambda *shape: pl.BlockSpec(shape, lambda b, i: (0,) * len(shape))
    return pl.pallas_call(
        functools.partial(_combine_body, alpha=alpha),
        grid=(bsz, nt),
        in_specs=[
            pl.BlockSpec((1, 8, tm), lambda b, i: (b, 0, i), memory_space=pltpu.SMEM),
            pl.BlockSpec((1, tm, d), lambda b, i: (b, i, 0)),
            pl.BlockSpec((1, tm, d), lambda b, i: (b, i, 0)),
            pl.BlockSpec((1, tm, LANES), lambda b, i: (b, i, 0)),
            pl.BlockSpec(memory_space=pl.ANY),
            full(1, d), full(1, d),
        ],
        out_specs=pl.BlockSpec((1, tm, d), lambda b, i: (b, i, 0)),
        out_shape=jax.ShapeDtypeStruct((bsz, seq, d), F32),
        scratch_shapes=[pltpu.VMEM((tm, d // 2), U32), pltpu.VMEM((tm, d // 2), U32),
                        pltpu.SemaphoreType.DMA(())],
        compiler_params=pltpu.CompilerParams(
            dimension_semantics=("arbitrary", "arbitrary"), vmem_limit_bytes=VMEM_LIMIT),
        name="combine",
    )(pos, x1, ple, wcol, ys, g2, b2)


def _router_weights(w_rg, b_rg, w_re, b_re):
    d, n_groups = w_rg.shape
    per_group = w_re.shape[2]
    assert n_groups <= 8 and per_group <= 8 and 8 + 8 * n_groups <= LANES
    w = jnp.zeros((d, LANES), F32).at[:, :n_groups].set(w_rg)
    b = jnp.zeros((LANES,), F32).at[:n_groups].set(b_rg)
    for g in range(n_groups):
        w = w.at[:, 8 + 8 * g:8 + 8 * g + per_group].set(w_re[g])
        b = b.at[8 + 8 * g:8 + 8 * g + per_group].set(b_re[g])
    w_hi = w.astype(BF16)
    w_lo = (w - w_hi.astype(F32)).astype(BF16)
    return jnp.concatenate([w_hi, w_lo], axis=1), b[None, :]


def kernel(x, p, ln0_g, ln0_b, w_in, b_in, conv_w, pool_w, pool_scale, w_o, ln1_g, ln1_b,
           w_router_group, b_router_group, w_router_expert, b_router_expert, w1, w3, w2,
           w_ple_gate, b_ple_gate, w_ple_proj, ln2_g, ln2_b):
    bsz, seq, d = x.shape
    depth = w_in.shape[0]
    n_groups, per_group = w_router_expert.shape[1], w_router_expert.shape[3]
    n_experts = n_groups * per_group
    alpha = (2 * depth) ** 0.25
    tm = min(512, seq)
    n_slots = bsz * seq * TOP_K
    assert seq % tm == 0 and tm % LANES == 0 and n_slots % SLOT_BLOCK == 0 and n_experts <= LANES
    row = lambda a: a[None, :]

    for i in range(depth):
        w_r, b_r = _router_weights(w_router_group[i], b_router_group[i],
                                   w_router_expert[i], b_router_expert[i])
        x1, ri, wcol, cnt = _mixer(
            x, row(ln0_g), row(ln0_b), w_in[i].astype(BF16), row(b_in[i]), conv_w[i],
            pool_w[i].astype(BF16), row(pool_scale[i]), w_o[i].astype(BF16),
            row(ln1_g[i]), row(ln1_b[i]), w_r, b_r,
            apply_ln0=(i == 0), alpha=alpha, n_groups=n_groups, per_group=per_group, tm=tm)
        cnt = cnt[:, :, 0, :].reshape(-1, LANES)
        pos = _positions(cnt, ri, tm=tm)
        items = _expert_items(cnt, n_experts, n_slots)
        ple, xs = _ple_dispatch(pos, x1, p[i], w_ple_gate[i].astype(BF16), row(b_ple_gate[i]),
                                w_ple_proj[i].astype(BF16), tm=tm, n_slots=n_slots)
        ys = _experts(*items, xs, w1[i].astype(BF16), w3[i].astype(BF16), w2[i].astype(BF16))
        x = _combine(pos, x1, ple, wcol, ys, row(ln2_g[i]), row(ln2_b[i]), alpha=alpha, tm=tm)
    return x
```

```python
import functools

import jax
import jax.numpy as jnp
from jax import lax
from jax.experimental import pallas as pl
from jax.experimental.pallas import tpu as pltpu

POOL_WINDOWS = (2, 4, 8, 16)
CONV_WIDTH = 3
TOP_K = 2
LN_EPS = 1e-5
HALO = 8
LANES = 128
SLOT_BLOCK = 256
VMEM_LIMIT = 56 * 1024 * 1024

F32 = jnp.float32
BF16 = jnp.bfloat16
I32 = jnp.int32
U32 = jnp.uint32


def _layernorm(v, g, b):
    mu = jnp.mean(v, axis=-1, keepdims=True)
    c = v - mu
    var = jnp.mean(c * c, axis=-1, keepdims=True)
    return c * lax.rsqrt(var + LN_EPS) * g + b


def _dot(a, b):
    return jnp.dot(a, b, preferred_element_type=F32)


def _pack_bf16_pairs(v):
    c = v.shape[1] // 2
    hi = pltpu.bitcast(v[:, :c].astype(BF16).astype(F32), U32)
    lo = pltpu.bitcast(v[:, c:].astype(BF16).astype(F32), U32)
    return hi | (lo >> 16)


def _unpack_bf16_pairs(u):
    hi = pltpu.bitcast(u & jnp.uint32(0xFFFF0000), F32)
    lo = pltpu.bitcast(u << 16, F32)
    return hi, lo


def _store_slot_rows(ref, packed):
    n, c = packed.shape
    rps = c // LANES
    for k in range(rps):
        ref[pl.ds(k, n, stride=rps), :] = packed[:, k * LANES:(k + 1) * LANES]


def _load_slot_rows(ref, n, rps):
    return jnp.concatenate([ref[pl.ds(k, n, stride=rps), :] for k in range(rps)], axis=1)


def _mixer_body(x_ref, xp_ref, xn_ref, g0_ref, b0_ref, win_ref, bin_ref, cw_ref, pw_ref, ps_ref,
                wo_ref, g1_ref, b1_ref, wr_ref, br_ref,
                x1_ref, ri_ref, wcol_ref, cnt_ref, *, apply_ln0, alpha, n_groups, per_group, seq):
    i = pl.program_id(1)
    nt = pl.num_programs(1)
    tm = x_ref.shape[1]
    rows = tm + 2 * HALO
    d_conv = cw_ref.shape[1]
    dg = pw_ref.shape[1]

    xe = jnp.concatenate([xp_ref[0], x_ref[0], xn_ref[0]], axis=0)
    if apply_ln0:
        xe = _layernorm(xe, g0_ref[...], b0_ref[...])
    z = _dot(xe.astype(BF16), win_ref[...]) + bin_ref[...]

    r = lax.broadcasted_iota(I32, (rows, 1), 0)
    valid = jnp.logical_and(jnp.logical_or(r >= HALO, i > 0),
                            jnp.logical_or(r < tm + HALO, i < nt - 1))

    h = z[:, :d_conv]
    gate_b = z[HALO:HALO + tm, d_conv:2 * d_conv]
    gate_c = z[:, 2 * d_conv:3 * d_conv]
    v = jnp.where(valid, gate_c * h, 0.0)
    cw = cw_ref[...]
    conv = (cw[0:1] * pltpu.roll(v, 1, 0)[HALO:HALO + tm]
            + cw[1:2] * v[HALO:HALO + tm]
            + cw[2:3] * pltpu.roll(v, rows - 1, 0)[HALO:HALO + tm])
    parts = [gate_b * conv]

    pos = i * tm + r[HALO:HALO + tm] - HALO
    ps = ps_ref[...]
    for g, w in enumerate(POOL_WINDOWS):
        left = w // 2
        right = w - 1 - left
        c0 = 3 * d_conv + g * dg
        ug = jnp.where(valid, z[:, c0:c0 + dg], 0.0)
        acc = ug
        k = 1
        while k < w:
            acc = acc + pltpu.roll(acc, k, 0)
            k *= 2
        if right:
            acc = pltpu.roll(acc, rows - right, 0)
        cnt = (jnp.minimum(pos + right + 1, seq) - jnp.maximum(pos - left, 0)).astype(F32)
        pooled = acc[HALO:HALO + tm] / cnt - ug[HALO:HALO + tm]
        parts.append(_dot(pooled.astype(BF16), pw_ref[g]) * ps[:, g * dg:(g + 1) * dg])
    cat = jnp.concatenate(parts, axis=1)
    mix = _dot(cat.astype(BF16), wo_ref[...])
    x1 = _layernorm(alpha * xe[HALO:HALO + tm] + mix, g1_ref[...], b1_ref[...])
    x1_ref[0] = x1

    xh = x1.astype(BF16)
    xl = (x1 - xh.astype(F32)).astype(BF16)
    l_hi = _dot(xh, wr_ref[...])
    l_lo = _dot(xl, wr_ref[:, :LANES])
    logits = l_hi[:, :LANES] + l_hi[:, LANES:] + l_lo + br_ref[...]
    lt = logits.T

    row8 = lax.broadcasted_iota(I32, (8, tm), 0)
    neg = jnp.float32(-jnp.inf)
    lg = jnp.where(row8 < n_groups, lt[0:8], neg)
    mg = jnp.max(lg, axis=0, keepdims=True)
    g_w = 1.0 / jnp.sum(jnp.exp(lg - mg), axis=0, keepdims=True)
    g_sel = jnp.min(jnp.where(lg == mg, row8, 8), axis=0, keepdims=True)
    le = lt[8:16]
    for g in range(1, n_groups):
        le = jnp.where(g_sel == g, lt[8 + 8 * g:16 + 8 * g], le)
    le = jnp.where(row8 < per_group, le, neg)
    v1 = jnp.max(le, axis=0, keepdims=True)
    i1 = jnp.min(jnp.where(le == v1, row8, 8), axis=0, keepdims=True)
    le2 = jnp.where(row8 == i1, neg, le)
    v2 = jnp.max(le2, axis=0, keepdims=True)
    i2 = jnp.min(jnp.where(le2 == v2, row8, 8), axis=0, keepdims=True)
    e21 = jnp.exp(v2 - v1)
    w_first = 1.0 / (1.0 + e21)
    gw0 = w_first * g_w
    gw1 = (e21 * w_first) * g_w
    eid0 = g_sel * per_group + i1
    eid1 = g_sel * per_group + i2

    rowl = lax.broadcasted_iota(I32, (LANES, tm), 0)
    oh0 = rowl == eid0
    oh1 = rowl == eid1
    a_idx = lax.broadcasted_iota(I32, (tm, tm), 0)
    b_idx = lax.broadcasted_iota(I32, (tm, tm), 1)
    before = (a_idx < b_idx).astype(BF16)
    r0 = _dot(oh0.astype(BF16), before)
    r1 = _dot(oh1.astype(BF16), before)
    cnt0 = jnp.sum(oh0.astype(F32), axis=1, keepdims=True)
    rank0 = jnp.sum(jnp.where(oh0, r0, 0.0), axis=0, keepdims=True)
    rank1 = jnp.sum(jnp.where(oh1, r1 + cnt0, 0.0), axis=0, keepdims=True)

    ri_ref[0] = jnp.zeros((8, tm), I32)
    ri_ref[0, 0:1, :] = eid0
    ri_ref[0, 1:2, :] = eid1
    ri_ref[0, 2:3, :] = rank0.astype(I32)
    ri_ref[0, 3:4, :] = rank1.astype(I32)

    wrows = jnp.where(rowl == 0, gw0, jnp.where(rowl == 1, gw1, 0.0))
    wcol_ref[0] = wrows.T

    both = jnp.logical_or(oh0, oh1).astype(BF16)
    ones = jnp.ones((8, tm), BF16)
    cnt_ref[0, 0] = lax.dot_general(ones, both, (((1,), (1,)), ((), ())),
                                    preferred_element_type=F32)


def _mixer(x, g0, b0, w_in, b_in, conv_w, pool_w, pool_scale, w_o, g1, b1, w_r, b_r, *,
           apply_ln0, alpha, n_groups, per_group, tm):
    bsz, seq, d = x.shape
    nt = seq // tm
    hb = tm // HALO
    nhb = seq // HALO
    d_in = w_in.shape[1]
    d_conv = conv_w.shape[1]
    full = lambda *shape: pl.BlockSpec(shape, lambda b, i: (0,) * len(shape))
    body = functools.partial(_mixer_body, apply_ln0=apply_ln0, alpha=alpha, n_groups=n_groups,
                             per_group=per_group, seq=seq)
    return pl.pallas_call(
        body,
        grid=(bsz, nt),
        in_specs=[
            pl.BlockSpec((1, tm, d), lambda b, i: (b, i, 0)),
            pl.BlockSpec((1, HALO, d), lambda b, i: (b, jnp.maximum(i * hb - 1, 0), 0)),
            pl.BlockSpec((1, HALO, d), lambda b, i: (b, jnp.minimum((i + 1) * hb, nhb - 1), 0)),
            full(1, d), full(1, d),
            full(d, d_in), full(1, d_in),
            full(CONV_WIDTH, d_conv),
            full(*pool_w.shape), full(1, pool_scale.shape[1]),
            full(d, d), full(1, d), full(1, d),
            full(d, 2 * LANES), full(1, LANES),
        ],
        out_specs=[
            pl.BlockSpec((1, tm, d), lambda b, i: (b, i, 0)),
            pl.BlockSpec((1, 8, tm), lambda b, i: (b, 0, i)),
            pl.BlockSpec((1, tm, LANES), lambda b, i: (b, i, 0)),
            pl.BlockSpec((1, 1, 8, LANES), lambda b, i: (b, i, 0, 0)),
        ],
        out_shape=[
            jax.ShapeDtypeStruct((bsz, seq, d), F32),
            jax.ShapeDtypeStruct((bsz, 8, seq), I32),
            jax.ShapeDtypeStruct((bsz, seq, LANES), F32),
            jax.ShapeDtypeStruct((bsz, nt, 8, LANES), F32),
        ],
        compiler_params=pltpu.CompilerParams(
            dimension_semantics=("arbitrary", "arbitrary"), vmem_limit_bytes=VMEM_LIMIT),
        name="mixer",
    )(x, x, x, g0, b0, w_in, b_in, conv_w, pool_w, pool_scale, w_o, g1, b1, w_r, b_r)


def _positions_body(cnt_ref, ri_ref, pos_ref, *, nt):
    t = pl.program_id(0) * nt + pl.program_id(1)
    tm = ri_ref.shape[2]
    c = cnt_ref[...]
    tot = jnp.sum(c, axis=0, keepdims=True)
    lane = lax.broadcasted_iota(I32, (8, LANES), 1)
    incl = jnp.broadcast_to(tot, (8, LANES))
    k = 1
    while k < LANES:
        incl = incl + jnp.where(lane >= k, pltpu.roll(incl, k, 1), 0.0)
        k *= 2
    tile = lax.broadcasted_iota(I32, c.shape, 0)
    prev = jnp.sum(jnp.where(tile < t, c, 0.0), axis=0, keepdims=True)
    base = (incl - tot + prev).astype(I32)

    row8 = lax.broadcasted_iota(I32, (8, LANES), 0)
    pieces = jnp.where(row8 == 0, base & 255,
                       jnp.where(row8 == 1, (base >> 8) & 255,
                                 jnp.where(row8 == 2, base >> 16, 0))).astype(F32).astype(BF16)
    rowl = lax.broadcasted_iota(I32, (LANES, tm), 0)
    ri = ri_ref[0]
    out = []
    for kk in range(TOP_K):
        oh = (rowl == ri[kk:kk + 1]).astype(BF16)
        pb = _dot(pieces, oh)
        off = pb[0:1] + 256.0 * pb[1:2] + 65536.0 * pb[2:3]
        out.append(off.astype(I32) + ri[TOP_K + kk:TOP_K + kk + 1])
    pos_ref[0] = jnp.zeros((8, tm), I32)
    pos_ref[0, 0:1, :] = out[0]
    pos_ref[0, 1:2, :] = out[1]


def _positions(cnt, ri, *, tm):
    bsz, _, seq = ri.shape
    nt = seq // tm
    return pl.pallas_call(
        functools.partial(_positions_body, nt=nt),
        grid=(bsz, nt),
        in_specs=[pl.BlockSpec(cnt.shape, lambda b, i: (0, 0)),
                  pl.BlockSpec((1, 8, tm), lambda b, i: (b, 0, i))],
        out_specs=pl.BlockSpec((1, 8, tm), lambda b, i: (b, 0, i)),
        out_shape=jax.ShapeDtypeStruct((bsz, 8, seq), I32),
        compiler_params=pltpu.CompilerParams(dimension_semantics=("arbitrary", "arbitrary")),
        name="positions",
    )(cnt, ri)


ROW_COPY_UNROLL = 8


def _start_row_copies(pos_ref, n_rows, make_copy):
    def issue(j, carry):
        for kk in range(TOP_K):
            make_copy(j, kk, pos_ref[0, kk, j]).start(priority=kk)
        return carry
    lax.fori_loop(0, n_rows, issue, 0, unroll=ROW_COPY_UNROLL)


def _wait_row_copies(n_rows, make_copy):
    def drain(j, carry):
        for kk in range(TOP_K):
            make_copy(j, kk, 0).wait()
        return carry
    lax.fori_loop(0, n_rows, drain, 0, unroll=ROW_COPY_UNROLL)


def _slot(ref, s, rps):
    start = s * rps
    if not isinstance(start, int):
        start = pl.multiple_of(start, rps)
    return ref.at[pl.ds(start, rps)]


def _ple_body(pos_ref, x1_ref, p_ref, wg_ref, bg_ref, wp_ref, ple_ref, xs_ref, buf_ref, sem):
    x1 = x1_ref[0]
    tm, d = x1.shape
    rps = d // 2 // LANES
    _store_slot_rows(buf_ref, _pack_bf16_pairs(x1))

    def make_copy(j, kk, slot):
        return pltpu.make_async_copy(_slot(buf_ref, j, rps), _slot(xs_ref, slot, rps), sem)

    _start_row_copies(pos_ref, tm, make_copy)
    gate = jax.nn.sigmoid(_dot(x1.astype(BF16), wg_ref[...]) + bg_ref[...])
    ple_ref[0] = gate * _dot(p_ref[0].astype(BF16), wp_ref[...])
    _wait_row_copies(tm, make_copy)


def _ple_dispatch(pos, x1, p, w_gate, b_gate, w_proj, *, layer, tm, n_slots):
    bsz, seq, d = x1.shape
    nt = seq // tm
    d_ple = p.shape[-1]
    rps = d // 2 // LANES
    full = lambda *shape: pl.BlockSpec(shape, lambda b, i: (0,) * len(shape))
    return pl.pallas_call(
        _ple_body,
        grid=(bsz, nt),
        in_specs=[
            pl.BlockSpec((1, 8, tm), lambda b, i: (b, 0, i), memory_space=pltpu.SMEM),
            pl.BlockSpec((1, tm, d), lambda b, i: (b, i, 0)),
            pl.BlockSpec((None, 1, tm, d_ple), lambda b, i: (layer, b, i, 0)),
            full(d, d), full(1, d), full(d_ple, d),
        ],
        out_specs=[
            pl.BlockSpec((1, tm, d), lambda b, i: (b, i, 0)),
            pl.BlockSpec(memory_space=pl.ANY),
        ],
        out_shape=[
            jax.ShapeDtypeStruct((bsz, seq, d), F32),
            jax.ShapeDtypeStruct((n_slots * rps, LANES), U32),
        ],
        scratch_shapes=[pltpu.VMEM((tm * rps, LANES), U32), pltpu.SemaphoreType.DMA(())],
        compiler_params=pltpu.CompilerParams(
            dimension_semantics=("arbitrary", "arbitrary"), vmem_limit_bytes=VMEM_LIMIT),
        name="ple_dispatch",
    )(pos, x1, p, w_gate, b_gate, w_proj)


def _experts_body(blk_ref, exp_ref, lo_ref, new_ref, xs_ref, w1_ref, w3_ref, w2_ref, ys_ref,
                  w1b_ref, w3b_ref, w2b_ref):
    it = pl.program_id(0)
    lo = lo_ref[it]
    rps = xs_ref.shape[0] // SLOT_BLOCK

    @pl.when(new_ref[it] == 1)
    def _():
        w1b_ref[...] = w1_ref[...].astype(BF16)
        w3b_ref[...] = w3_ref[...].astype(BF16)
        w2b_ref[...] = w2_ref[...].astype(BF16)

    @pl.when(lo < SLOT_BLOCK)
    def _():
        hi_half, lo_half = _unpack_bf16_pairs(_load_slot_rows(xs_ref, SLOT_BLOCK, rps))
        xb = jnp.concatenate([hi_half.astype(BF16), lo_half.astype(BF16)], axis=1)
        h = jax.nn.silu(_dot(xb, w1b_ref[...])) * _dot(xb, w3b_ref[...])
        y = _pack_bf16_pairs(_dot(h.astype(BF16), w2b_ref[...]))

        @pl.when(lo == 0)
        def _():
            _store_slot_rows(ys_ref, y)

        @pl.when(lo > 0)
        def _():
            rows = lax.broadcasted_iota(I32, y.shape, 0)
            _store_slot_rows(ys_ref, jnp.where(rows >= lo, y, _load_slot_rows(ys_ref, SLOT_BLOCK, rps)))


def _experts(items, xs, w1, w3, w2, *, layer):
    item_block, item_expert, item_lo, item_new = items
    rows = xs.shape[0]
    _, _, d, d_e = w1.shape
    rps = d // 2 // LANES
    n_items = item_block.shape[0]
    grid_spec = pltpu.PrefetchScalarGridSpec(
        num_scalar_prefetch=4,
        grid=(n_items,),
        in_specs=[
            pl.BlockSpec((SLOT_BLOCK * rps, LANES), lambda i, blk, exp, lo, new: (blk[i], 0)),
            pl.BlockSpec((None, None, d, d_e), lambda i, blk, exp, lo, new: (layer, exp[i], 0, 0)),
            pl.BlockSpec((None, None, d, d_e), lambda i, blk, exp, lo, new: (layer, exp[i], 0, 0)),
            pl.BlockSpec((None, None, d_e, d), lambda i, blk, exp, lo, new: (layer, exp[i], 0, 0)),
        ],
        out_specs=pl.BlockSpec((SLOT_BLOCK * rps, LANES), lambda i, blk, exp, lo, new: (blk[i], 0)),
        scratch_shapes=[pltpu.VMEM((d, d_e), BF16), pltpu.VMEM((d, d_e), BF16),
                        pltpu.VMEM((d_e, d), BF16)],
    )
    return pl.pallas_call(
        _experts_body,
        grid_spec=grid_spec,
        out_shape=jax.ShapeDtypeStruct((rows, LANES), U32),
        compiler_params=pltpu.CompilerParams(
            dimension_semantics=("arbitrary",), vmem_limit_bytes=VMEM_LIMIT),
        name="experts",
    )(item_block, item_expert, item_lo, item_new, xs, w1, w3, w2)


def _expert_items(cnt, n_experts, n_slots):
    n_blocks = n_slots // SLOT_BLOCK
    n_items = n_blocks + n_experts - 1
    tot = jnp.sum(cnt, axis=0)[:n_experts].astype(I32)
    end = jnp.cumsum(tot)
    start = end - tot
    first_blk = start // SLOT_BLOCK
    n_it = jnp.where(tot > 0, (end - 1) // SLOT_BLOCK - first_blk + 1, 0)
    it_end = jnp.cumsum(n_it)
    it_start = it_end - n_it
    idx = jnp.arange(n_items, dtype=I32)
    live = idx < it_end[-1]
    expert = jnp.minimum(jnp.sum(it_end[None, :] <= idx[:, None], axis=1), n_experts - 1).astype(I32)
    onehot = expert[:, None] == jnp.arange(n_experts, dtype=I32)[None, :]
    pick = lambda v: jnp.sum(jnp.where(onehot, v[None, :], 0), axis=1)
    nth = idx - pick(it_start)
    block = pick(first_blk) + nth
    item_block = jnp.where(live, block, n_blocks - 1).astype(I32)
    item_lo = jnp.where(live, jnp.maximum(pick(start) - block * SLOT_BLOCK, 0), SLOT_BLOCK).astype(I32)
    prev_expert = jnp.concatenate([jnp.full((1,), -1, I32), expert[:-1]])
    item_new = (expert != prev_expert).astype(I32)
    return item_block, expert, item_lo, item_new


def _combine_body(pos_ref, x1_ref, ple_ref, wcol_ref, ys_ref, g2_ref, b2_ref, out_ref,
                  y0_ref, y1_ref, sem, *, alpha):
    tm, d = x1_ref.shape[1:]
    rps = d // 2 // LANES
    bufs = (y0_ref, y1_ref)

    def make_copy(j, kk, slot):
        return pltpu.make_async_copy(_slot(ys_ref, slot, rps), _slot(bufs[kk], j, rps), sem)

    _start_row_copies(pos_ref, tm, make_copy)
    _wait_row_copies(tm, make_copy)

    wcol = wcol_ref[0]
    ffn = None
    for kk in range(TOP_K):
        hi_half, lo_half = _unpack_bf16_pairs(_load_slot_rows(bufs[kk], tm, rps))
        term = jnp.concatenate([hi_half, lo_half], axis=1) * wcol[:, kk:kk + 1]
        ffn = term if ffn is None else ffn + term
    out_ref[0] = _layernorm(alpha * x1_ref[0] + ffn + ple_ref[0], g2_ref[...], b2_ref[...])


def _combine(pos, x1, ple, wcol, ys, g2, b2, *, alpha, tm):
    bsz, seq, d = x1.shape
    nt = seq // tm
    rps = d // 2 // LANES
    full = lambda *shape: pl.BlockSpec(shape, lambda b, i: (0,) * len(shape))
    return pl.pallas_call(
        functools.partial(_combine_body, alpha=alpha),
        grid=(bsz, nt),
        in_specs=[
            pl.BlockSpec((1, 8, tm), lambda b, i: (b, 0, i), memory_space=pltpu.SMEM),
            pl.BlockSpec((1, tm, d), lambda b, i: (b, i, 0)),
            pl.BlockSpec((1, tm, d), lambda b, i: (b, i, 0)),
            pl.BlockSpec((1, tm, LANES), lambda b, i: (b, i, 0)),
            pl.BlockSpec(memory_space=pl.ANY),
            full(1, d), full(1, d),
        ],
        out_specs=pl.BlockSpec((1, tm, d), lambda b, i: (b, i, 0)),
        out_shape=jax.ShapeDtypeStruct((bsz, seq, d), F32),
        scratch_shapes=[pltpu.VMEM((tm * rps, LANES), U32), pltpu.VMEM((tm * rps, LANES), U32),
                        pltpu.SemaphoreType.DMA(())],
        compiler_params=pltpu.CompilerParams(
            dimension_semantics=("arbitrary", "arbitrary"), vmem_limit_bytes=VMEM_LIMIT),
        name="combine",
    )(pos, x1, ple, wcol, ys, g2, b2)


def _router_weights(w_rg, b_rg, w_re, b_re):
    d, n_groups = w_rg.shape
    per_group = w_re.shape[2]
    assert n_groups <= 8 and per_group <= 8 and 8 + 8 * n_groups <= LANES
    w_e = jnp.pad(jnp.transpose(w_re, (1, 0, 2)), ((0, 0), (0, 0), (0, 8 - per_group)))
    b_e = jnp.pad(b_re, ((0, 0), (0, 8 - per_group)))
    tail = LANES - 8 - 8 * n_groups
    w = jnp.concatenate([jnp.pad(w_rg, ((0, 0), (0, 8 - n_groups))), w_e.reshape(d, 8 * n_groups),
                         jnp.zeros((d, tail), F32)], axis=1)
    b = jnp.concatenate([jnp.pad(b_rg, (0, 8 - n_groups)), b_e.reshape(-1), jnp.zeros((tail,), F32)])
    w_hi = w.astype(BF16)
    w_lo = (w - w_hi.astype(F32)).astype(BF16)
    return jnp.concatenate([w_hi, w_lo], axis=1), b[None, :]


def kernel(x, p, ln0_g, ln0_b, w_in, b_in, conv_w, pool_w, pool_scale, w_o, ln1_g, ln1_b,
           w_router_group, b_router_group, w_router_expert, b_router_expert, w1, w3, w2,
           w_ple_gate, b_ple_gate, w_ple_proj, ln2_g, ln2_b):
    bsz, seq, d = x.shape
    depth = w_in.shape[0]
    n_groups, per_group = w_router_expert.shape[1], w_router_expert.shape[3]
    n_experts = n_groups * per_group
    alpha = (2 * depth) ** 0.25
    tm = min(512, seq)
    n_slots = bsz * seq * TOP_K
    assert seq % tm == 0 and tm % LANES == 0 and n_slots % SLOT_BLOCK == 0 and n_experts <= LANES
    assert d % (2 * LANES) == 0
    row = lambda a: a[None, :]

    for i in range(depth):
        w_r, b_r = _router_weights(w_router_group[i], b_router_group[i],
                                   w_router_expert[i], b_router_expert[i])
        x1, ri, wcol, cnt = _mixer(
            x, row(ln0_g), row(ln0_b), w_in[i].astype(BF16), row(b_in[i]), conv_w[i],
            pool_w[i].astype(BF16), row(pool_scale[i]), w_o[i].astype(BF16),
            row(ln1_g[i]), row(ln1_b[i]), w_r, b_r,
            apply_ln0=(i == 0), alpha=alpha, n_groups=n_groups, per_group=per_group, tm=tm)
        cnt = cnt[:, :, 0, :].reshape(-1, LANES)
        pos = _positions(cnt, ri, tm=tm)
        items = _expert_items(cnt, n_experts, n_slots)
        ple, xs = _ple_dispatch(pos, x1, p, w_ple_gate[i].astype(BF16), row(b_ple_gate[i]),
                                w_ple_proj[i].astype(BF16), layer=i, tm=tm, n_slots=n_slots)
        ys = _experts(items, xs, w1, w3, w2, layer=i)
        x = _combine(pos, x1, ple, wcol, ys, row(ln2_g[i]), row(ln2_b[i]), alpha=alpha, tm=tm)
    return x
```

```python
import functools

import jax
import jax.numpy as jnp
from jax import lax
from jax.experimental import pallas as pl
from jax.experimental.pallas import tpu as pltpu

POOL_WINDOWS = (2, 4, 8, 16)
CONV_WIDTH = 3
TOP_K = 2
LN_EPS = 1e-5
HALO = 8
LANES = 128
SLOT_BLOCK = 256
VMEM_LIMIT = 56 * 1024 * 1024

F32 = jnp.float32
BF16 = jnp.bfloat16
I32 = jnp.int32
U32 = jnp.uint32


def _layernorm(v, g, b):
    mu = jnp.mean(v, axis=-1, keepdims=True)
    c = v - mu
    var = jnp.mean(c * c, axis=-1, keepdims=True)
    return c * lax.rsqrt(var + LN_EPS) * g + b


def _dot(a, b):
    return jnp.dot(a, b, preferred_element_type=F32)


def _pack_bf16_pairs(v):
    c = v.shape[1] // 2
    hi = pltpu.bitcast(v[:, :c].astype(BF16).astype(F32), U32)
    lo = pltpu.bitcast(v[:, c:].astype(BF16).astype(F32), U32)
    return hi | (lo >> 16)


def _unpack_bf16_pairs(u):
    hi = pltpu.bitcast(u & jnp.uint32(0xFFFF0000), F32)
    lo = pltpu.bitcast(u << 16, F32)
    return hi, lo


def _store_slot_rows(ref, packed):
    n, c = packed.shape
    rps = c // LANES
    for k in range(rps):
        ref[pl.ds(k, n, stride=rps), :] = packed[:, k * LANES:(k + 1) * LANES]


def _load_slot_rows(ref, n, rps):
    return jnp.concatenate([ref[pl.ds(k, n, stride=rps), :] for k in range(rps)], axis=1)


def _mixer_body(x_ref, xp_ref, xn_ref, g0_ref, b0_ref, win_ref, bin_ref, cw_ref, pw_ref, ps_ref,
                wo_ref, g1_ref, b1_ref, wr_ref, br_ref,
                x1_ref, ri_ref, wcol_ref, cnt_ref, *, apply_ln0, alpha, n_groups, per_group, seq):
    i = pl.program_id(1)
    nt = pl.num_programs(1)
    tm = x_ref.shape[1]
    rows = tm + 2 * HALO
    d_conv = cw_ref.shape[1]
    dg = pw_ref.shape[1]

    xe = jnp.concatenate([xp_ref[0], x_ref[0], xn_ref[0]], axis=0)
    if apply_ln0:
        xe = _layernorm(xe, g0_ref[...], b0_ref[...])
    z = _dot(xe.astype(BF16), win_ref[...]) + bin_ref[...]

    r = lax.broadcasted_iota(I32, (rows, 1), 0)
    valid = jnp.logical_and(jnp.logical_or(r >= HALO, i > 0),
                            jnp.logical_or(r < tm + HALO, i < nt - 1))

    h = z[:, :d_conv]
    gate_b = z[HALO:HALO + tm, d_conv:2 * d_conv]
    gate_c = z[:, 2 * d_conv:3 * d_conv]
    v = jnp.where(valid, gate_c * h, 0.0)
    cw = cw_ref[...]
    conv = (cw[0:1] * pltpu.roll(v, 1, 0)[HALO:HALO + tm]
            + cw[1:2] * v[HALO:HALO + tm]
            + cw[2:3] * pltpu.roll(v, rows - 1, 0)[HALO:HALO + tm])
    parts = [gate_b * conv]

    pos = i * tm + r[HALO:HALO + tm] - HALO
    ps = ps_ref[...]
    for g, w in enumerate(POOL_WINDOWS):
        left = w // 2
        right = w - 1 - left
        c0 = 3 * d_conv + g * dg
        ug = jnp.where(valid, z[:, c0:c0 + dg], 0.0)
        acc = ug
        k = 1
        while k < w:
            acc = acc + pltpu.roll(acc, k, 0)
            k *= 2
        if right:
            acc = pltpu.roll(acc, rows - right, 0)
        cnt = (jnp.minimum(pos + right + 1, seq) - jnp.maximum(pos - left, 0)).astype(F32)
        pooled = acc[HALO:HALO + tm] / cnt - ug[HALO:HALO + tm]
        parts.append(_dot(pooled.astype(BF16), pw_ref[g]) * ps[:, g * dg:(g + 1) * dg])
    cat = jnp.concatenate(parts, axis=1)
    mix = _dot(cat.astype(BF16), wo_ref[...])
    x1 = _layernorm(alpha * xe[HALO:HALO + tm] + mix, g1_ref[...], b1_ref[...])
    x1_ref[0] = x1

    xh = x1.astype(BF16)
    xl = (x1 - xh.astype(F32)).astype(BF16)
    l_hi = _dot(xh, wr_ref[...])
    l_lo = _dot(xl, wr_ref[:, :LANES])
    logits = l_hi[:, :LANES] + l_hi[:, LANES:] + l_lo + br_ref[...]
    lt = logits.T

    row8 = lax.broadcasted_iota(I32, (8, tm), 0)
    neg = jnp.float32(-jnp.inf)
    lg = jnp.where(row8 < n_groups, lt[0:8], neg)
    mg = jnp.max(lg, axis=0, keepdims=True)
    g_w = 1.0 / jnp.sum(jnp.exp(lg - mg), axis=0, keepdims=True)
    g_sel = jnp.min(jnp.where(lg == mg, row8, 8), axis=0, keepdims=True)
    le = lt[8:16]
    for g in range(1, n_groups):
        le = jnp.where(g_sel == g, lt[8 + 8 * g:16 + 8 * g], le)
    le = jnp.where(row8 < per_group, le, neg)
    v1 = jnp.max(le, axis=0, keepdims=True)
    i1 = jnp.min(jnp.where(le == v1, row8, 8), axis=0, keepdims=True)
    le2 = jnp.where(row8 == i1, neg, le)
    v2 = jnp.max(le2, axis=0, keepdims=True)
    i2 = jnp.min(jnp.where(le2 == v2, row8, 8), axis=0, keepdims=True)
    e21 = jnp.exp(v2 - v1)
    w_first = 1.0 / (1.0 + e21)
    gw0 = w_first * g_w
    gw1 = (e21 * w_first) * g_w
    eid0 = g_sel * per_group + i1
    eid1 = g_sel * per_group + i2

    rowl = lax.broadcasted_iota(I32, (LANES, tm), 0)
    oh0 = rowl == eid0
    oh1 = rowl == eid1
    a_idx = lax.broadcasted_iota(I32, (tm, tm), 0)
    b_idx = lax.broadcasted_iota(I32, (tm, tm), 1)
    before = (a_idx < b_idx).astype(BF16)
    r0 = _dot(oh0.astype(BF16), before)
    r1 = _dot(oh1.astype(BF16), before)
    cnt0 = jnp.sum(oh0.astype(F32), axis=1, keepdims=True)
    lower = jnp.sum((rowl > eid0).astype(F32) + (rowl > eid1).astype(F32),
                    axis=1, keepdims=True)
    lpos0 = jnp.sum(jnp.where(oh0, lower + r0, 0.0), axis=0, keepdims=True)
    lpos1 = jnp.sum(jnp.where(oh1, lower + cnt0 + r1, 0.0), axis=0, keepdims=True)

    ri_ref[0] = jnp.zeros((8, tm), I32)
    ri_ref[0, 0:1, :] = lpos0.astype(I32)
    ri_ref[0, 1:2, :] = lpos1.astype(I32)

    wrows = jnp.where(rowl == 0, gw0, jnp.where(rowl == 1, gw1,
                      jnp.where(rowl == 2, lpos0, jnp.where(rowl == 3, lpos1, 0.0))))
    wcol_ref[0] = wrows.T

    both = jnp.logical_or(oh0, oh1).astype(BF16)
    ones = jnp.ones((8, tm), BF16)
    cnt_ref[0, 0] = lax.dot_general(ones, both, (((1,), (1,)), ((), ())),
                                    preferred_element_type=F32)


def _mixer(x, g0, b0, w_in, b_in, conv_w, pool_w, pool_scale, w_o, g1, b1, w_r, b_r, *,
           apply_ln0, alpha, n_groups, per_group, tm):
    bsz, seq, d = x.shape
    nt = seq // tm
    hb = tm // HALO
    nhb = seq // HALO
    d_in = w_in.shape[1]
    d_conv = conv_w.shape[1]
    full = lambda *shape: pl.BlockSpec(shape, lambda b, i: (0,) * len(shape))
    body = functools.partial(_mixer_body, apply_ln0=apply_ln0, alpha=alpha, n_groups=n_groups,
                             per_group=per_group, seq=seq)
    return pl.pallas_call(
        body,
        grid=(bsz, nt),
        in_specs=[
            pl.BlockSpec((1, tm, d), lambda b, i: (b, i, 0)),
            pl.BlockSpec((1, HALO, d), lambda b, i: (b, jnp.maximum(i * hb - 1, 0), 0)),
            pl.BlockSpec((1, HALO, d), lambda b, i: (b, jnp.minimum((i + 1) * hb, nhb - 1), 0)),
            full(1, d), full(1, d),
            full(d, d_in), full(1, d_in),
            full(CONV_WIDTH, d_conv),
            full(*pool_w.shape), full(1, pool_scale.shape[1]),
            full(d, d), full(1, d), full(1, d),
            full(d, 2 * LANES), full(1, LANES),
        ],
        out_specs=[
            pl.BlockSpec((1, tm, d), lambda b, i: (b, i, 0)),
            pl.BlockSpec((1, 8, tm), lambda b, i: (b, 0, i)),
            pl.BlockSpec((1, tm, LANES), lambda b, i: (b, i, 0)),
            pl.BlockSpec((1, 1, 8, LANES), lambda b, i: (b, i, 0, 0)),
        ],
        out_shape=[
            jax.ShapeDtypeStruct((bsz, seq, d), F32),
            jax.ShapeDtypeStruct((bsz, 8, seq), I32),
            jax.ShapeDtypeStruct((bsz, seq, LANES), F32),
            jax.ShapeDtypeStruct((bsz, nt, 8, LANES), F32),
        ],
        compiler_params=pltpu.CompilerParams(
            dimension_semantics=("arbitrary", "arbitrary"), vmem_limit_bytes=VMEM_LIMIT),
        name="mixer",
    )(x, x, x, g0, b0, w_in, b_in, conv_w, pool_w, pool_scale, w_o, g1, b1, w_r, b_r)


def _tables_body(cnt_ref, gstart_ref, lstart_ref, len_ref):
    c = cnt_ref[...]
    n_tiles = c.shape[0]
    row = lax.broadcasted_iota(I32, c.shape, 0)
    lane = lax.broadcasted_iota(I32, c.shape, 1)

    def prefix(v, idx, size, axis):
        k = 1
        while k < size:
            v = v + jnp.where(idx >= k, pltpu.roll(v, k, axis), 0.0)
            k *= 2
        return v

    over_tiles = prefix(c, row, n_tiles, 0)
    tot = jnp.broadcast_to(over_tiles[n_tiles - 1:n_tiles], c.shape)
    expert_start = prefix(tot, lane, LANES, 1) - tot
    gstart_ref[...] = (expert_start + over_tiles - c).astype(I32)
    lstart_ref[...] = (prefix(c, lane, LANES, 1) - c).astype(I32)
    len_ref[...] = c.astype(I32)


def _run_tables(cnt, n_experts):
    out = jax.ShapeDtypeStruct(cnt.shape, I32)
    tabs = pl.pallas_call(_tables_body, out_shape=[out, out, out], name="run_tables")(cnt)
    return tuple(t[:, :n_experts].reshape(-1) for t in tabs)


RUN_CHUNK_LOG2 = 6


def _slots(ref, first, n, rps):
    start = first * rps
    if not isinstance(start, int):
        start = pl.multiple_of(start, rps)
    return ref.at[pl.ds(start, n * rps)]


def _start_run_copies(tabs, tile, n_experts, make_copy):
    gstart_ref, lstart_ref, len_ref = tabs
    big = 1 << RUN_CHUNK_LOG2

    def per_expert(e, carry):
        idx = tile * n_experts + e
        n, ls, gs = len_ref[idx], lstart_ref[idx], gstart_ref[idx]
        n_big = n >> RUN_CHUNK_LOG2

        def big_piece(c, carry2):
            make_copy(ls + c * big, gs + c * big, big).start()
            return carry2
        lax.fori_loop(0, n_big, big_piece, 0)
        off = n_big * big
        for bit in reversed(range(RUN_CHUNK_LOG2)):
            size = 1 << bit

            @pl.when((n & size) != 0)
            def _(off=off, size=size, bit=bit):
                make_copy(ls + off, gs + off, size).start(priority=bit % 2)
            off = off + (n & size)
        return carry
    lax.fori_loop(0, n_experts, per_expert, 0)


def _ple_body(gstart_ref, lstart_ref, len_ref, x1_ref, p_ref, ri_ref, wg_ref, bg_ref, wp_ref,
              ple_ref, xs_ref, buf_ref, sem, *, n_experts):
    tile = pl.program_id(0) * pl.num_programs(1) + pl.program_id(1)
    tabs = (gstart_ref, lstart_ref, len_ref)
    x1 = x1_ref[0]
    tm, d = x1.shape
    rps = d // 2 // LANES

    lpos = ri_ref[0]
    s_idx = lax.broadcasted_iota(I32, (TOP_K * tm, tm), 0)
    perm = jnp.logical_or(s_idx == lpos[0:1], s_idx == lpos[1:2]).astype(BF16)
    xb = x1.astype(BF16)
    _store_slot_rows(buf_ref, _pack_bf16_pairs(_dot(perm, xb)))

    def make_copy(local, glob, n):
        return pltpu.make_async_copy(_slots(buf_ref, local, n, rps), _slots(xs_ref, glob, n, rps), sem)

    _start_run_copies(tabs, tile, n_experts, make_copy)
    gate = jax.nn.sigmoid(_dot(xb, wg_ref[...]) + bg_ref[...])
    ple_ref[0] = gate * _dot(p_ref[0].astype(BF16), wp_ref[...])
    make_copy(0, 0, TOP_K * tm).wait()


def _ple_dispatch(tabs, x1, p, ri, w_gate, b_gate, w_proj, *, layer, tm, n_slots, n_experts):
    bsz, seq, d = x1.shape
    nt = seq // tm
    d_ple = p.shape[-1]
    rps = d // 2 // LANES
    full = lambda *shape: pl.BlockSpec(shape, lambda b, i, *_: (0,) * len(shape))
    grid_spec = pltpu.PrefetchScalarGridSpec(
        num_scalar_prefetch=3,
        grid=(bsz, nt),
        in_specs=[
            pl.BlockSpec((1, tm, d), lambda b, i, *_: (b, i, 0)),
            pl.BlockSpec((None, 1, tm, d_ple), lambda b, i, *_: (layer, b, i, 0)),
            pl.BlockSpec((1, 8, tm), lambda b, i, *_: (b, 0, i)),
            full(d, d), full(1, d), full(d_ple, d),
        ],
        out_specs=[
            pl.BlockSpec((1, tm, d), lambda b, i, *_: (b, i, 0)),
            pl.BlockSpec(memory_space=pl.ANY),
        ],
        scratch_shapes=[pltpu.VMEM((TOP_K * tm * rps, LANES), U32), pltpu.SemaphoreType.DMA(())],
    )
    return pl.pallas_call(
        functools.partial(_ple_body, n_experts=n_experts),
        grid_spec=grid_spec,
        out_shape=[
            jax.ShapeDtypeStruct((bsz, seq, d), F32),
            jax.ShapeDtypeStruct((n_slots * rps, LANES), U32),
        ],
        compiler_params=pltpu.CompilerParams(
            dimension_semantics=("arbitrary", "arbitrary"), vmem_limit_bytes=VMEM_LIMIT),
        name="ple_dispatch",
    )(*tabs, x1, p, ri, w_gate, b_gate, w_proj)


def _experts_body(blk_ref, exp_ref, lo_ref, new_ref, xs_ref, w1_ref, w3_ref, w2_ref, ys_ref,
                  w1b_ref, w3b_ref, w2b_ref):
    it = pl.program_id(0)
    lo = lo_ref[it]
    rps = xs_ref.shape[0] // SLOT_BLOCK

    @pl.when(new_ref[it] == 1)
    def _():
        w1b_ref[...] = w1_ref[...].astype(BF16)
        w3b_ref[...] = w3_ref[...].astype(BF16)
        w2b_ref[...] = w2_ref[...].astype(BF16)

    @pl.when(lo < SLOT_BLOCK)
    def _():
        hi_half, lo_half = _unpack_bf16_pairs(_load_slot_rows(xs_ref, SLOT_BLOCK, rps))
        xb = jnp.concatenate([hi_half.astype(BF16), lo_half.astype(BF16)], axis=1)
        h = jax.nn.silu(_dot(xb, w1b_ref[...])) * _dot(xb, w3b_ref[...])
        y = _pack_bf16_pairs(_dot(h.astype(BF16), w2b_ref[...]))

        @pl.when(lo == 0)
        def _():
            _store_slot_rows(ys_ref, y)

        @pl.when(lo > 0)
        def _():
            rows = lax.broadcasted_iota(I32, y.shape, 0)
            _store_slot_rows(ys_ref, jnp.where(rows >= lo, y, _load_slot_rows(ys_ref, SLOT_BLOCK, rps)))


def _experts(items, xs, w1, w3, w2, *, layer):
    item_block, item_expert, item_lo, item_new = items
    rows = xs.shape[0]
    _, _, d, d_e = w1.shape
    rps = d // 2 // LANES
    n_items = item_block.shape[0]
    grid_spec = pltpu.PrefetchScalarGridSpec(
        num_scalar_prefetch=4,
        grid=(n_items,),
        in_specs=[
            pl.BlockSpec((SLOT_BLOCK * rps, LANES), lambda i, blk, exp, lo, new: (blk[i], 0)),
            pl.BlockSpec((None, None, d, d_e), lambda i, blk, exp, lo, new: (layer, exp[i], 0, 0)),
            pl.BlockSpec((None, None, d, d_e), lambda i, blk, exp, lo, new: (layer, exp[i], 0, 0)),
            pl.BlockSpec((None, None, d_e, d), lambda i, blk, exp, lo, new: (layer, exp[i], 0, 0)),
        ],
        out_specs=pl.BlockSpec((SLOT_BLOCK * rps, LANES), lambda i, blk, exp, lo, new: (blk[i], 0)),
        scratch_shapes=[pltpu.VMEM((d, d_e), BF16), pltpu.VMEM((d, d_e), BF16),
                        pltpu.VMEM((d_e, d), BF16)],
    )
    return pl.pallas_call(
        _experts_body,
        grid_spec=grid_spec,
        out_shape=jax.ShapeDtypeStruct((rows, LANES), U32),
        compiler_params=pltpu.CompilerParams(
            dimension_semantics=("arbitrary",), vmem_limit_bytes=VMEM_LIMIT),
        name="experts",
    )(item_block, item_expert, item_lo, item_new, xs, w1, w3, w2)


def _expert_items(cnt, n_experts, n_slots):
    n_blocks = n_slots // SLOT_BLOCK
    n_items = n_blocks + n_experts - 1
    tot = jnp.sum(cnt, axis=0)[:n_experts].astype(I32)
    end = jnp.cumsum(tot)
    start = end - tot
    first_blk = start // SLOT_BLOCK
    n_it = jnp.where(tot > 0, (end - 1) // SLOT_BLOCK - first_blk + 1, 0)
    it_end = jnp.cumsum(n_it)
    it_start = it_end - n_it
    idx = jnp.arange(n_items, dtype=I32)
    live = idx < it_end[-1]
    expert = jnp.minimum(jnp.sum(it_end[None, :] <= idx[:, None], axis=1), n_experts - 1).astype(I32)
    onehot = expert[:, None] == jnp.arange(n_experts, dtype=I32)[None, :]
    pick = lambda v: jnp.sum(jnp.where(onehot, v[None, :], 0), axis=1)
    nth = idx - pick(it_start)
    block = pick(first_blk) + nth
    item_block = jnp.where(live, block, n_blocks - 1).astype(I32)
    item_lo = jnp.where(live, jnp.maximum(pick(start) - block * SLOT_BLOCK, 0), SLOT_BLOCK).astype(I32)
    prev_expert = jnp.concatenate([jnp.full((1,), -1, I32), expert[:-1]])
    item_new = (expert != prev_expert).astype(I32)
    return item_block, expert, item_lo, item_new


def _combine_body(gstart_ref, lstart_ref, len_ref, x1_ref, ple_ref, wcol_ref, ys_ref, g2_ref, b2_ref,
                  out_ref, buf_ref, sem, *, alpha, n_experts):
    tile = pl.program_id(0) * pl.num_programs(1) + pl.program_id(1)
    tabs = (gstart_ref, lstart_ref, len_ref)
    tm, d = x1_ref.shape[1:]
    rps = d // 2 // LANES

    def make_copy(local, glob, n):
        return pltpu.make_async_copy(_slots(ys_ref, glob, n, rps), _slots(buf_ref, local, n, rps), sem)

    _start_run_copies(tabs, tile, n_experts, make_copy)
    make_copy(0, 0, TOP_K * tm).wait()

    hi_half, lo_half = _unpack_bf16_pairs(_load_slot_rows(buf_ref, TOP_K * tm, rps))
    ys = jnp.concatenate([hi_half.astype(BF16), lo_half.astype(BF16)], axis=1)
    wcol = wcol_ref[0]
    s_idx = lax.broadcasted_iota(I32, (tm, TOP_K * tm), 1)
    ffn = None
    for kk in range(TOP_K):
        pick = (s_idx == wcol[:, TOP_K + kk:TOP_K + kk + 1].astype(I32)).astype(BF16)
        term = _dot(pick, ys) * wcol[:, kk:kk + 1]
        ffn = term if ffn is None else ffn + term
    out_ref[0] = _layernorm(alpha * x1_ref[0] + ffn + ple_ref[0], g2_ref[...], b2_ref[...])


def _combine(tabs, x1, ple, wcol, ys, g2, b2, *, alpha, tm, n_experts):
    bsz, seq, d = x1.shape
    nt = seq // tm
    rps = d // 2 // LANES
    full = lambda *shape: pl.BlockSpec(shape, lambda b, i, *_: (0,) * len(shape))
    grid_spec = pltpu.PrefetchScalarGridSpec(
        num_scalar_prefetch=3,
        grid=(bsz, nt),
        in_specs=[
            pl.BlockSpec((1, tm, d), lambda b, i, *_: (b, i, 0)),
            pl.BlockSpec((1, tm, d), lambda b, i, *_: (b, i, 0)),
            pl.BlockSpec((1, tm, LANES), lambda b, i, *_: (b, i, 0)),
            pl.BlockSpec(memory_space=pl.ANY),
            full(1, d), full(1, d),
        ],
        out_specs=pl.BlockSpec((1, tm, d), lambda b, i, *_: (b, i, 0)),
        scratch_shapes=[pltpu.VMEM((TOP_K * tm * rps, LANES), U32), pltpu.SemaphoreType.DMA(())],
    )
    return pl.pallas_call(
        functools.partial(_combine_body, alpha=alpha, n_experts=n_experts),
        grid_spec=grid_spec,
        out_shape=jax.ShapeDtypeStruct((bsz, seq, d), F32),
        compiler_params=pltpu.CompilerParams(
            dimension_semantics=("arbitrary", "arbitrary"), vmem_limit_bytes=VMEM_LIMIT),
        name="combine",
    )(*tabs, x1, ple, wcol, ys, g2, b2)


def _router_weights(w_rg, b_rg, w_re, b_re):
    d, n_groups = w_rg.shape
    per_group = w_re.shape[2]
    assert n_groups <= 8 and per_group <= 8 and 8 + 8 * n_groups <= LANES
    w_e = jnp.pad(jnp.transpose(w_re, (1, 0, 2)), ((0, 0), (0, 0), (0, 8 - per_group)))
    b_e = jnp.pad(b_re, ((0, 0), (0, 8 - per_group)))
    tail = LANES - 8 - 8 * n_groups
    w = jnp.concatenate([jnp.pad(w_rg, ((0, 0), (0, 8 - n_groups))), w_e.reshape(d, 8 * n_groups),
                         jnp.zeros((d, tail), F32)], axis=1)
    b = jnp.concatenate([jnp.pad(b_rg, (0, 8 - n_groups)), b_e.reshape(-1), jnp.zeros((tail,), F32)])
    w_hi = w.astype(BF16)
    w_lo = (w - w_hi.astype(F32)).astype(BF16)
    return jnp.concatenate([w_hi, w_lo], axis=1), b[None, :]


def kernel(x, p, ln0_g, ln0_b, w_in, b_in, conv_w, pool_w, pool_scale, w_o, ln1_g, ln1_b,
           w_router_group, b_router_group, w_router_expert, b_router_expert, w1, w3, w2,
           w_ple_gate, b_ple_gate, w_ple_proj, ln2_g, ln2_b):
    bsz, seq, d = x.shape
    depth = w_in.shape[0]
    n_groups, per_group = w_router_expert.shape[1], w_router_expert.shape[3]
    n_experts = n_groups * per_group
    alpha = (2 * depth) ** 0.25
    tm = min(512, seq)
    n_slots = bsz * seq * TOP_K
    assert seq % tm == 0 and tm % LANES == 0 and n_slots % SLOT_BLOCK == 0 and n_experts <= LANES
    assert d % (2 * LANES) == 0
    row = lambda a: a[None, :]

    for i in range(depth):
        w_r, b_r = _router_weights(w_router_group[i], b_router_group[i],
                                   w_router_expert[i], b_router_expert[i])
        x1, ri, wcol, cnt = _mixer(
            x, row(ln0_g), row(ln0_b), w_in[i].astype(BF16), row(b_in[i]), conv_w[i],
            pool_w[i].astype(BF16), row(pool_scale[i]), w_o[i].astype(BF16),
            row(ln1_g[i]), row(ln1_b[i]), w_r, b_r,
            apply_ln0=(i == 0), alpha=alpha, n_groups=n_groups, per_group=per_group, tm=tm)
        cnt = cnt[:, :, 0, :].reshape(-1, LANES)
        tabs = _run_tables(cnt, n_experts)
        items = _expert_items(cnt, n_experts, n_slots)
        ple, xs = _ple_dispatch(tabs, x1, p, ri, w_ple_gate[i].astype(BF16), row(b_ple_gate[i]),
                                w_ple_proj[i].astype(BF16), layer=i, tm=tm, n_slots=n_slots,
                                n_experts=n_experts)
        ys = _experts(items, xs, w1, w3, w2, layer=i)
        x = _combine(tabs, x1, ple, wcol, ys, row(ln2_g[i]), row(ln2_b[i]), alpha=alpha, tm=tm,
                     n_experts=n_experts)
    return x
```

```python
import functools

import jax
import jax.numpy as jnp
from jax import lax
from jax.experimental import pallas as pl
from jax.experimental.pallas import tpu as pltpu

POOL_WINDOWS = (2, 4, 8, 16)
CONV_WIDTH = 3
TOP_K = 2
LN_EPS = 1e-5
HALO = 8
LANES = 128
SLOT_BLOCK = 256
VMEM_LIMIT = 56 * 1024 * 1024

F32 = jnp.float32
BF16 = jnp.bfloat16
I32 = jnp.int32
U32 = jnp.uint32


def _layernorm(v, g, b):
    mu = jnp.mean(v, axis=-1, keepdims=True)
    c = v - mu
    var = jnp.mean(c * c, axis=-1, keepdims=True)
    return c * lax.rsqrt(var + LN_EPS) * g + b


def _dot(a, b):
    return jnp.dot(a, b, preferred_element_type=F32)


def _pack_bf16_pairs(v):
    c = v.shape[1] // 2
    hi = pltpu.bitcast(v[:, :c].astype(BF16).astype(F32), U32)
    lo = pltpu.bitcast(v[:, c:].astype(BF16).astype(F32), U32)
    return hi | (lo >> 16)


def _unpack_bf16_pairs(u):
    hi = pltpu.bitcast(u & jnp.uint32(0xFFFF0000), F32)
    lo = pltpu.bitcast(u << 16, F32)
    return hi, lo


def _store_slot_rows(ref, packed):
    n, c = packed.shape
    rps = c // LANES
    for k in range(rps):
        ref[pl.ds(k, n, stride=rps), :] = packed[:, k * LANES:(k + 1) * LANES]


def _load_slot_rows(ref, n, rps):
    return jnp.concatenate([ref[pl.ds(k, n, stride=rps), :] for k in range(rps)], axis=1)


def _mixer_body(x_ref, xp_ref, xn_ref, g0_ref, b0_ref, win_ref, bin_ref, cw_ref, pw_ref, ps_ref,
                wo_ref, g1_ref, b1_ref, wr_ref, br_ref,
                x1_ref, ri_ref, wcol_ref, cnt_ref, *, apply_ln0, alpha, n_groups, per_group, seq):
    i = pl.program_id(1)
    nt = pl.num_programs(1)
    tm = x_ref.shape[1]
    rows = tm + 2 * HALO
    d_conv = cw_ref.shape[1]
    dg = pw_ref.shape[1]

    xe = jnp.concatenate([xp_ref[0], x_ref[0], xn_ref[0]], axis=0)
    if apply_ln0:
        xe = _layernorm(xe, g0_ref[...], b0_ref[...])
    z = _dot(xe.astype(BF16), win_ref[...]) + bin_ref[...]

    r = lax.broadcasted_iota(I32, (rows, 1), 0)
    valid = jnp.logical_and(jnp.logical_or(r >= HALO, i > 0),
                            jnp.logical_or(r < tm + HALO, i < nt - 1))

    h = z[:, :d_conv]
    gate_b = z[HALO:HALO + tm, d_conv:2 * d_conv]
    gate_c = z[:, 2 * d_conv:3 * d_conv]
    v = jnp.where(valid, gate_c * h, 0.0)
    cw = cw_ref[...]
    conv = (cw[0:1] * pltpu.roll(v, 1, 0)[HALO:HALO + tm]
            + cw[1:2] * v[HALO:HALO + tm]
            + cw[2:3] * pltpu.roll(v, rows - 1, 0)[HALO:HALO + tm])
    parts = [gate_b * conv]

    pos = i * tm + r[HALO:HALO + tm] - HALO
    ps = ps_ref[...]
    for g, w in enumerate(POOL_WINDOWS):
        left = w // 2
        right = w - 1 - left
        c0 = 3 * d_conv + g * dg
        ug = jnp.where(valid, z[:, c0:c0 + dg], 0.0)
        acc = ug
        k = 1
        while k < w:
            acc = acc + pltpu.roll(acc, k, 0)
            k *= 2
        if right:
            acc = pltpu.roll(acc, rows - right, 0)
        cnt = (jnp.minimum(pos + right + 1, seq) - jnp.maximum(pos - left, 0)).astype(F32)
        pooled = acc[HALO:HALO + tm] / cnt - ug[HALO:HALO + tm]
        parts.append(_dot(pooled.astype(BF16), pw_ref[g]) * ps[:, g * dg:(g + 1) * dg])
    cat = jnp.concatenate(parts, axis=1)
    mix = _dot(cat.astype(BF16), wo_ref[...])
    x1 = _layernorm(alpha * xe[HALO:HALO + tm] + mix, g1_ref[...], b1_ref[...])
    x1_ref[0] = x1

    xh = x1.astype(BF16)
    xl = (x1 - xh.astype(F32)).astype(BF16)
    l_hi = _dot(xh, wr_ref[...])
    l_lo = _dot(xl, wr_ref[:, :LANES])
    logits = l_hi[:, :LANES] + l_hi[:, LANES:] + l_lo + br_ref[...]
    lt = logits.T

    row8 = lax.broadcasted_iota(I32, (8, tm), 0)
    neg = jnp.float32(-jnp.inf)
    lg = jnp.where(row8 < n_groups, lt[0:8], neg)
    mg = jnp.max(lg, axis=0, keepdims=True)
    g_w = 1.0 / jnp.sum(jnp.exp(lg - mg), axis=0, keepdims=True)
    g_sel = jnp.min(jnp.where(lg == mg, row8, 8), axis=0, keepdims=True)
    le = lt[8:16]
    for g in range(1, n_groups):
        le = jnp.where(g_sel == g, lt[8 + 8 * g:16 + 8 * g], le)
    le = jnp.where(row8 < per_group, le, neg)
    v1 = jnp.max(le, axis=0, keepdims=True)
    i1 = jnp.min(jnp.where(le == v1, row8, 8), axis=0, keepdims=True)
    le2 = jnp.where(row8 == i1, neg, le)
    v2 = jnp.max(le2, axis=0, keepdims=True)
    i2 = jnp.min(jnp.where(le2 == v2, row8, 8), axis=0, keepdims=True)
    e21 = jnp.exp(v2 - v1)
    w_first = 1.0 / (1.0 + e21)
    gw0 = w_first * g_w
    gw1 = (e21 * w_first) * g_w
    eid0 = g_sel * per_group + i1
    eid1 = g_sel * per_group + i2

    rowl = lax.broadcasted_iota(I32, (LANES, tm), 0)
    oh0 = rowl == eid0
    oh1 = rowl == eid1
    a_idx = lax.broadcasted_iota(I32, (tm, tm), 0)
    b_idx = lax.broadcasted_iota(I32, (tm, tm), 1)
    before = (a_idx < b_idx).astype(BF16)
    r0 = _dot(oh0.astype(BF16), before)
    r1 = _dot(oh1.astype(BF16), before)
    cnt0 = jnp.sum(oh0.astype(F32), axis=1, keepdims=True)
    lower = jnp.sum((rowl > eid0).astype(F32) + (rowl > eid1).astype(F32),
                    axis=1, keepdims=True)
    lpos0 = jnp.sum(jnp.where(oh0, lower + r0, 0.0), axis=0, keepdims=True)
    lpos1 = jnp.sum(jnp.where(oh1, lower + cnt0 + r1, 0.0), axis=0, keepdims=True)

    ri_ref[0] = jnp.zeros((8, tm), I32)
    ri_ref[0, 0:1, :] = lpos0.astype(I32)
    ri_ref[0, 1:2, :] = lpos1.astype(I32)

    wrows = jnp.where(rowl == 0, gw0, jnp.where(rowl == 1, gw1,
                      jnp.where(rowl == 2, lpos0, jnp.where(rowl == 3, lpos1, 0.0))))
    wcol_ref[0] = wrows.T

    both = jnp.logical_or(oh0, oh1).astype(BF16)
    ones = jnp.ones((8, tm), BF16)
    cnt_ref[0, 0] = lax.dot_general(ones, both, (((1,), (1,)), ((), ())),
                                    preferred_element_type=F32)


def _mixer(x, g0, b0, w_in, b_in, conv_w, pool_w, pool_scale, w_o, g1, b1, w_r, b_r, *,
           apply_ln0, alpha, n_groups, per_group, tm):
    bsz, seq, d = x.shape
    nt = seq // tm
    hb = tm // HALO
    nhb = seq // HALO
    d_in = w_in.shape[1]
    d_conv = conv_w.shape[1]
    full = lambda *shape: pl.BlockSpec(shape, lambda b, i: (0,) * len(shape))
    body = functools.partial(_mixer_body, apply_ln0=apply_ln0, alpha=alpha, n_groups=n_groups,
                             per_group=per_group, seq=seq)
    return pl.pallas_call(
        body,
        grid=(bsz, nt),
        in_specs=[
            pl.BlockSpec((1, tm, d), lambda b, i: (b, i, 0)),
            pl.BlockSpec((1, HALO, d), lambda b, i: (b, jnp.maximum(i * hb - 1, 0), 0)),
            pl.BlockSpec((1, HALO, d), lambda b, i: (b, jnp.minimum((i + 1) * hb, nhb - 1), 0)),
            full(1, d), full(1, d),
            full(d, d_in), full(1, d_in),
            full(CONV_WIDTH, d_conv),
            full(*pool_w.shape), full(1, pool_scale.shape[1]),
            full(d, d), full(1, d), full(1, d),
            full(d, 2 * LANES), full(1, LANES),
        ],
        out_specs=[
            pl.BlockSpec((1, tm, d), lambda b, i: (b, i, 0)),
            pl.BlockSpec((1, 8, tm), lambda b, i: (b, 0, i)),
            pl.BlockSpec((1, tm, LANES), lambda b, i: (b, i, 0)),
            pl.BlockSpec((1, 1, 8, LANES), lambda b, i: (b, i, 0, 0)),
        ],
        out_shape=[
            jax.ShapeDtypeStruct((bsz, seq, d), F32),
            jax.ShapeDtypeStruct((bsz, 8, seq), I32),
            jax.ShapeDtypeStruct((bsz, seq, LANES), F32),
            jax.ShapeDtypeStruct((bsz, nt, 8, LANES), F32),
        ],
        compiler_params=pltpu.CompilerParams(
            dimension_semantics=("arbitrary", "arbitrary"), vmem_limit_bytes=VMEM_LIMIT),
        name="mixer",
    )(x, x, x, g0, b0, w_in, b_in, conv_w, pool_w, pool_scale, w_o, g1, b1, w_r, b_r)


def _tables_body(cnt_ref, gstart_ref, lstart_ref, len_ref):
    c = cnt_ref[...]
    n_tiles = c.shape[0]
    row = lax.broadcasted_iota(I32, c.shape, 0)
    lane = lax.broadcasted_iota(I32, c.shape, 1)

    def prefix(v, idx, size, axis):
        k = 1
        while k < size:
            v = v + jnp.where(idx >= k, pltpu.roll(v, k, axis), 0.0)
            k *= 2
        return v

    over_tiles = prefix(c, row, n_tiles, 0)
    tot = jnp.broadcast_to(over_tiles[n_tiles - 1:n_tiles], c.shape)
    expert_start = prefix(tot, lane, LANES, 1) - tot
    gstart_ref[...] = (expert_start + over_tiles - c).astype(I32)
    lstart_ref[...] = (prefix(c, lane, LANES, 1) - c).astype(I32)
    len_ref[...] = c.astype(I32)


def _run_tables(cnt, n_experts):
    out = jax.ShapeDtypeStruct(cnt.shape, I32)
    tabs = pl.pallas_call(_tables_body, out_shape=[out, out, out], name="run_tables")(cnt)
    return tuple(t[:, :n_experts].reshape(-1) for t in tabs)


RUN_CHUNK_LOG2 = 6


def _slots(ref, first, n, rps):
    start = first * rps
    if not isinstance(start, int):
        start = pl.multiple_of(start, rps)
    return ref.at[pl.ds(start, n * rps)]


def _start_run_copies(tabs, tile, n_experts, make_copy):
    gstart_ref, lstart_ref, len_ref = tabs
    big = 1 << RUN_CHUNK_LOG2

    def per_expert(e, carry):
        idx = tile * n_experts + e
        n, ls, gs = len_ref[idx], lstart_ref[idx], gstart_ref[idx]
        n_big = n >> RUN_CHUNK_LOG2

        def big_piece(c, carry2):
            make_copy(ls + c * big, gs + c * big, big).start()
            return carry2
        lax.fori_loop(0, n_big, big_piece, 0)
        off = n_big * big
        for bit in reversed(range(RUN_CHUNK_LOG2)):
            size = 1 << bit

            @pl.when((n & size) != 0)
            def _(off=off, size=size, bit=bit):
                make_copy(ls + off, gs + off, size).start(priority=bit % 2)
            off = off + (n & size)
        return carry
    lax.fori_loop(0, n_experts, per_expert, 0)


def _ple_body(gstart_ref, lstart_ref, len_ref, x1_ref, p_ref, ri_ref, wg_ref, bg_ref, wp_ref,
              ple_ref, xs_ref, buf_ref, sem, *, n_experts):
    tile = pl.program_id(0) * pl.num_programs(1) + pl.program_id(1)
    tabs = (gstart_ref, lstart_ref, len_ref)
    x1 = x1_ref[0]
    tm, d = x1.shape
    rps = d // 2 // LANES

    lpos = ri_ref[0]
    s_idx = lax.broadcasted_iota(I32, (TOP_K * tm, tm), 0)
    perm = jnp.logical_or(s_idx == lpos[0:1], s_idx == lpos[1:2]).astype(BF16)
    xb = x1.astype(BF16)
    sorted_rows = _pack_bf16_pairs(_dot(perm, xb))

    n_tiles = pl.num_programs(0) * pl.num_programs(1)
    par = lax.rem(tile, 2)

    def copier(which):
        def make_copy(local, glob, n):
            return pltpu.make_async_copy(_slots(buf_ref.at[which], local, n, rps),
                                         _slots(xs_ref, glob, n, rps), sem.at[which])
        return make_copy

    def wait_all(which):
        copier(which)(0, 0, TOP_K * tm).wait()

    @pl.when(tile >= 2)
    def _():
        wait_all(par)
    _store_slot_rows(buf_ref.at[par], sorted_rows)
    _start_run_copies(tabs, tile, n_experts, copier(par))
    gate = jax.nn.sigmoid(_dot(xb, wg_ref[...]) + bg_ref[...])
    ple_ref[0] = gate * _dot(p_ref[0].astype(BF16), wp_ref[...])

    @pl.when(tile == n_tiles - 1)
    def _():
        wait_all(par)

        @pl.when(tile >= 1)
        def _():
            wait_all(1 - par)


def _ple_dispatch(tabs, x1, p, ri, w_gate, b_gate, w_proj, *, layer, tm, n_slots, n_experts):
    bsz, seq, d = x1.shape
    nt = seq // tm
    d_ple = p.shape[-1]
    rps = d // 2 // LANES
    full = lambda *shape: pl.BlockSpec(shape, lambda b, i, *_: (0,) * len(shape))
    grid_spec = pltpu.PrefetchScalarGridSpec(
        num_scalar_prefetch=3,
        grid=(bsz, nt),
        in_specs=[
            pl.BlockSpec((1, tm, d), lambda b, i, *_: (b, i, 0)),
            pl.BlockSpec((None, 1, tm, d_ple), lambda b, i, *_: (layer, b, i, 0)),
            pl.BlockSpec((1, 8, tm), lambda b, i, *_: (b, 0, i)),
            full(d, d), full(1, d), full(d_ple, d),
        ],
        out_specs=[
            pl.BlockSpec((1, tm, d), lambda b, i, *_: (b, i, 0)),
            pl.BlockSpec(memory_space=pl.ANY),
        ],
        scratch_shapes=[pltpu.VMEM((2, TOP_K * tm * rps, LANES), U32), pltpu.SemaphoreType.DMA((2,))],
    )
    return pl.pallas_call(
        functools.partial(_ple_body, n_experts=n_experts),
        grid_spec=grid_spec,
        out_shape=[
            jax.ShapeDtypeStruct((bsz, seq, d), F32),
            jax.ShapeDtypeStruct((n_slots * rps, LANES), U32),
        ],
        compiler_params=pltpu.CompilerParams(
            dimension_semantics=("arbitrary", "arbitrary"), vmem_limit_bytes=VMEM_LIMIT),
        name="ple_dispatch",
    )(*tabs, x1, p, ri, w_gate, b_gate, w_proj)


def _experts_body(blk_ref, exp_ref, lo_ref, new_ref, xs_ref, w1_ref, w3_ref, w2_ref, ys_ref,
                  w1b_ref, w3b_ref, w2b_ref):
    it = pl.program_id(0)
    lo = lo_ref[it]
    rps = xs_ref.shape[0] // SLOT_BLOCK

    @pl.when(new_ref[it] == 1)
    def _():
        w1b_ref[...] = w1_ref[...].astype(BF16)
        w3b_ref[...] = w3_ref[...].astype(BF16)
        w2b_ref[...] = w2_ref[...].astype(BF16)

    @pl.when(lo < SLOT_BLOCK)
    def _():
        hi_half, lo_half = _unpack_bf16_pairs(_load_slot_rows(xs_ref, SLOT_BLOCK, rps))
        xb = jnp.concatenate([hi_half.astype(BF16), lo_half.astype(BF16)], axis=1)
        h = jax.nn.silu(_dot(xb, w1b_ref[...])) * _dot(xb, w3b_ref[...])
        y = _pack_bf16_pairs(_dot(h.astype(BF16), w2b_ref[...]))

        @pl.when(lo == 0)
        def _():
            _store_slot_rows(ys_ref, y)

        @pl.when(lo > 0)
        def _():
            rows = lax.broadcasted_iota(I32, y.shape, 0)
            _store_slot_rows(ys_ref, jnp.where(rows >= lo, y, _load_slot_rows(ys_ref, SLOT_BLOCK, rps)))


def _experts(items, xs, w1, w3, w2, *, layer):
    item_block, item_expert, item_lo, item_new = items
    rows = xs.shape[0]
    _, _, d, d_e = w1.shape
    rps = d // 2 // LANES
    n_items = item_block.shape[0]
    grid_spec = pltpu.PrefetchScalarGridSpec(
        num_scalar_prefetch=4,
        grid=(n_items,),
        in_specs=[
            pl.BlockSpec((SLOT_BLOCK * rps, LANES), lambda i, blk, exp, lo, new: (blk[i], 0)),
            pl.BlockSpec((None, None, d, d_e), lambda i, blk, exp, lo, new: (layer, exp[i], 0, 0)),
            pl.BlockSpec((None, None, d, d_e), lambda i, blk, exp, lo, new: (layer, exp[i], 0, 0)),
            pl.BlockSpec((None, None, d_e, d), lambda i, blk, exp, lo, new: (layer, exp[i], 0, 0)),
        ],
        out_specs=pl.BlockSpec((SLOT_BLOCK * rps, LANES), lambda i, blk, exp, lo, new: (blk[i], 0)),
        scratch_shapes=[pltpu.VMEM((d, d_e), BF16), pltpu.VMEM((d, d_e), BF16),
                        pltpu.VMEM((d_e, d), BF16)],
    )
    return pl.pallas_call(
        _experts_body,
        grid_spec=grid_spec,
        out_shape=jax.ShapeDtypeStruct((rows, LANES), U32),
        compiler_params=pltpu.CompilerParams(
            dimension_semantics=("arbitrary",), vmem_limit_bytes=VMEM_LIMIT),
        name="experts",
    )(item_block, item_expert, item_lo, item_new, xs, w1, w3, w2)


def _expert_items(cnt, n_experts, n_slots):
    n_blocks = n_slots // SLOT_BLOCK
    n_items = n_blocks + n_experts - 1
    tot = jnp.sum(cnt, axis=0)[:n_experts].astype(I32)
    end = jnp.cumsum(tot)
    start = end - tot
    first_blk = start // SLOT_BLOCK
    n_it = jnp.where(tot > 0, (end - 1) // SLOT_BLOCK - first_blk + 1, 0)
    it_end = jnp.cumsum(n_it)
    it_start = it_end - n_it
    idx = jnp.arange(n_items, dtype=I32)
    live = idx < it_end[-1]
    expert = jnp.minimum(jnp.sum(it_end[None, :] <= idx[:, None], axis=1), n_experts - 1).astype(I32)
    onehot = expert[:, None] == jnp.arange(n_experts, dtype=I32)[None, :]
    pick = lambda v: jnp.sum(jnp.where(onehot, v[None, :], 0), axis=1)
    nth = idx - pick(it_start)
    block = pick(first_blk) + nth
    item_block = jnp.where(live, block, n_blocks - 1).astype(I32)
    item_lo = jnp.where(live, jnp.maximum(pick(start) - block * SLOT_BLOCK, 0), SLOT_BLOCK).astype(I32)
    prev_expert = jnp.concatenate([jnp.full((1,), -1, I32), expert[:-1]])
    item_new = (expert != prev_expert).astype(I32)
    return item_block, expert, item_lo, item_new


def _combine_body(gstart_ref, lstart_ref, len_ref, x1_ref, ple_ref, wcol_ref, ys_ref, g2_ref, b2_ref,
                  out_ref, buf_ref, sem, *, alpha, n_experts):
    tile = pl.program_id(0) * pl.num_programs(1) + pl.program_id(1)
    tabs = (gstart_ref, lstart_ref, len_ref)
    tm, d = x1_ref.shape[1:]
    rps = d // 2 // LANES

    n_tiles = pl.num_programs(0) * pl.num_programs(1)
    par = lax.rem(tile, 2)

    def copier(which):
        def make_copy(local, glob, n):
            return pltpu.make_async_copy(_slots(ys_ref, glob, n, rps),
                                         _slots(buf_ref.at[which], local, n, rps), sem.at[which])
        return make_copy

    @pl.when(tile == 0)
    def _():
        _start_run_copies(tabs, tile, n_experts, copier(par))

    @pl.when(tile + 1 < n_tiles)
    def _():
        _start_run_copies(tabs, tile + 1, n_experts, copier(1 - par))
    copier(par)(0, 0, TOP_K * tm).wait()

    hi_half, lo_half = _unpack_bf16_pairs(_load_slot_rows(buf_ref.at[par], TOP_K * tm, rps))
    ys = jnp.concatenate([hi_half.astype(BF16), lo_half.astype(BF16)], axis=1)
    wcol = wcol_ref[0]
    s_idx = lax.broadcasted_iota(I32, (tm, TOP_K * tm), 1)
    ffn = None
    for kk in range(TOP_K):
        pick = (s_idx == wcol[:, TOP_K + kk:TOP_K + kk + 1].astype(I32)).astype(BF16)
        term = _dot(pick, ys) * wcol[:, kk:kk + 1]
        ffn = term if ffn is None else ffn + term
    out_ref[0] = _layernorm(alpha * x1_ref[0] + ffn + ple_ref[0], g2_ref[...], b2_ref[...])


def _combine(tabs, x1, ple, wcol, ys, g2, b2, *, alpha, tm, n_experts):
    bsz, seq, d = x1.shape
    nt = seq // tm
    rps = d // 2 // LANES
    full = lambda *shape: pl.BlockSpec(shape, lambda b, i, *_: (0,) * len(shape))
    grid_spec = pltpu.PrefetchScalarGridSpec(
        num_scalar_prefetch=3,
        grid=(bsz, nt),
        in_specs=[
            pl.BlockSpec((1, tm, d), lambda b, i, *_: (b, i, 0)),
            pl.BlockSpec((1, tm, d), lambda b, i, *_: (b, i, 0)),
            pl.BlockSpec((1, tm, LANES), lambda b, i, *_: (b, i, 0)),
            pl.BlockSpec(memory_space=pl.ANY),
            full(1, d), full(1, d),
        ],
        out_specs=pl.BlockSpec((1, tm, d), lambda b, i, *_: (b, i, 0)),
        scratch_shapes=[pltpu.VMEM((2, TOP_K * tm * rps, LANES), U32), pltpu.SemaphoreType.DMA((2,))],
    )
    return pl.pallas_call(
        functools.partial(_combine_body, alpha=alpha, n_experts=n_experts),
        grid_spec=grid_spec,
        out_shape=jax.ShapeDtypeStruct((bsz, seq, d), F32),
        compiler_params=pltpu.CompilerParams(
            dimension_semantics=("arbitrary", "arbitrary"), vmem_limit_bytes=VMEM_LIMIT),
        name="combine",
    )(*tabs, x1, ple, wcol, ys, g2, b2)


def _router_weights(w_rg, b_rg, w_re, b_re):
    d, n_groups = w_rg.shape
    per_group = w_re.shape[2]
    assert n_groups <= 8 and per_group <= 8 and 8 + 8 * n_groups <= LANES
    w_e = jnp.pad(jnp.transpose(w_re, (1, 0, 2)), ((0, 0), (0, 0), (0, 8 - per_group)))
    b_e = jnp.pad(b_re, ((0, 0), (0, 8 - per_group)))
    tail = LANES - 8 - 8 * n_groups
    w = jnp.concatenate([jnp.pad(w_rg, ((0, 0), (0, 8 - n_groups))), w_e.reshape(d, 8 * n_groups),
                         jnp.zeros((d, tail), F32)], axis=1)
    b = jnp.concatenate([jnp.pad(b_rg, (0, 8 - n_groups)), b_e.reshape(-1), jnp.zeros((tail,), F32)])
    w_hi = w.astype(BF16)
    w_lo = (w - w_hi.astype(F32)).astype(BF16)
    return jnp.concatenate([w_hi, w_lo], axis=1), b[None, :]


def kernel(x, p, ln0_g, ln0_b, w_in, b_in, conv_w, pool_w, pool_scale, w_o, ln1_g, ln1_b,
           w_router_group, b_router_group, w_router_expert, b_router_expert, w1, w3, w2,
           w_ple_gate, b_ple_gate, w_ple_proj, ln2_g, ln2_b):
    bsz, seq, d = x.shape
    depth = w_in.shape[0]
    n_groups, per_group = w_router_expert.shape[1], w_router_expert.shape[3]
    n_experts = n_groups * per_group
    alpha = (2 * depth) ** 0.25
    tm = min(512, seq)
    n_slots = bsz * seq * TOP_K
    assert seq % tm == 0 and tm % LANES == 0 and n_slots % SLOT_BLOCK == 0 and n_experts <= LANES
    assert d % (2 * LANES) == 0
    row = lambda a: a[None, :]

    for i in range(depth):
        w_r, b_r = _router_weights(w_router_group[i], b_router_group[i],
                                   w_router_expert[i], b_router_expert[i])
        x1, ri, wcol, cnt = _mixer(
            x, row(ln0_g), row(ln0_b), w_in[i].astype(BF16), row(b_in[i]), conv_w[i],
            pool_w[i].astype(BF16), row(pool_scale[i]), w_o[i].astype(BF16),
            row(ln1_g[i]), row(ln1_b[i]), w_r, b_r,
            apply_ln0=(i == 0), alpha=alpha, n_groups=n_groups, per_group=per_group, tm=tm)
        cnt = cnt[:, :, 0, :].reshape(-1, LANES)
        tabs = _run_tables(cnt, n_experts)
        items = _expert_items(cnt, n_experts, n_slots)
        ple, xs = _ple_dispatch(tabs, x1, p, ri, w_ple_gate[i].astype(BF16), row(b_ple_gate[i]),
                                w_ple_proj[i].astype(BF16), layer=i, tm=tm, n_slots=n_slots,
                                n_experts=n_experts)
        ys = _experts(items, xs, w1, w3, w2, layer=i)
        x = _combine(tabs, x1, ple, wcol, ys, row(ln2_g[i]), row(ln2_b[i]), alpha=alpha, tm=tm,
                     n_experts=n_experts)
    return x
```

```python
import functools

import jax
import jax.numpy as jnp
from jax import lax
from jax.experimental import pallas as pl
from jax.experimental.pallas import tpu as pltpu

POOL_WINDOWS = (2, 4, 8, 16)
CONV_WIDTH = 3
TOP_K = 2
LN_EPS = 1e-5
HALO = 8
LANES = 128
SLOT_BLOCK = 512
VMEM_LIMIT = 56 * 1024 * 1024

F32 = jnp.float32
BF16 = jnp.bfloat16
I32 = jnp.int32
U32 = jnp.uint32


def _layernorm(v, g, b):
    mu = jnp.mean(v, axis=-1, keepdims=True)
    c = v - mu
    var = jnp.mean(c * c, axis=-1, keepdims=True)
    return c * lax.rsqrt(var + LN_EPS) * g + b


def _dot(a, b):
    return jnp.dot(a, b, preferred_element_type=F32)


def _pack_bf16_pairs(v):
    c = v.shape[1] // 2
    hi = pltpu.bitcast(v[:, :c].astype(BF16).astype(F32), U32)
    lo = pltpu.bitcast(v[:, c:].astype(BF16).astype(F32), U32)
    return hi | (lo >> 16)


def _pack_exact_bf16_pairs(v):
    c = v.shape[1] // 2
    return pltpu.bitcast(v[:, :c], U32) | (pltpu.bitcast(v[:, c:], U32) >> 16)


def _unpack_bf16_pairs(u):
    hi = pltpu.bitcast(u & jnp.uint32(0xFFFF0000), F32)
    lo = pltpu.bitcast(u << 16, F32)
    return hi, lo


def _store_slot_rows(ref, packed):
    n, c = packed.shape
    rps = c // LANES
    for k in range(rps):
        ref[pl.ds(k, n, stride=rps), :] = packed[:, k * LANES:(k + 1) * LANES]


def _load_slot_rows(ref, n, rps):
    return jnp.concatenate([ref[pl.ds(k, n, stride=rps), :] for k in range(rps)], axis=1)


def _mixer_body(x_ref, xp_ref, xn_ref, g0_ref, b0_ref, win_ref, bin_ref, cw_ref, pw_ref, ps_ref,
                wo_ref, g1_ref, b1_ref, wr_ref, br_ref,
                x1_ref, ri_ref, wcol_ref, cnt_ref, *, apply_ln0, alpha, n_groups, per_group, seq):
    i = pl.program_id(1)
    nt = pl.num_programs(1)
    tm = x_ref.shape[1]
    rows = tm + 2 * HALO
    d_conv = cw_ref.shape[1]
    dg = pw_ref.shape[1]

    xe = jnp.concatenate([xp_ref[0], x_ref[0], xn_ref[0]], axis=0)
    if apply_ln0:
        xe = _layernorm(xe, g0_ref[...], b0_ref[...])
    z = _dot(xe.astype(BF16), win_ref[...]) + bin_ref[...]

    r = lax.broadcasted_iota(I32, (rows, 1), 0)
    valid = jnp.logical_and(jnp.logical_or(r >= HALO, i > 0),
                            jnp.logical_or(r < tm + HALO, i < nt - 1))

    h = z[:, :d_conv]
    gate_b = z[HALO:HALO + tm, d_conv:2 * d_conv]
    gate_c = z[:, 2 * d_conv:3 * d_conv]
    v = jnp.where(valid, gate_c * h, 0.0)
    cw = cw_ref[...]
    conv = (cw[0:1] * pltpu.roll(v, 1, 0)[HALO:HALO + tm]
            + cw[1:2] * v[HALO:HALO + tm]
            + cw[2:3] * pltpu.roll(v, rows - 1, 0)[HALO:HALO + tm])
    parts = [gate_b * conv]

    pos = i * tm + r[HALO:HALO + tm] - HALO
    ps = ps_ref[...]
    for g, w in enumerate(POOL_WINDOWS):
        left = w // 2
        right = w - 1 - left
        c0 = 3 * d_conv + g * dg
        ug = jnp.where(valid, z[:, c0:c0 + dg], 0.0)
        acc = ug
        k = 1
        while k < w:
            acc = acc + pltpu.roll(acc, k, 0)
            k *= 2
        if right:
            acc = pltpu.roll(acc, rows - right, 0)
        def edge(lo):
            p = pos[lo:lo + HALO]
            cnt = (jnp.minimum(p + right + 1, seq) - jnp.maximum(p - left, 0)).astype(F32)
            return acc[HALO + lo:2 * HALO + lo] / cnt - ug[HALO + lo:2 * HALO + lo]
        inner = acc[2 * HALO:tm] * (1.0 / w) - ug[2 * HALO:tm]
        pooled = jnp.concatenate([edge(0), inner, edge(tm - HALO)], axis=0)
        parts.append(_dot(pooled.astype(BF16), pw_ref[g]) * ps[:, g * dg:(g + 1) * dg])
    cat = jnp.concatenate(parts, axis=1)
    mix = _dot(cat.astype(BF16), wo_ref[...])
    x1 = _layernorm(alpha * xe[HALO:HALO + tm] + mix, g1_ref[...], b1_ref[...])
    x1_ref[0] = x1

    xh = x1.astype(BF16)
    xl = (x1 - xh.astype(F32)).astype(BF16)
    l_hi = _dot(xh, wr_ref[...])
    l_lo = _dot(xl, wr_ref[:, :LANES])
    logits = l_hi[:, :LANES] + l_hi[:, LANES:] + l_lo + br_ref[...]
    lt = logits.T

    row8 = lax.broadcasted_iota(I32, (8, tm), 0)
    neg = jnp.float32(-jnp.inf)
    lg = jnp.where(row8 < n_groups, lt[0:8], neg)
    mg = jnp.max(lg, axis=0, keepdims=True)
    g_w = 1.0 / jnp.sum(jnp.exp(lg - mg), axis=0, keepdims=True)
    g_sel = jnp.min(jnp.where(lg == mg, row8, 8), axis=0, keepdims=True)
    le = lt[8:16]
    for g in range(1, n_groups):
        le = jnp.where(g_sel == g, lt[8 + 8 * g:16 + 8 * g], le)
    le = jnp.where(row8 < per_group, le, neg)
    v1 = jnp.max(le, axis=0, keepdims=True)
    i1 = jnp.min(jnp.where(le == v1, row8, 8), axis=0, keepdims=True)
    le2 = jnp.where(row8 == i1, neg, le)
    v2 = jnp.max(le2, axis=0, keepdims=True)
    i2 = jnp.min(jnp.where(le2 == v2, row8, 8), axis=0, keepdims=True)
    e21 = jnp.exp(v2 - v1)
    w_first = 1.0 / (1.0 + e21)
    gw0 = w_first * g_w
    gw1 = (e21 * w_first) * g_w
    eid0 = g_sel * per_group + i1
    eid1 = g_sel * per_group + i2

    rowl = lax.broadcasted_iota(I32, (LANES, tm), 0)
    oh0 = rowl == eid0
    oh1 = rowl == eid1
    a_idx = lax.broadcasted_iota(I32, (tm, tm), 0)
    b_idx = lax.broadcasted_iota(I32, (tm, tm), 1)
    before = (a_idx < b_idx).astype(BF16)
    r0 = _dot(oh0.astype(BF16), before)
    r1 = _dot(oh1.astype(BF16), before)
    cnt0 = jnp.sum(oh0.astype(F32), axis=1, keepdims=True)
    lower = jnp.sum((rowl > eid0).astype(F32) + (rowl > eid1).astype(F32),
                    axis=1, keepdims=True)
    lpos0 = jnp.sum(jnp.where(oh0, lower + r0, 0.0), axis=0, keepdims=True)
    lpos1 = jnp.sum(jnp.where(oh1, lower + cnt0 + r1, 0.0), axis=0, keepdims=True)

    ri_ref[0] = jnp.zeros((8, tm), I32)
    ri_ref[0, 0:1, :] = lpos0.astype(I32)
    ri_ref[0, 1:2, :] = lpos1.astype(I32)

    wrows = jnp.where(rowl == 0, gw0, jnp.where(rowl == 1, gw1,
                      jnp.where(rowl == 2, lpos0, jnp.where(rowl == 3, lpos1, 0.0))))
    wcol_ref[0] = wrows.T

    both = jnp.logical_or(oh0, oh1).astype(BF16)
    ones = jnp.ones((8, tm), BF16)
    cnt_ref[0, 0] = lax.dot_general(ones, both, (((1,), (1,)), ((), ())),
                                    preferred_element_type=F32)


def _mixer(x, g0, b0, w_in, b_in, conv_w, pool_w, pool_scale, w_o, g1, b1, w_r, b_r, *,
           apply_ln0, alpha, n_groups, per_group, tm):
    bsz, seq, d = x.shape
    nt = seq // tm
    hb = tm // HALO
    nhb = seq // HALO
    d_in = w_in.shape[1]
    d_conv = conv_w.shape[1]
    full = lambda *shape: pl.BlockSpec(shape, lambda b, i: (0,) * len(shape))
    body = functools.partial(_mixer_body, apply_ln0=apply_ln0, alpha=alpha, n_groups=n_groups,
                             per_group=per_group, seq=seq)
    return pl.pallas_call(
        body,
        grid=(bsz, nt),
        in_specs=[
            pl.BlockSpec((1, tm, d), lambda b, i: (b, i, 0)),
            pl.BlockSpec((1, HALO, d), lambda b, i: (b, jnp.maximum(i * hb - 1, 0), 0)),
            pl.BlockSpec((1, HALO, d), lambda b, i: (b, jnp.minimum((i + 1) * hb, nhb - 1), 0)),
            full(1, d), full(1, d),
            full(d, d_in), full(1, d_in),
            full(CONV_WIDTH, d_conv),
            full(*pool_w.shape), full(1, pool_scale.shape[1]),
            full(d, d), full(1, d), full(1, d),
            full(d, 2 * LANES), full(1, LANES),
        ],
        out_specs=[
            pl.BlockSpec((1, tm, d), lambda b, i: (b, i, 0)),
            pl.BlockSpec((1, 8, tm), lambda b, i: (b, 0, i)),
            pl.BlockSpec((1, tm, LANES), lambda b, i: (b, i, 0)),
            pl.BlockSpec((1, 1, 8, LANES), lambda b, i: (b, i, 0, 0)),
        ],
        out_shape=[
            jax.ShapeDtypeStruct((bsz, seq, d), F32),
            jax.ShapeDtypeStruct((bsz, 8, seq), I32),
            jax.ShapeDtypeStruct((bsz, seq, LANES), F32),
            jax.ShapeDtypeStruct((bsz, nt, 8, LANES), F32),
        ],
        compiler_params=pltpu.CompilerParams(
            dimension_semantics=("arbitrary", "arbitrary"), vmem_limit_bytes=VMEM_LIMIT),
        name="mixer",
    )(x, x, x, g0, b0, w_in, b_in, conv_w, pool_w, pool_scale, w_o, g1, b1, w_r, b_r)


def _tables_body(cnt_ref, gstart_ref, lstart_ref, len_ref):
    c = cnt_ref[...]
    n_tiles = c.shape[0]
    row = lax.broadcasted_iota(I32, c.shape, 0)
    lane = lax.broadcasted_iota(I32, c.shape, 1)

    def prefix(v, idx, size, axis):
        k = 1
        while k < size:
            v = v + jnp.where(idx >= k, pltpu.roll(v, k, axis), 0.0)
            k *= 2
        return v

    over_tiles = prefix(c, row, n_tiles, 0)
    tot = jnp.broadcast_to(over_tiles[n_tiles - 1:n_tiles], c.shape)
    expert_start = prefix(tot, lane, LANES, 1) - tot
    gstart_ref[...] = (expert_start + over_tiles - c).astype(I32)
    lstart_ref[...] = (prefix(c, lane, LANES, 1) - c).astype(I32)
    len_ref[...] = c.astype(I32)


def _run_tables(cnt, n_experts):
    out = jax.ShapeDtypeStruct(cnt.shape, I32)
    tabs = pl.pallas_call(_tables_body, out_shape=[out, out, out], name="run_tables")(cnt)
    return tuple(t[:, :n_experts].reshape(-1) for t in tabs)


RUN_CHUNK_LOG2 = 6


def _slots(ref, first, n, rps):
    start = first * rps
    if not isinstance(start, int):
        start = pl.multiple_of(start, rps)
    return ref.at[pl.ds(start, n * rps)]


def _start_run_copies(tabs, tile, n_experts, make_copy):
    gstart_ref, lstart_ref, len_ref = tabs
    big = 1 << RUN_CHUNK_LOG2

    def per_expert(e, carry):
        idx = tile * n_experts + e
        n, ls, gs = len_ref[idx], lstart_ref[idx], gstart_ref[idx]
        n_big = n >> RUN_CHUNK_LOG2

        def big_piece(c, carry2):
            make_copy(ls + c * big, gs + c * big, big).start()
            return carry2
        lax.fori_loop(0, n_big, big_piece, 0)
        off = n_big * big
        for bit in reversed(range(RUN_CHUNK_LOG2)):
            size = 1 << bit

            @pl.when((n & size) != 0)
            def _(off=off, size=size, bit=bit):
                make_copy(ls + off, gs + off, size).start(priority=bit % 2)
            off = off + (n & size)
        return carry
    lax.fori_loop(0, n_experts, per_expert, 0)


def _ple_body(gstart_ref, lstart_ref, len_ref, x1_ref, p_ref, ri_ref, wg_ref, bg_ref, wp_ref,
              ple_ref, xs_ref, buf_ref, sem, *, n_experts):
    tile = pl.program_id(0) * pl.num_programs(1) + pl.program_id(1)
    tabs = (gstart_ref, lstart_ref, len_ref)
    x1 = x1_ref[0]
    tm, d = x1.shape
    rps = d // 2 // LANES

    lpos = ri_ref[0]
    s_idx = lax.broadcasted_iota(I32, (TOP_K * tm, tm), 0)
    perm = jnp.logical_or(s_idx == lpos[0:1], s_idx == lpos[1:2]).astype(BF16)
    xb = x1.astype(BF16)
    sorted_rows = _pack_exact_bf16_pairs(_dot(perm, xb))

    n_tiles = pl.num_programs(0) * pl.num_programs(1)
    par = lax.rem(tile, 2)

    def copier(which):
        def make_copy(local, glob, n):
            return pltpu.make_async_copy(_slots(buf_ref.at[which], local, n, rps),
                                         _slots(xs_ref, glob, n, rps), sem.at[which])
        return make_copy

    def wait_all(which):
        copier(which)(0, 0, TOP_K * tm).wait()

    @pl.when(tile >= 2)
    def _():
        wait_all(par)
    _store_slot_rows(buf_ref.at[par], sorted_rows)
    _start_run_copies(tabs, tile, n_experts, copier(par))
    gate = jax.nn.sigmoid(_dot(xb, wg_ref[...]) + bg_ref[...])
    ple_ref[0] = gate * _dot(p_ref[0].astype(BF16), wp_ref[...])

    @pl.when(tile == n_tiles - 1)
    def _():
        wait_all(par)

        @pl.when(tile >= 1)
        def _():
            wait_all(1 - par)


def _ple_dispatch(tabs, x1, p, ri, w_gate, b_gate, w_proj, *, layer, tm, n_slots, n_experts):
    bsz, seq, d = x1.shape
    nt = seq // tm
    d_ple = p.shape[-1]
    rps = d // 2 // LANES
    full = lambda *shape: pl.BlockSpec(shape, lambda b, i, *_: (0,) * len(shape))
    grid_spec = pltpu.PrefetchScalarGridSpec(
        num_scalar_prefetch=3,
        grid=(bsz, nt),
        in_specs=[
            pl.BlockSpec((1, tm, d), lambda b, i, *_: (b, i, 0)),
            pl.BlockSpec((None, 1, tm, d_ple), lambda b, i, *_: (layer, b, i, 0)),
            pl.BlockSpec((1, 8, tm), lambda b, i, *_: (b, 0, i)),
            full(d, d), full(1, d), full(d_ple, d),
        ],
        out_specs=[
            pl.BlockSpec((1, tm, d), lambda b, i, *_: (b, i, 0)),
            pl.BlockSpec(memory_space=pl.ANY),
        ],
        scratch_shapes=[pltpu.VMEM((2, TOP_K * tm * rps, LANES), U32), pltpu.SemaphoreType.DMA((2,))],
    )
    return pl.pallas_call(
        functools.partial(_ple_body, n_experts=n_experts),
        grid_spec=grid_spec,
        out_shape=[
            jax.ShapeDtypeStruct((bsz, seq, d), F32),
            jax.ShapeDtypeStruct((n_slots * rps, LANES), U32),
        ],
        compiler_params=pltpu.CompilerParams(
            dimension_semantics=("arbitrary", "arbitrary"), vmem_limit_bytes=VMEM_LIMIT),
        name="ple_dispatch",
    )(*tabs, x1, p, ri, w_gate, b_gate, w_proj)


def _experts_body(blk_ref, exp_ref, lo_ref, new_ref, xs_ref, w1_ref, w3_ref, w2_ref, ys_ref,
                  w1b_ref, w3b_ref, w2b_ref):
    it = pl.program_id(0)
    lo = lo_ref[it]
    rps = xs_ref.shape[0] // SLOT_BLOCK

    @pl.when(new_ref[it] == 1)
    def _():
        w1b_ref[...] = w1_ref[...].astype(BF16)
        w3b_ref[...] = w3_ref[...].astype(BF16)
        w2b_ref[...] = w2_ref[...].astype(BF16)

    @pl.when(lo < SLOT_BLOCK)
    def _():
        hi_half, lo_half = _unpack_bf16_pairs(_load_slot_rows(xs_ref, SLOT_BLOCK, rps))
        xb = jnp.concatenate([hi_half.astype(BF16), lo_half.astype(BF16)], axis=1)
        h = jax.nn.silu(_dot(xb, w1b_ref[...])) * _dot(xb, w3b_ref[...])
        y = _pack_bf16_pairs(_dot(h.astype(BF16), w2b_ref[...]))

        @pl.when(lo == 0)
        def _():
            _store_slot_rows(ys_ref, y)

        @pl.when(lo > 0)
        def _():
            rows = lax.broadcasted_iota(I32, y.shape, 0)
            _store_slot_rows(ys_ref, jnp.where(rows >= lo, y, _load_slot_rows(ys_ref, SLOT_BLOCK, rps)))


def _experts(items, xs, w1, w3, w2, *, layer):
    item_block, item_expert, item_lo, item_new = items
    rows = xs.shape[0]
    _, _, d, d_e = w1.shape
    rps = d // 2 // LANES
    n_items = item_block.shape[0]
    grid_spec = pltpu.PrefetchScalarGridSpec(
        num_scalar_prefetch=4,
        grid=(n_items,),
        in_specs=[
            pl.BlockSpec((SLOT_BLOCK * rps, LANES), lambda i, blk, exp, lo, new: (blk[i], 0)),
            pl.BlockSpec((None, None, d, d_e), lambda i, blk, exp, lo, new: (layer, exp[i], 0, 0)),
            pl.BlockSpec((None, None, d, d_e), lambda i, blk, exp, lo, new: (layer, exp[i], 0, 0)),
            pl.BlockSpec((None, None, d_e, d), lambda i, blk, exp, lo, new: (layer, exp[i], 0, 0)),
        ],
        out_specs=pl.BlockSpec((SLOT_BLOCK * rps, LANES), lambda i, blk, exp, lo, new: (blk[i], 0)),
        scratch_shapes=[pltpu.VMEM((d, d_e), BF16), pltpu.VMEM((d, d_e), BF16),
                        pltpu.VMEM((d_e, d), BF16)],
    )
    return pl.pallas_call(
        _experts_body,
        grid_spec=grid_spec,
        out_shape=jax.ShapeDtypeStruct((rows, LANES), U32),
        compiler_params=pltpu.CompilerParams(
            dimension_semantics=("arbitrary",), vmem_limit_bytes=VMEM_LIMIT),
        name="experts",
    )(item_block, item_expert, item_lo, item_new, xs, w1, w3, w2)


def _expert_items(cnt, n_experts, n_slots):
    n_blocks = n_slots // SLOT_BLOCK
    n_items = n_blocks + n_experts - 1
    tot = jnp.sum(cnt, axis=0)[:n_experts].astype(I32)
    end = jnp.cumsum(tot)
    start = end - tot
    first_blk = start // SLOT_BLOCK
    n_it = jnp.where(tot > 0, (end - 1) // SLOT_BLOCK - first_blk + 1, 0)
    it_end = jnp.cumsum(n_it)
    it_start = it_end - n_it
    idx = jnp.arange(n_items, dtype=I32)
    live = idx < it_end[-1]
    expert = jnp.minimum(jnp.sum(it_end[None, :] <= idx[:, None], axis=1), n_experts - 1).astype(I32)
    onehot = expert[:, None] == jnp.arange(n_experts, dtype=I32)[None, :]
    pick = lambda v: jnp.sum(jnp.where(onehot, v[None, :], 0), axis=1)
    nth = idx - pick(it_start)
    block = pick(first_blk) + nth
    item_block = jnp.where(live, block, n_blocks - 1).astype(I32)
    item_lo = jnp.where(live, jnp.maximum(pick(start) - block * SLOT_BLOCK, 0), SLOT_BLOCK).astype(I32)
    prev_expert = jnp.concatenate([jnp.full((1,), -1, I32), expert[:-1]])
    item_new = (expert != prev_expert).astype(I32)
    return item_block, expert, item_lo, item_new


def _combine_body(gstart_ref, lstart_ref, len_ref, x1_ref, ple_ref, wcol_ref, ys_ref, g2_ref, b2_ref,
                  out_ref, buf_ref, sem, *, alpha, n_experts):
    tile = pl.program_id(0) * pl.num_programs(1) + pl.program_id(1)
    tabs = (gstart_ref, lstart_ref, len_ref)
    tm, d = x1_ref.shape[1:]
    rps = d // 2 // LANES

    n_tiles = pl.num_programs(0) * pl.num_programs(1)
    par = lax.rem(tile, 2)

    def copier(which):
        def make_copy(local, glob, n):
            return pltpu.make_async_copy(_slots(ys_ref, glob, n, rps),
                                         _slots(buf_ref.at[which], local, n, rps), sem.at[which])
        return make_copy

    @pl.when(tile == 0)
    def _():
        _start_run_copies(tabs, tile, n_experts, copier(par))

    @pl.when(tile + 1 < n_tiles)
    def _():
        _start_run_copies(tabs, tile + 1, n_experts, copier(1 - par))
    copier(par)(0, 0, TOP_K * tm).wait()

    hi_half, lo_half = _unpack_bf16_pairs(_load_slot_rows(buf_ref.at[par], TOP_K * tm, rps))
    ys = jnp.concatenate([hi_half.astype(BF16), lo_half.astype(BF16)], axis=1)
    wcol = wcol_ref[0]
    s_idx = lax.broadcasted_iota(I32, (tm, TOP_K * tm), 1)
    ffn = None
    for kk in range(TOP_K):
        pick = (s_idx == wcol[:, TOP_K + kk:TOP_K + kk + 1].astype(I32)).astype(BF16)
        term = _dot(pick, ys) * wcol[:, kk:kk + 1]
        ffn = term if ffn is None else ffn + term
    out_ref[0] = _layernorm(alpha * x1_ref[0] + ffn + ple_ref[0], g2_ref[...], b2_ref[...])


def _combine(tabs, x1, ple, wcol, ys, g2, b2, *, alpha, tm, n_experts):
    bsz, seq, d = x1.shape
    nt = seq // tm
    rps = d // 2 // LANES
    full = lambda *shape: pl.BlockSpec(shape, lambda b, i, *_: (0,) * len(shape))
    grid_spec = pltpu.PrefetchScalarGridSpec(
        num_scalar_prefetch=3,
        grid=(bsz, nt),
        in_specs=[
            pl.BlockSpec((1, tm, d), lambda b, i, *_: (b, i, 0)),
            pl.BlockSpec((1, tm, d), lambda b, i, *_: (b, i, 0)),
            pl.BlockSpec((1, tm, LANES), lambda b, i, *_: (b, i, 0)),
            pl.BlockSpec(memory_space=pl.ANY),
            full(1, d), full(1, d),
        ],
        out_specs=pl.BlockSpec((1, tm, d), lambda b, i, *_: (b, i, 0)),
        scratch_shapes=[pltpu.VMEM((2, TOP_K * tm * rps, LANES), U32), pltpu.SemaphoreType.DMA((2,))],
    )
    return pl.pallas_call(
        functools.partial(_combine_body, alpha=alpha, n_experts=n_experts),
        grid_spec=grid_spec,
        out_shape=jax.ShapeDtypeStruct((bsz, seq, d), F32),
        compiler_params=pltpu.CompilerParams(
            dimension_semantics=("arbitrary", "arbitrary"), vmem_limit_bytes=VMEM_LIMIT),
        name="combine",
    )(*tabs, x1, ple, wcol, ys, g2, b2)


def _router_weights(w_rg, b_rg, w_re, b_re):
    d, n_groups = w_rg.shape
    per_group = w_re.shape[2]
    assert n_groups <= 8 and per_group <= 8 and 8 + 8 * n_groups <= LANES
    w_e = jnp.pad(jnp.transpose(w_re, (1, 0, 2)), ((0, 0), (0, 0), (0, 8 - per_group)))
    b_e = jnp.pad(b_re, ((0, 0), (0, 8 - per_group)))
    tail = LANES - 8 - 8 * n_groups
    w = jnp.concatenate([jnp.pad(w_rg, ((0, 0), (0, 8 - n_groups))), w_e.reshape(d, 8 * n_groups),
                         jnp.zeros((d, tail), F32)], axis=1)
    b = jnp.concatenate([jnp.pad(b_rg, (0, 8 - n_groups)), b_e.reshape(-1), jnp.zeros((tail,), F32)])
    w_hi = w.astype(BF16)
    w_lo = (w - w_hi.astype(F32)).astype(BF16)
    return jnp.concatenate([w_hi, w_lo], axis=1), b[None, :]


def kernel(x, p, ln0_g, ln0_b, w_in, b_in, conv_w, pool_w, pool_scale, w_o, ln1_g, ln1_b,
           w_router_group, b_router_group, w_router_expert, b_router_expert, w1, w3, w2,
           w_ple_gate, b_ple_gate, w_ple_proj, ln2_g, ln2_b):
    bsz, seq, d = x.shape
    depth = w_in.shape[0]
    n_groups, per_group = w_router_expert.shape[1], w_router_expert.shape[3]
    n_experts = n_groups * per_group
    alpha = (2 * depth) ** 0.25
    tm = min(512, seq)
    n_slots = bsz * seq * TOP_K
    assert seq % tm == 0 and tm % LANES == 0 and n_slots % SLOT_BLOCK == 0 and n_experts <= LANES
    assert d % (2 * LANES) == 0
    row = lambda a: a[None, :]

    for i in range(depth):
        w_r, b_r = _router_weights(w_router_group[i], b_router_group[i],
                                   w_router_expert[i], b_router_expert[i])
        x1, ri, wcol, cnt = _mixer(
            x, row(ln0_g), row(ln0_b), w_in[i].astype(BF16), row(b_in[i]), conv_w[i],
            pool_w[i].astype(BF16), row(pool_scale[i]), w_o[i].astype(BF16),
            row(ln1_g[i]), row(ln1_b[i]), w_r, b_r,
            apply_ln0=(i == 0), alpha=alpha, n_groups=n_groups, per_group=per_group, tm=tm)
        cnt = cnt[:, :, 0, :].reshape(-1, LANES)
        tabs = _run_tables(cnt, n_experts)
        items = _expert_items(cnt, n_experts, n_slots)
        ple, xs = _ple_dispatch(tabs, x1, p, ri, w_ple_gate[i].astype(BF16), row(b_ple_gate[i]),
                                w_ple_proj[i].astype(BF16), layer=i, tm=tm, n_slots=n_slots,
                                n_experts=n_experts)
        ys = _experts(items, xs, w1, w3, w2, layer=i)
        x = _combine(tabs, x1, ple, wcol, ys, row(ln2_g[i]), row(ln2_b[i]), alpha=alpha, tm=tm,
                     n_experts=n_experts)
    return x
```

```python
import functools

import jax
import jax.numpy as jnp
from jax import lax
from jax.experimental import pallas as pl
from jax.experimental.pallas import tpu as pltpu

POOL_WINDOWS = (2, 4, 8, 16)
CONV_WIDTH = 3
TOP_K = 2
LN_EPS = 1e-5
HALO = 8
LANES = 128
MIX_CHAINS = 2
SLOT_BLOCK = 512
VMEM_LIMIT = 56 * 1024 * 1024

F32 = jnp.float32
BF16 = jnp.bfloat16
I32 = jnp.int32
U32 = jnp.uint32


def _layernorm(v, g, b):
    mu = jnp.mean(v, axis=-1, keepdims=True)
    c = v - mu
    var = jnp.mean(c * c, axis=-1, keepdims=True)
    return c * lax.rsqrt(var + LN_EPS) * g + b


def _dot(a, b):
    return jnp.dot(a, b, preferred_element_type=F32)


def _pack_bf16_pairs(v):
    c = v.shape[1] // 2
    hi = pltpu.bitcast(v[:, :c].astype(BF16).astype(F32), U32)
    lo = pltpu.bitcast(v[:, c:].astype(BF16).astype(F32), U32)
    return hi | (lo >> 16)


def _pack_exact_bf16_pairs(v):
    c = v.shape[1] // 2
    return pltpu.bitcast(v[:, :c], U32) | (pltpu.bitcast(v[:, c:], U32) >> 16)


def _unpack_bf16_pairs(u):
    hi = pltpu.bitcast(u & jnp.uint32(0xFFFF0000), F32)
    lo = pltpu.bitcast(u << 16, F32)
    return hi, lo


def _store_slot_rows(ref, packed):
    n, c = packed.shape
    rps = c // LANES
    for k in range(rps):
        ref[pl.ds(k, n, stride=rps), :] = packed[:, k * LANES:(k + 1) * LANES]


def _load_slot_rows(ref, n, rps):
    return jnp.concatenate([ref[pl.ds(k, n, stride=rps), :] for k in range(rps)], axis=1)


def _mixer_body(x_ref, xp_ref, xn_ref, g0_ref, b0_ref, win_ref, bin_ref, cw_ref, pw_ref, ps_ref,
                wo_ref, g1_ref, b1_ref, wr_ref, br_ref,
                x1_ref, ri_ref, wcol_ref, cnt_ref, *, apply_ln0, alpha, n_groups, per_group, seq):
    i = pl.program_id(1)
    nt = pl.num_programs(1)
    tm = x_ref.shape[1]
    n = tm // MIX_CHAINS
    rows = n + 2 * HALO
    d_conv = cw_ref.shape[1]
    dg = pw_ref.shape[1]

    xe = jnp.concatenate([xp_ref[0], x_ref[0], xn_ref[0]], axis=0)
    if apply_ln0:
        xe = _layernorm(xe, g0_ref[...], b0_ref[...])
    cw = cw_ref[...]
    ps = ps_ref[...]

    def in_proj(s):
        xs = xe[s * n:s * n + rows]
        return xs, _dot(xs.astype(BF16), win_ref[...]) + bin_ref[...]

    def token_mix(s, z):
        r = lax.broadcasted_iota(I32, (rows, 1), 0)
        lo_ok = jnp.logical_or(r >= HALO, i > 0) if s == 0 else True
        hi_ok = jnp.logical_or(r < n + HALO, i < nt - 1) if s == MIX_CHAINS - 1 else True
        if s == 0 or s == MIX_CHAINS - 1:
            valid = jnp.logical_and(lo_ok, hi_ok)
            mask = lambda a: jnp.where(valid, a, 0.0)
        else:
            mask = lambda a: a

        h = z[:, :d_conv]
        gate_b = z[HALO:HALO + n, d_conv:2 * d_conv]
        gate_c = z[:, 2 * d_conv:3 * d_conv]
        v = mask(gate_c * h)
        conv = (cw[0:1] * pltpu.roll(v, 1, 0)[HALO:HALO + n]
                + cw[1:2] * v[HALO:HALO + n]
                + cw[2:3] * pltpu.roll(v, rows - 1, 0)[HALO:HALO + n])
        parts = [gate_b * conv]

        pos = i * tm + s * n + r[HALO:HALO + n] - HALO
        for g, w in enumerate(POOL_WINDOWS):
            left = w // 2
            right = w - 1 - left
            c0 = 3 * d_conv + g * dg
            ug = mask(z[:, c0:c0 + dg])
            acc = ug
            k = 1
            while k < w:
                acc = acc + pltpu.roll(acc, k, 0)
                k *= 2
            if right:
                acc = pltpu.roll(acc, rows - right, 0)

            def edge(lo):
                p = pos[lo:lo + HALO]
                cnt = (jnp.minimum(p + right + 1, seq) - jnp.maximum(p - left, 0)).astype(F32)
                return acc[HALO + lo:2 * HALO + lo] / cnt - ug[HALO + lo:2 * HALO + lo]
            inner = acc[2 * HALO:n] * (1.0 / w) - ug[2 * HALO:n]
            pooled = jnp.concatenate([edge(0), inner, edge(n - HALO)], axis=0)
            parts.append(_dot(pooled.astype(BF16), pw_ref[g]) * ps[:, g * dg:(g + 1) * dg])
        return jnp.concatenate(parts, axis=1).astype(BF16)

    def out_proj(xs, cat):
        mix = _dot(cat, wo_ref[...])
        return _layernorm(alpha * xs[HALO:HALO + n] + mix, g1_ref[...], b1_ref[...])

    def route(x1s):
        xh = x1s.astype(BF16)
        xl = (x1s - xh.astype(F32)).astype(BF16)
        l_hi = _dot(xh, wr_ref[...])
        l_lo = _dot(xl, wr_ref[:, :LANES])
        return l_hi[:, :LANES] + l_hi[:, LANES:] + l_lo + br_ref[...]

    chains = [in_proj(s) for s in range(MIX_CHAINS)]
    x1s = [out_proj(xs, token_mix(s, z)) for s, (xs, z) in enumerate(chains)]
    x1 = jnp.concatenate(x1s, axis=0)
    x1_ref[0] = x1
    logits = jnp.concatenate([route(v) for v in x1s], axis=0)
    lt = logits.T

    row8 = lax.broadcasted_iota(I32, (8, tm), 0)
    neg = jnp.float32(-jnp.inf)
    lg = jnp.where(row8 < n_groups, lt[0:8], neg)
    mg = jnp.max(lg, axis=0, keepdims=True)
    g_w = 1.0 / jnp.sum(jnp.exp(lg - mg), axis=0, keepdims=True)
    g_sel = jnp.min(jnp.where(lg == mg, row8, 8), axis=0, keepdims=True)
    le = lt[8:16]
    for g in range(1, n_groups):
        le = jnp.where(g_sel == g, lt[8 + 8 * g:16 + 8 * g], le)
    le = jnp.where(row8 < per_group, le, neg)
    v1 = jnp.max(le, axis=0, keepdims=True)
    i1 = jnp.min(jnp.where(le == v1, row8, 8), axis=0, keepdims=True)
    le2 = jnp.where(row8 == i1, neg, le)
    v2 = jnp.max(le2, axis=0, keepdims=True)
    i2 = jnp.min(jnp.where(le2 == v2, row8, 8), axis=0, keepdims=True)
    e21 = jnp.exp(v2 - v1)
    w_first = 1.0 / (1.0 + e21)
    gw0 = w_first * g_w
    gw1 = (e21 * w_first) * g_w
    eid0 = g_sel * per_group + i1
    eid1 = g_sel * per_group + i2

    rowl = lax.broadcasted_iota(I32, (LANES, tm), 0)
    oh0 = rowl == eid0
    oh1 = rowl == eid1
    a_idx = lax.broadcasted_iota(I32, (tm, tm), 0)
    b_idx = lax.broadcasted_iota(I32, (tm, tm), 1)
    before = (a_idx < b_idx).astype(BF16)
    r0 = _dot(oh0.astype(BF16), before)
    r1 = _dot(oh1.astype(BF16), before)
    cnt0 = jnp.sum(oh0.astype(F32), axis=1, keepdims=True)
    lower = jnp.sum((rowl > eid0).astype(F32) + (rowl > eid1).astype(F32),
                    axis=1, keepdims=True)
    lpos0 = jnp.sum(jnp.where(oh0, lower + r0, 0.0), axis=0, keepdims=True)
    lpos1 = jnp.sum(jnp.where(oh1, lower + cnt0 + r1, 0.0), axis=0, keepdims=True)

    ri_ref[0] = jnp.zeros((8, tm), I32)
    ri_ref[0, 0:1, :] = lpos0.astype(I32)
    ri_ref[0, 1:2, :] = lpos1.astype(I32)

    wrows = jnp.where(rowl == 0, gw0, jnp.where(rowl == 1, gw1,
                      jnp.where(rowl == 2, lpos0, jnp.where(rowl == 3, lpos1, 0.0))))
    wcol_ref[0] = wrows.T

    both = jnp.logical_or(oh0, oh1).astype(BF16)
    ones = jnp.ones((8, tm), BF16)
    cnt_ref[0, 0] = lax.dot_general(ones, both, (((1,), (1,)), ((), ())),
                                    preferred_element_type=F32)


def _mixer(x, g0, b0, w_in, b_in, conv_w, pool_w, pool_scale, w_o, g1, b1, w_r, b_r, *,
           apply_ln0, alpha, n_groups, per_group, tm):
    bsz, seq, d = x.shape
    nt = seq // tm
    hb = tm // HALO
    nhb = seq // HALO
    d_in = w_in.shape[1]
    d_conv = conv_w.shape[1]
    full = lambda *shape: pl.BlockSpec(shape, lambda b, i: (0,) * len(shape))
    body = functools.partial(_mixer_body, apply_ln0=apply_ln0, alpha=alpha, n_groups=n_groups,
                             per_group=per_group, seq=seq)
    return pl.pallas_call(
        body,
        grid=(bsz, nt),
        in_specs=[
            pl.BlockSpec((1, tm, d), lambda b, i: (b, i, 0)),
            pl.BlockSpec((1, HALO, d), lambda b, i: (b, jnp.maximum(i * hb - 1, 0), 0)),
            pl.BlockSpec((1, HALO, d), lambda b, i: (b, jnp.minimum((i + 1) * hb, nhb - 1), 0)),
            full(1, d), full(1, d),
            full(d, d_in), full(1, d_in),
            full(CONV_WIDTH, d_conv),
            full(*pool_w.shape), full(1, pool_scale.shape[1]),
            full(d, d), full(1, d), full(1, d),
            full(d, 2 * LANES), full(1, LANES),
        ],
        out_specs=[
            pl.BlockSpec((1, tm, d), lambda b, i: (b, i, 0)),
            pl.BlockSpec((1, 8, tm), lambda b, i: (b, 0, i)),
            pl.BlockSpec((1, tm, LANES), lambda b, i: (b, i, 0)),
            pl.BlockSpec((1, 1, 8, LANES), lambda b, i: (b, i, 0, 0)),
        ],
        out_shape=[
            jax.ShapeDtypeStruct((bsz, seq, d), F32),
            jax.ShapeDtypeStruct((bsz, 8, seq), I32),
            jax.ShapeDtypeStruct((bsz, seq, LANES), F32),
            jax.ShapeDtypeStruct((bsz, nt, 8, LANES), F32),
        ],
        compiler_params=pltpu.CompilerParams(
            dimension_semantics=("arbitrary", "arbitrary"), vmem_limit_bytes=VMEM_LIMIT),
        name="mixer",
    )(x, x, x, g0, b0, w_in, b_in, conv_w, pool_w, pool_scale, w_o, g1, b1, w_r, b_r)


def _tables_body(cnt_ref, gstart_ref, lstart_ref, len_ref):
    c = cnt_ref[...]
    n_tiles = c.shape[0]
    row = lax.broadcasted_iota(I32, c.shape, 0)
    lane = lax.broadcasted_iota(I32, c.shape, 1)

    def prefix(v, idx, size, axis):
        k = 1
        while k < size:
            v = v + jnp.where(idx >= k, pltpu.roll(v, k, axis), 0.0)
            k *= 2
        return v

    over_tiles = prefix(c, row, n_tiles, 0)
    tot = jnp.broadcast_to(over_tiles[n_tiles - 1:n_tiles], c.shape)
    expert_start = prefix(tot, lane, LANES, 1) - tot
    gstart_ref[...] = (expert_start + over_tiles - c).astype(I32)
    lstart_ref[...] = (prefix(c, lane, LANES, 1) - c).astype(I32)
    len_ref[...] = c.astype(I32)


def _run_tables(cnt, n_experts):
    out = jax.ShapeDtypeStruct(cnt.shape, I32)
    tabs = pl.pallas_call(_tables_body, out_shape=[out, out, out], name="run_tables")(cnt)
    return tuple(t[:, :n_experts].reshape(-1) for t in tabs)


RUN_CHUNK_LOG2 = 6


def _slots(ref, first, n, rps):
    start = first * rps
    if not isinstance(start, int):
        start = pl.multiple_of(start, rps)
    return ref.at[pl.ds(start, n * rps)]


def _start_run_copies(tabs, tile, n_experts, make_copy):
    gstart_ref, lstart_ref, len_ref = tabs
    big = 1 << RUN_CHUNK_LOG2

    def per_expert(e, carry):
        idx = tile * n_experts + e
        n, ls, gs = len_ref[idx], lstart_ref[idx], gstart_ref[idx]
        n_big = n >> RUN_CHUNK_LOG2

        def big_piece(c, carry2):
            make_copy(ls + c * big, gs + c * big, big).start()
            return carry2
        lax.fori_loop(0, n_big, big_piece, 0)
        off = n_big * big
        for bit in reversed(range(RUN_CHUNK_LOG2)):
            size = 1 << bit

            @pl.when((n & size) != 0)
            def _(off=off, size=size, bit=bit):
                make_copy(ls + off, gs + off, size).start(priority=bit % 2)
            off = off + (n & size)
        return carry
    lax.fori_loop(0, n_experts, per_expert, 0)


def _ple_body(gstart_ref, lstart_ref, len_ref, x1_ref, p_ref, ri_ref, wg_ref, bg_ref, wp_ref,
              ple_ref, xs_ref, buf_ref, sem, *, n_experts):
    tile = pl.program_id(0) * pl.num_programs(1) + pl.program_id(1)
    tabs = (gstart_ref, lstart_ref, len_ref)
    tm, d = x1_ref.shape[1:]
    rps = d // 2 // LANES

    n_tiles = pl.num_programs(0) * pl.num_programs(1)
    par = lax.rem(tile, 2)

    def copier(which):
        def make_copy(local, glob, n):
            return pltpu.make_async_copy(_slots(buf_ref.at[which], local, n, rps),
                                         _slots(xs_ref, glob, n, rps), sem.at[which])
        return make_copy

    def wait_all(which):
        copier(which)(0, 0, TOP_K * tm).wait()

    @pl.when(tile >= 2)
    def _():
        wait_all(par)

    x1 = x1_ref[0]
    lpos = ri_ref[0]
    s_idx = lax.broadcasted_iota(I32, (TOP_K * tm, tm), 0)
    perm = jnp.logical_or(s_idx == lpos[0:1], s_idx == lpos[1:2]).astype(BF16)
    xb = x1.astype(BF16)
    sorted_rows = _pack_exact_bf16_pairs(_dot(perm, xb))
    gate = jax.nn.sigmoid(_dot(xb, wg_ref[...]) + bg_ref[...])
    ple_ref[0] = gate * _dot(p_ref[0].astype(BF16), wp_ref[...])
    _store_slot_rows(buf_ref.at[par], sorted_rows)
    _start_run_copies(tabs, tile, n_experts, copier(par))

    @pl.when(tile == n_tiles - 1)
    def _():
        wait_all(par)

        @pl.when(tile >= 1)
        def _():
            wait_all(1 - par)


def _ple_dispatch(tabs, x1, p, ri, w_gate, b_gate, w_proj, *, layer, tm, n_slots, n_experts):
    bsz, seq, d = x1.shape
    nt = seq // tm
    d_ple = p.shape[-1]
    rps = d // 2 // LANES
    full = lambda *shape: pl.BlockSpec(shape, lambda b, i, *_: (0,) * len(shape))
    grid_spec = pltpu.PrefetchScalarGridSpec(
        num_scalar_prefetch=3,
        grid=(bsz, nt),
        in_specs=[
            pl.BlockSpec((1, tm, d), lambda b, i, *_: (b, i, 0)),
            pl.BlockSpec((None, 1, tm, d_ple), lambda b, i, *_: (layer, b, i, 0)),
            pl.BlockSpec((1, 8, tm), lambda b, i, *_: (b, 0, i)),
            full(d, d), full(1, d), full(d_ple, d),
        ],
        out_specs=[
            pl.BlockSpec((1, tm, d), lambda b, i, *_: (b, i, 0)),
            pl.BlockSpec(memory_space=pl.ANY),
        ],
        scratch_shapes=[pltpu.VMEM((2, TOP_K * tm * rps, LANES), U32), pltpu.SemaphoreType.DMA((2,))],
    )
    return pl.pallas_call(
        functools.partial(_ple_body, n_experts=n_experts),
        grid_spec=grid_spec,
        out_shape=[
            jax.ShapeDtypeStruct((bsz, seq, d), F32),
            jax.ShapeDtypeStruct((n_slots * rps, LANES), U32),
        ],
        compiler_params=pltpu.CompilerParams(
            dimension_semantics=("arbitrary", "arbitrary"), vmem_limit_bytes=VMEM_LIMIT),
        name="ple_dispatch",
    )(*tabs, x1, p, ri, w_gate, b_gate, w_proj)


def _experts_body(blk_ref, exp_ref, lo_ref, new_ref, xs_ref, w1_ref, w3_ref, w2_ref, ys_ref,
                  w1b_ref, w3b_ref, w2b_ref):
    it = pl.program_id(0)
    lo = lo_ref[it]
    rps = xs_ref.shape[0] // SLOT_BLOCK

    @pl.when(new_ref[it] == 1)
    def _():
        w1b_ref[...] = w1_ref[...].astype(BF16)
        w3b_ref[...] = w3_ref[...].astype(BF16)
        w2b_ref[...] = w2_ref[...].astype(BF16)

    @pl.when(lo < SLOT_BLOCK)
    def _():
        hi_half, lo_half = _unpack_bf16_pairs(_load_slot_rows(xs_ref, SLOT_BLOCK, rps))
        xb = jnp.concatenate([hi_half.astype(BF16), lo_half.astype(BF16)], axis=1)
        h = jax.nn.silu(_dot(xb, w1b_ref[...])) * _dot(xb, w3b_ref[...])
        y = _pack_bf16_pairs(_dot(h.astype(BF16), w2b_ref[...]))

        @pl.when(lo == 0)
        def _():
            _store_slot_rows(ys_ref, y)

        @pl.when(lo > 0)
        def _():
            rows = lax.broadcasted_iota(I32, y.shape, 0)
            _store_slot_rows(ys_ref, jnp.where(rows >= lo, y, _load_slot_rows(ys_ref, SLOT_BLOCK, rps)))


def _experts(items, xs, w1, w3, w2, *, layer):
    item_block, item_expert, item_lo, item_new = items
    rows = xs.shape[0]
    _, _, d, d_e = w1.shape
    rps = d // 2 // LANES
    n_items = item_block.shape[0]
    grid_spec = pltpu.PrefetchScalarGridSpec(
        num_scalar_prefetch=4,
        grid=(n_items,),
        in_specs=[
            pl.BlockSpec((SLOT_BLOCK * rps, LANES), lambda i, blk, exp, lo, new: (blk[i], 0)),
            pl.BlockSpec((None, None, d, d_e), lambda i, blk, exp, lo, new: (layer, exp[i], 0, 0)),
            pl.BlockSpec((None, None, d, d_e), lambda i, blk, exp, lo, new: (layer, exp[i], 0, 0)),
            pl.BlockSpec((None, None, d_e, d), lambda i, blk, exp, lo, new: (layer, exp[i], 0, 0)),
        ],
        out_specs=pl.BlockSpec((SLOT_BLOCK * rps, LANES), lambda i, blk, exp, lo, new: (blk[i], 0)),
        scratch_shapes=[pltpu.VMEM((d, d_e), BF16), pltpu.VMEM((d, d_e), BF16),
                        pltpu.VMEM((d_e, d), BF16)],
    )
    return pl.pallas_call(
        _experts_body,
        grid_spec=grid_spec,
        out_shape=jax.ShapeDtypeStruct((rows, LANES), U32),
        compiler_params=pltpu.CompilerParams(
            dimension_semantics=("arbitrary",), vmem_limit_bytes=VMEM_LIMIT),
        name="experts",
    )(item_block, item_expert, item_lo, item_new, xs, w1, w3, w2)


def _expert_items(cnt, n_experts, n_slots):
    n_blocks = n_slots // SLOT_BLOCK
    n_items = n_blocks + n_experts - 1
    tot = jnp.sum(cnt, axis=0)[:n_experts].astype(I32)
    end = jnp.cumsum(tot)
    start = end - tot
    first_blk = start // SLOT_BLOCK
    n_it = jnp.where(tot > 0, (end - 1) // SLOT_BLOCK - first_blk + 1, 0)
    it_end = jnp.cumsum(n_it)
    it_start = it_end - n_it
    idx = jnp.arange(n_items, dtype=I32)
    live = idx < it_end[-1]
    expert = jnp.minimum(jnp.sum(it_end[None, :] <= idx[:, None], axis=1), n_experts - 1).astype(I32)
    onehot = expert[:, None] == jnp.arange(n_experts, dtype=I32)[None, :]
    pick = lambda v: jnp.sum(jnp.where(onehot, v[None, :], 0), axis=1)
    nth = idx - pick(it_start)
    block = pick(first_blk) + nth
    item_block = jnp.where(live, block, n_blocks - 1).astype(I32)
    item_lo = jnp.where(live, jnp.maximum(pick(start) - block * SLOT_BLOCK, 0), SLOT_BLOCK).astype(I32)
    prev_expert = jnp.concatenate([jnp.full((1,), -1, I32), expert[:-1]])
    item_new = (expert != prev_expert).astype(I32)
    return item_block, expert, item_lo, item_new


def _combine_body(gstart_ref, lstart_ref, len_ref, x1_ref, ple_ref, wcol_ref, ys_ref, g2_ref, b2_ref,
                  out_ref, buf_ref, sem, *, alpha, n_experts):
    tile = pl.program_id(0) * pl.num_programs(1) + pl.program_id(1)
    tabs = (gstart_ref, lstart_ref, len_ref)
    tm, d = x1_ref.shape[1:]
    rps = d // 2 // LANES

    n_tiles = pl.num_programs(0) * pl.num_programs(1)
    par = lax.rem(tile, 2)

    def copier(which):
        def make_copy(local, glob, n):
            return pltpu.make_async_copy(_slots(ys_ref, glob, n, rps),
                                         _slots(buf_ref.at[which], local, n, rps), sem.at[which])
        return make_copy

    @pl.when(tile == 0)
    def _():
        _start_run_copies(tabs, tile, n_experts, copier(par))

    @pl.when(tile + 1 < n_tiles)
    def _():
        _start_run_copies(tabs, tile + 1, n_experts, copier(1 - par))
    copier(par)(0, 0, TOP_K * tm).wait()

    hi_half, lo_half = _unpack_bf16_pairs(_load_slot_rows(buf_ref.at[par], TOP_K * tm, rps))
    ys = jnp.concatenate([hi_half.astype(BF16), lo_half.astype(BF16)], axis=1)
    wcol = wcol_ref[0]
    s_idx = lax.broadcasted_iota(I32, (tm, TOP_K * tm), 1)
    ffn = None
    for kk in range(TOP_K):
        pick = (s_idx == wcol[:, TOP_K + kk:TOP_K + kk + 1].astype(I32)).astype(BF16)
        term = _dot(pick, ys) * wcol[:, kk:kk + 1]
        ffn = term if ffn is None else ffn + term
    out_ref[0] = _layernorm(alpha * x1_ref[0] + ffn + ple_ref[0], g2_ref[...], b2_ref[...])


def _combine(tabs, x1, ple, wcol, ys, g2, b2, *, alpha, tm, n_experts):
    bsz, seq, d = x1.shape
    nt = seq // tm
    rps = d // 2 // LANES
    full = lambda *shape: pl.BlockSpec(shape, lambda b, i, *_: (0,) * len(shape))
    grid_spec = pltpu.PrefetchScalarGridSpec(
        num_scalar_prefetch=3,
        grid=(bsz, nt),
        in_specs=[
            pl.BlockSpec((1, tm, d), lambda b, i, *_: (b, i, 0)),
            pl.BlockSpec((1, tm, d), lambda b, i, *_: (b, i, 0)),
            pl.BlockSpec((1, tm, LANES), lambda b, i, *_: (b, i, 0)),
            pl.BlockSpec(memory_space=pl.ANY),
            full(1, d), full(1, d),
        ],
        out_specs=pl.BlockSpec((1, tm, d), lambda b, i, *_: (b, i, 0)),
        scratch_shapes=[pltpu.VMEM((2, TOP_K * tm * rps, LANES), U32), pltpu.SemaphoreType.DMA((2,))],
    )
    return pl.pallas_call(
        functools.partial(_combine_body, alpha=alpha, n_experts=n_experts),
        grid_spec=grid_spec,
        out_shape=jax.ShapeDtypeStruct((bsz, seq, d), F32),
        compiler_params=pltpu.CompilerParams(
            dimension_semantics=("arbitrary", "arbitrary"), vmem_limit_bytes=VMEM_LIMIT),
        name="combine",
    )(*tabs, x1, ple, wcol, ys, g2, b2)


def _router_weights(w_rg, b_rg, w_re, b_re):
    d, n_groups = w_rg.shape
    per_group = w_re.shape[2]
    assert n_groups <= 8 and per_group <= 8 and 8 + 8 * n_groups <= LANES
    w_e = jnp.pad(jnp.transpose(w_re, (1, 0, 2)), ((0, 0), (0, 0), (0, 8 - per_group)))
    b_e = jnp.pad(b_re, ((0, 0), (0, 8 - per_group)))
    tail = LANES - 8 - 8 * n_groups
    w = jnp.concatenate([jnp.pad(w_rg, ((0, 0), (0, 8 - n_groups))), w_e.reshape(d, 8 * n_groups),
                         jnp.zeros((d, tail), F32)], axis=1)
    b = jnp.concatenate([jnp.pad(b_rg, (0, 8 - n_groups)), b_e.reshape(-1), jnp.zeros((tail,), F32)])
    w_hi = w.astype(BF16)
    w_lo = (w - w_hi.astype(F32)).astype(BF16)
    return jnp.concatenate([w_hi, w_lo], axis=1), b[None, :]


def kernel(x, p, ln0_g, ln0_b, w_in, b_in, conv_w, pool_w, pool_scale, w_o, ln1_g, ln1_b,
           w_router_group, b_router_group, w_router_expert, b_router_expert, w1, w3, w2,
           w_ple_gate, b_ple_gate, w_ple_proj, ln2_g, ln2_b):
    bsz, seq, d = x.shape
    depth = w_in.shape[0]
    n_groups, per_group = w_router_expert.shape[1], w_router_expert.shape[3]
    n_experts = n_groups * per_group
    alpha = (2 * depth) ** 0.25
    tm = min(512, seq)
    n_slots = bsz * seq * TOP_K
    assert seq % tm == 0 and tm % LANES == 0 and n_slots % SLOT_BLOCK == 0 and n_experts <= LANES
    assert d % (2 * LANES) == 0
    row = lambda a: a[None, :]

    for i in range(depth):
        w_r, b_r = _router_weights(w_router_group[i], b_router_group[i],
                                   w_router_expert[i], b_router_expert[i])
        x1, ri, wcol, cnt = _mixer(
            x, row(ln0_g), row(ln0_b), w_in[i].astype(BF16), row(b_in[i]), conv_w[i],
            pool_w[i].astype(BF16), row(pool_scale[i]), w_o[i].astype(BF16),
            row(ln1_g[i]), row(ln1_b[i]), w_r, b_r,
            apply_ln0=(i == 0), alpha=alpha, n_groups=n_groups, per_group=per_group, tm=tm)
        cnt = cnt[:, :, 0, :].reshape(-1, LANES)
        tabs = _run_tables(cnt, n_experts)
        items = _expert_items(cnt, n_experts, n_slots)
        ple, xs = _ple_dispatch(tabs, x1, p, ri, w_ple_gate[i].astype(BF16), row(b_ple_gate[i]),
                                w_ple_proj[i].astype(BF16), layer=i, tm=tm, n_slots=n_slots,
                                n_experts=n_experts)
        ys = _experts(items, xs, w1, w3, w2, layer=i)
        x = _combine(tabs, x1, ple, wcol, ys, row(ln2_g[i]), row(ln2_b[i]), alpha=alpha, tm=tm,
                     n_experts=n_experts)
    return x
```

```python
import functools

import jax
import jax.numpy as jnp
from jax import lax
from jax.experimental import pallas as pl
from jax.experimental.pallas import tpu as pltpu

POOL_WINDOWS = (2, 4, 8, 16)
CONV_WIDTH = 3
TOP_K = 2
LN_EPS = 1e-5
HALO = 8
LANES = 128
MIX_CHAINS = 2
SLOT_BLOCK = 512
VMEM_LIMIT = 56 * 1024 * 1024

F32 = jnp.float32
BF16 = jnp.bfloat16
I32 = jnp.int32
U32 = jnp.uint32


def _layernorm(v, g, b):
    mu = jnp.mean(v, axis=-1, keepdims=True)
    c = v - mu
    var = jnp.mean(c * c, axis=-1, keepdims=True)
    return c * lax.rsqrt(var + LN_EPS) * g + b


def _dot(a, b):
    return jnp.dot(a, b, preferred_element_type=F32)


def _pack_bf16_pairs(v):
    c = v.shape[1] // 2
    hi = pltpu.bitcast(v[:, :c].astype(BF16).astype(F32), U32)
    lo = pltpu.bitcast(v[:, c:].astype(BF16).astype(F32), U32)
    return hi | (lo >> 16)


def _pack_exact_bf16_pairs(v):
    c = v.shape[1] // 2
    return pltpu.bitcast(v[:, :c], U32) | (pltpu.bitcast(v[:, c:], U32) >> 16)


def _unpack_bf16_pairs(u):
    hi = pltpu.bitcast(u & jnp.uint32(0xFFFF0000), F32)
    lo = pltpu.bitcast(u << 16, F32)
    return hi, lo


def _store_slot_rows(ref, packed):
    n, c = packed.shape
    rps = c // LANES
    for k in range(rps):
        ref[pl.ds(k, n, stride=rps), :] = packed[:, k * LANES:(k + 1) * LANES]


def _load_slot_rows(ref, n, rps):
    return jnp.concatenate([ref[pl.ds(k, n, stride=rps), :] for k in range(rps)], axis=1)


def _mixer_body(x_ref, xp_ref, xn_ref, g0_ref, b0_ref, win_ref, bin_ref, cw_ref, pw_ref, ps_ref,
                wo_ref, g1_ref, b1_ref, wr_ref, br_ref,
                x1_ref, ri_ref, wcol_ref, cnt_ref, *, apply_ln0, alpha, n_groups, per_group, seq):
    i = pl.program_id(1)
    nt = pl.num_programs(1)
    tm = x_ref.shape[1]
    n = tm // MIX_CHAINS
    rows = n + 2 * HALO
    d_conv = cw_ref.shape[1]
    dg = pw_ref.shape[1]

    xe = jnp.concatenate([xp_ref[0], x_ref[0], xn_ref[0]], axis=0)
    if apply_ln0:
        xe = _layernorm(xe, g0_ref[...], b0_ref[...])
    cw = cw_ref[...]
    ps = ps_ref[...]

    def in_proj(s):
        xs = xe[s * n:s * n + rows]
        return xs, _dot(xs.astype(BF16), win_ref[...]) + bin_ref[...]

    def token_mix(s, z):
        r = lax.broadcasted_iota(I32, (rows, 1), 0)
        lo_ok = jnp.logical_or(r >= HALO, i > 0) if s == 0 else True
        hi_ok = jnp.logical_or(r < n + HALO, i < nt - 1) if s == MIX_CHAINS - 1 else True
        if s == 0 or s == MIX_CHAINS - 1:
            valid = jnp.logical_and(lo_ok, hi_ok)
            mask = lambda a: jnp.where(valid, a, 0.0)
        else:
            mask = lambda a: a

        h = z[:, :d_conv]
        gate_b = z[HALO:HALO + n, d_conv:2 * d_conv]
        gate_c = z[:, 2 * d_conv:3 * d_conv]
        v = mask(gate_c * h)
        conv = (cw[0:1] * pltpu.roll(v, 1, 0)[HALO:HALO + n]
                + cw[1:2] * v[HALO:HALO + n]
                + cw[2:3] * pltpu.roll(v, rows - 1, 0)[HALO:HALO + n])
        parts = [gate_b * conv]

        pos = i * tm + s * n + r[HALO:HALO + n] - HALO
        for g, w in enumerate(POOL_WINDOWS):
            left = w // 2
            right = w - 1 - left
            c0 = 3 * d_conv + g * dg
            ug = mask(z[:, c0:c0 + dg])
            acc = ug
            k = 1
            while k < w:
                acc = acc + pltpu.roll(acc, k, 0)
                k *= 2
            if right:
                acc = pltpu.roll(acc, rows - right, 0)

            def edge(lo):
                p = pos[lo:lo + HALO]
                cnt = (jnp.minimum(p + right + 1, seq) - jnp.maximum(p - left, 0)).astype(F32)
                return acc[HALO + lo:2 * HALO + lo] / cnt - ug[HALO + lo:2 * HALO + lo]
            inner = acc[2 * HALO:n] * (1.0 / w) - ug[2 * HALO:n]
            pooled = jnp.concatenate([edge(0), inner, edge(n - HALO)], axis=0)
            parts.append(_dot(pooled.astype(BF16), pw_ref[g]) * ps[:, g * dg:(g + 1) * dg])
        return jnp.concatenate(parts, axis=1).astype(BF16)

    def out_proj(xs, cat):
        mix = _dot(cat, wo_ref[...])
        return _layernorm(alpha * xs[HALO:HALO + n] + mix, g1_ref[...], b1_ref[...])

    def route(x1s):
        xh = x1s.astype(BF16)
        xl = (x1s - xh.astype(F32)).astype(BF16)
        l_hi = _dot(xh, wr_ref[...])
        l_lo = _dot(xl, wr_ref[:, :LANES])
        return l_hi[:, :LANES] + l_hi[:, LANES:] + l_lo + br_ref[...]

    chains = [in_proj(s) for s in range(MIX_CHAINS)]
    x1s = [out_proj(xs, token_mix(s, z)) for s, (xs, z) in enumerate(chains)]
    x1 = jnp.concatenate(x1s, axis=0)
    x1_ref[0] = x1
    logits = jnp.concatenate([route(v) for v in x1s], axis=0)
    lt = logits.T

    row8 = lax.broadcasted_iota(I32, (8, tm), 0)
    neg = jnp.float32(-jnp.inf)
    lg = jnp.where(row8 < n_groups, lt[0:8], neg)
    mg = jnp.max(lg, axis=0, keepdims=True)
    g_w = 1.0 / jnp.sum(jnp.exp(lg - mg), axis=0, keepdims=True)
    g_sel = jnp.min(jnp.where(lg == mg, row8, 8), axis=0, keepdims=True)
    le = lt[8:16]
    for g in range(1, n_groups):
        le = jnp.where(g_sel == g, lt[8 + 8 * g:16 + 8 * g], le)
    le = jnp.where(row8 < per_group, le, neg)
    v1 = jnp.max(le, axis=0, keepdims=True)
    i1 = jnp.min(jnp.where(le == v1, row8, 8), axis=0, keepdims=True)
    le2 = jnp.where(row8 == i1, neg, le)
    v2 = jnp.max(le2, axis=0, keepdims=True)
    i2 = jnp.min(jnp.where(le2 == v2, row8, 8), axis=0, keepdims=True)
    e21 = jnp.exp(v2 - v1)
    w_first = 1.0 / (1.0 + e21)
    gw0 = w_first * g_w
    gw1 = (e21 * w_first) * g_w
    eid0 = g_sel * per_group + i1
    eid1 = g_sel * per_group + i2

    rowl = lax.broadcasted_iota(I32, (LANES, tm), 0)
    oh0 = rowl == eid0
    oh1 = rowl == eid1
    a_idx = lax.broadcasted_iota(I32, (tm, tm), 0)
    b_idx = lax.broadcasted_iota(I32, (tm, tm), 1)
    before = (a_idx < b_idx).astype(BF16)
    r0 = _dot(oh0.astype(BF16), before)
    r1 = _dot(oh1.astype(BF16), before)
    cnt0 = jnp.sum(oh0.astype(F32), axis=1, keepdims=True)
    lower = jnp.sum((rowl > eid0).astype(F32) + (rowl > eid1).astype(F32),
                    axis=1, keepdims=True)
    lpos0 = jnp.sum(jnp.where(oh0, lower + r0, 0.0), axis=0, keepdims=True)
    lpos1 = jnp.sum(jnp.where(oh1, lower + cnt0 + r1, 0.0), axis=0, keepdims=True)

    ri_ref[0] = jnp.zeros((8, tm), I32)
    ri_ref[0, 0:1, :] = lpos0.astype(I32)
    ri_ref[0, 1:2, :] = lpos1.astype(I32)

    wrows = jnp.where(rowl == 0, gw0, jnp.where(rowl == 1, gw1,
                      jnp.where(rowl == 2, lpos0, jnp.where(rowl == 3, lpos1, 0.0))))
    wcol_ref[0] = wrows.T

    both = jnp.logical_or(oh0, oh1).astype(BF16)
    ones = jnp.ones((8, tm), BF16)
    cnt_ref[0, 0] = lax.dot_general(ones, both, (((1,), (1,)), ((), ())),
                                    preferred_element_type=F32)


def _mixer(x, g0, b0, w_in, b_in, conv_w, pool_w, pool_scale, w_o, g1, b1, w_r, b_r, *,
           apply_ln0, alpha, n_groups, per_group, tm):
    bsz, seq, d = x.shape
    nt = seq // tm
    hb = tm // HALO
    nhb = seq // HALO
    d_in = w_in.shape[1]
    d_conv = conv_w.shape[1]
    full = lambda *shape: pl.BlockSpec(shape, lambda b, i: (0,) * len(shape))
    body = functools.partial(_mixer_body, apply_ln0=apply_ln0, alpha=alpha, n_groups=n_groups,
                             per_group=per_group, seq=seq)
    return pl.pallas_call(
        body,
        grid=(bsz, nt),
        in_specs=[
            pl.BlockSpec((1, tm, d), lambda b, i: (b, i, 0)),
            pl.BlockSpec((1, HALO, d), lambda b, i: (b, jnp.maximum(i * hb - 1, 0), 0)),
            pl.BlockSpec((1, HALO, d), lambda b, i: (b, jnp.minimum((i + 1) * hb, nhb - 1), 0)),
            full(1, d), full(1, d),
            full(d, d_in), full(1, d_in),
            full(CONV_WIDTH, d_conv),
            full(*pool_w.shape), full(1, pool_scale.shape[1]),
            full(d, d), full(1, d), full(1, d),
            full(d, 2 * LANES), full(1, LANES),
        ],
        out_specs=[
            pl.BlockSpec((1, tm, d), lambda b, i: (b, i, 0)),
            pl.BlockSpec((1, 8, tm), lambda b, i: (b, 0, i)),
            pl.BlockSpec((1, tm, LANES), lambda b, i: (b, i, 0)),
            pl.BlockSpec((1, 1, 8, LANES), lambda b, i: (b, i, 0, 0)),
        ],
        out_shape=[
            jax.ShapeDtypeStruct((bsz, seq, d), F32),
            jax.ShapeDtypeStruct((bsz, 8, seq), I32),
            jax.ShapeDtypeStruct((bsz, seq, LANES), F32),
            jax.ShapeDtypeStruct((bsz, nt, 8, LANES), F32),
        ],
        compiler_params=pltpu.CompilerParams(
            dimension_semantics=("arbitrary", "arbitrary"), vmem_limit_bytes=VMEM_LIMIT),
        name="mixer",
    )(x, x, x, g0, b0, w_in, b_in, conv_w, pool_w, pool_scale, w_o, g1, b1, w_r, b_r)


def _tables_body(cnt_ref, gstart_ref, lstart_ref, len_ref):
    c = cnt_ref[...]
    n_tiles = c.shape[0]
    row = lax.broadcasted_iota(I32, c.shape, 0)
    lane = lax.broadcasted_iota(I32, c.shape, 1)

    def prefix(v, idx, size, axis):
        k = 1
        while k < size:
            v = v + jnp.where(idx >= k, pltpu.roll(v, k, axis), 0.0)
            k *= 2
        return v

    over_tiles = prefix(c, row, n_tiles, 0)
    tot = jnp.broadcast_to(over_tiles[n_tiles - 1:n_tiles], c.shape)
    expert_start = prefix(tot, lane, LANES, 1) - tot
    gstart_ref[...] = (expert_start + over_tiles - c).astype(I32)
    lstart_ref[...] = (prefix(c, lane, LANES, 1) - c).astype(I32)
    len_ref[...] = c.astype(I32)


def _run_tables(cnt, n_experts):
    out = jax.ShapeDtypeStruct(cnt.shape, I32)
    tabs = pl.pallas_call(_tables_body, out_shape=[out, out, out], name="run_tables")(cnt)
    return tuple(t[:, :n_experts].reshape(-1) for t in tabs)


RUN_CHUNK_LOG2 = 6


def _slots(ref, first, n, rps):
    start = first * rps
    if not isinstance(start, int):
        start = pl.multiple_of(start, rps)
    return ref.at[pl.ds(start, n * rps)]


def _start_run_copies(tabs, tile, n_experts, make_copy):
    gstart_ref, lstart_ref, len_ref = tabs
    big = 1 << RUN_CHUNK_LOG2

    def per_expert(e, carry):
        idx = tile * n_experts + e
        n, ls, gs = len_ref[idx], lstart_ref[idx], gstart_ref[idx]
        n_big = n >> RUN_CHUNK_LOG2

        def big_piece(c, carry2):
            make_copy(ls + c * big, gs + c * big, big).start()
            return carry2
        lax.fori_loop(0, n_big, big_piece, 0)
        off = n_big * big
        for bit in reversed(range(RUN_CHUNK_LOG2)):
            size = 1 << bit

            @pl.when((n & size) != 0)
            def _(off=off, size=size, bit=bit):
                make_copy(ls + off, gs + off, size).start(priority=bit % 2)
            off = off + (n & size)
        return carry
    lax.fori_loop(0, n_experts, per_expert, 0)


def _ple_body(gstart_ref, lstart_ref, len_ref, x1_ref, p_ref, ri_ref, wg_ref, bg_ref, wp_ref,
              ple_ref, xs_ref, buf_ref, sem, *, n_experts):
    tile = pl.program_id(0) * pl.num_programs(1) + pl.program_id(1)
    tabs = (gstart_ref, lstart_ref, len_ref)
    tm, d = x1_ref.shape[1:]
    rps = d // 2 // LANES

    n_tiles = pl.num_programs(0) * pl.num_programs(1)
    par = lax.rem(tile, 2)

    def copier(which):
        def make_copy(local, glob, n):
            return pltpu.make_async_copy(_slots(buf_ref.at[which], local, n, rps),
                                         _slots(xs_ref, glob, n, rps), sem.at[which])
        return make_copy

    def wait_all(which):
        copier(which)(0, 0, TOP_K * tm).wait()

    @pl.when(tile >= 2)
    def _():
        wait_all(par)

    x1 = x1_ref[0]
    lpos = ri_ref[0]
    s_idx = lax.broadcasted_iota(I32, (TOP_K * tm, tm), 0)
    perm = jnp.logical_or(s_idx == lpos[0:1], s_idx == lpos[1:2]).astype(BF16)
    xb = x1.astype(BF16)
    sorted_rows = _pack_exact_bf16_pairs(_dot(perm, xb))
    gate = jax.nn.sigmoid(_dot(xb, wg_ref[...]) + bg_ref[...])
    ple_ref[0] = gate * _dot(p_ref[0].astype(BF16), wp_ref[...])
    _store_slot_rows(buf_ref.at[par], sorted_rows)
    _start_run_copies(tabs, tile, n_experts, copier(par))

    @pl.when(tile == n_tiles - 1)
    def _():
        wait_all(par)

        @pl.when(tile >= 1)
        def _():
            wait_all(1 - par)


def _ple_dispatch(tabs, x1, p, ri, w_gate, b_gate, w_proj, *, layer, tm, n_slots, n_experts):
    bsz, seq, d = x1.shape
    nt = seq // tm
    d_ple = p.shape[-1]
    rps = d // 2 // LANES
    full = lambda *shape: pl.BlockSpec(shape, lambda b, i, *_: (0,) * len(shape))
    grid_spec = pltpu.PrefetchScalarGridSpec(
        num_scalar_prefetch=3,
        grid=(bsz, nt),
        in_specs=[
            pl.BlockSpec((1, tm, d), lambda b, i, *_: (b, i, 0)),
            pl.BlockSpec((None, 1, tm, d_ple), lambda b, i, *_: (layer, b, i, 0)),
            pl.BlockSpec((1, 8, tm), lambda b, i, *_: (b, 0, i)),
            full(d, d), full(1, d), full(d_ple, d),
        ],
        out_specs=[
            pl.BlockSpec((1, tm, d), lambda b, i, *_: (b, i, 0)),
            pl.BlockSpec(memory_space=pl.ANY),
        ],
        scratch_shapes=[pltpu.VMEM((2, TOP_K * tm * rps, LANES), U32), pltpu.SemaphoreType.DMA((2,))],
    )
    return pl.pallas_call(
        functools.partial(_ple_body, n_experts=n_experts),
        grid_spec=grid_spec,
        out_shape=[
            jax.ShapeDtypeStruct((bsz, seq, d), F32),
            jax.ShapeDtypeStruct((n_slots * rps, LANES), U32),
        ],
        compiler_params=pltpu.CompilerParams(
            dimension_semantics=("arbitrary", "arbitrary"), vmem_limit_bytes=VMEM_LIMIT),
        name="ple_dispatch",
    )(*tabs, x1, p, ri, w_gate, b_gate, w_proj)


def _experts_body(blk_ref, exp_ref, lo_ref, new_ref, par_ref, nxt_ref, xs_ref, w1_ref, w3_ref, w2_ref,
                  ys_ref, f1_ref, f3_ref, f2_ref, w1b_ref, w3b_ref, w2b_ref, sem, *, layer):
    it = pl.program_id(0)
    lo = lo_ref[it]
    rps = xs_ref.shape[0] // SLOT_BLOCK

    def weight_copies(expert, slot):
        return [pltpu.make_async_copy(w_ref.at[layer, expert], f_ref.at[slot], sem.at[slot, k])
                for k, (w_ref, f_ref) in enumerate(((w1_ref, f1_ref), (w3_ref, f3_ref), (w2_ref, f2_ref)))]

    @pl.when(it == 0)
    def _():
        for c in weight_copies(exp_ref[0], 0):
            c.start()

    @pl.when(new_ref[it] == 1)
    def _():
        slot = par_ref[it]
        for c in weight_copies(exp_ref[it], slot):
            c.wait()
        w1b_ref[...] = f1_ref[slot].astype(BF16)
        w3b_ref[...] = f3_ref[slot].astype(BF16)
        w2b_ref[...] = f2_ref[slot].astype(BF16)

        @pl.when(nxt_ref[it] >= 0)
        def _():
            for c in weight_copies(nxt_ref[it], 1 - slot):
                c.start()

    @pl.when(lo < SLOT_BLOCK)
    def _():
        hi_half, lo_half = _unpack_bf16_pairs(_load_slot_rows(xs_ref, SLOT_BLOCK, rps))
        xb = jnp.concatenate([hi_half.astype(BF16), lo_half.astype(BF16)], axis=1)
        h = jax.nn.silu(_dot(xb, w1b_ref[...])) * _dot(xb, w3b_ref[...])
        y = _pack_bf16_pairs(_dot(h.astype(BF16), w2b_ref[...]))

        @pl.when(lo == 0)
        def _():
            _store_slot_rows(ys_ref, y)

        @pl.when(lo > 0)
        def _():
            rows = lax.broadcasted_iota(I32, y.shape, 0)
            _store_slot_rows(ys_ref, jnp.where(rows >= lo, y, _load_slot_rows(ys_ref, SLOT_BLOCK, rps)))


def _experts(items, xs, w1, w3, w2, *, layer):
    rows = xs.shape[0]
    _, _, d, d_e = w1.shape
    rps = d // 2 // LANES
    n_items = items[0].shape[0]
    grid_spec = pltpu.PrefetchScalarGridSpec(
        num_scalar_prefetch=len(items),
        grid=(n_items,),
        in_specs=[
            pl.BlockSpec((SLOT_BLOCK * rps, LANES), lambda i, blk, *_: (blk[i], 0)),
            pl.BlockSpec(memory_space=pl.ANY),
            pl.BlockSpec(memory_space=pl.ANY),
            pl.BlockSpec(memory_space=pl.ANY),
        ],
        out_specs=pl.BlockSpec((SLOT_BLOCK * rps, LANES), lambda i, blk, *_: (blk[i], 0)),
        scratch_shapes=[pltpu.VMEM((2, d, d_e), F32), pltpu.VMEM((2, d, d_e), F32),
                        pltpu.VMEM((2, d_e, d), F32),
                        pltpu.VMEM((d, d_e), BF16), pltpu.VMEM((d, d_e), BF16),
                        pltpu.VMEM((d_e, d), BF16), pltpu.SemaphoreType.DMA((2, 3))],
    )
    return pl.pallas_call(
        functools.partial(_experts_body, layer=layer),
        grid_spec=grid_spec,
        out_shape=jax.ShapeDtypeStruct((rows, LANES), U32),
        compiler_params=pltpu.CompilerParams(
            dimension_semantics=("arbitrary",), vmem_limit_bytes=VMEM_LIMIT),
        name="experts",
    )(*items, xs, w1, w3, w2)


def _expert_items(cnt, n_experts, n_slots):
    n_blocks = n_slots // SLOT_BLOCK
    n_items = n_blocks + n_experts - 1
    tot = jnp.sum(cnt, axis=0)[:n_experts].astype(I32)
    end = jnp.cumsum(tot)
    start = end - tot
    first_blk = start // SLOT_BLOCK
    n_it = jnp.where(tot > 0, (end - 1) // SLOT_BLOCK - first_blk + 1, 0)
    it_end = jnp.cumsum(n_it)
    it_start = it_end - n_it
    idx = jnp.arange(n_items, dtype=I32)
    last_live = it_end[-1] - 1
    live = idx <= last_live
    at = jnp.minimum(idx, last_live)
    expert = jnp.sum(it_end[None, :] <= at[:, None], axis=1).astype(I32)
    onehot = expert[:, None] == jnp.arange(n_experts, dtype=I32)[None, :]
    pick = lambda v: jnp.sum(jnp.where(onehot, v[None, :], 0), axis=1)
    block = pick(first_blk) + idx - pick(it_start)
    item_block = jnp.where(live, block, n_blocks - 1).astype(I32)
    item_lo = jnp.where(live, jnp.maximum(pick(start) - block * SLOT_BLOCK, 0), SLOT_BLOCK).astype(I32)
    prev_expert = jnp.concatenate([jnp.full((1,), -1, I32), expert[:-1]])
    is_new = expert != prev_expert
    slot = (jnp.cumsum(is_new.astype(I32)) - 1) % 2
    later_new = lax.cummin(jnp.where(is_new, idx, n_items)[::-1], axis=0)[::-1]
    next_new = jnp.concatenate([later_new[1:], jnp.full((1,), n_items, I32)])
    next_expert = jnp.where(next_new < n_items, expert[jnp.minimum(next_new, n_items - 1)], -1)
    return (item_block, expert, item_lo, is_new.astype(I32), slot.astype(I32), next_expert.astype(I32))


def _combine_body(gstart_ref, lstart_ref, len_ref, x1_ref, ple_ref, wcol_ref, ys_ref, g2_ref, b2_ref,
                  out_ref, buf_ref, sem, *, alpha, n_experts):
    tile = pl.program_id(0) * pl.num_programs(1) + pl.program_id(1)
    tabs = (gstart_ref, lstart_ref, len_ref)
    tm, d = x1_ref.shape[1:]
    rps = d // 2 // LANES

    n_tiles = pl.num_programs(0) * pl.num_programs(1)
    par = lax.rem(tile, 2)

    def copier(which):
        def make_copy(local, glob, n):
            return pltpu.make_async_copy(_slots(ys_ref, glob, n, rps),
                                         _slots(buf_ref.at[which], local, n, rps), sem.at[which])
        return make_copy

    @pl.when(tile == 0)
    def _():
        _start_run_copies(tabs, tile, n_experts, copier(par))

    @pl.when(tile + 1 < n_tiles)
    def _():
        _start_run_copies(tabs, tile + 1, n_experts, copier(1 - par))
    copier(par)(0, 0, TOP_K * tm).wait()

    hi_half, lo_half = _unpack_bf16_pairs(_load_slot_rows(buf_ref.at[par], TOP_K * tm, rps))
    ys = jnp.concatenate([hi_half.astype(BF16), lo_half.astype(BF16)], axis=1)
    wcol = wcol_ref[0]
    s_idx = lax.broadcasted_iota(I32, (tm, TOP_K * tm), 1)
    ffn = None
    for kk in range(TOP_K):
        pick = (s_idx == wcol[:, TOP_K + kk:TOP_K + kk + 1].astype(I32)).astype(BF16)
        term = _dot(pick, ys) * wcol[:, kk:kk + 1]
        ffn = term if ffn is None else ffn + term
    out_ref[0] = _layernorm(alpha * x1_ref[0] + ffn + ple_ref[0], g2_ref[...], b2_ref[...])


def _combine(tabs, x1, ple, wcol, ys, g2, b2, *, alpha, tm, n_experts):
    bsz, seq, d = x1.shape
    nt = seq // tm
    rps = d // 2 // LANES
    full = lambda *shape: pl.BlockSpec(shape, lambda b, i, *_: (0,) * len(shape))
    grid_spec = pltpu.PrefetchScalarGridSpec(
        num_scalar_prefetch=3,
        grid=(bsz, nt),
        in_specs=[
            pl.BlockSpec((1, tm, d), lambda b, i, *_: (b, i, 0)),
            pl.BlockSpec((1, tm, d), lambda b, i, *_: (b, i, 0)),
            pl.BlockSpec((1, tm, LANES), lambda b, i, *_: (b, i, 0)),
            pl.BlockSpec(memory_space=pl.ANY),
            full(1, d), full(1, d),
        ],
        out_specs=pl.BlockSpec((1, tm, d), lambda b, i, *_: (b, i, 0)),
        scratch_shapes=[pltpu.VMEM((2, TOP_K * tm * rps, LANES), U32), pltpu.SemaphoreType.DMA((2,))],
    )
    return pl.pallas_call(
        functools.partial(_combine_body, alpha=alpha, n_experts=n_experts),
        grid_spec=grid_spec,
        out_shape=jax.ShapeDtypeStruct((bsz, seq, d), F32),
        compiler_params=pltpu.CompilerParams(
            dimension_semantics=("arbitrary", "arbitrary"), vmem_limit_bytes=VMEM_LIMIT),
        name="combine",
    )(*tabs, x1, ple, wcol, ys, g2, b2)


def _router_weights(w_rg, b_rg, w_re, b_re):
    d, n_groups = w_rg.shape
    per_group = w_re.shape[2]
    assert n_groups <= 8 and per_group <= 8 and 8 + 8 * n_groups <= LANES
    w_e = jnp.pad(jnp.transpose(w_re, (1, 0, 2)), ((0, 0), (0, 0), (0, 8 - per_group)))
    b_e = jnp.pad(b_re, ((0, 0), (0, 8 - per_group)))
    tail = LANES - 8 - 8 * n_groups
    w = jnp.concatenate([jnp.pad(w_rg, ((0, 0), (0, 8 - n_groups))), w_e.reshape(d, 8 * n_groups),
                         jnp.zeros((d, tail), F32)], axis=1)
    b = jnp.concatenate([jnp.pad(b_rg, (0, 8 - n_groups)), b_e.reshape(-1), jnp.zeros((tail,), F32)])
    w_hi = w.astype(BF16)
    w_lo = (w - w_hi.astype(F32)).astype(BF16)
    return jnp.concatenate([w_hi, w_lo], axis=1), b[None, :]


def kernel(x, p, ln0_g, ln0_b, w_in, b_in, conv_w, pool_w, pool_scale, w_o, ln1_g, ln1_b,
           w_router_group, b_router_group, w_router_expert, b_router_expert, w1, w3, w2,
           w_ple_gate, b_ple_gate, w_ple_proj, ln2_g, ln2_b):
    bsz, seq, d = x.shape
    depth = w_in.shape[0]
    n_groups, per_group = w_router_expert.shape[1], w_router_expert.shape[3]
    n_experts = n_groups * per_group
    alpha = (2 * depth) ** 0.25
    tm = min(512, seq)
    n_slots = bsz * seq * TOP_K
    assert seq % tm == 0 and tm % LANES == 0 and n_slots % SLOT_BLOCK == 0 and n_experts <= LANES
    assert d % (2 * LANES) == 0
    row = lambda a: a[None, :]

    for i in range(depth):
        w_r, b_r = _router_weights(w_router_group[i], b_router_group[i],
                                   w_router_expert[i], b_router_expert[i])
        x1, ri, wcol, cnt = _mixer(
            x, row(ln0_g), row(ln0_b), w_in[i].astype(BF16), row(b_in[i]), conv_w[i],
            pool_w[i].astype(BF16), row(pool_scale[i]), w_o[i].astype(BF16),
            row(ln1_g[i]), row(ln1_b[i]), w_r, b_r,
            apply_ln0=(i == 0), alpha=alpha, n_groups=n_groups, per_group=per_group, tm=tm)
        cnt = cnt[:, :, 0, :].reshape(-1, LANES)
        tabs = _run_tables(cnt, n_experts)
        items = _expert_items(cnt, n_experts, n_slots)
        ple, xs = _ple_dispatch(tabs, x1, p, ri, w_ple_gate[i].astype(BF16), row(b_ple_gate[i]),
                                w_ple_proj[i].astype(BF16), layer=i, tm=tm, n_slots=n_slots,
                                n_experts=n_experts)
        ys = _experts(items, xs, w1, w3, w2, layer=i)
        x = _combine(tabs, x1, ple, wcol, ys, row(ln2_g[i]), row(ln2_b[i]), alpha=alpha, tm=tm,
                     n_experts=n_experts)
    return x
```

```python
import functools

import jax
import jax.numpy as jnp
from jax import lax
from jax.experimental import pallas as pl
from jax.experimental.pallas import tpu as pltpu

POOL_WINDOWS = (2, 4, 8, 16)
CONV_WIDTH = 3
TOP_K = 2
LN_EPS = 1e-5
HALO = 8
LANES = 128
MIX_CHAINS = 2
SLOT_BLOCK = 512
VMEM_LIMIT = 56 * 1024 * 1024

F32 = jnp.float32
BF16 = jnp.bfloat16
I32 = jnp.int32
U32 = jnp.uint32


def _layernorm(v, g, b):
    mu = jnp.mean(v, axis=-1, keepdims=True)
    c = v - mu
    var = jnp.mean(c * c, axis=-1, keepdims=True)
    return c * lax.rsqrt(var + LN_EPS) * g + b


def _dot(a, b):
    return jnp.dot(a, b, preferred_element_type=F32)


def _pack_bf16_pairs(v):
    c = v.shape[1] // 2
    hi = pltpu.bitcast(v[:, :c].astype(BF16).astype(F32), U32)
    lo = pltpu.bitcast(v[:, c:].astype(BF16).astype(F32), U32)
    return hi | (lo >> 16)


def _pack_exact_bf16_pairs(v):
    c = v.shape[1] // 2
    return pltpu.bitcast(v[:, :c], U32) | (pltpu.bitcast(v[:, c:], U32) >> 16)


def _unpack_bf16_pairs(u):
    hi = pltpu.bitcast(u & jnp.uint32(0xFFFF0000), F32)
    lo = pltpu.bitcast(u << 16, F32)
    return hi, lo


def _store_slot_rows(ref, packed):
    n, c = packed.shape
    rps = c // LANES
    for k in range(rps):
        ref[pl.ds(k, n, stride=rps), :] = packed[:, k * LANES:(k + 1) * LANES]


def _load_slot_rows(ref, n, rps):
    return jnp.concatenate([ref[pl.ds(k, n, stride=rps), :] for k in range(rps)], axis=1)


def _mixer_body(x_ref, xp_ref, xn_ref, g0_ref, b0_ref, win_ref, bin_ref, cw_ref, pw_ref, ps_ref,
                wo_ref, g1_ref, b1_ref, wr_ref, br_ref,
                x1_ref, ri_ref, wcol_ref, cnt_ref, *, apply_ln0, alpha, n_groups, per_group, seq):
    i = pl.program_id(1)
    nt = pl.num_programs(1)
    tm = x_ref.shape[1]
    n = tm // MIX_CHAINS
    rows = n + 2 * HALO
    d_conv = cw_ref.shape[1]
    dg = pw_ref.shape[1]

    xe = jnp.concatenate([xp_ref[0], x_ref[0], xn_ref[0]], axis=0)
    if apply_ln0:
        xe = _layernorm(xe, g0_ref[...], b0_ref[...])
    cw = cw_ref[...]
    ps = ps_ref[...]

    def in_proj(s):
        xs = xe[s * n:s * n + rows]
        return xs, _dot(xs.astype(BF16), win_ref[...]) + bin_ref[...]

    def token_mix(s, z):
        r = lax.broadcasted_iota(I32, (rows, 1), 0)
        lo_ok = jnp.logical_or(r >= HALO, i > 0) if s == 0 else True
        hi_ok = jnp.logical_or(r < n + HALO, i < nt - 1) if s == MIX_CHAINS - 1 else True
        if s == 0 or s == MIX_CHAINS - 1:
            valid = jnp.logical_and(lo_ok, hi_ok)
            mask = lambda a: jnp.where(valid, a, 0.0)
        else:
            mask = lambda a: a

        h = z[:, :d_conv]
        gate_b = z[HALO:HALO + n, d_conv:2 * d_conv]
        gate_c = z[:, 2 * d_conv:3 * d_conv]
        v = mask(gate_c * h)
        conv = (cw[0:1] * pltpu.roll(v, 1, 0)[HALO:HALO + n]
                + cw[1:2] * v[HALO:HALO + n]
                + cw[2:3] * pltpu.roll(v, rows - 1, 0)[HALO:HALO + n])
        parts = [gate_b * conv]

        pos = i * tm + s * n + r[HALO:HALO + n] - HALO
        for g, w in enumerate(POOL_WINDOWS):
            left = w // 2
            right = w - 1 - left
            c0 = 3 * d_conv + g * dg
            ug = mask(z[:, c0:c0 + dg])
            acc = ug
            k = 1
            while k < w:
                acc = acc + pltpu.roll(acc, k, 0)
                k *= 2
            if right:
                acc = pltpu.roll(acc, rows - right, 0)

            def edge(lo):
                p = pos[lo:lo + HALO]
                cnt = (jnp.minimum(p + right + 1, seq) - jnp.maximum(p - left, 0)).astype(F32)
                return acc[HALO + lo:2 * HALO + lo] / cnt - ug[HALO + lo:2 * HALO + lo]
            inner = acc[2 * HALO:n] * (1.0 / w) - ug[2 * HALO:n]
            pooled = jnp.concatenate([edge(0), inner, edge(n - HALO)], axis=0)
            parts.append(_dot(pooled.astype(BF16), pw_ref[g]) * ps[:, g * dg:(g + 1) * dg])
        return jnp.concatenate(parts, axis=1).astype(BF16)

    def out_proj(xs, cat):
        mix = _dot(cat, wo_ref[...])
        return _layernorm(alpha * xs[HALO:HALO + n] + mix, g1_ref[...], b1_ref[...])

    def route(x1s):
        xh = x1s.astype(BF16)
        xl = (x1s - xh.astype(F32)).astype(BF16)
        l_hi = _dot(xh, wr_ref[...])
        l_lo = _dot(xl, wr_ref[:, :LANES])
        return l_hi[:, :LANES] + l_hi[:, LANES:] + l_lo + br_ref[...]

    chains = [in_proj(s) for s in range(MIX_CHAINS)]
    x1s = [out_proj(xs, token_mix(s, z)) for s, (xs, z) in enumerate(chains)]
    x1 = jnp.concatenate(x1s, axis=0)
    x1_ref[0] = x1
    logits = jnp.concatenate([route(v) for v in x1s], axis=0)
    lt = logits.T

    row8 = lax.broadcasted_iota(I32, (8, tm), 0)
    neg = jnp.float32(-jnp.inf)
    lg = jnp.where(row8 < n_groups, lt[0:8], neg)
    mg = jnp.max(lg, axis=0, keepdims=True)
    g_w = 1.0 / jnp.sum(jnp.exp(lg - mg), axis=0, keepdims=True)
    g_sel = jnp.min(jnp.where(lg == mg, row8, 8), axis=0, keepdims=True)
    le = lt[8:16]
    for g in range(1, n_groups):
        le = jnp.where(g_sel == g, lt[8 + 8 * g:16 + 8 * g], le)
    le = jnp.where(row8 < per_group, le, neg)
    v1 = jnp.max(le, axis=0, keepdims=True)
    i1 = jnp.min(jnp.where(le == v1, row8, 8), axis=0, keepdims=True)
    le2 = jnp.where(row8 == i1, neg, le)
    v2 = jnp.max(le2, axis=0, keepdims=True)
    i2 = jnp.min(jnp.where(le2 == v2, row8, 8), axis=0, keepdims=True)
    e21 = jnp.exp(v2 - v1)
    w_first = 1.0 / (1.0 + e21)
    gw0 = w_first * g_w
    gw1 = (e21 * w_first) * g_w
    eid0 = g_sel * per_group + i1
    eid1 = g_sel * per_group + i2

    rowl = lax.broadcasted_iota(I32, (LANES, tm), 0)
    oh0 = rowl == eid0
    oh1 = rowl == eid1
    a_idx = lax.broadcasted_iota(I32, (tm, tm), 0)
    b_idx = lax.broadcasted_iota(I32, (tm, tm), 1)
    before = (a_idx < b_idx).astype(BF16)
    r0 = _dot(oh0.astype(BF16), before)
    r1 = _dot(oh1.astype(BF16), before)
    cnt0 = jnp.sum(oh0.astype(F32), axis=1, keepdims=True)
    lower = jnp.sum((rowl > eid0).astype(F32) + (rowl > eid1).astype(F32),
                    axis=1, keepdims=True)
    lpos0 = jnp.sum(jnp.where(oh0, lower + r0, 0.0), axis=0, keepdims=True)
    lpos1 = jnp.sum(jnp.where(oh1, lower + cnt0 + r1, 0.0), axis=0, keepdims=True)

    ri_ref[0] = jnp.zeros((8, tm), I32)
    ri_ref[0, 0:1, :] = lpos0.astype(I32)
    ri_ref[0, 1:2, :] = lpos1.astype(I32)

    wrows = jnp.where(rowl == 0, gw0, jnp.where(rowl == 1, gw1,
                      jnp.where(rowl == 2, lpos0, jnp.where(rowl == 3, lpos1, 0.0))))
    wcol_ref[0] = wrows.T

    both = jnp.logical_or(oh0, oh1).astype(BF16)
    ones = jnp.ones((8, tm), BF16)
    cnt_ref[0, 0] = lax.dot_general(ones, both, (((1,), (1,)), ((), ())),
                                    preferred_element_type=F32)


def _mixer(x, g0, b0, w_in, b_in, conv_w, pool_w, pool_scale, w_o, g1, b1, w_r, b_r, *,
           apply_ln0, alpha, n_groups, per_group, tm):
    bsz, seq, d = x.shape
    nt = seq // tm
    hb = tm // HALO
    nhb = seq // HALO
    d_in = w_in.shape[1]
    d_conv = conv_w.shape[1]
    full = lambda *shape: pl.BlockSpec(shape, lambda b, i: (0,) * len(shape))
    body = functools.partial(_mixer_body, apply_ln0=apply_ln0, alpha=alpha, n_groups=n_groups,
                             per_group=per_group, seq=seq)
    return pl.pallas_call(
        body,
        grid=(bsz, nt),
        in_specs=[
            pl.BlockSpec((1, tm, d), lambda b, i: (b, i, 0)),
            pl.BlockSpec((1, HALO, d), lambda b, i: (b, jnp.maximum(i * hb - 1, 0), 0)),
            pl.BlockSpec((1, HALO, d), lambda b, i: (b, jnp.minimum((i + 1) * hb, nhb - 1), 0)),
            full(1, d), full(1, d),
            full(d, d_in), full(1, d_in),
            full(CONV_WIDTH, d_conv),
            full(*pool_w.shape), full(1, pool_scale.shape[1]),
            full(d, d), full(1, d), full(1, d),
            full(d, 2 * LANES), full(1, LANES),
        ],
        out_specs=[
            pl.BlockSpec((1, tm, d), lambda b, i: (b, i, 0)),
            pl.BlockSpec((1, 8, tm), lambda b, i: (b, 0, i)),
            pl.BlockSpec((1, tm, LANES), lambda b, i: (b, i, 0)),
            pl.BlockSpec((1, 1, 8, LANES), lambda b, i: (b, i, 0, 0)),
        ],
        out_shape=[
            jax.ShapeDtypeStruct((bsz, seq, d), F32),
            jax.ShapeDtypeStruct((bsz, 8, seq), I32),
            jax.ShapeDtypeStruct((bsz, seq, LANES), F32),
            jax.ShapeDtypeStruct((bsz, nt, 8, LANES), F32),
        ],
        compiler_params=pltpu.CompilerParams(
            dimension_semantics=("arbitrary", "arbitrary"), vmem_limit_bytes=VMEM_LIMIT),
        name="mixer",
    )(x, x, x, g0, b0, w_in, b_in, conv_w, pool_w, pool_scale, w_o, g1, b1, w_r, b_r)


def _tables_body(cnt_ref, gstart_ref, lstart_ref, len_ref):
    c = cnt_ref[...]
    n_tiles = c.shape[0]
    row = lax.broadcasted_iota(I32, c.shape, 0)
    lane = lax.broadcasted_iota(I32, c.shape, 1)

    def prefix(v, idx, size, axis):
        k = 1
        while k < size:
            v = v + jnp.where(idx >= k, pltpu.roll(v, k, axis), 0.0)
            k *= 2
        return v

    over_tiles = prefix(c, row, n_tiles, 0)
    tot = jnp.broadcast_to(over_tiles[n_tiles - 1:n_tiles], c.shape)
    expert_start = prefix(tot, lane, LANES, 1) - tot
    gstart_ref[...] = (expert_start + over_tiles - c).astype(I32)
    lstart_ref[...] = (prefix(c, lane, LANES, 1) - c).astype(I32)
    len_ref[...] = c.astype(I32)


def _run_tables(cnt, n_experts):
    out = jax.ShapeDtypeStruct(cnt.shape, I32)
    tabs = pl.pallas_call(_tables_body, out_shape=[out, out, out], name="run_tables")(cnt)
    return tuple(t[:, :n_experts].reshape(-1) for t in tabs)


RUN_CHUNK_LOG2 = 6


def _slots(ref, first, n, rps):
    start = first * rps
    if not isinstance(start, int):
        start = pl.multiple_of(start, rps)
    return ref.at[pl.ds(start, n * rps)]


def _start_run_copies(tabs, tile, n_experts, make_copy):
    gstart_ref, lstart_ref, len_ref = tabs
    big = 1 << RUN_CHUNK_LOG2

    def per_expert(e, carry):
        idx = tile * n_experts + e
        n, ls, gs = len_ref[idx], lstart_ref[idx], gstart_ref[idx]
        n_big = n >> RUN_CHUNK_LOG2

        def big_piece(c, carry2):
            make_copy(ls + c * big, gs + c * big, big).start()
            return carry2
        lax.fori_loop(0, n_big, big_piece, 0)
        off = n_big * big
        for bit in reversed(range(RUN_CHUNK_LOG2)):
            size = 1 << bit

            @pl.when((n & size) != 0)
            def _(off=off, size=size, bit=bit):
                make_copy(ls + off, gs + off, size).start(priority=bit % 2)
            off = off + (n & size)
        return carry
    lax.fori_loop(0, n_experts, per_expert, 0)


def _ple_body(gstart_ref, lstart_ref, len_ref, x1_ref, p_ref, ri_ref, wg_ref, bg_ref, wp_ref,
              ple_ref, xs_ref, buf_ref, sem, *, n_experts):
    tile = pl.program_id(0) * pl.num_programs(1) + pl.program_id(1)
    tabs = (gstart_ref, lstart_ref, len_ref)
    tm, d = x1_ref.shape[1:]
    rps = d // 2 // LANES

    n_tiles = pl.num_programs(0) * pl.num_programs(1)
    par = lax.rem(tile, 2)

    def copier(which):
        def make_copy(local, glob, n):
            return pltpu.make_async_copy(_slots(buf_ref.at[which], local, n, rps),
                                         _slots(xs_ref, glob, n, rps), sem.at[which])
        return make_copy

    def wait_all(which):
        copier(which)(0, 0, TOP_K * tm).wait()

    @pl.when(tile >= 2)
    def _():
        wait_all(par)

    x1 = x1_ref[0]
    lpos = ri_ref[0]
    s_idx = lax.broadcasted_iota(I32, (TOP_K * tm, tm), 0)
    perm = jnp.logical_or(s_idx == lpos[0:1], s_idx == lpos[1:2]).astype(BF16)
    xb = x1.astype(BF16)
    sorted_rows = _pack_exact_bf16_pairs(_dot(perm, xb))
    gate = jax.nn.sigmoid(_dot(xb, wg_ref[...]) + bg_ref[...])
    ple_ref[0] = gate * _dot(p_ref[0].astype(BF16), wp_ref[...])
    _store_slot_rows(buf_ref.at[par], sorted_rows)
    _start_run_copies(tabs, tile, n_experts, copier(par))

    @pl.when(tile == n_tiles - 1)
    def _():
        wait_all(par)

        @pl.when(tile >= 1)
        def _():
            wait_all(1 - par)


def _ple_dispatch(tabs, x1, p, ri, w_gate, b_gate, w_proj, *, layer, tm, n_slots, n_experts):
    bsz, seq, d = x1.shape
    nt = seq // tm
    d_ple = p.shape[-1]
    rps = d // 2 // LANES
    full = lambda *shape: pl.BlockSpec(shape, lambda b, i, *_: (0,) * len(shape))
    grid_spec = pltpu.PrefetchScalarGridSpec(
        num_scalar_prefetch=3,
        grid=(bsz, nt),
        in_specs=[
            pl.BlockSpec((1, tm, d), lambda b, i, *_: (b, i, 0)),
            pl.BlockSpec((None, 1, tm, d_ple), lambda b, i, *_: (layer, b, i, 0)),
            pl.BlockSpec((1, 8, tm), lambda b, i, *_: (b, 0, i)),
            full(d, d), full(1, d), full(d_ple, d),
        ],
        out_specs=[
            pl.BlockSpec((1, tm, d), lambda b, i, *_: (b, i, 0)),
            pl.BlockSpec(memory_space=pl.ANY),
        ],
        scratch_shapes=[pltpu.VMEM((2, TOP_K * tm * rps, LANES), U32), pltpu.SemaphoreType.DMA((2,))],
    )
    return pl.pallas_call(
        functools.partial(_ple_body, n_experts=n_experts),
        grid_spec=grid_spec,
        out_shape=[
            jax.ShapeDtypeStruct((bsz, seq, d), F32),
            jax.ShapeDtypeStruct((n_slots * rps, LANES), U32),
        ],
        compiler_params=pltpu.CompilerParams(
            dimension_semantics=("arbitrary", "arbitrary"), vmem_limit_bytes=VMEM_LIMIT),
        name="ple_dispatch",
    )(*tabs, x1, p, ri, w_gate, b_gate, w_proj)


def _experts_body(blk_ref, exp_ref, lo_ref, new_ref, par_ref, nxt_ref, xs_ref, w1_ref, w3_ref, w2_ref,
                  ys_ref, f1_ref, f3_ref, f2_ref, w1b_ref, w3b_ref, w2b_ref, sem, *, layer):
    it = pl.program_id(0)
    lo = lo_ref[it]
    rps = xs_ref.shape[0] // SLOT_BLOCK

    def weight_copies(expert, slot):
        return [pltpu.make_async_copy(w_ref.at[layer, expert], f_ref.at[slot], sem.at[slot, k])
                for k, (w_ref, f_ref) in enumerate(((w1_ref, f1_ref), (w3_ref, f3_ref), (w2_ref, f2_ref)))]

    @pl.when(it == 0)
    def _():
        for c in weight_copies(exp_ref[0], 0):
            c.start()

    @pl.when(new_ref[it] == 1)
    def _():
        slot = par_ref[it]
        for c in weight_copies(exp_ref[it], slot):
            c.wait()
        w1b_ref[...] = f1_ref[slot].astype(BF16)
        w3b_ref[...] = f3_ref[slot].astype(BF16)
        w2b_ref[...] = f2_ref[slot].astype(BF16)

        @pl.when(nxt_ref[it] >= 0)
        def _():
            for c in weight_copies(nxt_ref[it], 1 - slot):
                c.start()

    @pl.when(lo < SLOT_BLOCK)
    def _():
        hi_half, lo_half = _unpack_bf16_pairs(_load_slot_rows(xs_ref, SLOT_BLOCK, rps))
        xb = jnp.concatenate([hi_half.astype(BF16), lo_half.astype(BF16)], axis=1)
        h = jax.nn.silu(_dot(xb, w1b_ref[...])) * _dot(xb, w3b_ref[...])
        y = _pack_bf16_pairs(_dot(h.astype(BF16), w2b_ref[...]))

        @pl.when(lo == 0)
        def _():
            _store_slot_rows(ys_ref, y)

        @pl.when(lo > 0)
        def _():
            rows = lax.broadcasted_iota(I32, y.shape, 0)
            _store_slot_rows(ys_ref, jnp.where(rows >= lo, y, _load_slot_rows(ys_ref, SLOT_BLOCK, rps)))


def _experts(items, xs, w1, w3, w2, *, layer):
    rows = xs.shape[0]
    _, _, d, d_e = w1.shape
    rps = d // 2 // LANES
    n_items = items[0].shape[0]
    grid_spec = pltpu.PrefetchScalarGridSpec(
        num_scalar_prefetch=len(items),
        grid=(n_items,),
        in_specs=[
            pl.BlockSpec((SLOT_BLOCK * rps, LANES), lambda i, blk, *_: (blk[i], 0)),
            pl.BlockSpec(memory_space=pl.ANY),
            pl.BlockSpec(memory_space=pl.ANY),
            pl.BlockSpec(memory_space=pl.ANY),
        ],
        out_specs=pl.BlockSpec((SLOT_BLOCK * rps, LANES), lambda i, blk, *_: (blk[i], 0)),
        scratch_shapes=[pltpu.VMEM((2, d, d_e), F32), pltpu.VMEM((2, d, d_e), F32),
                        pltpu.VMEM((2, d_e, d), F32),
                        pltpu.VMEM((d, d_e), BF16), pltpu.VMEM((d, d_e), BF16),
                        pltpu.VMEM((d_e, d), BF16), pltpu.SemaphoreType.DMA((2, 3))],
    )
    return pl.pallas_call(
        functools.partial(_experts_body, layer=layer),
        grid_spec=grid_spec,
        out_shape=jax.ShapeDtypeStruct((rows, LANES), U32),
        compiler_params=pltpu.CompilerParams(
            dimension_semantics=("arbitrary",), vmem_limit_bytes=VMEM_LIMIT),
        name="experts",
    )(*items, xs, w1, w3, w2)


def _expert_items(cnt, n_experts, n_slots):
    n_blocks = n_slots // SLOT_BLOCK
    n_items = n_blocks + n_experts - 1
    tot = jnp.sum(cnt, axis=0)[:n_experts].astype(I32)
    end = jnp.cumsum(tot)
    start = end - tot
    first_blk = start // SLOT_BLOCK
    n_it = jnp.where(tot > 0, (end - 1) // SLOT_BLOCK - first_blk + 1, 0)
    it_end = jnp.cumsum(n_it)
    it_start = it_end - n_it
    idx = jnp.arange(n_items, dtype=I32)
    last_live = it_end[-1] - 1
    live = idx <= last_live
    at = jnp.minimum(idx, last_live)
    expert = jnp.sum(it_end[None, :] <= at[:, None], axis=1).astype(I32)
    onehot = expert[:, None] == jnp.arange(n_experts, dtype=I32)[None, :]
    pick = lambda v: jnp.sum(jnp.where(onehot, v[None, :], 0), axis=1)
    block = pick(first_blk) + idx - pick(it_start)
    item_block = jnp.where(live, block, n_blocks - 1).astype(I32)
    item_lo = jnp.where(live, jnp.maximum(pick(start) - block * SLOT_BLOCK, 0), SLOT_BLOCK).astype(I32)
    prev_expert = jnp.concatenate([jnp.full((1,), -1, I32), expert[:-1]])
    is_new = expert != prev_expert
    slot = (jnp.cumsum(is_new.astype(I32)) - 1) % 2
    later_new = lax.cummin(jnp.where(is_new, idx, n_items)[::-1], axis=0)[::-1]
    next_new = jnp.concatenate([later_new[1:], jnp.full((1,), n_items, I32)])
    next_expert = jnp.where(next_new < n_items, expert[jnp.minimum(next_new, n_items - 1)], -1)
    return (item_block, expert, item_lo, is_new.astype(I32), slot.astype(I32), next_expert.astype(I32))


def _combine_body(gstart_ref, lstart_ref, len_ref, x1_ref, ple_ref, wcol_ref, ys_ref, g2_ref, b2_ref,
                  out_ref, buf_ref, sem, *, alpha, n_experts):
    tile = pl.program_id(0) * pl.num_programs(1) + pl.program_id(1)
    tabs = (gstart_ref, lstart_ref, len_ref)
    tm, d = x1_ref.shape[1:]
    rps = d // 2 // LANES

    n_tiles = pl.num_programs(0) * pl.num_programs(1)
    par = lax.rem(tile, 2)

    def copier(which):
        def make_copy(local, glob, n):
            return pltpu.make_async_copy(_slots(ys_ref, glob, n, rps),
                                         _slots(buf_ref.at[which], local, n, rps), sem.at[which])
        return make_copy

    @pl.when(tile == 0)
    def _():
        _start_run_copies(tabs, tile, n_experts, copier(par))

    @pl.when(tile + 1 < n_tiles)
    def _():
        _start_run_copies(tabs, tile + 1, n_experts, copier(1 - par))
    copier(par)(0, 0, TOP_K * tm).wait()

    hi_half, lo_half = _unpack_bf16_pairs(_load_slot_rows(buf_ref.at[par], TOP_K * tm, rps))
    ys = jnp.concatenate([hi_half.astype(BF16), lo_half.astype(BF16)], axis=1)
    wcol = wcol_ref[0]
    s_idx = lax.broadcasted_iota(I32, (tm, TOP_K * tm), 1)
    select = jnp.zeros((tm, TOP_K * tm), F32)
    for kk in range(TOP_K):
        at_kk = s_idx == wcol[:, TOP_K + kk:TOP_K + kk + 1].astype(I32)
        select = jnp.where(at_kk, wcol[:, kk:kk + 1], select)
    ffn = _dot(select.astype(BF16), ys)
    out_ref[0] = _layernorm(alpha * x1_ref[0] + ffn + ple_ref[0], g2_ref[...], b2_ref[...])


def _combine(tabs, x1, ple, wcol, ys, g2, b2, *, alpha, tm, n_experts):
    bsz, seq, d = x1.shape
    nt = seq // tm
    rps = d // 2 // LANES
    full = lambda *shape: pl.BlockSpec(shape, lambda b, i, *_: (0,) * len(shape))
    grid_spec = pltpu.PrefetchScalarGridSpec(
        num_scalar_prefetch=3,
        grid=(bsz, nt),
        in_specs=[
            pl.BlockSpec((1, tm, d), lambda b, i, *_: (b, i, 0)),
            pl.BlockSpec((1, tm, d), lambda b, i, *_: (b, i, 0)),
            pl.BlockSpec((1, tm, LANES), lambda b, i, *_: (b, i, 0)),
            pl.BlockSpec(memory_space=pl.ANY),
            full(1, d), full(1, d),
        ],
        out_specs=pl.BlockSpec((1, tm, d), lambda b, i, *_: (b, i, 0)),
        scratch_shapes=[pltpu.VMEM((2, TOP_K * tm * rps, LANES), U32), pltpu.SemaphoreType.DMA((2,))],
    )
    return pl.pallas_call(
        functools.partial(_combine_body, alpha=alpha, n_experts=n_experts),
        grid_spec=grid_spec,
        out_shape=jax.ShapeDtypeStruct((bsz, seq, d), F32),
        compiler_params=pltpu.CompilerParams(
            dimension_semantics=("arbitrary", "arbitrary"), vmem_limit_bytes=VMEM_LIMIT),
        name="combine",
    )(*tabs, x1, ple, wcol, ys, g2, b2)


def _router_weights(w_rg, b_rg, w_re, b_re):
    d, n_groups = w_rg.shape
    per_group = w_re.shape[2]
    assert n_groups <= 8 and per_group <= 8 and 8 + 8 * n_groups <= LANES
    w_e = jnp.pad(jnp.transpose(w_re, (1, 0, 2)), ((0, 0), (0, 0), (0, 8 - per_group)))
    b_e = jnp.pad(b_re, ((0, 0), (0, 8 - per_group)))
    tail = LANES - 8 - 8 * n_groups
    w = jnp.concatenate([jnp.pad(w_rg, ((0, 0), (0, 8 - n_groups))), w_e.reshape(d, 8 * n_groups),
                         jnp.zeros((d, tail), F32)], axis=1)
    b = jnp.concatenate([jnp.pad(b_rg, (0, 8 - n_groups)), b_e.reshape(-1), jnp.zeros((tail,), F32)])
    w_hi = w.astype(BF16)
    w_lo = (w - w_hi.astype(F32)).astype(BF16)
    return jnp.concatenate([w_hi, w_lo], axis=1), b[None, :]


def kernel(x, p, ln0_g, ln0_b, w_in, b_in, conv_w, pool_w, pool_scale, w_o, ln1_g, ln1_b,
           w_router_group, b_router_group, w_router_expert, b_router_expert, w1, w3, w2,
           w_ple_gate, b_ple_gate, w_ple_proj, ln2_g, ln2_b):
    bsz, seq, d = x.shape
    depth = w_in.shape[0]
    n_groups, per_group = w_router_expert.shape[1], w_router_expert.shape[3]
    n_experts = n_groups * per_group
    alpha = (2 * depth) ** 0.25
    tm = min(512, seq)
    n_slots = bsz * seq * TOP_K
    assert seq % tm == 0 and tm % LANES == 0 and n_slots % SLOT_BLOCK == 0 and n_experts <= LANES
    assert d % (2 * LANES) == 0
    row = lambda a: a[None, :]

    for i in range(depth):
        w_r, b_r = _router_weights(w_router_group[i], b_router_group[i],
                                   w_router_expert[i], b_router_expert[i])
        x1, ri, wcol, cnt = _mixer(
            x, row(ln0_g), row(ln0_b), w_in[i].astype(BF16), row(b_in[i]), conv_w[i],
            pool_w[i].astype(BF16), row(pool_scale[i]), w_o[i].astype(BF16),
            row(ln1_g[i]), row(ln1_b[i]), w_r, b_r,
            apply_ln0=(i == 0), alpha=alpha, n_groups=n_groups, per_group=per_group, tm=tm)
        cnt = cnt[:, :, 0, :].reshape(-1, LANES)
        tabs = _run_tables(cnt, n_experts)
        items = _expert_items(cnt, n_experts, n_slots)
        ple, xs = _ple_dispatch(tabs, x1, p, ri, w_ple_gate[i].astype(BF16), row(b_ple_gate[i]),
                                w_ple_proj[i].astype(BF16), layer=i, tm=tm, n_slots=n_slots,
                                n_experts=n_experts)
        ys = _experts(items, xs, w1, w3, w2, layer=i)
        x = _combine(tabs, x1, ple, wcol, ys, row(ln2_g[i]), row(ln2_b[i]), alpha=alpha, tm=tm,
                     n_experts=n_experts)
    return x
```

```python
import functools

import jax
import jax.numpy as jnp
from jax import lax
from jax.experimental import pallas as pl
from jax.experimental.pallas import tpu as pltpu

POOL_WINDOWS = (2, 4, 8, 16)
CONV_WIDTH = 3
TOP_K = 2
LN_EPS = 1e-5
HALO = 8
LANES = 128
SORT_ROWS = 512
MIX_ROWS = 1024
MIX_CHAINS = 2
SLOT_BLOCK = 512
VMEM_LIMIT = 56 * 1024 * 1024

F32 = jnp.float32
BF16 = jnp.bfloat16
I32 = jnp.int32
U32 = jnp.uint32


def _layernorm(v, g, b):
    mu = jnp.mean(v, axis=-1, keepdims=True)
    c = v - mu
    var = jnp.mean(c * c, axis=-1, keepdims=True)
    return c * lax.rsqrt(var + LN_EPS) * g + b


def _dot(a, b):
    return jnp.dot(a, b, preferred_element_type=F32)


def _pack_bf16_pairs(v):
    c = v.shape[1] // 2
    hi = pltpu.bitcast(v[:, :c].astype(BF16).astype(F32), U32)
    lo = pltpu.bitcast(v[:, c:].astype(BF16).astype(F32), U32)
    return hi | (lo >> 16)


def _pack_exact_bf16_pairs(v):
    c = v.shape[1] // 2
    return pltpu.bitcast(v[:, :c], U32) | (pltpu.bitcast(v[:, c:], U32) >> 16)


def _unpack_bf16_pairs(u):
    hi = pltpu.bitcast(u & jnp.uint32(0xFFFF0000), F32)
    lo = pltpu.bitcast(u << 16, F32)
    return hi, lo


def _store_slot_rows(ref, packed):
    n, c = packed.shape
    rps = c // LANES
    for k in range(rps):
        ref[pl.ds(k, n, stride=rps), :] = packed[:, k * LANES:(k + 1) * LANES]


def _load_slot_rows(ref, n, rps):
    return jnp.concatenate([ref[pl.ds(k, n, stride=rps), :] for k in range(rps)], axis=1)


def _mixer_body(x_ref, xp_ref, xn_ref, g0_ref, b0_ref, win_ref, bin_ref, cw_ref, pw_ref, ps_ref,
                wo_ref, g1_ref, b1_ref, wr_ref, br_ref,
                x1_ref, ri_ref, wcol_ref, cnt_ref, *, apply_ln0, alpha, n_groups, per_group, seq,
                sort_rows):
    i = pl.program_id(1)
    nt = pl.num_programs(1)
    tm = x_ref.shape[1]
    n = tm // MIX_CHAINS
    rows = n + 2 * HALO
    d_conv = cw_ref.shape[1]
    dg = pw_ref.shape[1]

    xe = jnp.concatenate([xp_ref[0], x_ref[0], xn_ref[0]], axis=0)
    if apply_ln0:
        xe = _layernorm(xe, g0_ref[...], b0_ref[...])
    cw = cw_ref[...]
    ps = ps_ref[...]

    def in_proj(s):
        xs = xe[s * n:s * n + rows]
        return xs, _dot(xs.astype(BF16), win_ref[...]) + bin_ref[...]

    def token_mix(s, z):
        r = lax.broadcasted_iota(I32, (rows, 1), 0)
        lo_ok = jnp.logical_or(r >= HALO, i > 0) if s == 0 else True
        hi_ok = jnp.logical_or(r < n + HALO, i < nt - 1) if s == MIX_CHAINS - 1 else True
        if s == 0 or s == MIX_CHAINS - 1:
            valid = jnp.logical_and(lo_ok, hi_ok)
            mask = lambda a: jnp.where(valid, a, 0.0)
        else:
            mask = lambda a: a

        h = z[:, :d_conv]
        gate_b = z[HALO:HALO + n, d_conv:2 * d_conv]
        gate_c = z[:, 2 * d_conv:3 * d_conv]
        v = mask(gate_c * h)
        conv = (cw[0:1] * pltpu.roll(v, 1, 0)[HALO:HALO + n]
                + cw[1:2] * v[HALO:HALO + n]
                + cw[2:3] * pltpu.roll(v, rows - 1, 0)[HALO:HALO + n])
        parts = [gate_b * conv]

        pos = i * tm + s * n + r[HALO:HALO + n] - HALO
        for g, w in enumerate(POOL_WINDOWS):
            left = w // 2
            right = w - 1 - left
            c0 = 3 * d_conv + g * dg
            ug = mask(z[:, c0:c0 + dg])
            acc = ug
            k = 1
            while k < w:
                acc = acc + pltpu.roll(acc, k, 0)
                k *= 2
            if right:
                acc = pltpu.roll(acc, rows - right, 0)

            def edge(lo):
                p = pos[lo:lo + HALO]
                cnt = (jnp.minimum(p + right + 1, seq) - jnp.maximum(p - left, 0)).astype(F32)
                return acc[HALO + lo:2 * HALO + lo] / cnt - ug[HALO + lo:2 * HALO + lo]
            inner = acc[2 * HALO:n] * (1.0 / w) - ug[2 * HALO:n]
            pooled = jnp.concatenate([edge(0), inner, edge(n - HALO)], axis=0)
            parts.append(_dot(pooled.astype(BF16), pw_ref[g]) * ps[:, g * dg:(g + 1) * dg])
        return jnp.concatenate(parts, axis=1).astype(BF16)

    def out_proj(xs, cat):
        mix = _dot(cat, wo_ref[...])
        return _layernorm(alpha * xs[HALO:HALO + n] + mix, g1_ref[...], b1_ref[...])

    def route(x1s):
        xh = x1s.astype(BF16)
        xl = (x1s - xh.astype(F32)).astype(BF16)
        l_hi = _dot(xh, wr_ref[...])
        l_lo = _dot(xl, wr_ref[:, :LANES])
        return l_hi[:, :LANES] + l_hi[:, LANES:] + l_lo + br_ref[...]

    chains = [in_proj(s) for s in range(MIX_CHAINS)]
    x1s = [out_proj(xs, token_mix(s, z)) for s, (xs, z) in enumerate(chains)]
    x1 = jnp.concatenate(x1s, axis=0)
    x1_ref[0] = x1
    logits = jnp.concatenate([route(v) for v in x1s], axis=0)
    ri_ref[0] = jnp.zeros((8, tm), I32)
    for q in range(tm // sort_rows):
        rows_q = slice(q * sort_rows, (q + 1) * sort_rows)
        lpos0, lpos1, gw0, gw1, counts = _route_and_sort(logits[rows_q], n_groups, per_group)
        ri_ref[0, 0:1, rows_q] = lpos0.astype(I32)
        ri_ref[0, 1:2, rows_q] = lpos1.astype(I32)
        rowl = lax.broadcasted_iota(I32, (LANES, sort_rows), 0)
        wrows = jnp.where(rowl == 0, gw0, jnp.where(rowl == 1, gw1,
                          jnp.where(rowl == 2, lpos0, jnp.where(rowl == 3, lpos1, 0.0))))
        wcol_ref[0, rows_q, :] = wrows.T
        cnt_ref[0, q] = counts


def _route_and_sort(logits, n_groups, per_group):
    n = logits.shape[0]
    lt = logits.T
    row8 = lax.broadcasted_iota(I32, (8, n), 0)
    neg = jnp.float32(-jnp.inf)
    lg = jnp.where(row8 < n_groups, lt[0:8], neg)
    mg = jnp.max(lg, axis=0, keepdims=True)
    g_w = 1.0 / jnp.sum(jnp.exp(lg - mg), axis=0, keepdims=True)
    g_sel = jnp.min(jnp.where(lg == mg, row8, 8), axis=0, keepdims=True)
    le = lt[8:16]
    for g in range(1, n_groups):
        le = jnp.where(g_sel == g, lt[8 + 8 * g:16 + 8 * g], le)
    le = jnp.where(row8 < per_group, le, neg)
    v1 = jnp.max(le, axis=0, keepdims=True)
    i1 = jnp.min(jnp.where(le == v1, row8, 8), axis=0, keepdims=True)
    le2 = jnp.where(row8 == i1, neg, le)
    v2 = jnp.max(le2, axis=0, keepdims=True)
    i2 = jnp.min(jnp.where(le2 == v2, row8, 8), axis=0, keepdims=True)
    e21 = jnp.exp(v2 - v1)
    w_first = 1.0 / (1.0 + e21)
    gw0 = w_first * g_w
    gw1 = (e21 * w_first) * g_w
    eid0 = g_sel * per_group + i1
    eid1 = g_sel * per_group + i2

    rowl = lax.broadcasted_iota(I32, (LANES, n), 0)
    oh0 = rowl == eid0
    oh1 = rowl == eid1
    a_idx = lax.broadcasted_iota(I32, (n, n), 0)
    b_idx = lax.broadcasted_iota(I32, (n, n), 1)
    before = (a_idx < b_idx).astype(BF16)
    r0 = _dot(oh0.astype(BF16), before)
    r1 = _dot(oh1.astype(BF16), before)
    cnt0 = jnp.sum(oh0.astype(F32), axis=1, keepdims=True)
    lower = jnp.sum((rowl > eid0).astype(F32) + (rowl > eid1).astype(F32),
                    axis=1, keepdims=True)
    lpos0 = jnp.sum(jnp.where(oh0, lower + r0, 0.0), axis=0, keepdims=True)
    lpos1 = jnp.sum(jnp.where(oh1, lower + cnt0 + r1, 0.0), axis=0, keepdims=True)

    both = jnp.logical_or(oh0, oh1).astype(BF16)
    counts = lax.dot_general(jnp.ones((8, n), BF16), both, (((1,), (1,)), ((), ())),
                             preferred_element_type=F32)
    return lpos0, lpos1, gw0, gw1, counts


def _mixer(x, g0, b0, w_in, b_in, conv_w, pool_w, pool_scale, w_o, g1, b1, w_r, b_r, *,
           apply_ln0, alpha, n_groups, per_group, tm, sort_rows):
    bsz, seq, d = x.shape
    nt = seq // tm
    hb = tm // HALO
    nhb = seq // HALO
    d_in = w_in.shape[1]
    d_conv = conv_w.shape[1]
    full = lambda *shape: pl.BlockSpec(shape, lambda b, i: (0,) * len(shape))
    body = functools.partial(_mixer_body, apply_ln0=apply_ln0, alpha=alpha, n_groups=n_groups,
                             per_group=per_group, seq=seq, sort_rows=sort_rows)
    nq = tm // sort_rows
    return pl.pallas_call(
        body,
        grid=(bsz, nt),
        in_specs=[
            pl.BlockSpec((1, tm, d), lambda b, i: (b, i, 0)),
            pl.BlockSpec((1, HALO, d), lambda b, i: (b, jnp.maximum(i * hb - 1, 0), 0)),
            pl.BlockSpec((1, HALO, d), lambda b, i: (b, jnp.minimum((i + 1) * hb, nhb - 1), 0)),
            full(1, d), full(1, d),
            full(d, d_in), full(1, d_in),
            full(CONV_WIDTH, d_conv),
            full(*pool_w.shape), full(1, pool_scale.shape[1]),
            full(d, d), full(1, d), full(1, d),
            full(d, 2 * LANES), full(1, LANES),
        ],
        out_specs=[
            pl.BlockSpec((1, tm, d), lambda b, i: (b, i, 0)),
            pl.BlockSpec((1, 8, tm), lambda b, i: (b, 0, i)),
            pl.BlockSpec((1, tm, LANES), lambda b, i: (b, i, 0)),
            pl.BlockSpec((1, nq, 8, LANES), lambda b, i: (b, i, 0, 0)),
        ],
        out_shape=[
            jax.ShapeDtypeStruct((bsz, seq, d), F32),
            jax.ShapeDtypeStruct((bsz, 8, seq), I32),
            jax.ShapeDtypeStruct((bsz, seq, LANES), F32),
            jax.ShapeDtypeStruct((bsz, nt * nq, 8, LANES), F32),
        ],
        compiler_params=pltpu.CompilerParams(
            dimension_semantics=("arbitrary", "arbitrary"), vmem_limit_bytes=VMEM_LIMIT),
        name="mixer",
    )(x, x, x, g0, b0, w_in, b_in, conv_w, pool_w, pool_scale, w_o, g1, b1, w_r, b_r)


def _tables_body(cnt_ref, gstart_ref, lstart_ref, len_ref):
    c = cnt_ref[...]
    n_tiles = c.shape[0]
    row = lax.broadcasted_iota(I32, c.shape, 0)
    lane = lax.broadcasted_iota(I32, c.shape, 1)

    def prefix(v, idx, size, axis):
        k = 1
        while k < size:
            v = v + jnp.where(idx >= k, pltpu.roll(v, k, axis), 0.0)
            k *= 2
        return v

    over_tiles = prefix(c, row, n_tiles, 0)
    tot = jnp.broadcast_to(over_tiles[n_tiles - 1:n_tiles], c.shape)
    expert_start = prefix(tot, lane, LANES, 1) - tot
    gstart_ref[...] = (expert_start + over_tiles - c).astype(I32)
    lstart_ref[...] = (prefix(c, lane, LANES, 1) - c).astype(I32)
    len_ref[...] = c.astype(I32)


def _run_tables(cnt, n_experts):
    out = jax.ShapeDtypeStruct(cnt.shape, I32)
    tabs = pl.pallas_call(_tables_body, out_shape=[out, out, out], name="run_tables")(cnt)
    return tuple(t[:, :n_experts].reshape(-1) for t in tabs)


RUN_CHUNK_LOG2 = 6


def _slots(ref, first, n, rps):
    start = first * rps
    if not isinstance(start, int):
        start = pl.multiple_of(start, rps)
    return ref.at[pl.ds(start, n * rps)]


def _start_run_copies(tabs, tile, n_experts, make_copy):
    gstart_ref, lstart_ref, len_ref = tabs
    big = 1 << RUN_CHUNK_LOG2

    def per_expert(e, carry):
        idx = tile * n_experts + e
        n, ls, gs = len_ref[idx], lstart_ref[idx], gstart_ref[idx]
        n_big = n >> RUN_CHUNK_LOG2

        def big_piece(c, carry2):
            make_copy(ls + c * big, gs + c * big, big).start()
            return carry2
        lax.fori_loop(0, n_big, big_piece, 0)
        off = n_big * big
        for bit in reversed(range(RUN_CHUNK_LOG2)):
            size = 1 << bit

            @pl.when((n & size) != 0)
            def _(off=off, size=size, bit=bit):
                make_copy(ls + off, gs + off, size).start(priority=bit % 2)
            off = off + (n & size)
        return carry
    lax.fori_loop(0, n_experts, per_expert, 0)


def _dispatch_body(gstart_ref, lstart_ref, len_ref, x1_ref, ri_ref, xs_ref, buf_ref, sem, *, n_experts):
    tile = pl.program_id(0) * pl.num_programs(1) + pl.program_id(1)
    tabs = (gstart_ref, lstart_ref, len_ref)
    tm, d = x1_ref.shape[1:]
    rps = d // 2 // LANES

    n_tiles = pl.num_programs(0) * pl.num_programs(1)
    par = lax.rem(tile, 2)

    def copier(which):
        def make_copy(local, glob, n):
            return pltpu.make_async_copy(_slots(buf_ref.at[which], local, n, rps),
                                         _slots(xs_ref, glob, n, rps), sem.at[which])
        return make_copy

    def wait_all(which):
        copier(which)(0, 0, TOP_K * tm).wait()

    @pl.when(tile >= 2)
    def _():
        wait_all(par)

    lpos = ri_ref[0]
    s_idx = lax.broadcasted_iota(I32, (TOP_K * tm, tm), 0)
    perm = jnp.logical_or(s_idx == lpos[0:1], s_idx == lpos[1:2]).astype(BF16)
    sorted_rows = _pack_exact_bf16_pairs(_dot(perm, x1_ref[0].astype(BF16)))
    _store_slot_rows(buf_ref.at[par], sorted_rows)
    _start_run_copies(tabs, tile, n_experts, copier(par))

    @pl.when(tile == n_tiles - 1)
    def _():
        wait_all(par)

        @pl.when(tile >= 1)
        def _():
            wait_all(1 - par)


def _dispatch(tabs, x1, ri, *, tm, n_slots, n_experts):
    bsz, seq, d = x1.shape
    nt = seq // tm
    rps = d // 2 // LANES
    grid_spec = pltpu.PrefetchScalarGridSpec(
        num_scalar_prefetch=3,
        grid=(bsz, nt),
        in_specs=[
            pl.BlockSpec((1, tm, d), lambda b, i, *_: (b, i, 0)),
            pl.BlockSpec((1, 8, tm), lambda b, i, *_: (b, 0, i)),
        ],
        out_specs=pl.BlockSpec(memory_space=pl.ANY),
        scratch_shapes=[pltpu.VMEM((2, TOP_K * tm * rps, LANES), U32), pltpu.SemaphoreType.DMA((2,))],
    )
    return pl.pallas_call(
        functools.partial(_dispatch_body, n_experts=n_experts),
        grid_spec=grid_spec,
        out_shape=jax.ShapeDtypeStruct((n_slots * rps, LANES), U32),
        compiler_params=pltpu.CompilerParams(
            dimension_semantics=("arbitrary", "arbitrary"), vmem_limit_bytes=VMEM_LIMIT),
        name="dispatch",
    )(*tabs, x1, ri)


def _experts_body(blk_ref, exp_ref, lo_ref, new_ref, par_ref, nxt_ref, xs_ref, w1_ref, w3_ref, w2_ref,
                  ys_ref, f1_ref, f3_ref, f2_ref, w1b_ref, w3b_ref, w2b_ref, sem, *, layer):
    it = pl.program_id(0)
    lo = lo_ref[it]
    rps = xs_ref.shape[0] // SLOT_BLOCK

    def weight_copies(expert, slot):
        return [pltpu.make_async_copy(w_ref.at[layer, expert], f_ref.at[slot], sem.at[slot, k])
                for k, (w_ref, f_ref) in enumerate(((w1_ref, f1_ref), (w3_ref, f3_ref), (w2_ref, f2_ref)))]

    @pl.when(it == 0)
    def _():
        for c in weight_copies(exp_ref[0], 0):
            c.start()

    @pl.when(new_ref[it] == 1)
    def _():
        slot = par_ref[it]
        for c in weight_copies(exp_ref[it], slot):
            c.wait()
        w1b_ref[...] = f1_ref[slot].astype(BF16)
        w3b_ref[...] = f3_ref[slot].astype(BF16)
        w2b_ref[...] = f2_ref[slot].astype(BF16)

        @pl.when(nxt_ref[it] >= 0)
        def _():
            for c in weight_copies(nxt_ref[it], 1 - slot):
                c.start()

    @pl.when(lo < SLOT_BLOCK)
    def _():
        hi_half, lo_half = _unpack_bf16_pairs(_load_slot_rows(xs_ref, SLOT_BLOCK, rps))
        xb = jnp.concatenate([hi_half.astype(BF16), lo_half.astype(BF16)], axis=1)
        h = jax.nn.silu(_dot(xb, w1b_ref[...])) * _dot(xb, w3b_ref[...])
        y = _pack_bf16_pairs(_dot(h.astype(BF16), w2b_ref[...]))

        @pl.when(lo == 0)
        def _():
            _store_slot_rows(ys_ref, y)

        @pl.when(lo > 0)
        def _():
            rows = lax.broadcasted_iota(I32, y.shape, 0)
            _store_slot_rows(ys_ref, jnp.where(rows >= lo, y, _load_slot_rows(ys_ref, SLOT_BLOCK, rps)))


def _experts(items, xs, w1, w3, w2, *, layer):
    rows = xs.shape[0]
    _, _, d, d_e = w1.shape
    rps = d // 2 // LANES
    n_items = items[0].shape[0]
    grid_spec = pltpu.PrefetchScalarGridSpec(
        num_scalar_prefetch=len(items),
        grid=(n_items,),
        in_specs=[
            pl.BlockSpec((SLOT_BLOCK * rps, LANES), lambda i, blk, *_: (blk[i], 0)),
            pl.BlockSpec(memory_space=pl.ANY),
            pl.BlockSpec(memory_space=pl.ANY),
            pl.BlockSpec(memory_space=pl.ANY),
        ],
        out_specs=pl.BlockSpec((SLOT_BLOCK * rps, LANES), lambda i, blk, *_: (blk[i], 0)),
        scratch_shapes=[pltpu.VMEM((2, d, d_e), F32), pltpu.VMEM((2, d, d_e), F32),
                        pltpu.VMEM((2, d_e, d), F32),
                        pltpu.VMEM((d, d_e), BF16), pltpu.VMEM((d, d_e), BF16),
                        pltpu.VMEM((d_e, d), BF16), pltpu.SemaphoreType.DMA((2, 3))],
    )
    return pl.pallas_call(
        functools.partial(_experts_body, layer=layer),
        grid_spec=grid_spec,
        out_shape=jax.ShapeDtypeStruct((rows, LANES), U32),
        compiler_params=pltpu.CompilerParams(
            dimension_semantics=("arbitrary",), vmem_limit_bytes=VMEM_LIMIT),
        name="experts",
    )(*items, xs, w1, w3, w2)


def _expert_items(cnt, n_experts, n_slots):
    n_blocks = n_slots // SLOT_BLOCK
    n_items = n_blocks + n_experts - 1
    tot = jnp.sum(cnt, axis=0)[:n_experts].astype(I32)
    end = jnp.cumsum(tot)
    start = end - tot
    first_blk = start // SLOT_BLOCK
    n_it = jnp.where(tot > 0, (end - 1) // SLOT_BLOCK - first_blk + 1, 0)
    it_end = jnp.cumsum(n_it)
    it_start = it_end - n_it
    idx = jnp.arange(n_items, dtype=I32)
    last_live = it_end[-1] - 1
    live = idx <= last_live
    at = jnp.minimum(idx, last_live)
    expert = jnp.sum(it_end[None, :] <= at[:, None], axis=1).astype(I32)
    onehot = expert[:, None] == jnp.arange(n_experts, dtype=I32)[None, :]
    pick = lambda v: jnp.sum(jnp.where(onehot, v[None, :], 0), axis=1)
    block = pick(first_blk) + idx - pick(it_start)
    item_block = jnp.where(live, block, n_blocks - 1).astype(I32)
    item_lo = jnp.where(live, jnp.maximum(pick(start) - block * SLOT_BLOCK, 0), SLOT_BLOCK).astype(I32)
    prev_expert = jnp.concatenate([jnp.full((1,), -1, I32), expert[:-1]])
    is_new = expert != prev_expert
    slot = (jnp.cumsum(is_new.astype(I32)) - 1) % 2
    later_new = lax.cummin(jnp.where(is_new, idx, n_items)[::-1], axis=0)[::-1]
    next_new = jnp.concatenate([later_new[1:], jnp.full((1,), n_items, I32)])
    next_expert = jnp.where(next_new < n_items, expert[jnp.minimum(next_new, n_items - 1)], -1)
    return (item_block, expert, item_lo, is_new.astype(I32), slot.astype(I32), next_expert.astype(I32))


def _combine_body(gstart_ref, lstart_ref, len_ref, x1_ref, p_ref, wcol_ref, ys_ref, wg_ref, bg_ref, wp_ref,
                  g2_ref, b2_ref, out_ref, buf_ref, sem, *, alpha, n_experts):
    tile = pl.program_id(0) * pl.num_programs(1) + pl.program_id(1)
    tabs = (gstart_ref, lstart_ref, len_ref)
    tm, d = x1_ref.shape[1:]
    rps = d // 2 // LANES

    n_tiles = pl.num_programs(0) * pl.num_programs(1)
    par = lax.rem(tile, 2)

    def copier(which):
        def make_copy(local, glob, n):
            return pltpu.make_async_copy(_slots(ys_ref, glob, n, rps),
                                         _slots(buf_ref.at[which], local, n, rps), sem.at[which])
        return make_copy

    @pl.when(tile == 0)
    def _():
        _start_run_copies(tabs, tile, n_experts, copier(par))

    @pl.when(tile + 1 < n_tiles)
    def _():
        _start_run_copies(tabs, tile + 1, n_experts, copier(1 - par))
    copier(par)(0, 0, TOP_K * tm).wait()

    hi_half, lo_half = _unpack_bf16_pairs(_load_slot_rows(buf_ref.at[par], TOP_K * tm, rps))
    ys = jnp.concatenate([hi_half.astype(BF16), lo_half.astype(BF16)], axis=1)
    wcol = wcol_ref[0]
    s_idx = lax.broadcasted_iota(I32, (tm, TOP_K * tm), 1)
    select = jnp.zeros((tm, TOP_K * tm), F32)
    for kk in range(TOP_K):
        at_kk = s_idx == wcol[:, TOP_K + kk:TOP_K + kk + 1].astype(I32)
        select = jnp.where(at_kk, wcol[:, kk:kk + 1], select)
    ffn = _dot(select.astype(BF16), ys)

    x1 = x1_ref[0]
    gate = jax.nn.sigmoid(_dot(x1.astype(BF16), wg_ref[...]) + bg_ref[...])
    ple = gate * _dot(p_ref[0].astype(BF16), wp_ref[...])
    out_ref[0] = _layernorm(alpha * x1 + ffn + ple, g2_ref[...], b2_ref[...])


def _combine(tabs, x1, p, wcol, ys, w_gate, b_gate, w_proj, g2, b2, *, layer, alpha, tm, n_experts):
    bsz, seq, d = x1.shape
    nt = seq // tm
    d_ple = p.shape[-1]
    rps = d // 2 // LANES
    full = lambda *shape: pl.BlockSpec(shape, lambda b, i, *_: (0,) * len(shape))
    grid_spec = pltpu.PrefetchScalarGridSpec(
        num_scalar_prefetch=3,
        grid=(bsz, nt),
        in_specs=[
            pl.BlockSpec((1, tm, d), lambda b, i, *_: (b, i, 0)),
            pl.BlockSpec((None, 1, tm, d_ple), lambda b, i, *_: (layer, b, i, 0)),
            pl.BlockSpec((1, tm, LANES), lambda b, i, *_: (b, i, 0)),
            pl.BlockSpec(memory_space=pl.ANY),
            full(d, d), full(1, d), full(d_ple, d),
            full(1, d), full(1, d),
        ],
        out_specs=pl.BlockSpec((1, tm, d), lambda b, i, *_: (b, i, 0)),
        scratch_shapes=[pltpu.VMEM((2, TOP_K * tm * rps, LANES), U32), pltpu.SemaphoreType.DMA((2,))],
    )
    return pl.pallas_call(
        functools.partial(_combine_body, alpha=alpha, n_experts=n_experts),
        grid_spec=grid_spec,
        out_shape=jax.ShapeDtypeStruct((bsz, seq, d), F32),
        compiler_params=pltpu.CompilerParams(
            dimension_semantics=("arbitrary", "arbitrary"), vmem_limit_bytes=VMEM_LIMIT),
        name="combine",
    )(*tabs, x1, p, wcol, ys, w_gate, b_gate, w_proj, g2, b2)


def _router_weights(w_rg, b_rg, w_re, b_re):
    d, n_groups = w_rg.shape
    per_group = w_re.shape[2]
    assert n_groups <= 8 and per_group <= 8 and 8 + 8 * n_groups <= LANES
    w_e = jnp.pad(jnp.transpose(w_re, (1, 0, 2)), ((0, 0), (0, 0), (0, 8 - per_group)))
    b_e = jnp.pad(b_re, ((0, 0), (0, 8 - per_group)))
    tail = LANES - 8 - 8 * n_groups
    w = jnp.concatenate([jnp.pad(w_rg, ((0, 0), (0, 8 - n_groups))), w_e.reshape(d, 8 * n_groups),
                         jnp.zeros((d, tail), F32)], axis=1)
    b = jnp.concatenate([jnp.pad(b_rg, (0, 8 - n_groups)), b_e.reshape(-1), jnp.zeros((tail,), F32)])
    w_hi = w.astype(BF16)
    w_lo = (w - w_hi.astype(F32)).astype(BF16)
    return jnp.concatenate([w_hi, w_lo], axis=1), b[None, :]


def kernel(x, p, ln0_g, ln0_b, w_in, b_in, conv_w, pool_w, pool_scale, w_o, ln1_g, ln1_b,
           w_router_group, b_router_group, w_router_expert, b_router_expert, w1, w3, w2,
           w_ple_gate, b_ple_gate, w_ple_proj, ln2_g, ln2_b):
    bsz, seq, d = x.shape
    depth = w_in.shape[0]
    n_groups, per_group = w_router_expert.shape[1], w_router_expert.shape[3]
    n_experts = n_groups * per_group
    alpha = (2 * depth) ** 0.25
    tm = min(SORT_ROWS, seq)
    tm_mix = min(MIX_ROWS, seq)
    n_slots = bsz * seq * TOP_K
    assert seq % tm_mix == 0 and tm_mix % tm == 0 and tm % LANES == 0
    assert n_slots % SLOT_BLOCK == 0 and n_experts <= LANES
    assert d % (2 * LANES) == 0
    row = lambda a: a[None, :]

    for i in range(depth):
        w_r, b_r = _router_weights(w_router_group[i], b_router_group[i],
                                   w_router_expert[i], b_router_expert[i])
        x1, ri, wcol, cnt = _mixer(
            x, row(ln0_g), row(ln0_b), w_in[i].astype(BF16), row(b_in[i]), conv_w[i],
            pool_w[i].astype(BF16), row(pool_scale[i]), w_o[i].astype(BF16),
            row(ln1_g[i]), row(ln1_b[i]), w_r, b_r,
            apply_ln0=(i == 0), alpha=alpha, n_groups=n_groups, per_group=per_group, tm=tm_mix,
            sort_rows=tm)
        cnt = cnt[:, :, 0, :].reshape(-1, LANES)
        tabs = _run_tables(cnt, n_experts)
        items = _expert_items(cnt, n_experts, n_slots)
        xs = _dispatch(tabs, x1, ri, tm=tm, n_slots=n_slots, n_experts=n_experts)
        ys = _experts(items, xs, w1, w3, w2, layer=i)
        x = _combine(tabs, x1, p, wcol, ys, w_ple_gate[i].astype(BF16), row(b_ple_gate[i]),
                     w_ple_proj[i].astype(BF16), row(ln2_g[i]), row(ln2_b[i]),
                     layer=i, alpha=alpha, tm=tm, n_experts=n_experts)
    return x
```

```python
import functools

import jax
import jax.numpy as jnp
from jax import lax
from jax.experimental import pallas as pl
from jax.experimental.pallas import tpu as pltpu

POOL_WINDOWS = (2, 4, 8, 16)
CONV_WIDTH = 3
TOP_K = 2
LN_EPS = 1e-5
HALO = 8
LANES = 128
SORT_ROWS = 512
MIX_ROWS = 1024
MIX_CHAINS = 2
COMBINE_CHAINS = 2
SLOT_BLOCK = 512
VMEM_LIMIT = 56 * 1024 * 1024

F32 = jnp.float32
BF16 = jnp.bfloat16
I32 = jnp.int32
U32 = jnp.uint32


def _layernorm(v, g, b):
    mu = jnp.mean(v, axis=-1, keepdims=True)
    c = v - mu
    var = jnp.mean(c * c, axis=-1, keepdims=True)
    return c * lax.rsqrt(var + LN_EPS) * g + b


def _dot(a, b):
    return jnp.dot(a, b, preferred_element_type=F32)


def _pack_bf16_pairs(v):
    c = v.shape[1] // 2
    hi = pltpu.bitcast(v[:, :c].astype(BF16).astype(F32), U32)
    lo = pltpu.bitcast(v[:, c:].astype(BF16).astype(F32), U32)
    return hi | (lo >> 16)


def _pack_exact_bf16_pairs(v):
    c = v.shape[1] // 2
    return pltpu.bitcast(v[:, :c], U32) | (pltpu.bitcast(v[:, c:], U32) >> 16)


def _unpack_bf16_pairs(u):
    hi = pltpu.bitcast(u & jnp.uint32(0xFFFF0000), F32)
    lo = pltpu.bitcast(u << 16, F32)
    return hi, lo


def _store_slot_rows(ref, packed, first=0):
    n, c = packed.shape
    rps = c // LANES
    for k in range(rps):
        ref[pl.ds(first * rps + k, n, stride=rps), :] = packed[:, k * LANES:(k + 1) * LANES]


def _load_slot_rows(ref, n, rps, first=0):
    return jnp.concatenate([ref[pl.ds(first * rps + k, n, stride=rps), :] for k in range(rps)], axis=1)


def _mixer_body(x_ref, xp_ref, xn_ref, g0_ref, b0_ref, win_ref, bin_ref, cw_ref, pw_ref, ps_ref,
                wo_ref, g1_ref, b1_ref, wr_ref, br_ref,
                x1_ref, ri_ref, wcol_ref, cnt_ref, *, apply_ln0, alpha, n_groups, per_group, seq,
                sort_rows):
    i = pl.program_id(1)
    nt = pl.num_programs(1)
    tm = x_ref.shape[1]
    n = tm // MIX_CHAINS
    rows = n + 2 * HALO
    d_conv = cw_ref.shape[1]
    dg = pw_ref.shape[1]

    xe = jnp.concatenate([xp_ref[0], x_ref[0], xn_ref[0]], axis=0)
    if apply_ln0:
        xe = _layernorm(xe, g0_ref[...], b0_ref[...])
    cw = cw_ref[...]
    ps = ps_ref[...]

    def in_proj(s):
        xs = xe[s * n:s * n + rows]
        return xs, _dot(xs.astype(BF16), win_ref[...]) + bin_ref[...]

    def token_mix(s, z):
        r = lax.broadcasted_iota(I32, (rows, 1), 0)
        lo_ok = jnp.logical_or(r >= HALO, i > 0) if s == 0 else True
        hi_ok = jnp.logical_or(r < n + HALO, i < nt - 1) if s == MIX_CHAINS - 1 else True
        if s == 0 or s == MIX_CHAINS - 1:
            valid = jnp.logical_and(lo_ok, hi_ok)
            mask = lambda a: jnp.where(valid, a, 0.0)
        else:
            mask = lambda a: a

        h = z[:, :d_conv]
        gate_b = z[HALO:HALO + n, d_conv:2 * d_conv]
        gate_c = z[:, 2 * d_conv:3 * d_conv]
        v = mask(gate_c * h)
        conv = (cw[0:1] * pltpu.roll(v, 1, 0)[HALO:HALO + n]
                + cw[1:2] * v[HALO:HALO + n]
                + cw[2:3] * pltpu.roll(v, rows - 1, 0)[HALO:HALO + n])
        parts = [gate_b * conv]

        pos = i * tm + s * n + r[HALO:HALO + n] - HALO
        for g, w in enumerate(POOL_WINDOWS):
            left = w // 2
            right = w - 1 - left
            c0 = 3 * d_conv + g * dg
            ug = mask(z[:, c0:c0 + dg])
            acc = ug
            k = 1
            while k < w:
                acc = acc + pltpu.roll(acc, k, 0)
                k *= 2
            if right:
                acc = pltpu.roll(acc, rows - right, 0)

            def edge(lo):
                p = pos[lo:lo + HALO]
                cnt = (jnp.minimum(p + right + 1, seq) - jnp.maximum(p - left, 0)).astype(F32)
                return acc[HALO + lo:2 * HALO + lo] / cnt - ug[HALO + lo:2 * HALO + lo]
            inner = acc[2 * HALO:n] * (1.0 / w) - ug[2 * HALO:n]
            pooled = jnp.concatenate([edge(0), inner, edge(n - HALO)], axis=0)
            parts.append(_dot(pooled.astype(BF16), pw_ref[g]) * ps[:, g * dg:(g + 1) * dg])
        return jnp.concatenate(parts, axis=1).astype(BF16)

    def out_proj(xs, cat):
        mix = _dot(cat, wo_ref[...])
        return _layernorm(alpha * xs[HALO:HALO + n] + mix, g1_ref[...], b1_ref[...])

    def route(x1s):
        xh = x1s.astype(BF16)
        xl = (x1s - xh.astype(F32)).astype(BF16)
        l_hi = _dot(xh, wr_ref[...])
        l_lo = _dot(xl, wr_ref[:, :LANES])
        return l_hi[:, :LANES] + l_hi[:, LANES:] + l_lo + br_ref[...]

    chains = [in_proj(s) for s in range(MIX_CHAINS)]
    x1s = [out_proj(xs, token_mix(s, z)) for s, (xs, z) in enumerate(chains)]
    x1 = jnp.concatenate(x1s, axis=0)
    x1_ref[0] = x1
    logits = jnp.concatenate([route(v) for v in x1s], axis=0)
    ri_ref[0] = jnp.zeros((8, tm), I32)
    for q in range(tm // sort_rows):
        rows_q = slice(q * sort_rows, (q + 1) * sort_rows)
        lpos0, lpos1, gw0, gw1, counts = _route_and_sort(logits[rows_q], n_groups, per_group)
        ri_ref[0, 0:1, rows_q] = lpos0.astype(I32)
        ri_ref[0, 1:2, rows_q] = lpos1.astype(I32)
        rowl = lax.broadcasted_iota(I32, (LANES, sort_rows), 0)
        wrows = jnp.where(rowl == 0, gw0, jnp.where(rowl == 1, gw1,
                          jnp.where(rowl == 2, lpos0, jnp.where(rowl == 3, lpos1, 0.0))))
        wcol_ref[0, rows_q, :] = wrows.T
        cnt_ref[0, q] = counts


def _route_and_sort(logits, n_groups, per_group):
    n = logits.shape[0]
    lt = logits.T
    row8 = lax.broadcasted_iota(I32, (8, n), 0)
    neg = jnp.float32(-jnp.inf)
    lg = jnp.where(row8 < n_groups, lt[0:8], neg)
    mg = jnp.max(lg, axis=0, keepdims=True)
    g_w = 1.0 / jnp.sum(jnp.exp(lg - mg), axis=0, keepdims=True)
    g_sel = jnp.min(jnp.where(lg == mg, row8, 8), axis=0, keepdims=True)
    le = lt[8:16]
    for g in range(1, n_groups):
        le = jnp.where(g_sel == g, lt[8 + 8 * g:16 + 8 * g], le)
    le = jnp.where(row8 < per_group, le, neg)
    v1 = jnp.max(le, axis=0, keepdims=True)
    i1 = jnp.min(jnp.where(le == v1, row8, 8), axis=0, keepdims=True)
    le2 = jnp.where(row8 == i1, neg, le)
    v2 = jnp.max(le2, axis=0, keepdims=True)
    i2 = jnp.min(jnp.where(le2 == v2, row8, 8), axis=0, keepdims=True)
    e21 = jnp.exp(v2 - v1)
    w_first = 1.0 / (1.0 + e21)
    gw0 = w_first * g_w
    gw1 = (e21 * w_first) * g_w
    eid0 = g_sel * per_group + i1
    eid1 = g_sel * per_group + i2

    rowl = lax.broadcasted_iota(I32, (LANES, n), 0)
    oh0 = rowl == eid0
    oh1 = rowl == eid1
    a_idx = lax.broadcasted_iota(I32, (n, n), 0)
    b_idx = lax.broadcasted_iota(I32, (n, n), 1)
    before = (a_idx < b_idx).astype(BF16)
    r0 = _dot(oh0.astype(BF16), before)
    r1 = _dot(oh1.astype(BF16), before)
    cnt0 = jnp.sum(oh0.astype(F32), axis=1, keepdims=True)
    lower = jnp.sum((rowl > eid0).astype(F32) + (rowl > eid1).astype(F32),
                    axis=1, keepdims=True)
    lpos0 = jnp.sum(jnp.where(oh0, lower + r0, 0.0), axis=0, keepdims=True)
    lpos1 = jnp.sum(jnp.where(oh1, lower + cnt0 + r1, 0.0), axis=0, keepdims=True)

    both = jnp.logical_or(oh0, oh1).astype(BF16)
    counts = lax.dot_general(jnp.ones((8, n), BF16), both, (((1,), (1,)), ((), ())),
                             preferred_element_type=F32)
    return lpos0, lpos1, gw0, gw1, counts


def _mixer(x, g0, b0, w_in, b_in, conv_w, pool_w, pool_scale, w_o, g1, b1, w_r, b_r, *,
           apply_ln0, alpha, n_groups, per_group, tm, sort_rows):
    bsz, seq, d = x.shape
    nt = seq // tm
    hb = tm // HALO
    nhb = seq // HALO
    d_in = w_in.shape[1]
    d_conv = conv_w.shape[1]
    full = lambda *shape: pl.BlockSpec(shape, lambda b, i: (0,) * len(shape))
    body = functools.partial(_mixer_body, apply_ln0=apply_ln0, alpha=alpha, n_groups=n_groups,
                             per_group=per_group, seq=seq, sort_rows=sort_rows)
    nq = tm // sort_rows
    return pl.pallas_call(
        body,
        grid=(bsz, nt),
        in_specs=[
            pl.BlockSpec((1, tm, d), lambda b, i: (b, i, 0)),
            pl.BlockSpec((1, HALO, d), lambda b, i: (b, jnp.maximum(i * hb - 1, 0), 0)),
            pl.BlockSpec((1, HALO, d), lambda b, i: (b, jnp.minimum((i + 1) * hb, nhb - 1), 0)),
            full(1, d), full(1, d),
            full(d, d_in), full(1, d_in),
            full(CONV_WIDTH, d_conv),
            full(*pool_w.shape), full(1, pool_scale.shape[1]),
            full(d, d), full(1, d), full(1, d),
            full(d, 2 * LANES), full(1, LANES),
        ],
        out_specs=[
            pl.BlockSpec((1, tm, d), lambda b, i: (b, i, 0)),
            pl.BlockSpec((1, 8, tm), lambda b, i: (b, 0, i)),
            pl.BlockSpec((1, tm, LANES), lambda b, i: (b, i, 0)),
            pl.BlockSpec((1, nq, 8, LANES), lambda b, i: (b, i, 0, 0)),
        ],
        out_shape=[
            jax.ShapeDtypeStruct((bsz, seq, d), F32),
            jax.ShapeDtypeStruct((bsz, 8, seq), I32),
            jax.ShapeDtypeStruct((bsz, seq, LANES), F32),
            jax.ShapeDtypeStruct((bsz, nt * nq, 8, LANES), F32),
        ],
        compiler_params=pltpu.CompilerParams(
            dimension_semantics=("arbitrary", "arbitrary"), vmem_limit_bytes=VMEM_LIMIT),
        name="mixer",
    )(x, x, x, g0, b0, w_in, b_in, conv_w, pool_w, pool_scale, w_o, g1, b1, w_r, b_r)


def _tables_body(cnt_ref, gstart_ref, lstart_ref, len_ref):
    c = cnt_ref[...]
    n_tiles = c.shape[0]
    row = lax.broadcasted_iota(I32, c.shape, 0)
    lane = lax.broadcasted_iota(I32, c.shape, 1)

    def prefix(v, idx, size, axis):
        k = 1
        while k < size:
            v = v + jnp.where(idx >= k, pltpu.roll(v, k, axis), 0.0)
            k *= 2
        return v

    over_tiles = prefix(c, row, n_tiles, 0)
    tot = jnp.broadcast_to(over_tiles[n_tiles - 1:n_tiles], c.shape)
    expert_start = prefix(tot, lane, LANES, 1) - tot
    gstart_ref[...] = (expert_start + over_tiles - c).astype(I32)
    lstart_ref[...] = (prefix(c, lane, LANES, 1) - c).astype(I32)
    len_ref[...] = c.astype(I32)


def _run_tables(cnt, n_experts):
    out = jax.ShapeDtypeStruct(cnt.shape, I32)
    tabs = pl.pallas_call(_tables_body, out_shape=[out, out, out], name="run_tables")(cnt)
    return tuple(t[:, :n_experts].reshape(-1) for t in tabs)


RUN_CHUNK_LOG2 = 6


def _slots(ref, first, n, rps):
    start = first * rps
    if not isinstance(start, int):
        start = pl.multiple_of(start, rps)
    return ref.at[pl.ds(start, n * rps)]


def _start_run_copies(tabs, tile, n_experts, make_copy):
    gstart_ref, lstart_ref, len_ref = tabs
    big = 1 << RUN_CHUNK_LOG2

    def per_expert(e, carry):
        idx = tile * n_experts + e
        n, ls, gs = len_ref[idx], lstart_ref[idx], gstart_ref[idx]
        n_big = n >> RUN_CHUNK_LOG2

        def big_piece(c, carry2):
            make_copy(ls + c * big, gs + c * big, big).start()
            return carry2
        lax.fori_loop(0, n_big, big_piece, 0)
        off = n_big * big
        for bit in reversed(range(RUN_CHUNK_LOG2)):
            size = 1 << bit

            @pl.when((n & size) != 0)
            def _(off=off, size=size, bit=bit):
                make_copy(ls + off, gs + off, size).start(priority=bit % 2)
            off = off + (n & size)
        return carry
    lax.fori_loop(0, n_experts, per_expert, 0)


def _dispatch_body(gstart_ref, lstart_ref, len_ref, x1_ref, ri_ref, xs_ref, buf_ref, sem, *, n_experts):
    tile = pl.program_id(0) * pl.num_programs(1) + pl.program_id(1)
    tabs = (gstart_ref, lstart_ref, len_ref)
    tm, d = x1_ref.shape[1:]
    rps = d // 2 // LANES

    n_tiles = pl.num_programs(0) * pl.num_programs(1)
    par = lax.rem(tile, 2)

    def copier(which):
        def make_copy(local, glob, n):
            return pltpu.make_async_copy(_slots(buf_ref.at[which], local, n, rps),
                                         _slots(xs_ref, glob, n, rps), sem.at[which])
        return make_copy

    def wait_all(which):
        copier(which)(0, 0, TOP_K * tm).wait()

    @pl.when(tile >= 2)
    def _():
        wait_all(par)

    lpos = ri_ref[0]
    s_idx = lax.broadcasted_iota(I32, (TOP_K * tm, tm), 0)
    perm = jnp.logical_or(s_idx == lpos[0:1], s_idx == lpos[1:2]).astype(BF16)
    sorted_rows = _pack_exact_bf16_pairs(_dot(perm, x1_ref[0].astype(BF16)))
    _store_slot_rows(buf_ref.at[par], sorted_rows)
    _start_run_copies(tabs, tile, n_experts, copier(par))

    @pl.when(tile == n_tiles - 1)
    def _():
        wait_all(par)

        @pl.when(tile >= 1)
        def _():
            wait_all(1 - par)


def _dispatch(tabs, x1, ri, *, tm, n_slots, n_experts):
    bsz, seq, d = x1.shape
    nt = seq // tm
    rps = d // 2 // LANES
    grid_spec = pltpu.PrefetchScalarGridSpec(
        num_scalar_prefetch=3,
        grid=(bsz, nt),
        in_specs=[
            pl.BlockSpec((1, tm, d), lambda b, i, *_: (b, i, 0)),
            pl.BlockSpec((1, 8, tm), lambda b, i, *_: (b, 0, i)),
        ],
        out_specs=pl.BlockSpec(memory_space=pl.ANY),
        scratch_shapes=[pltpu.VMEM((2, TOP_K * tm * rps, LANES), U32), pltpu.SemaphoreType.DMA((2,))],
    )
    return pl.pallas_call(
        functools.partial(_dispatch_body, n_experts=n_experts),
        grid_spec=grid_spec,
        out_shape=jax.ShapeDtypeStruct((n_slots * rps, LANES), U32),
        compiler_params=pltpu.CompilerParams(
            dimension_semantics=("arbitrary", "arbitrary"), vmem_limit_bytes=VMEM_LIMIT),
        name="dispatch",
    )(*tabs, x1, ri)


def _experts_body(blk_ref, exp_ref, lo_ref, hi_ref, new_ref, par_ref, nxt_ref, xs_ref, w1_ref, w3_ref, w2_ref,
                  ys_ref, f1_ref, f3_ref, f2_ref, w1b_ref, w3b_ref, w2b_ref, sem, *, layer):
    it = pl.program_id(0)
    lo = lo_ref[it]
    rps = xs_ref.shape[0] // SLOT_BLOCK

    def weight_copies(expert, slot):
        return [pltpu.make_async_copy(w_ref.at[layer, expert], f_ref.at[slot], sem.at[slot, k])
                for k, (w_ref, f_ref) in enumerate(((w1_ref, f1_ref), (w3_ref, f3_ref), (w2_ref, f2_ref)))]

    @pl.when(it == 0)
    def _():
        for c in weight_copies(exp_ref[0], 0):
            c.start()

    @pl.when(new_ref[it] == 1)
    def _():
        slot = par_ref[it]
        for c in weight_copies(exp_ref[it], slot):
            c.wait()
        w1b_ref[...] = f1_ref[slot].astype(BF16)
        w3b_ref[...] = f3_ref[slot].astype(BF16)
        w2b_ref[...] = f2_ref[slot].astype(BF16)

        @pl.when(nxt_ref[it] >= 0)
        def _():
            for c in weight_copies(nxt_ref[it], 1 - slot):
                c.start()

    def ffn(first, n):
        hi_half, lo_half = _unpack_bf16_pairs(_load_slot_rows(xs_ref, n, rps, first))
        xb = jnp.concatenate([hi_half.astype(BF16), lo_half.astype(BF16)], axis=1)
        h = jax.nn.silu(_dot(xb, w1b_ref[...])) * _dot(xb, w3b_ref[...])
        return _pack_bf16_pairs(_dot(h.astype(BF16), w2b_ref[...]))

    def run(first, n):
        y = ffn(first, n)

        @pl.when(lo <= first)
        def _():
            _store_slot_rows(ys_ref, y, first)

        @pl.when(lo > first)
        def _():
            rows = first + lax.broadcasted_iota(I32, y.shape, 0)
            _store_slot_rows(ys_ref, jnp.where(rows >= lo, y, _load_slot_rows(ys_ref, n, rps, first)), first)

    half = SLOT_BLOCK // 2
    need_lo = lo < half
    need_hi = hi_ref[it] > half
    pl.when(jnp.logical_and(need_lo, need_hi))(lambda: run(0, SLOT_BLOCK))
    pl.when(jnp.logical_and(need_lo, jnp.logical_not(need_hi)))(lambda: run(0, half))
    pl.when(jnp.logical_and(jnp.logical_not(need_lo), need_hi))(lambda: run(half, half))


def _experts(items, xs, w1, w3, w2, *, layer):
    rows = xs.shape[0]
    _, _, d, d_e = w1.shape
    rps = d // 2 // LANES
    n_items = items[0].shape[0]
    grid_spec = pltpu.PrefetchScalarGridSpec(
        num_scalar_prefetch=len(items),
        grid=(n_items,),
        in_specs=[
            pl.BlockSpec((SLOT_BLOCK * rps, LANES), lambda i, blk, *_: (blk[i], 0)),
            pl.BlockSpec(memory_space=pl.ANY),
            pl.BlockSpec(memory_space=pl.ANY),
            pl.BlockSpec(memory_space=pl.ANY),
        ],
        out_specs=pl.BlockSpec((SLOT_BLOCK * rps, LANES), lambda i, blk, *_: (blk[i], 0)),
        scratch_shapes=[pltpu.VMEM((2, d, d_e), F32), pltpu.VMEM((2, d, d_e), F32),
                        pltpu.VMEM((2, d_e, d), F32),
                        pltpu.VMEM((d, d_e), BF16), pltpu.VMEM((d, d_e), BF16),
                        pltpu.VMEM((d_e, d), BF16), pltpu.SemaphoreType.DMA((2, 3))],
    )
    return pl.pallas_call(
        functools.partial(_experts_body, layer=layer),
        grid_spec=grid_spec,
        out_shape=jax.ShapeDtypeStruct((rows, LANES), U32),
        compiler_params=pltpu.CompilerParams(
            dimension_semantics=("arbitrary",), vmem_limit_bytes=VMEM_LIMIT),
        name="experts",
    )(*items, xs, w1, w3, w2)


def _expert_items(cnt, n_experts, n_slots):
    n_blocks = n_slots // SLOT_BLOCK
    n_items = n_blocks + n_experts - 1
    tot = jnp.sum(cnt, axis=0)[:n_experts].astype(I32)
    end = jnp.cumsum(tot)
    start = end - tot
    first_blk = start // SLOT_BLOCK
    n_it = jnp.where(tot > 0, (end - 1) // SLOT_BLOCK - first_blk + 1, 0)
    it_end = jnp.cumsum(n_it)
    it_start = it_end - n_it
    idx = jnp.arange(n_items, dtype=I32)
    last_live = it_end[-1] - 1
    live = idx <= last_live
    at = jnp.minimum(idx, last_live)
    expert = jnp.sum(it_end[None, :] <= at[:, None], axis=1).astype(I32)
    onehot = expert[:, None] == jnp.arange(n_experts, dtype=I32)[None, :]
    pick = lambda v: jnp.sum(jnp.where(onehot, v[None, :], 0), axis=1)
    block = pick(first_blk) + idx - pick(it_start)
    item_block = jnp.where(live, block, n_blocks - 1).astype(I32)
    item_lo = jnp.where(live, jnp.maximum(pick(start) - block * SLOT_BLOCK, 0), SLOT_BLOCK).astype(I32)
    item_hi = jnp.where(live, jnp.minimum(pick(end) - block * SLOT_BLOCK, SLOT_BLOCK), 0).astype(I32)
    prev_expert = jnp.concatenate([jnp.full((1,), -1, I32), expert[:-1]])
    is_new = expert != prev_expert
    slot = (jnp.cumsum(is_new.astype(I32)) - 1) % 2
    later_new = lax.cummin(jnp.where(is_new, idx, n_items)[::-1], axis=0)[::-1]
    next_new = jnp.concatenate([later_new[1:], jnp.full((1,), n_items, I32)])
    next_expert = jnp.where(next_new < n_items, expert[jnp.minimum(next_new, n_items - 1)], -1)
    return (item_block, expert, item_lo, item_hi, is_new.astype(I32), slot.astype(I32),
            next_expert.astype(I32))


def _combine_body(gstart_ref, lstart_ref, len_ref, x1_ref, p_ref, wcol_ref, ys_ref, wg_ref, bg_ref, wp_ref,
                  g2_ref, b2_ref, out_ref, buf_ref, sem, *, alpha, n_experts):
    tile = pl.program_id(0) * pl.num_programs(1) + pl.program_id(1)
    tabs = (gstart_ref, lstart_ref, len_ref)
    tm, d = x1_ref.shape[1:]
    rps = d // 2 // LANES

    n_tiles = pl.num_programs(0) * pl.num_programs(1)
    par = lax.rem(tile, 2)

    def copier(which):
        def make_copy(local, glob, n):
            return pltpu.make_async_copy(_slots(ys_ref, glob, n, rps),
                                         _slots(buf_ref.at[which], local, n, rps), sem.at[which])
        return make_copy

    @pl.when(tile == 0)
    def _():
        _start_run_copies(tabs, tile, n_experts, copier(par))

    @pl.when(tile + 1 < n_tiles)
    def _():
        _start_run_copies(tabs, tile + 1, n_experts, copier(1 - par))
    copier(par)(0, 0, TOP_K * tm).wait()

    hi_half, lo_half = _unpack_bf16_pairs(_load_slot_rows(buf_ref.at[par], TOP_K * tm, rps))
    ys = jnp.concatenate([hi_half.astype(BF16), lo_half.astype(BF16)], axis=1)
    n = tm // COMBINE_CHAINS

    def ple_matmuls(s):
        x1 = x1_ref[0, s * n:(s + 1) * n]
        gate_logits = _dot(x1.astype(BF16), wg_ref[...]) + bg_ref[...]
        return x1, gate_logits, _dot(p_ref[0, s * n:(s + 1) * n].astype(BF16), wp_ref[...])

    def finish(s, x1, gate_logits, proj):
        wcol = wcol_ref[0, s * n:(s + 1) * n]
        s_idx = lax.broadcasted_iota(I32, (n, TOP_K * tm), 1)
        select = jnp.zeros((n, TOP_K * tm), F32)
        for kk in range(TOP_K):
            at_kk = s_idx == wcol[:, TOP_K + kk:TOP_K + kk + 1].astype(I32)
            select = jnp.where(at_kk, wcol[:, kk:kk + 1], select)
        ffn = _dot(select.astype(BF16), ys)
        ple = jax.nn.sigmoid(gate_logits) * proj
        out_ref[0, s * n:(s + 1) * n] = _layernorm(alpha * x1 + ffn + ple, g2_ref[...], b2_ref[...])

    chains = [ple_matmuls(s) for s in range(COMBINE_CHAINS)]
    for s, c in enumerate(chains):
        finish(s, *c)


def _combine(tabs, x1, p, wcol, ys, w_gate, b_gate, w_proj, g2, b2, *, layer, alpha, tm, n_experts):
    bsz, seq, d = x1.shape
    nt = seq // tm
    d_ple = p.shape[-1]
    rps = d // 2 // LANES
    full = lambda *shape: pl.BlockSpec(shape, lambda b, i, *_: (0,) * len(shape))
    grid_spec = pltpu.PrefetchScalarGridSpec(
        num_scalar_prefetch=3,
        grid=(bsz, nt),
        in_specs=[
            pl.BlockSpec((1, tm, d), lambda b, i, *_: (b, i, 0)),
            pl.BlockSpec((None, 1, tm, d_ple), lambda b, i, *_: (layer, b, i, 0)),
            pl.BlockSpec((1, tm, LANES), lambda b, i, *_: (b, i, 0)),
            pl.BlockSpec(memory_space=pl.ANY),
            full(d, d), full(1, d), full(d_ple, d),
            full(1, d), full(1, d),
        ],
        out_specs=pl.BlockSpec((1, tm, d), lambda b, i, *_: (b, i, 0)),
        scratch_shapes=[pltpu.VMEM((2, TOP_K * tm * rps, LANES), U32), pltpu.SemaphoreType.DMA((2,))],
    )
    return pl.pallas_call(
        functools.partial(_combine_body, alpha=alpha, n_experts=n_experts),
        grid_spec=grid_spec,
        out_shape=jax.ShapeDtypeStruct((bsz, seq, d), F32),
        compiler_params=pltpu.CompilerParams(
            dimension_semantics=("arbitrary", "arbitrary"), vmem_limit_bytes=VMEM_LIMIT),
        name="combine",
    )(*tabs, x1, p, wcol, ys, w_gate, b_gate, w_proj, g2, b2)


def _router_weights(w_rg, b_rg, w_re, b_re):
    d, n_groups = w_rg.shape
    per_group = w_re.shape[2]
    assert n_groups <= 8 and per_group <= 8 and 8 + 8 * n_groups <= LANES
    w_e = jnp.pad(jnp.transpose(w_re, (1, 0, 2)), ((0, 0), (0, 0), (0, 8 - per_group)))
    b_e = jnp.pad(b_re, ((0, 0), (0, 8 - per_group)))
    tail = LANES - 8 - 8 * n_groups
    w = jnp.concatenate([jnp.pad(w_rg, ((0, 0), (0, 8 - n_groups))), w_e.reshape(d, 8 * n_groups),
                         jnp.zeros((d, tail), F32)], axis=1)
    b = jnp.concatenate([jnp.pad(b_rg, (0, 8 - n_groups)), b_e.reshape(-1), jnp.zeros((tail,), F32)])
    w_hi = w.astype(BF16)
    w_lo = (w - w_hi.astype(F32)).astype(BF16)
    return jnp.concatenate([w_hi, w_lo], axis=1), b[None, :]


def kernel(x, p, ln0_g, ln0_b, w_in, b_in, conv_w, pool_w, pool_scale, w_o, ln1_g, ln1_b,
           w_router_group, b_router_group, w_router_expert, b_router_expert, w1, w3, w2,
           w_ple_gate, b_ple_gate, w_ple_proj, ln2_g, ln2_b):
    bsz, seq, d = x.shape
    depth = w_in.shape[0]
    n_groups, per_group = w_router_expert.shape[1], w_router_expert.shape[3]
    n_experts = n_groups * per_group
    alpha = (2 * depth) ** 0.25
    tm = min(SORT_ROWS, seq)
    tm_mix = min(MIX_ROWS, seq)
    n_slots = bsz * seq * TOP_K
    assert seq % tm_mix == 0 and tm_mix % tm == 0 and tm % LANES == 0
    assert n_slots % SLOT_BLOCK == 0 and n_experts <= LANES
    assert d % (2 * LANES) == 0
    row = lambda a: a[None, :]

    for i in range(depth):
        w_r, b_r = _router_weights(w_router_group[i], b_router_group[i],
                                   w_router_expert[i], b_router_expert[i])
        x1, ri, wcol, cnt = _mixer(
            x, row(ln0_g), row(ln0_b), w_in[i].astype(BF16), row(b_in[i]), conv_w[i],
            pool_w[i].astype(BF16), row(pool_scale[i]), w_o[i].astype(BF16),
            row(ln1_g[i]), row(ln1_b[i]), w_r, b_r,
            apply_ln0=(i == 0), alpha=alpha, n_groups=n_groups, per_group=per_group, tm=tm_mix,
            sort_rows=tm)
        cnt = cnt[:, :, 0, :].reshape(-1, LANES)
        tabs = _run_tables(cnt, n_experts)
        items = _expert_items(cnt, n_experts, n_slots)
        xs = _dispatch(tabs, x1, ri, tm=tm, n_slots=n_slots, n_experts=n_experts)
        ys = _experts(items, xs, w1, w3, w2, layer=i)
        x = _combine(tabs, x1, p, wcol, ys, w_ple_gate[i].astype(BF16), row(b_ple_gate[i]),
                     w_ple_proj[i].astype(BF16), row(ln2_g[i]), row(ln2_b[i]),
                     layer=i, alpha=alpha, tm=tm, n_experts=n_experts)
    return x
```

```python
import functools

import jax
import jax.numpy as jnp
from jax import lax
from jax.experimental import pallas as pl
from jax.experimental.pallas import tpu as pltpu

POOL_WINDOWS = (2, 4, 8, 16)
CONV_WIDTH = 3
TOP_K = 2
LN_EPS = 1e-5
HALO = 8
LANES = 128
SORT_ROWS = 512
MIX_ROWS = 1024
MIX_CHAINS = 2
COMBINE_CHAINS = 2
SLOT_BLOCK = 512
VMEM_LIMIT = 56 * 1024 * 1024

F32 = jnp.float32
BF16 = jnp.bfloat16
I32 = jnp.int32
U32 = jnp.uint32


def _layernorm(v, g, b):
    mu = jnp.mean(v, axis=-1, keepdims=True)
    c = v - mu
    var = jnp.mean(c * c, axis=-1, keepdims=True)
    return c * lax.rsqrt(var + LN_EPS) * g + b


def _dot(a, b):
    return jnp.dot(a, b, preferred_element_type=F32)


def _pack_bf16_pairs(v):
    c = v.shape[1] // 2
    hi = pltpu.bitcast(v[:, :c].astype(BF16).astype(F32), U32)
    lo = pltpu.bitcast(v[:, c:].astype(BF16).astype(F32), U32)
    return hi | (lo >> 16)


def _pack_exact_bf16_pairs(v):
    c = v.shape[1] // 2
    return pltpu.bitcast(v[:, :c], U32) | (pltpu.bitcast(v[:, c:], U32) >> 16)


def _unpack_bf16_pairs(u):
    hi = pltpu.bitcast(u & jnp.uint32(0xFFFF0000), F32)
    lo = pltpu.bitcast(u << 16, F32)
    return hi, lo


def _store_slot_rows(ref, packed, first=0):
    n, c = packed.shape
    rps = c // LANES
    for k in range(rps):
        ref[pl.ds(first * rps + k, n, stride=rps), :] = packed[:, k * LANES:(k + 1) * LANES]


def _load_slot_rows(ref, n, rps, first=0):
    return jnp.concatenate([ref[pl.ds(first * rps + k, n, stride=rps), :] for k in range(rps)], axis=1)


def _mixer_body(x_ref, xp_ref, xn_ref, g0_ref, b0_ref, win_ref, bin_ref, cw_ref, pw_ref, ps_ref,
                wo_ref, g1_ref, b1_ref, wr_ref, br_ref,
                x1_ref, ri_ref, wcol_ref, cnt_ref, *, apply_ln0, alpha, n_groups, per_group, seq,
                sort_rows):
    i = pl.program_id(1)
    nt = pl.num_programs(1)
    tm = x_ref.shape[1]
    n = tm // MIX_CHAINS
    rows = n + 2 * HALO
    d_conv = cw_ref.shape[1]
    dg = pw_ref.shape[1]

    xe = jnp.concatenate([xp_ref[0], x_ref[0], xn_ref[0]], axis=0)
    if apply_ln0:
        xe = _layernorm(xe, g0_ref[...], b0_ref[...])
    cw = cw_ref[...]
    ps = ps_ref[...]

    def in_proj(s):
        xs = xe[s * n:s * n + rows]
        return xs, _dot(xs.astype(BF16), win_ref[...]) + bin_ref[...]

    def token_mix(s, z):
        r = lax.broadcasted_iota(I32, (rows, 1), 0)
        lo_ok = jnp.logical_or(r >= HALO, i > 0) if s == 0 else True
        hi_ok = jnp.logical_or(r < n + HALO, i < nt - 1) if s == MIX_CHAINS - 1 else True
        if s == 0 or s == MIX_CHAINS - 1:
            valid = jnp.logical_and(lo_ok, hi_ok)
            mask = lambda a: jnp.where(valid, a, 0.0)
        else:
            mask = lambda a: a

        h = z[:, :d_conv]
        gate_b = z[HALO:HALO + n, d_conv:2 * d_conv]
        gate_c = z[:, 2 * d_conv:3 * d_conv]
        v = mask(gate_c * h)
        conv = (cw[0:1] * pltpu.roll(v, 1, 0)[HALO:HALO + n]
                + cw[1:2] * v[HALO:HALO + n]
                + cw[2:3] * pltpu.roll(v, rows - 1, 0)[HALO:HALO + n])
        parts = [gate_b * conv]

        pos = i * tm + s * n + r[HALO:HALO + n] - HALO
        for g, w in enumerate(POOL_WINDOWS):
            left = w // 2
            right = w - 1 - left
            c0 = 3 * d_conv + g * dg
            ug = mask(z[:, c0:c0 + dg])
            acc = ug
            k = 1
            while k < w:
                acc = acc + pltpu.roll(acc, k, 0)
                k *= 2
            if right:
                acc = pltpu.roll(acc, rows - right, 0)

            def edge(lo):
                p = pos[lo:lo + HALO]
                cnt = (jnp.minimum(p + right + 1, seq) - jnp.maximum(p - left, 0)).astype(F32)
                return acc[HALO + lo:2 * HALO + lo] / cnt - ug[HALO + lo:2 * HALO + lo]
            inner = acc[2 * HALO:n] * (1.0 / w) - ug[2 * HALO:n]
            pooled = jnp.concatenate([edge(0), inner, edge(n - HALO)], axis=0)
            parts.append(_dot(pooled.astype(BF16), pw_ref[g]) * ps[:, g * dg:(g + 1) * dg])
        return jnp.concatenate(parts, axis=1).astype(BF16)

    def out_proj(xs, cat):
        mix = _dot(cat, wo_ref[...])
        return _layernorm(alpha * xs[HALO:HALO + n] + mix, g1_ref[...], b1_ref[...])

    def route(x1s):
        xh = x1s.astype(BF16)
        xl = (x1s - xh.astype(F32)).astype(BF16)
        l_hi = _dot(xh, wr_ref[...])
        l_lo = _dot(xl, wr_ref[:, :LANES])
        return l_hi[:, :LANES] + l_hi[:, LANES:] + l_lo + br_ref[...]

    chains = [in_proj(s) for s in range(MIX_CHAINS)]
    x1s = [out_proj(xs, token_mix(s, z)) for s, (xs, z) in enumerate(chains)]
    x1 = jnp.concatenate(x1s, axis=0)
    x1_ref[0] = x1
    logits = jnp.concatenate([route(v) for v in x1s], axis=0)
    ri_ref[0] = jnp.zeros((8, tm), I32)
    for q in range(tm // sort_rows):
        rows_q = slice(q * sort_rows, (q + 1) * sort_rows)
        lpos0, lpos1, gw0, gw1, counts = _route_and_sort(logits[rows_q], n_groups, per_group)
        ri_ref[0, 0:1, rows_q] = lpos0.astype(I32)
        ri_ref[0, 1:2, rows_q] = lpos1.astype(I32)
        rowl = lax.broadcasted_iota(I32, (LANES, sort_rows), 0)
        wrows = jnp.where(rowl == 0, gw0, jnp.where(rowl == 1, gw1,
                          jnp.where(rowl == 2, lpos0, jnp.where(rowl == 3, lpos1, 0.0))))
        wcol_ref[0, rows_q, :] = wrows.T
        cnt_ref[0, q] = counts


def _route_and_sort(logits, n_groups, per_group):
    n = logits.shape[0]
    lt = logits.T
    row8 = lax.broadcasted_iota(I32, (8, n), 0)
    neg = jnp.float32(-jnp.inf)
    lg = jnp.where(row8 < n_groups, lt[0:8], neg)
    mg = jnp.max(lg, axis=0, keepdims=True)
    g_w = 1.0 / jnp.sum(jnp.exp(lg - mg), axis=0, keepdims=True)
    g_sel = jnp.min(jnp.where(lg == mg, row8, 8), axis=0, keepdims=True)
    le = lt[8:16]
    for g in range(1, n_groups):
        le = jnp.where(g_sel == g, lt[8 + 8 * g:16 + 8 * g], le)
    le = jnp.where(row8 < per_group, le, neg)
    v1 = jnp.max(le, axis=0, keepdims=True)
    i1 = jnp.min(jnp.where(le == v1, row8, 8), axis=0, keepdims=True)
    le2 = jnp.where(row8 == i1, neg, le)
    v2 = jnp.max(le2, axis=0, keepdims=True)
    i2 = jnp.min(jnp.where(le2 == v2, row8, 8), axis=0, keepdims=True)
    e21 = jnp.exp(v2 - v1)
    w_first = 1.0 / (1.0 + e21)
    gw0 = w_first * g_w
    gw1 = (e21 * w_first) * g_w
    eid0 = g_sel * per_group + i1
    eid1 = g_sel * per_group + i2

    rowl = lax.broadcasted_iota(I32, (LANES, n), 0)
    oh0 = rowl == eid0
    oh1 = rowl == eid1
    a_idx = lax.broadcasted_iota(I32, (n, n), 0)
    b_idx = lax.broadcasted_iota(I32, (n, n), 1)
    before = (a_idx < b_idx).astype(BF16)
    r0 = _dot(oh0.astype(BF16), before)
    r1 = _dot(oh1.astype(BF16), before)
    cnt0 = jnp.sum(oh0.astype(F32), axis=1, keepdims=True)
    lower = jnp.sum((rowl > eid0).astype(F32) + (rowl > eid1).astype(F32),
                    axis=1, keepdims=True)
    lpos0 = jnp.sum(jnp.where(oh0, lower + r0, 0.0), axis=0, keepdims=True)
    lpos1 = jnp.sum(jnp.where(oh1, lower + cnt0 + r1, 0.0), axis=0, keepdims=True)

    both = jnp.logical_or(oh0, oh1).astype(BF16)
    counts = lax.dot_general(jnp.ones((8, n), BF16), both, (((1,), (1,)), ((), ())),
                             preferred_element_type=F32)
    return lpos0, lpos1, gw0, gw1, counts


def _mixer(x, g0, b0, w_in, b_in, conv_w, pool_w, pool_scale, w_o, g1, b1, w_r, b_r, *,
           apply_ln0, alpha, n_groups, per_group, tm, sort_rows):
    bsz, seq, d = x.shape
    nt = seq // tm
    hb = tm // HALO
    nhb = seq // HALO
    d_in = w_in.shape[1]
    d_conv = conv_w.shape[1]
    full = lambda *shape: pl.BlockSpec(shape, lambda b, i: (0,) * len(shape))
    body = functools.partial(_mixer_body, apply_ln0=apply_ln0, alpha=alpha, n_groups=n_groups,
                             per_group=per_group, seq=seq, sort_rows=sort_rows)
    nq = tm // sort_rows
    return pl.pallas_call(
        body,
        grid=(bsz, nt),
        in_specs=[
            pl.BlockSpec((1, tm, d), lambda b, i: (b, i, 0)),
            pl.BlockSpec((1, HALO, d), lambda b, i: (b, jnp.maximum(i * hb - 1, 0), 0)),
            pl.BlockSpec((1, HALO, d), lambda b, i: (b, jnp.minimum((i + 1) * hb, nhb - 1), 0)),
            full(1, d), full(1, d),
            full(d, d_in), full(1, d_in),
            full(CONV_WIDTH, d_conv),
            full(*pool_w.shape), full(1, pool_scale.shape[1]),
            full(d, d), full(1, d), full(1, d),
            full(d, 2 * LANES), full(1, LANES),
        ],
        out_specs=[
            pl.BlockSpec((1, tm, d), lambda b, i: (b, i, 0)),
            pl.BlockSpec((1, 8, tm), lambda b, i: (b, 0, i)),
            pl.BlockSpec((1, tm, LANES), lambda b, i: (b, i, 0)),
            pl.BlockSpec((1, nq, 8, LANES), lambda b, i: (b, i, 0, 0)),
        ],
        out_shape=[
            jax.ShapeDtypeStruct((bsz, seq, d), F32),
            jax.ShapeDtypeStruct((bsz, 8, seq), I32),
            jax.ShapeDtypeStruct((bsz, seq, LANES), F32),
            jax.ShapeDtypeStruct((bsz, nt * nq, 8, LANES), F32),
        ],
        compiler_params=pltpu.CompilerParams(
            dimension_semantics=("arbitrary", "arbitrary"), vmem_limit_bytes=VMEM_LIMIT),
        name="mixer",
    )(x, x, x, g0, b0, w_in, b_in, conv_w, pool_w, pool_scale, w_o, g1, b1, w_r, b_r)


def _tables_body(cnt_ref, gstart_ref, lstart_ref, len_ref):
    c = cnt_ref[...]
    n_tiles = c.shape[0]
    row = lax.broadcasted_iota(I32, c.shape, 0)
    lane = lax.broadcasted_iota(I32, c.shape, 1)

    def prefix(v, idx, size, axis):
        k = 1
        while k < size:
            v = v + jnp.where(idx >= k, pltpu.roll(v, k, axis), 0.0)
            k *= 2
        return v

    over_tiles = prefix(c, row, n_tiles, 0)
    tot = jnp.broadcast_to(over_tiles[n_tiles - 1:n_tiles], c.shape)
    expert_start = prefix(tot, lane, LANES, 1) - tot
    gstart_ref[...] = (expert_start + over_tiles - c).astype(I32)
    lstart_ref[...] = (prefix(c, lane, LANES, 1) - c).astype(I32)
    len_ref[...] = c.astype(I32)


def _run_tables(cnt, n_experts):
    out = jax.ShapeDtypeStruct(cnt.shape, I32)
    tabs = pl.pallas_call(_tables_body, out_shape=[out, out, out], name="run_tables")(cnt)
    return tuple(t[:, :n_experts].reshape(-1) for t in tabs)


RUN_CHUNK_LOG2 = 6


def _slots(ref, first, n, rps):
    start = first * rps
    if not isinstance(start, int):
        start = pl.multiple_of(start, rps)
    return ref.at[pl.ds(start, n * rps)]


def _start_run_copies(tabs, tile, n_experts, make_copy):
    gstart_ref, lstart_ref, len_ref = tabs
    big = 1 << RUN_CHUNK_LOG2

    def run_of(e):
        idx = tile * n_experts + jnp.minimum(e, n_experts - 1)
        return len_ref[idx], lstart_ref[idx], gstart_ref[idx]

    def per_expert(e, run):
        n, ls, gs = run
        run = run_of(e + 1)
        n_big = n >> RUN_CHUNK_LOG2

        def big_piece(c, carry2):
            make_copy(ls + c * big, gs + c * big, big).start()
            return carry2
        lax.fori_loop(0, n_big, big_piece, 0)
        off = n_big * big
        for bit in reversed(range(RUN_CHUNK_LOG2)):
            size = 1 << bit

            @pl.when((n & size) != 0)
            def _(off=off, size=size, bit=bit):
                make_copy(ls + off, gs + off, size).start(priority=bit % 2)
            off = off + (n & size)
        return run
    lax.fori_loop(0, n_experts, per_expert, run_of(0), unroll=2)


def _dispatch_body(gstart_ref, lstart_ref, len_ref, x1_ref, ri_ref, xs_ref, buf_ref, sem, *, n_experts):
    tile = pl.program_id(0) * pl.num_programs(1) + pl.program_id(1)
    tabs = (gstart_ref, lstart_ref, len_ref)
    tm, d = x1_ref.shape[1:]
    rps = d // 2 // LANES

    n_tiles = pl.num_programs(0) * pl.num_programs(1)
    par = lax.rem(tile, 2)

    def copier(which):
        def make_copy(local, glob, n):
            return pltpu.make_async_copy(_slots(buf_ref.at[which], local, n, rps),
                                         _slots(xs_ref, glob, n, rps), sem.at[which])
        return make_copy

    def wait_all(which):
        copier(which)(0, 0, TOP_K * tm).wait()

    @pl.when(tile >= 2)
    def _():
        wait_all(par)

    lpos = ri_ref[0]
    s_idx = lax.broadcasted_iota(I32, (TOP_K * tm, tm), 0)
    perm = jnp.logical_or(s_idx == lpos[0:1], s_idx == lpos[1:2]).astype(BF16)
    sorted_rows = _pack_exact_bf16_pairs(_dot(perm, x1_ref[0].astype(BF16)))
    _store_slot_rows(buf_ref.at[par], sorted_rows)
    _start_run_copies(tabs, tile, n_experts, copier(par))

    @pl.when(tile == n_tiles - 1)
    def _():
        wait_all(par)

        @pl.when(tile >= 1)
        def _():
            wait_all(1 - par)


def _dispatch(tabs, x1, ri, *, tm, n_slots, n_experts):
    bsz, seq, d = x1.shape
    nt = seq // tm
    rps = d // 2 // LANES
    grid_spec = pltpu.PrefetchScalarGridSpec(
        num_scalar_prefetch=3,
        grid=(bsz, nt),
        in_specs=[
            pl.BlockSpec((1, tm, d), lambda b, i, *_: (b, i, 0)),
            pl.BlockSpec((1, 8, tm), lambda b, i, *_: (b, 0, i)),
        ],
        out_specs=pl.BlockSpec(memory_space=pl.ANY),
        scratch_shapes=[pltpu.VMEM((2, TOP_K * tm * rps, LANES), U32), pltpu.SemaphoreType.DMA((2,))],
    )
    return pl.pallas_call(
        functools.partial(_dispatch_body, n_experts=n_experts),
        grid_spec=grid_spec,
        out_shape=jax.ShapeDtypeStruct((n_slots * rps, LANES), U32),
        compiler_params=pltpu.CompilerParams(
            dimension_semantics=("arbitrary", "arbitrary"), vmem_limit_bytes=VMEM_LIMIT),
        name="dispatch",
    )(*tabs, x1, ri)


def _experts_body(blk_ref, exp_ref, lo_ref, hi_ref, new_ref, par_ref, nxt_ref, xs_ref, w1_ref, w3_ref, w2_ref,
                  ys_ref, f1_ref, f3_ref, f2_ref, w1b_ref, w3b_ref, w2b_ref, sem, *, layer):
    it = pl.program_id(0)
    lo = lo_ref[it]
    rps = xs_ref.shape[0] // SLOT_BLOCK

    def weight_copies(expert, slot):
        return [pltpu.make_async_copy(w_ref.at[layer, expert], f_ref.at[slot], sem.at[slot, k])
                for k, (w_ref, f_ref) in enumerate(((w1_ref, f1_ref), (w3_ref, f3_ref), (w2_ref, f2_ref)))]

    @pl.when(it == 0)
    def _():
        for c in weight_copies(exp_ref[0], 0):
            c.start()

    @pl.when(new_ref[it] == 1)
    def _():
        slot = par_ref[it]
        for c in weight_copies(exp_ref[it], slot):
            c.wait()
        w1b_ref[...] = f1_ref[slot].astype(BF16)
        w3b_ref[...] = f3_ref[slot].astype(BF16)
        w2b_ref[...] = f2_ref[slot].astype(BF16)

        @pl.when(nxt_ref[it] >= 0)
        def _():
            for c in weight_copies(nxt_ref[it], 1 - slot):
                c.start()

    def ffn(first, n):
        hi_half, lo_half = _unpack_bf16_pairs(_load_slot_rows(xs_ref, n, rps, first))
        xb = jnp.concatenate([hi_half.astype(BF16), lo_half.astype(BF16)], axis=1)
        h = jax.nn.silu(_dot(xb, w1b_ref[...])) * _dot(xb, w3b_ref[...])
        return _pack_bf16_pairs(_dot(h.astype(BF16), w2b_ref[...]))

    def run(first, n):
        y = ffn(first, n)

        @pl.when(lo <= first)
        def _():
            _store_slot_rows(ys_ref, y, first)

        @pl.when(lo > first)
        def _():
            rows = first + lax.broadcasted_iota(I32, y.shape, 0)
            _store_slot_rows(ys_ref, jnp.where(rows >= lo, y, _load_slot_rows(ys_ref, n, rps, first)), first)

    half = SLOT_BLOCK // 2
    need_lo = lo < half
    need_hi = hi_ref[it] > half
    pl.when(jnp.logical_and(need_lo, need_hi))(lambda: run(0, SLOT_BLOCK))
    pl.when(jnp.logical_and(need_lo, jnp.logical_not(need_hi)))(lambda: run(0, half))
    pl.when(jnp.logical_and(jnp.logical_not(need_lo), need_hi))(lambda: run(half, half))


def _experts(items, xs, w1, w3, w2, *, layer):
    rows = xs.shape[0]
    _, _, d, d_e = w1.shape
    rps = d // 2 // LANES
    n_items = items[0].shape[0]
    grid_spec = pltpu.PrefetchScalarGridSpec(
        num_scalar_prefetch=len(items),
        grid=(n_items,),
        in_specs=[
            pl.BlockSpec((SLOT_BLOCK * rps, LANES), lambda i, blk, *_: (blk[i], 0)),
            pl.BlockSpec(memory_space=pl.ANY),
            pl.BlockSpec(memory_space=pl.ANY),
            pl.BlockSpec(memory_space=pl.ANY),
        ],
        out_specs=pl.BlockSpec((SLOT_BLOCK * rps, LANES), lambda i, blk, *_: (blk[i], 0)),
        scratch_shapes=[pltpu.VMEM((2, d, d_e), F32), pltpu.VMEM((2, d, d_e), F32),
                        pltpu.VMEM((2, d_e, d), F32),
                        pltpu.VMEM((d, d_e), BF16), pltpu.VMEM((d, d_e), BF16),
                        pltpu.VMEM((d_e, d), BF16), pltpu.SemaphoreType.DMA((2, 3))],
    )
    return pl.pallas_call(
        functools.partial(_experts_body, layer=layer),
        grid_spec=grid_spec,
        out_shape=jax.ShapeDtypeStruct((rows, LANES), U32),
        compiler_params=pltpu.CompilerParams(
            dimension_semantics=("arbitrary",), vmem_limit_bytes=VMEM_LIMIT),
        name="experts",
    )(*items, xs, w1, w3, w2)


def _expert_items(cnt, n_experts, n_slots):
    n_blocks = n_slots // SLOT_BLOCK
    n_items = n_blocks + n_experts - 1
    tot = jnp.sum(cnt, axis=0)[:n_experts].astype(I32)
    end = jnp.cumsum(tot)
    start = end - tot
    first_blk = start // SLOT_BLOCK
    n_it = jnp.where(tot > 0, (end - 1) // SLOT_BLOCK - first_blk + 1, 0)
    it_end = jnp.cumsum(n_it)
    it_start = it_end - n_it
    idx = jnp.arange(n_items, dtype=I32)
    last_live = it_end[-1] - 1
    live = idx <= last_live
    at = jnp.minimum(idx, last_live)
    expert = jnp.sum(it_end[None, :] <= at[:, None], axis=1).astype(I32)
    onehot = expert[:, None] == jnp.arange(n_experts, dtype=I32)[None, :]
    pick = lambda v: jnp.sum(jnp.where(onehot, v[None, :], 0), axis=1)
    block = pick(first_blk) + idx - pick(it_start)
    item_block = jnp.where(live, block, n_blocks - 1).astype(I32)
    item_lo = jnp.where(live, jnp.maximum(pick(start) - block * SLOT_BLOCK, 0), SLOT_BLOCK).astype(I32)
    item_hi = jnp.where(live, jnp.minimum(pick(end) - block * SLOT_BLOCK, SLOT_BLOCK), 0).astype(I32)
    prev_expert = jnp.concatenate([jnp.full((1,), -1, I32), expert[:-1]])
    is_new = expert != prev_expert
    slot = (jnp.cumsum(is_new.astype(I32)) - 1) % 2
    later_new = lax.cummin(jnp.where(is_new, idx, n_items)[::-1], axis=0)[::-1]
    next_new = jnp.concatenate([later_new[1:], jnp.full((1,), n_items, I32)])
    next_expert = jnp.where(next_new < n_items, expert[jnp.minimum(next_new, n_items - 1)], -1)
    return (item_block, expert, item_lo, item_hi, is_new.astype(I32), slot.astype(I32),
            next_expert.astype(I32))


def _combine_body(gstart_ref, lstart_ref, len_ref, x1_ref, p_ref, wcol_ref, ys_ref, wg_ref, bg_ref, wp_ref,
                  g2_ref, b2_ref, out_ref, buf_ref, sem, *, alpha, n_experts):
    tile = pl.program_id(0) * pl.num_programs(1) + pl.program_id(1)
    tabs = (gstart_ref, lstart_ref, len_ref)
    tm, d = x1_ref.shape[1:]
    rps = d // 2 // LANES

    n_tiles = pl.num_programs(0) * pl.num_programs(1)
    par = lax.rem(tile, 2)

    def copier(which):
        def make_copy(local, glob, n):
            return pltpu.make_async_copy(_slots(ys_ref, glob, n, rps),
                                         _slots(buf_ref.at[which], local, n, rps), sem.at[which])
        return make_copy

    @pl.when(tile == 0)
    def _():
        _start_run_copies(tabs, tile, n_experts, copier(par))

    @pl.when(tile + 1 < n_tiles)
    def _():
        _start_run_copies(tabs, tile + 1, n_experts, copier(1 - par))
    copier(par)(0, 0, TOP_K * tm).wait()

    hi_half, lo_half = _unpack_bf16_pairs(_load_slot_rows(buf_ref.at[par], TOP_K * tm, rps))
    ys = jnp.concatenate([hi_half.astype(BF16), lo_half.astype(BF16)], axis=1)
    n = tm // COMBINE_CHAINS

    def ple_matmuls(s):
        x1 = x1_ref[0, s * n:(s + 1) * n]
        gate_logits = _dot(x1.astype(BF16), wg_ref[...]) + bg_ref[...]
        return x1, gate_logits, _dot(p_ref[0, s * n:(s + 1) * n].astype(BF16), wp_ref[...])

    def finish(s, x1, gate_logits, proj):
        wcol = wcol_ref[0, s * n:(s + 1) * n]
        s_idx = lax.broadcasted_iota(I32, (n, TOP_K * tm), 1)
        select = jnp.zeros((n, TOP_K * tm), F32)
        for kk in range(TOP_K):
            at_kk = s_idx == wcol[:, TOP_K + kk:TOP_K + kk + 1].astype(I32)
            select = jnp.where(at_kk, wcol[:, kk:kk + 1], select)
        ffn = _dot(select.astype(BF16), ys)
        ple = jax.nn.sigmoid(gate_logits) * proj
        out_ref[0, s * n:(s + 1) * n] = _layernorm(alpha * x1 + ffn + ple, g2_ref[...], b2_ref[...])

    chains = [ple_matmuls(s) for s in range(COMBINE_CHAINS)]
    for s, c in enumerate(chains):
        finish(s, *c)


def _combine(tabs, x1, p, wcol, ys, w_gate, b_gate, w_proj, g2, b2, *, layer, alpha, tm, n_experts):
    bsz, seq, d = x1.shape
    nt = seq // tm
    d_ple = p.shape[-1]
    rps = d // 2 // LANES
    full = lambda *shape: pl.BlockSpec(shape, lambda b, i, *_: (0,) * len(shape))
    grid_spec = pltpu.PrefetchScalarGridSpec(
        num_scalar_prefetch=3,
        grid=(bsz, nt),
        in_specs=[
            pl.BlockSpec((1, tm, d), lambda b, i, *_: (b, i, 0)),
            pl.BlockSpec((None, 1, tm, d_ple), lambda b, i, *_: (layer, b, i, 0)),
            pl.BlockSpec((1, tm, LANES), lambda b, i, *_: (b, i, 0)),
            pl.BlockSpec(memory_space=pl.ANY),
            full(d, d), full(1, d), full(d_ple, d),
            full(1, d), full(1, d),
        ],
        out_specs=pl.BlockSpec((1, tm, d), lambda b, i, *_: (b, i, 0)),
        scratch_shapes=[pltpu.VMEM((2, TOP_K * tm * rps, LANES), U32), pltpu.SemaphoreType.DMA((2,))],
    )
    return pl.pallas_call(
        functools.partial(_combine_body, alpha=alpha, n_experts=n_experts),
        grid_spec=grid_spec,
        out_shape=jax.ShapeDtypeStruct((bsz, seq, d), F32),
        compiler_params=pltpu.CompilerParams(
            dimension_semantics=("arbitrary", "arbitrary"), vmem_limit_bytes=VMEM_LIMIT),
        name="combine",
    )(*tabs, x1, p, wcol, ys, w_gate, b_gate, w_proj, g2, b2)


def _router_weights(w_rg, b_rg, w_re, b_re):
    d, n_groups = w_rg.shape
    per_group = w_re.shape[2]
    assert n_groups <= 8 and per_group <= 8 and 8 + 8 * n_groups <= LANES
    w_e = jnp.pad(jnp.transpose(w_re, (1, 0, 2)), ((0, 0), (0, 0), (0, 8 - per_group)))
    b_e = jnp.pad(b_re, ((0, 0), (0, 8 - per_group)))
    tail = LANES - 8 - 8 * n_groups
    w = jnp.concatenate([jnp.pad(w_rg, ((0, 0), (0, 8 - n_groups))), w_e.reshape(d, 8 * n_groups),
                         jnp.zeros((d, tail), F32)], axis=1)
    b = jnp.concatenate([jnp.pad(b_rg, (0, 8 - n_groups)), b_e.reshape(-1), jnp.zeros((tail,), F32)])
    w_hi = w.astype(BF16)
    w_lo = (w - w_hi.astype(F32)).astype(BF16)
    return jnp.concatenate([w_hi, w_lo], axis=1), b[None, :]


def kernel(x, p, ln0_g, ln0_b, w_in, b_in, conv_w, pool_w, pool_scale, w_o, ln1_g, ln1_b,
           w_router_group, b_router_group, w_router_expert, b_router_expert, w1, w3, w2,
           w_ple_gate, b_ple_gate, w_ple_proj, ln2_g, ln2_b):
    bsz, seq, d = x.shape
    depth = w_in.shape[0]
    n_groups, per_group = w_router_expert.shape[1], w_router_expert.shape[3]
    n_experts = n_groups * per_group
    alpha = (2 * depth) ** 0.25
    tm = min(SORT_ROWS, seq)
    tm_mix = min(MIX_ROWS, seq)
    n_slots = bsz * seq * TOP_K
    assert seq % tm_mix == 0 and tm_mix % tm == 0 and tm % LANES == 0
    assert n_slots % SLOT_BLOCK == 0 and n_experts <= LANES
    assert d % (2 * LANES) == 0
    row = lambda a: a[None, :]

    for i in range(depth):
        w_r, b_r = _router_weights(w_router_group[i], b_router_group[i],
                                   w_router_expert[i], b_router_expert[i])
        x1, ri, wcol, cnt = _mixer(
            x, row(ln0_g), row(ln0_b), w_in[i].astype(BF16), row(b_in[i]), conv_w[i],
            pool_w[i].astype(BF16), row(pool_scale[i]), w_o[i].astype(BF16),
            row(ln1_g[i]), row(ln1_b[i]), w_r, b_r,
            apply_ln0=(i == 0), alpha=alpha, n_groups=n_groups, per_group=per_group, tm=tm_mix,
            sort_rows=tm)
        cnt = cnt[:, :, 0, :].reshape(-1, LANES)
        tabs = _run_tables(cnt, n_experts)
        items = _expert_items(cnt, n_experts, n_slots)
        xs = _dispatch(tabs, x1, ri, tm=tm, n_slots=n_slots, n_experts=n_experts)
        ys = _experts(items, xs, w1, w3, w2, layer=i)
        x = _combine(tabs, x1, p, wcol, ys, w_ple_gate[i].astype(BF16), row(b_ple_gate[i]),
                     w_ple_proj[i].astype(BF16), row(ln2_g[i]), row(ln2_b[i]),
                     layer=i, alpha=alpha, tm=tm, n_experts=n_experts)
    return x
```

```python
import functools

import jax
import jax.numpy as jnp
from jax import lax
from jax.experimental import pallas as pl
from jax.experimental.pallas import tpu as pltpu

POOL_WINDOWS = (2, 4, 8, 16)
CONV_WIDTH = 3
TOP_K = 2
LN_EPS = 1e-5
HALO = 8
LANES = 128
SUBLANES = 8
SORT_ROWS = 512
MIX_ROWS = 1024
MIX_CHAINS = 2
COMBINE_CHAINS = 2
SLOT_BLOCK = 512
VMEM_LIMIT = 56 * 1024 * 1024

F32 = jnp.float32
BF16 = jnp.bfloat16
I32 = jnp.int32
U32 = jnp.uint32


def _layernorm(v, g, b):
    mu = jnp.mean(v, axis=-1, keepdims=True)
    c = v - mu
    var = jnp.mean(c * c, axis=-1, keepdims=True)
    return c * lax.rsqrt(var + LN_EPS) * g + b


def _dot(a, b):
    return jnp.dot(a, b, preferred_element_type=F32)


def _pack_bf16_pairs(v):
    c = v.shape[1] // 2
    hi = pltpu.bitcast(v[:, :c].astype(BF16).astype(F32), U32)
    lo = pltpu.bitcast(v[:, c:].astype(BF16).astype(F32), U32)
    return hi | (lo >> 16)


def _pack_exact_bf16_pairs(v):
    c = v.shape[1] // 2
    return pltpu.bitcast(v[:, :c], U32) | (pltpu.bitcast(v[:, c:], U32) >> 16)


def _unpack_bf16_pairs(u):
    hi = pltpu.bitcast(u & jnp.uint32(0xFFFF0000), F32)
    lo = pltpu.bitcast(u << 16, F32)
    return hi, lo


def _store_slot_rows(ref, packed, first=0):
    n, c = packed.shape
    rps = c // LANES
    for k in range(rps):
        ref[pl.ds(first * rps + k, n, stride=rps), :] = packed[:, k * LANES:(k + 1) * LANES]


def _load_slot_rows(ref, n, rps, first=0):
    return jnp.concatenate([ref[pl.ds(first * rps + k, n, stride=rps), :] for k in range(rps)], axis=1)


def _mixer_body(x_ref, xp_ref, xn_ref, g0_ref, b0_ref, win_ref, bin_ref, cw_ref, pw_ref, ps_ref,
                wo_ref, g1_ref, b1_ref, wr_ref, br_ref,
                x1_ref, ri_ref, wcol_ref, cnt_ref, *, apply_ln0, alpha, n_groups, per_group, seq,
                sort_rows):
    i = pl.program_id(1)
    nt = pl.num_programs(1)
    tm = x_ref.shape[1]
    n = tm // MIX_CHAINS
    rows = n + 2 * HALO
    d_conv = cw_ref.shape[1]
    dg = pw_ref.shape[1]

    xe = jnp.concatenate([xp_ref[0], x_ref[0], xn_ref[0]], axis=0)
    if apply_ln0:
        xe = _layernorm(xe, g0_ref[...], b0_ref[...])
    cw = cw_ref[...]
    ps = ps_ref[...]

    def in_proj(s):
        xs = xe[s * n:s * n + rows]
        return xs, _dot(xs.astype(BF16), win_ref[...]) + bin_ref[...]

    def token_mix(s, z):
        r = lax.broadcasted_iota(I32, (rows, 1), 0)
        lo_ok = jnp.logical_or(r >= HALO, i > 0) if s == 0 else True
        hi_ok = jnp.logical_or(r < n + HALO, i < nt - 1) if s == MIX_CHAINS - 1 else True
        if s == 0 or s == MIX_CHAINS - 1:
            valid = jnp.logical_and(lo_ok, hi_ok)
            mask = lambda a: jnp.where(valid, a, 0.0)
        else:
            mask = lambda a: a

        h = z[:, :d_conv]
        gate_b = z[HALO:HALO + n, d_conv:2 * d_conv]
        gate_c = z[:, 2 * d_conv:3 * d_conv]
        v = mask(gate_c * h)
        conv = (cw[0:1] * pltpu.roll(v, 1, 0)[HALO:HALO + n]
                + cw[1:2] * v[HALO:HALO + n]
                + cw[2:3] * pltpu.roll(v, rows - 1, 0)[HALO:HALO + n])
        parts = [gate_b * conv]

        pos = i * tm + s * n + r[HALO:HALO + n] - HALO
        for g, w in enumerate(POOL_WINDOWS):
            left = w // 2
            right = w - 1 - left
            c0 = 3 * d_conv + g * dg
            ug = mask(z[:, c0:c0 + dg])
            acc = ug
            k = 1
            while k < w:
                acc = acc + pltpu.roll(acc, k, 0)
                k *= 2
            if right:
                acc = pltpu.roll(acc, rows - right, 0)

            def edge(lo):
                p = pos[lo:lo + HALO]
                cnt = (jnp.minimum(p + right + 1, seq) - jnp.maximum(p - left, 0)).astype(F32)
                return acc[HALO + lo:2 * HALO + lo] / cnt - ug[HALO + lo:2 * HALO + lo]
            inner = acc[2 * HALO:n] * (1.0 / w) - ug[2 * HALO:n]
            pooled = jnp.concatenate([edge(0), inner, edge(n - HALO)], axis=0)
            parts.append(_dot(pooled.astype(BF16), pw_ref[g]) * ps[:, g * dg:(g + 1) * dg])
        return jnp.concatenate(parts, axis=1).astype(BF16)

    def out_proj(xs, cat):
        mix = _dot(cat, wo_ref[...])
        return _layernorm(alpha * xs[HALO:HALO + n] + mix, g1_ref[...], b1_ref[...])

    def route(x1s):
        xh = x1s.astype(BF16)
        xl = (x1s - xh.astype(F32)).astype(BF16)
        l_hi = _dot(xh, wr_ref[...])
        l_lo = _dot(xl, wr_ref[:, :LANES])
        return l_hi[:, :LANES] + l_hi[:, LANES:] + l_lo + br_ref[...]

    chains = [in_proj(s) for s in range(MIX_CHAINS)]
    x1s = [out_proj(xs, token_mix(s, z)) for s, (xs, z) in enumerate(chains)]
    x1 = jnp.concatenate(x1s, axis=0)
    x1_ref[0] = x1
    logits = jnp.concatenate([route(v) for v in x1s], axis=0)
    ri_ref[0] = jnp.zeros((SUBLANES, tm), I32)
    for q in range(tm // sort_rows):
        rows_q = slice(q * sort_rows, (q + 1) * sort_rows)
        lpos0, lpos1, gw0, gw1, counts = _route_and_sort(logits[rows_q], n_groups, per_group)
        ri_ref[0, 0:1, rows_q] = lpos0.astype(I32)
        ri_ref[0, 1:2, rows_q] = lpos1.astype(I32)
        rowl = lax.broadcasted_iota(I32, (LANES, sort_rows), 0)
        wrows = jnp.where(rowl == 0, gw0, jnp.where(rowl == 1, gw1,
                          jnp.where(rowl == 2, lpos0, jnp.where(rowl == 3, lpos1, 0.0))))
        wcol_ref[0, rows_q, :] = wrows.T
        cnt_ref[0, q] = counts


def _route_and_sort(logits, n_groups, per_group):
    n = logits.shape[0]
    lt = logits.T
    row8 = lax.broadcasted_iota(I32, (SUBLANES, n), 0)
    neg = jnp.float32(-jnp.inf)
    lg = jnp.where(row8 < n_groups, lt[0:SUBLANES], neg)
    mg = jnp.max(lg, axis=0, keepdims=True)
    g_w = 1.0 / jnp.sum(jnp.exp(lg - mg), axis=0, keepdims=True)
    g_sel = jnp.min(jnp.where(lg == mg, row8, SUBLANES), axis=0, keepdims=True)
    le = lt[SUBLANES:2 * SUBLANES]
    for g in range(1, n_groups):
        le = jnp.where(g_sel == g, lt[SUBLANES * (g + 1):SUBLANES * (g + 2)], le)
    le = jnp.where(row8 < per_group, le, neg)
    v1 = jnp.max(le, axis=0, keepdims=True)
    i1 = jnp.min(jnp.where(le == v1, row8, SUBLANES), axis=0, keepdims=True)
    le2 = jnp.where(row8 == i1, neg, le)
    v2 = jnp.max(le2, axis=0, keepdims=True)
    i2 = jnp.min(jnp.where(le2 == v2, row8, SUBLANES), axis=0, keepdims=True)
    e21 = jnp.exp(v2 - v1)
    w_first = 1.0 / (1.0 + e21)
    gw0 = w_first * g_w
    gw1 = (e21 * w_first) * g_w
    eid0 = g_sel * per_group + i1
    eid1 = g_sel * per_group + i2

    rowl = lax.broadcasted_iota(I32, (LANES, n), 0)
    oh0 = rowl == eid0
    oh1 = rowl == eid1
    a_idx = lax.broadcasted_iota(I32, (n, n), 0)
    b_idx = lax.broadcasted_iota(I32, (n, n), 1)
    before = (a_idx < b_idx).astype(BF16)
    r0 = _dot(oh0.astype(BF16), before)
    r1 = _dot(oh1.astype(BF16), before)
    cnt0 = jnp.sum(oh0.astype(F32), axis=1, keepdims=True)
    lower = jnp.sum((rowl > eid0).astype(F32) + (rowl > eid1).astype(F32),
                    axis=1, keepdims=True)
    lpos0 = jnp.sum(jnp.where(oh0, lower + r0, 0.0), axis=0, keepdims=True)
    lpos1 = jnp.sum(jnp.where(oh1, lower + cnt0 + r1, 0.0), axis=0, keepdims=True)

    both = jnp.logical_or(oh0, oh1).astype(BF16)
    counts = lax.dot_general(jnp.ones((SUBLANES, n), BF16), both, (((1,), (1,)), ((), ())),
                             preferred_element_type=F32)
    return lpos0, lpos1, gw0, gw1, counts


def _mixer(x, g0, b0, w_in, b_in, conv_w, pool_w, pool_scale, w_o, g1, b1, w_r, b_r, *,
           apply_ln0, alpha, n_groups, per_group, tm, sort_rows):
    bsz, seq, d = x.shape
    nt = seq // tm
    hb = tm // HALO
    nhb = seq // HALO
    d_in = w_in.shape[1]
    d_conv = conv_w.shape[1]
    full = lambda *shape: pl.BlockSpec(shape, lambda b, i: (0,) * len(shape))
    body = functools.partial(_mixer_body, apply_ln0=apply_ln0, alpha=alpha, n_groups=n_groups,
                             per_group=per_group, seq=seq, sort_rows=sort_rows)
    nq = tm // sort_rows
    return pl.pallas_call(
        body,
        grid=(bsz, nt),
        in_specs=[
            pl.BlockSpec((1, tm, d), lambda b, i: (b, i, 0)),
            pl.BlockSpec((1, HALO, d), lambda b, i: (b, jnp.maximum(i * hb - 1, 0), 0)),
            pl.BlockSpec((1, HALO, d), lambda b, i: (b, jnp.minimum((i + 1) * hb, nhb - 1), 0)),
            full(1, d), full(1, d),
            full(d, d_in), full(1, d_in),
            full(CONV_WIDTH, d_conv),
            full(*pool_w.shape), full(1, pool_scale.shape[1]),
            full(d, d), full(1, d), full(1, d),
            full(d, 2 * LANES), full(1, LANES),
        ],
        out_specs=[
            pl.BlockSpec((1, tm, d), lambda b, i: (b, i, 0)),
            pl.BlockSpec((1, SUBLANES, tm), lambda b, i: (b, 0, i)),
            pl.BlockSpec((1, tm, LANES), lambda b, i: (b, i, 0)),
            pl.BlockSpec((1, nq, SUBLANES, LANES), lambda b, i: (b, i, 0, 0)),
        ],
        out_shape=[
            jax.ShapeDtypeStruct((bsz, seq, d), F32),
            jax.ShapeDtypeStruct((bsz, SUBLANES, seq), I32),
            jax.ShapeDtypeStruct((bsz, seq, LANES), F32),
            jax.ShapeDtypeStruct((bsz, nt * nq, SUBLANES, LANES), F32),
        ],
        compiler_params=pltpu.CompilerParams(
            dimension_semantics=("arbitrary", "arbitrary"), vmem_limit_bytes=VMEM_LIMIT),
        name="mixer",
    )(x, x, x, g0, b0, w_in, b_in, conv_w, pool_w, pool_scale, w_o, g1, b1, w_r, b_r)


def _tables_body(cnt_ref, gstart_ref, lstart_ref, len_ref):
    c = cnt_ref[...]
    n_tiles = c.shape[0]
    row = lax.broadcasted_iota(I32, c.shape, 0)
    lane = lax.broadcasted_iota(I32, c.shape, 1)

    def prefix(v, idx, size, axis):
        k = 1
        while k < size:
            v = v + jnp.where(idx >= k, pltpu.roll(v, k, axis), 0.0)
            k *= 2
        return v

    over_tiles = prefix(c, row, n_tiles, 0)
    tot = jnp.broadcast_to(over_tiles[n_tiles - 1:n_tiles], c.shape)
    expert_start = prefix(tot, lane, LANES, 1) - tot
    gstart_ref[...] = (expert_start + over_tiles - c).astype(I32)
    lstart_ref[...] = (prefix(c, lane, LANES, 1) - c).astype(I32)
    len_ref[...] = c.astype(I32)


def _run_tables(cnt, n_experts):
    out = jax.ShapeDtypeStruct(cnt.shape, I32)
    tabs = pl.pallas_call(_tables_body, out_shape=[out, out, out], name="run_tables")(cnt)
    return tuple(t[:, :n_experts].reshape(-1) for t in tabs)


RUN_CHUNK_LOG2 = 6


def _slots(ref, first, n, rps):
    start = first * rps
    if not isinstance(start, int):
        start = pl.multiple_of(start, rps)
    return ref.at[pl.ds(start, n * rps)]


def _start_run_copies(tabs, tile, n_experts, make_copy):
    gstart_ref, lstart_ref, len_ref = tabs
    big = 1 << RUN_CHUNK_LOG2

    def run_of(e):
        idx = tile * n_experts + jnp.minimum(e, n_experts - 1)
        return len_ref[idx], lstart_ref[idx], gstart_ref[idx]

    def per_expert(e, run):
        n, ls, gs = run
        run = run_of(e + 1)
        n_big = n >> RUN_CHUNK_LOG2

        def big_piece(c, carry2):
            make_copy(ls + c * big, gs + c * big, big).start()
            return carry2
        lax.fori_loop(0, n_big, big_piece, 0)
        off = n_big * big
        for bit in reversed(range(RUN_CHUNK_LOG2)):
            size = 1 << bit

            @pl.when((n & size) != 0)
            def _(off=off, size=size, bit=bit):
                make_copy(ls + off, gs + off, size).start(priority=bit % 2)
            off = off + (n & size)
        return run
    lax.fori_loop(0, n_experts, per_expert, run_of(0), unroll=2)


def _dispatch_body(gstart_ref, lstart_ref, len_ref, x1_ref, ri_ref, xs_ref, buf_ref, sem, *, n_experts):
    tile = pl.program_id(0) * pl.num_programs(1) + pl.program_id(1)
    tabs = (gstart_ref, lstart_ref, len_ref)
    tm, d = x1_ref.shape[1:]
    rps = d // 2 // LANES

    n_tiles = pl.num_programs(0) * pl.num_programs(1)
    par = lax.rem(tile, 2)

    def copier(which):
        def make_copy(local, glob, n):
            return pltpu.make_async_copy(_slots(buf_ref.at[which], local, n, rps),
                                         _slots(xs_ref, glob, n, rps), sem.at[which])
        return make_copy

    def wait_all(which):
        copier(which)(0, 0, TOP_K * tm).wait()

    @pl.when(tile >= 2)
    def _():
        wait_all(par)

    lpos = ri_ref[0]
    s_idx = lax.broadcasted_iota(I32, (TOP_K * tm, tm), 0)
    perm = jnp.logical_or(s_idx == lpos[0:1], s_idx == lpos[1:2]).astype(BF16)
    sorted_rows = _pack_exact_bf16_pairs(_dot(perm, x1_ref[0].astype(BF16)))
    _store_slot_rows(buf_ref.at[par], sorted_rows)
    _start_run_copies(tabs, tile, n_experts, copier(par))

    @pl.when(tile == n_tiles - 1)
    def _():
        wait_all(par)

        @pl.when(tile >= 1)
        def _():
            wait_all(1 - par)


def _dispatch(tabs, x1, ri, *, tm, n_slots, n_experts):
    bsz, seq, d = x1.shape
    nt = seq // tm
    rps = d // 2 // LANES
    grid_spec = pltpu.PrefetchScalarGridSpec(
        num_scalar_prefetch=3,
        grid=(bsz, nt),
        in_specs=[
            pl.BlockSpec((1, tm, d), lambda b, i, *_: (b, i, 0)),
            pl.BlockSpec((1, SUBLANES, tm), lambda b, i, *_: (b, 0, i)),
        ],
        out_specs=pl.BlockSpec(memory_space=pl.ANY),
        scratch_shapes=[pltpu.VMEM((2, TOP_K * tm * rps, LANES), U32), pltpu.SemaphoreType.DMA((2,))],
    )
    return pl.pallas_call(
        functools.partial(_dispatch_body, n_experts=n_experts),
        grid_spec=grid_spec,
        out_shape=jax.ShapeDtypeStruct((n_slots * rps, LANES), U32),
        compiler_params=pltpu.CompilerParams(
            dimension_semantics=("arbitrary", "arbitrary"), vmem_limit_bytes=VMEM_LIMIT),
        name="dispatch",
    )(*tabs, x1, ri)


def _experts_body(blk_ref, exp_ref, lo_ref, hi_ref, new_ref, par_ref, nxt_ref, xs_ref, w1_ref, w3_ref, w2_ref,
                  ys_ref, f1_ref, f3_ref, f2_ref, w1b_ref, w3b_ref, w2b_ref, sem, *, layer):
    it = pl.program_id(0)
    lo = lo_ref[it]
    rps = xs_ref.shape[0] // SLOT_BLOCK

    def weight_copies(expert, slot):
        return [pltpu.make_async_copy(w_ref.at[layer, expert], f_ref.at[slot], sem.at[slot, k])
                for k, (w_ref, f_ref) in enumerate(((w1_ref, f1_ref), (w3_ref, f3_ref), (w2_ref, f2_ref)))]

    @pl.when(it == 0)
    def _():
        for c in weight_copies(exp_ref[0], 0):
            c.start()

    @pl.when(new_ref[it] == 1)
    def _():
        slot = par_ref[it]
        for c in weight_copies(exp_ref[it], slot):
            c.wait()
        w1b_ref[...] = f1_ref[slot].astype(BF16)
        w3b_ref[...] = f3_ref[slot].astype(BF16)
        w2b_ref[...] = f2_ref[slot].astype(BF16)

        @pl.when(nxt_ref[it] >= 0)
        def _():
            for c in weight_copies(nxt_ref[it], 1 - slot):
                c.start()

    def ffn(first, n):
        hi_half, lo_half = _unpack_bf16_pairs(_load_slot_rows(xs_ref, n, rps, first))
        xb = jnp.concatenate([hi_half.astype(BF16), lo_half.astype(BF16)], axis=1)
        h = jax.nn.silu(_dot(xb, w1b_ref[...])) * _dot(xb, w3b_ref[...])
        return _pack_bf16_pairs(_dot(h.astype(BF16), w2b_ref[...]))

    def run(first, n):
        y = ffn(first, n)

        @pl.when(lo <= first)
        def _():
            _store_slot_rows(ys_ref, y, first)

        @pl.when(lo > first)
        def _():
            m = min(n, half)
            rows = first + lax.broadcasted_iota(I32, (m, y.shape[1]), 0)
            _store_slot_rows(ys_ref, jnp.where(rows >= lo, y[:m], _load_slot_rows(ys_ref, m, rps, first)), first)
            if n > m:
                _store_slot_rows(ys_ref, y[m:], first + m)

    half = SLOT_BLOCK // 2
    need_lo = lo < half
    need_hi = hi_ref[it] > half
    pl.when(jnp.logical_and(need_lo, need_hi))(lambda: run(0, SLOT_BLOCK))
    pl.when(jnp.logical_and(need_lo, jnp.logical_not(need_hi)))(lambda: run(0, half))
    pl.when(jnp.logical_and(jnp.logical_not(need_lo), need_hi))(lambda: run(half, half))


def _experts(items, xs, w1, w3, w2, *, layer):
    rows = xs.shape[0]
    _, _, d, d_e = w1.shape
    rps = d // 2 // LANES
    n_items = items[0].shape[0]
    grid_spec = pltpu.PrefetchScalarGridSpec(
        num_scalar_prefetch=len(items),
        grid=(n_items,),
        in_specs=[
            pl.BlockSpec((SLOT_BLOCK * rps, LANES), lambda i, blk, *_: (blk[i], 0)),
            pl.BlockSpec(memory_space=pl.ANY),
            pl.BlockSpec(memory_space=pl.ANY),
            pl.BlockSpec(memory_space=pl.ANY),
        ],
        out_specs=pl.BlockSpec((SLOT_BLOCK * rps, LANES), lambda i, blk, *_: (blk[i], 0)),
        scratch_shapes=[pltpu.VMEM((2, d, d_e), F32), pltpu.VMEM((2, d, d_e), F32),
                        pltpu.VMEM((2, d_e, d), F32),
                        pltpu.VMEM((d, d_e), BF16), pltpu.VMEM((d, d_e), BF16),
                        pltpu.VMEM((d_e, d), BF16), pltpu.SemaphoreType.DMA((2, 3))],
    )
    return pl.pallas_call(
        functools.partial(_experts_body, layer=layer),
        grid_spec=grid_spec,
        out_shape=jax.ShapeDtypeStruct((rows, LANES), U32),
        compiler_params=pltpu.CompilerParams(
            dimension_semantics=("arbitrary",), vmem_limit_bytes=VMEM_LIMIT),
        name="experts",
    )(*items, xs, w1, w3, w2)


def _expert_items(cnt, n_experts, n_slots):
    n_blocks = n_slots // SLOT_BLOCK
    n_items = n_blocks + n_experts - 1
    tot = jnp.sum(cnt, axis=0)[:n_experts].astype(I32)
    end = jnp.cumsum(tot)
    start = end - tot
    first_blk = start // SLOT_BLOCK
    n_it = jnp.where(tot > 0, (end - 1) // SLOT_BLOCK - first_blk + 1, 0)
    it_end = jnp.cumsum(n_it)
    it_start = it_end - n_it
    idx = jnp.arange(n_items, dtype=I32)
    last_live = it_end[-1] - 1
    live = idx <= last_live
    at = jnp.minimum(idx, last_live)
    expert = jnp.sum(it_end[None, :] <= at[:, None], axis=1).astype(I32)
    onehot = expert[:, None] == jnp.arange(n_experts, dtype=I32)[None, :]
    pick = lambda v: jnp.sum(jnp.where(onehot, v[None, :], 0), axis=1)
    block = pick(first_blk) + idx - pick(it_start)
    item_block = jnp.where(live, block, n_blocks - 1).astype(I32)
    item_lo = jnp.where(live, jnp.maximum(pick(start) - block * SLOT_BLOCK, 0), SLOT_BLOCK).astype(I32)
    item_hi = jnp.where(live, jnp.minimum(pick(end) - block * SLOT_BLOCK, SLOT_BLOCK), 0).astype(I32)
    prev_expert = jnp.concatenate([jnp.full((1,), -1, I32), expert[:-1]])
    is_new = expert != prev_expert
    slot = (jnp.cumsum(is_new.astype(I32)) - 1) % 2
    later_new = lax.cummin(jnp.where(is_new, idx, n_items)[::-1], axis=0)[::-1]
    next_new = jnp.concatenate([later_new[1:], jnp.full((1,), n_items, I32)])
    next_expert = jnp.where(next_new < n_items, expert[jnp.minimum(next_new, n_items - 1)], -1)
    return (item_block, expert, item_lo, item_hi, is_new.astype(I32), slot.astype(I32),
            next_expert.astype(I32))


def _combine_body(gstart_ref, lstart_ref, len_ref, x1_ref, p_ref, wcol_ref, ys_ref, wg_ref, bg_ref, wp_ref,
                  g2_ref, b2_ref, out_ref, buf_ref, sem, *, alpha, n_experts):
    tile = pl.program_id(0) * pl.num_programs(1) + pl.program_id(1)
    tabs = (gstart_ref, lstart_ref, len_ref)
    tm, d = x1_ref.shape[1:]
    rps = d // 2 // LANES

    n_tiles = pl.num_programs(0) * pl.num_programs(1)
    par = lax.rem(tile, 2)

    def copier(which):
        def make_copy(local, glob, n):
            return pltpu.make_async_copy(_slots(ys_ref, glob, n, rps),
                                         _slots(buf_ref.at[which], local, n, rps), sem.at[which])
        return make_copy

    @pl.when(tile == 0)
    def _():
        _start_run_copies(tabs, tile, n_experts, copier(par))

    @pl.when(tile + 1 < n_tiles)
    def _():
        _start_run_copies(tabs, tile + 1, n_experts, copier(1 - par))
    copier(par)(0, 0, TOP_K * tm).wait()

    hi_half, lo_half = _unpack_bf16_pairs(_load_slot_rows(buf_ref.at[par], TOP_K * tm, rps))
    ys = jnp.concatenate([hi_half.astype(BF16), lo_half.astype(BF16)], axis=1)
    n = tm // COMBINE_CHAINS

    def ple_matmuls(s):
        x1 = x1_ref[0, s * n:(s + 1) * n]
        gate_logits = _dot(x1.astype(BF16), wg_ref[...]) + bg_ref[...]
        return x1, gate_logits, _dot(p_ref[0, s * n:(s + 1) * n].astype(BF16), wp_ref[...])

    def finish(s, x1, gate_logits, proj):
        wcol = wcol_ref[0, s * n:(s + 1) * n]
        s_idx = lax.broadcasted_iota(I32, (n, TOP_K * tm), 1)
        select = jnp.zeros((n, TOP_K * tm), F32)
        for kk in range(TOP_K):
            at_kk = s_idx == wcol[:, TOP_K + kk:TOP_K + kk + 1].astype(I32)
            select = jnp.where(at_kk, wcol[:, kk:kk + 1], select)
        ffn = _dot(select.astype(BF16), ys)
        ple = jax.nn.sigmoid(gate_logits) * proj
        out_ref[0, s * n:(s + 1) * n] = _layernorm(alpha * x1 + ffn + ple, g2_ref[...], b2_ref[...])

    chains = [ple_matmuls(s) for s in range(COMBINE_CHAINS)]
    for s, c in enumerate(chains):
        finish(s, *c)


def _combine(tabs, x1, p, wcol, ys, w_gate, b_gate, w_proj, g2, b2, *, layer, alpha, tm, n_experts):
    bsz, seq, d = x1.shape
    nt = seq // tm
    d_ple = p.shape[-1]
    rps = d // 2 // LANES
    full = lambda *shape: pl.BlockSpec(shape, lambda b, i, *_: (0,) * len(shape))
    grid_spec = pltpu.PrefetchScalarGridSpec(
        num_scalar_prefetch=3,
        grid=(bsz, nt),
        in_specs=[
            pl.BlockSpec((1, tm, d), lambda b, i, *_: (b, i, 0)),
            pl.BlockSpec((None, 1, tm, d_ple), lambda b, i, *_: (layer, b, i, 0)),
            pl.BlockSpec((1, tm, LANES), lambda b, i, *_: (b, i, 0)),
            pl.BlockSpec(memory_space=pl.ANY),
            full(d, d), full(1, d), full(d_ple, d),
            full(1, d), full(1, d),
        ],
        out_specs=pl.BlockSpec((1, tm, d), lambda b, i, *_: (b, i, 0)),
        scratch_shapes=[pltpu.VMEM((2, TOP_K * tm * rps, LANES), U32), pltpu.SemaphoreType.DMA((2,))],
    )
    return pl.pallas_call(
        functools.partial(_combine_body, alpha=alpha, n_experts=n_experts),
        grid_spec=grid_spec,
        out_shape=jax.ShapeDtypeStruct((bsz, seq, d), F32),
        compiler_params=pltpu.CompilerParams(
            dimension_semantics=("arbitrary", "arbitrary"), vmem_limit_bytes=VMEM_LIMIT),
        name="combine",
    )(*tabs, x1, p, wcol, ys, w_gate, b_gate, w_proj, g2, b2)


def _router_weights(w_rg, b_rg, w_re, b_re):
    d, n_groups = w_rg.shape
    per_group = w_re.shape[2]
    assert n_groups <= SUBLANES and per_group <= SUBLANES and SUBLANES * (n_groups + 1) <= LANES
    w_e = jnp.pad(jnp.transpose(w_re, (1, 0, 2)), ((0, 0), (0, 0), (0, SUBLANES - per_group)))
    b_e = jnp.pad(b_re, ((0, 0), (0, SUBLANES - per_group)))
    tail = LANES - SUBLANES * (n_groups + 1)
    w = jnp.concatenate([jnp.pad(w_rg, ((0, 0), (0, SUBLANES - n_groups))), w_e.reshape(d, SUBLANES * n_groups),
                         jnp.zeros((d, tail), F32)], axis=1)
    b = jnp.concatenate([jnp.pad(b_rg, (0, SUBLANES - n_groups)), b_e.reshape(-1), jnp.zeros((tail,), F32)])
    w_hi = w.astype(BF16)
    w_lo = (w - w_hi.astype(F32)).astype(BF16)
    return jnp.concatenate([w_hi, w_lo], axis=1), b[None, :]


def kernel(x, p, ln0_g, ln0_b, w_in, b_in, conv_w, pool_w, pool_scale, w_o, ln1_g, ln1_b,
           w_router_group, b_router_group, w_router_expert, b_router_expert, w1, w3, w2,
           w_ple_gate, b_ple_gate, w_ple_proj, ln2_g, ln2_b):
    bsz, seq, d = x.shape
    depth = w_in.shape[0]
    n_groups, per_group = w_router_expert.shape[1], w_router_expert.shape[3]
    n_experts = n_groups * per_group
    alpha = (2 * depth) ** 0.25
    tm = min(SORT_ROWS, seq)
    tm_mix = min(MIX_ROWS, seq)
    n_slots = bsz * seq * TOP_K
    assert seq % tm_mix == 0 and tm_mix % tm == 0 and tm % LANES == 0
    assert n_slots % SLOT_BLOCK == 0 and n_experts <= LANES
    assert d % (2 * LANES) == 0
    row = lambda a: a[None, :]

    for i in range(depth):
        w_r, b_r = _router_weights(w_router_group[i], b_router_group[i],
                                   w_router_expert[i], b_router_expert[i])
        x1, ri, wcol, cnt = _mixer(
            x, row(ln0_g), row(ln0_b), w_in[i].astype(BF16), row(b_in[i]), conv_w[i],
            pool_w[i].astype(BF16), row(pool_scale[i]), w_o[i].astype(BF16),
            row(ln1_g[i]), row(ln1_b[i]), w_r, b_r,
            apply_ln0=(i == 0), alpha=alpha, n_groups=n_groups, per_group=per_group, tm=tm_mix,
            sort_rows=tm)
        cnt = cnt[:, :, 0, :].reshape(-1, LANES)
        tabs = _run_tables(cnt, n_experts)
        items = _expert_items(cnt, n_experts, n_slots)
        xs = _dispatch(tabs, x1, ri, tm=tm, n_slots=n_slots, n_experts=n_experts)
        ys = _experts(items, xs, w1, w3, w2, layer=i)
        x = _combine(tabs, x1, p, wcol, ys, w_ple_gate[i].astype(BF16), row(b_ple_gate[i]),
                     w_ple_proj[i].astype(BF16), row(ln2_g[i]), row(ln2_b[i]),
                     layer=i, alpha=alpha, tm=tm, n_experts=n_experts)
    return x
```

```python
import functools

import jax
import jax.numpy as jnp
from jax import lax
from jax.experimental import pallas as pl
from jax.experimental.pallas import tpu as pltpu

POOL_WINDOWS = (2, 4, 8, 16)
CONV_WIDTH = 3
TOP_K = 2
LN_EPS = 1e-5
HALO = 8
LANES = 128
SUBLANES = 8
SORT_ROWS = 512
MIX_ROWS = 1024
MIX_CHAINS = 2
COMBINE_CHAINS = 2
STEP_TILES = 2
SLOT_BLOCK = 512
VMEM_LIMIT = 56 * 1024 * 1024

F32 = jnp.float32
BF16 = jnp.bfloat16
I32 = jnp.int32
U32 = jnp.uint32


def _layernorm(v, g, b):
    mu = jnp.mean(v, axis=-1, keepdims=True)
    c = v - mu
    var = jnp.mean(c * c, axis=-1, keepdims=True)
    return c * lax.rsqrt(var + LN_EPS) * g + b


def _dot(a, b):
    return jnp.dot(a, b, preferred_element_type=F32)


def _pack_bf16_pairs(v):
    c = v.shape[1] // 2
    hi = pltpu.bitcast(v[:, :c].astype(BF16).astype(F32), U32)
    lo = pltpu.bitcast(v[:, c:].astype(BF16).astype(F32), U32)
    return hi | (lo >> 16)


def _pack_exact_bf16_pairs(v):
    c = v.shape[1] // 2
    return pltpu.bitcast(v[:, :c], U32) | (pltpu.bitcast(v[:, c:], U32) >> 16)


def _unpack_bf16_pairs(u):
    hi = pltpu.bitcast(u & jnp.uint32(0xFFFF0000), F32)
    lo = pltpu.bitcast(u << 16, F32)
    return hi, lo


def _store_slot_rows(ref, packed, first=0):
    n, c = packed.shape
    rps = c // LANES
    for k in range(rps):
        ref[pl.ds(first * rps + k, n, stride=rps), :] = packed[:, k * LANES:(k + 1) * LANES]


def _load_slot_rows(ref, n, rps, first=0):
    return jnp.concatenate([ref[pl.ds(first * rps + k, n, stride=rps), :] for k in range(rps)], axis=1)


def _mixer_body(x_ref, xp_ref, xn_ref, g0_ref, b0_ref, win_ref, bin_ref, cw_ref, pw_ref, ps_ref,
                wo_ref, g1_ref, b1_ref, wr_ref, br_ref,
                x1_ref, ri_ref, wcol_ref, cnt_ref, *, apply_ln0, alpha, n_groups, per_group, seq,
                sort_rows):
    i = pl.program_id(1)
    nt = pl.num_programs(1)
    tm = x_ref.shape[1]
    n = tm // MIX_CHAINS
    rows = n + 2 * HALO
    d_conv = cw_ref.shape[1]
    dg = pw_ref.shape[1]

    xe = jnp.concatenate([xp_ref[0], x_ref[0], xn_ref[0]], axis=0)
    if apply_ln0:
        xe = _layernorm(xe, g0_ref[...], b0_ref[...])
    cw = cw_ref[...]
    ps = ps_ref[...]

    def in_proj(s):
        xs = xe[s * n:s * n + rows]
        return xs, _dot(xs.astype(BF16), win_ref[...]) + bin_ref[...]

    def token_mix(s, z):
        r = lax.broadcasted_iota(I32, (rows, 1), 0)
        lo_ok = jnp.logical_or(r >= HALO, i > 0) if s == 0 else True
        hi_ok = jnp.logical_or(r < n + HALO, i < nt - 1) if s == MIX_CHAINS - 1 else True
        if s == 0 or s == MIX_CHAINS - 1:
            valid = jnp.logical_and(lo_ok, hi_ok)
            mask = lambda a: jnp.where(valid, a, 0.0)
        else:
            mask = lambda a: a

        h = z[:, :d_conv]
        gate_b = z[HALO:HALO + n, d_conv:2 * d_conv]
        gate_c = z[:, 2 * d_conv:3 * d_conv]
        v = mask(gate_c * h)
        conv = (cw[0:1] * pltpu.roll(v, 1, 0)[HALO:HALO + n]
                + cw[1:2] * v[HALO:HALO + n]
                + cw[2:3] * pltpu.roll(v, rows - 1, 0)[HALO:HALO + n])
        parts = [gate_b * conv]

        pos = i * tm + s * n + r[HALO:HALO + n] - HALO
        for g, w in enumerate(POOL_WINDOWS):
            left = w // 2
            right = w - 1 - left
            c0 = 3 * d_conv + g * dg
            ug = mask(z[:, c0:c0 + dg])
            acc = ug
            k = 1
            while k < w:
                acc = acc + pltpu.roll(acc, k, 0)
                k *= 2
            if right:
                acc = pltpu.roll(acc, rows - right, 0)

            def edge(lo):
                p = pos[lo:lo + HALO]
                cnt = (jnp.minimum(p + right + 1, seq) - jnp.maximum(p - left, 0)).astype(F32)
                return acc[HALO + lo:2 * HALO + lo] / cnt - ug[HALO + lo:2 * HALO + lo]
            inner = acc[2 * HALO:n] * (1.0 / w) - ug[2 * HALO:n]
            pooled = jnp.concatenate([edge(0), inner, edge(n - HALO)], axis=0)
            parts.append(_dot(pooled.astype(BF16), pw_ref[g]) * ps[:, g * dg:(g + 1) * dg])
        return jnp.concatenate(parts, axis=1).astype(BF16)

    def out_proj(xs, cat):
        mix = _dot(cat, wo_ref[...])
        return _layernorm(alpha * xs[HALO:HALO + n] + mix, g1_ref[...], b1_ref[...])

    def route(x1s):
        xh = x1s.astype(BF16)
        xl = (x1s - xh.astype(F32)).astype(BF16)
        l_hi = _dot(xh, wr_ref[...])
        l_lo = _dot(xl, wr_ref[:, :LANES])
        return l_hi[:, :LANES] + l_hi[:, LANES:] + l_lo + br_ref[...]

    chains = [in_proj(s) for s in range(MIX_CHAINS)]
    x1s = [out_proj(xs, token_mix(s, z)) for s, (xs, z) in enumerate(chains)]
    x1 = jnp.concatenate(x1s, axis=0)
    x1_ref[0] = x1
    logits = jnp.concatenate([route(v) for v in x1s], axis=0)
    ri_ref[0] = jnp.zeros(ri_ref.shape[1:], I32)
    for q in range(tm // sort_rows):
        rows_q = slice(q * sort_rows, (q + 1) * sort_rows)
        lpos0, lpos1, gw0, gw1, counts = _route_and_sort(logits[rows_q], n_groups, per_group)
        ri_ref[0, q, 0:1, :] = lpos0.astype(I32)
        ri_ref[0, q, 1:2, :] = lpos1.astype(I32)
        rowl = lax.broadcasted_iota(I32, (LANES, sort_rows), 0)
        wrows = jnp.where(rowl == 0, gw0, jnp.where(rowl == 1, gw1,
                          jnp.where(rowl == 2, lpos0, jnp.where(rowl == 3, lpos1, 0.0))))
        wcol_ref[0, rows_q, :] = wrows.T
        cnt_ref[0, q] = counts


def _route_and_sort(logits, n_groups, per_group):
    n = logits.shape[0]
    lt = logits.T
    row8 = lax.broadcasted_iota(I32, (SUBLANES, n), 0)
    neg = jnp.float32(-jnp.inf)
    lg = jnp.where(row8 < n_groups, lt[0:SUBLANES], neg)
    mg = jnp.max(lg, axis=0, keepdims=True)
    g_w = 1.0 / jnp.sum(jnp.exp(lg - mg), axis=0, keepdims=True)
    g_sel = jnp.min(jnp.where(lg == mg, row8, SUBLANES), axis=0, keepdims=True)
    le = lt[SUBLANES:2 * SUBLANES]
    for g in range(1, n_groups):
        le = jnp.where(g_sel == g, lt[SUBLANES * (g + 1):SUBLANES * (g + 2)], le)
    le = jnp.where(row8 < per_group, le, neg)
    v1 = jnp.max(le, axis=0, keepdims=True)
    i1 = jnp.min(jnp.where(le == v1, row8, SUBLANES), axis=0, keepdims=True)
    le2 = jnp.where(row8 == i1, neg, le)
    v2 = jnp.max(le2, axis=0, keepdims=True)
    i2 = jnp.min(jnp.where(le2 == v2, row8, SUBLANES), axis=0, keepdims=True)
    e21 = jnp.exp(v2 - v1)
    w_first = 1.0 / (1.0 + e21)
    gw0 = w_first * g_w
    gw1 = (e21 * w_first) * g_w
    eid0 = g_sel * per_group + i1
    eid1 = g_sel * per_group + i2

    rowl = lax.broadcasted_iota(I32, (LANES, n), 0)
    oh0 = rowl == eid0
    oh1 = rowl == eid1
    a_idx = lax.broadcasted_iota(I32, (n, n), 0)
    b_idx = lax.broadcasted_iota(I32, (n, n), 1)
    before = (a_idx < b_idx).astype(BF16)
    r0 = _dot(oh0.astype(BF16), before)
    r1 = _dot(oh1.astype(BF16), before)
    cnt0 = jnp.sum(oh0.astype(F32), axis=1, keepdims=True)
    lower = jnp.sum((rowl > eid0).astype(F32) + (rowl > eid1).astype(F32),
                    axis=1, keepdims=True)
    lpos0 = jnp.sum(jnp.where(oh0, lower + r0, 0.0), axis=0, keepdims=True)
    lpos1 = jnp.sum(jnp.where(oh1, lower + cnt0 + r1, 0.0), axis=0, keepdims=True)

    both = jnp.logical_or(oh0, oh1).astype(BF16)
    counts = lax.dot_general(jnp.ones((SUBLANES, n), BF16), both, (((1,), (1,)), ((), ())),
                             preferred_element_type=F32)
    return lpos0, lpos1, gw0, gw1, counts


def _mixer(x, g0, b0, w_in, b_in, conv_w, pool_w, pool_scale, w_o, g1, b1, w_r, b_r, *,
           apply_ln0, alpha, n_groups, per_group, tm, sort_rows):
    bsz, seq, d = x.shape
    nt = seq // tm
    hb = tm // HALO
    nhb = seq // HALO
    d_in = w_in.shape[1]
    d_conv = conv_w.shape[1]
    full = lambda *shape: pl.BlockSpec(shape, lambda b, i: (0,) * len(shape))
    body = functools.partial(_mixer_body, apply_ln0=apply_ln0, alpha=alpha, n_groups=n_groups,
                             per_group=per_group, seq=seq, sort_rows=sort_rows)
    nq = tm // sort_rows
    return pl.pallas_call(
        body,
        grid=(bsz, nt),
        in_specs=[
            pl.BlockSpec((1, tm, d), lambda b, i: (b, i, 0)),
            pl.BlockSpec((1, HALO, d), lambda b, i: (b, jnp.maximum(i * hb - 1, 0), 0)),
            pl.BlockSpec((1, HALO, d), lambda b, i: (b, jnp.minimum((i + 1) * hb, nhb - 1), 0)),
            full(1, d), full(1, d),
            full(d, d_in), full(1, d_in),
            full(CONV_WIDTH, d_conv),
            full(*pool_w.shape), full(1, pool_scale.shape[1]),
            full(d, d), full(1, d), full(1, d),
            full(d, 2 * LANES), full(1, LANES),
        ],
        out_specs=[
            pl.BlockSpec((1, tm, d), lambda b, i: (b, i, 0)),
            pl.BlockSpec((1, nq, SUBLANES, sort_rows), lambda b, i: (b, i, 0, 0)),
            pl.BlockSpec((1, tm, LANES), lambda b, i: (b, i, 0)),
            pl.BlockSpec((1, nq, SUBLANES, LANES), lambda b, i: (b, i, 0, 0)),
        ],
        out_shape=[
            jax.ShapeDtypeStruct((bsz, seq, d), F32),
            jax.ShapeDtypeStruct((bsz, nt * nq, SUBLANES, sort_rows), I32),
            jax.ShapeDtypeStruct((bsz, seq, LANES), F32),
            jax.ShapeDtypeStruct((bsz, nt * nq, SUBLANES, LANES), F32),
        ],
        compiler_params=pltpu.CompilerParams(
            dimension_semantics=("arbitrary", "arbitrary"), vmem_limit_bytes=VMEM_LIMIT),
        name="mixer",
    )(x, x, x, g0, b0, w_in, b_in, conv_w, pool_w, pool_scale, w_o, g1, b1, w_r, b_r)


def _tables_body(cnt_ref, gstart_ref, lstart_ref, len_ref, tot_ref):
    c = cnt_ref[:, 0, :]
    n_tiles = c.shape[0]
    row = lax.broadcasted_iota(I32, c.shape, 0)
    lane = lax.broadcasted_iota(I32, c.shape, 1)

    def prefix(v, idx, size, axis):
        k = 1
        while k < size:
            v = v + jnp.where(idx >= k, pltpu.roll(v, k, axis), 0.0)
            k *= 2
        return v

    over_tiles = prefix(c, row, n_tiles, 0)
    tot = jnp.broadcast_to(over_tiles[n_tiles - 1:n_tiles], c.shape)
    expert_start = prefix(tot, lane, LANES, 1) - tot
    gstart_ref[...] = (expert_start + over_tiles - c).astype(I32)
    lstart_ref[...] = (prefix(c, lane, LANES, 1) - c).astype(I32)
    len_ref[...] = c.astype(I32)
    tot_ref[...] = jnp.broadcast_to(over_tiles[n_tiles - 1:n_tiles], tot_ref.shape).astype(I32)


def _run_tables(cnt):
    table = jax.ShapeDtypeStruct((cnt.shape[0], LANES), I32)
    *tabs, tot = pl.pallas_call(
        _tables_body, out_shape=[table, table, table, jax.ShapeDtypeStruct((SUBLANES, LANES), I32)],
        name="run_tables")(cnt)
    return tuple(tabs), tot[0]


def _slots(ref, first, n, rps):
    start = first * rps
    if not isinstance(start, int):
        start = pl.multiple_of(start, rps)
    return ref.at[pl.ds(start, n * rps)]


def _start_run_copies(tabs, tile, n_experts, make_copy):
    gstart_ref, lstart_ref, len_ref = tabs

    def run_of(e):
        e = jnp.minimum(e, n_experts - 1)
        return len_ref[tile, e], lstart_ref[tile, e], gstart_ref[tile, e]

    def per_pair(i, runs):
        nxt = (run_of(2 * i + 2), run_of(2 * i + 3))
        for prio, (n, ls, gs) in enumerate(runs):

            @pl.when(n > 0)
            def _(n=n, ls=ls, gs=gs, prio=prio):
                make_copy(ls, gs, n).start(priority=prio)
        return nxt
    assert n_experts % 2 == 0
    lax.fori_loop(0, n_experts // 2, per_pair, (run_of(0), run_of(1)))


def _dispatch_body(gstart_ref, lstart_ref, len_ref, x1_ref, ri_ref, xs_ref, buf_ref, sem, *, n_experts):
    step = pl.program_id(0) * pl.num_programs(1) + pl.program_id(1)
    n_tiles = pl.num_programs(0) * pl.num_programs(1) * STEP_TILES
    tabs = (gstart_ref, lstart_ref, len_ref)
    tm = ri_ref.shape[3]
    d = x1_ref.shape[2]
    rps = d // 2 // LANES

    def copier(which):
        def make_copy(local, glob, n):
            return pltpu.make_async_copy(_slots(buf_ref.at[which], local, n, rps),
                                         _slots(xs_ref, glob, n, rps), sem.at[which])
        return make_copy

    def wait_all(which):
        copier(which)(0, 0, TOP_K * tm).wait()

    def one_tile(q, carry):
        tile = step * STEP_TILES + q
        par = lax.rem(tile, 2)

        @pl.when(tile >= 2)
        def _():
            wait_all(par)

        lpos = ri_ref[0, q]
        s_idx = lax.broadcasted_iota(I32, (TOP_K * tm, tm), 0)
        perm = jnp.logical_or(s_idx == lpos[0:1], s_idx == lpos[1:2]).astype(BF16)
        x1 = x1_ref[0, pl.ds(pl.multiple_of(q * tm, tm), tm)]
        _store_slot_rows(buf_ref.at[par], _pack_exact_bf16_pairs(_dot(perm, x1.astype(BF16))))
        _start_run_copies(tabs, tile, n_experts, copier(par))

        @pl.when(tile == n_tiles - 1)
        def _():
            wait_all(par)

            @pl.when(tile >= 1)
            def _():
                wait_all(1 - par)
        return carry
    lax.fori_loop(0, STEP_TILES, one_tile, 0)


def _dispatch(tabs, x1, ri, *, tm, n_slots, n_experts):
    bsz, seq, d = x1.shape
    steps = seq // (tm * STEP_TILES)
    rps = d // 2 // LANES
    grid_spec = pltpu.PrefetchScalarGridSpec(
        num_scalar_prefetch=3,
        grid=(bsz, steps),
        in_specs=[
            pl.BlockSpec((1, STEP_TILES * tm, d), lambda b, i, *_: (b, i, 0)),
            pl.BlockSpec((1, STEP_TILES, SUBLANES, tm), lambda b, i, *_: (b, i, 0, 0)),
        ],
        out_specs=pl.BlockSpec(memory_space=pl.ANY),
        scratch_shapes=[pltpu.VMEM((2, TOP_K * tm * rps, LANES), U32), pltpu.SemaphoreType.DMA((2,))],
    )
    return pl.pallas_call(
        functools.partial(_dispatch_body, n_experts=n_experts),
        grid_spec=grid_spec,
        out_shape=jax.ShapeDtypeStruct((n_slots * rps, LANES), U32),
        compiler_params=pltpu.CompilerParams(
            dimension_semantics=("arbitrary", "arbitrary"), vmem_limit_bytes=VMEM_LIMIT),
        name="dispatch",
    )(*tabs, x1, ri)


def _experts_body(blk_ref, exp_ref, lo_ref, hi_ref, new_ref, par_ref, nxt_ref, xs_ref, w1_ref, w3_ref, w2_ref,
                  ys_ref, f1_ref, f3_ref, f2_ref, w1b_ref, w3b_ref, w2b_ref, sem, *, layer):
    it = pl.program_id(0)
    lo = lo_ref[it]
    rps = xs_ref.shape[0] // SLOT_BLOCK

    def weight_copies(expert, slot):
        return [pltpu.make_async_copy(w_ref.at[layer, expert], f_ref.at[slot], sem.at[slot, k])
                for k, (w_ref, f_ref) in enumerate(((w1_ref, f1_ref), (w3_ref, f3_ref), (w2_ref, f2_ref)))]

    @pl.when(it == 0)
    def _():
        for c in weight_copies(exp_ref[0], 0):
            c.start()

    @pl.when(new_ref[it] == 1)
    def _():
        slot = par_ref[it]
        for c in weight_copies(exp_ref[it], slot):
            c.wait()
        w1b_ref[...] = f1_ref[slot].astype(BF16)
        w3b_ref[...] = f3_ref[slot].astype(BF16)
        w2b_ref[...] = f2_ref[slot].astype(BF16)

        @pl.when(nxt_ref[it] >= 0)
        def _():
            for c in weight_copies(nxt_ref[it], 1 - slot):
                c.start()

    def ffn(first, n):
        hi_half, lo_half = _unpack_bf16_pairs(_load_slot_rows(xs_ref, n, rps, first))
        xb = jnp.concatenate([hi_half.astype(BF16), lo_half.astype(BF16)], axis=1)
        h = jax.nn.silu(_dot(xb, w1b_ref[...])) * _dot(xb, w3b_ref[...])
        return _pack_bf16_pairs(_dot(h.astype(BF16), w2b_ref[...]))

    def run(first, n):
        y = ffn(first, n)

        @pl.when(lo <= first)
        def _():
            _store_slot_rows(ys_ref, y, first)

        @pl.when(lo > first)
        def _():
            m = min(n, half)
            rows = first + lax.broadcasted_iota(I32, (m, y.shape[1]), 0)
            _store_slot_rows(ys_ref, jnp.where(rows >= lo, y[:m], _load_slot_rows(ys_ref, m, rps, first)), first)
            if n > m:
                _store_slot_rows(ys_ref, y[m:], first + m)

    half = SLOT_BLOCK // 2
    need_lo = lo < half
    need_hi = hi_ref[it] > half
    pl.when(jnp.logical_and(need_lo, need_hi))(lambda: run(0, SLOT_BLOCK))
    pl.when(jnp.logical_and(need_lo, jnp.logical_not(need_hi)))(lambda: run(0, half))
    pl.when(jnp.logical_and(jnp.logical_not(need_lo), need_hi))(lambda: run(half, half))


def _experts(items, xs, w1, w3, w2, *, layer):
    rows = xs.shape[0]
    _, _, d, d_e = w1.shape
    rps = d // 2 // LANES
    n_items = items[0].shape[0]
    grid_spec = pltpu.PrefetchScalarGridSpec(
        num_scalar_prefetch=len(items),
        grid=(n_items,),
        in_specs=[
            pl.BlockSpec((SLOT_BLOCK * rps, LANES), lambda i, blk, *_: (blk[i], 0)),
            pl.BlockSpec(memory_space=pl.ANY),
            pl.BlockSpec(memory_space=pl.ANY),
            pl.BlockSpec(memory_space=pl.ANY),
        ],
        out_specs=pl.BlockSpec((SLOT_BLOCK * rps, LANES), lambda i, blk, *_: (blk[i], 0)),
        scratch_shapes=[pltpu.VMEM((2, d, d_e), F32), pltpu.VMEM((2, d, d_e), F32),
                        pltpu.VMEM((2, d_e, d), F32),
                        pltpu.VMEM((d, d_e), BF16), pltpu.VMEM((d, d_e), BF16),
                        pltpu.VMEM((d_e, d), BF16), pltpu.SemaphoreType.DMA((2, 3))],
    )
    return pl.pallas_call(
        functools.partial(_experts_body, layer=layer),
        grid_spec=grid_spec,
        out_shape=jax.ShapeDtypeStruct((rows, LANES), U32),
        compiler_params=pltpu.CompilerParams(
            dimension_semantics=("arbitrary",), vmem_limit_bytes=VMEM_LIMIT),
        name="experts",
    )(*items, xs, w1, w3, w2)


def _expert_items(totals, n_experts, n_slots):
    n_blocks = n_slots // SLOT_BLOCK
    n_items = n_blocks + n_experts - 1
    tot = totals[:n_experts]
    end = jnp.cumsum(tot)
    start = end - tot
    first_blk = start // SLOT_BLOCK
    n_it = jnp.where(tot > 0, (end - 1) // SLOT_BLOCK - first_blk + 1, 0)
    it_end = jnp.cumsum(n_it)
    it_start = it_end - n_it
    idx = jnp.arange(n_items, dtype=I32)
    last_live = it_end[-1] - 1
    live = idx <= last_live
    at = jnp.minimum(idx, last_live)
    expert = jnp.sum(it_end[None, :] <= at[:, None], axis=1).astype(I32)
    onehot = expert[:, None] == jnp.arange(n_experts, dtype=I32)[None, :]
    pick = lambda v: jnp.sum(jnp.where(onehot, v[None, :], 0), axis=1)
    block = pick(first_blk) + idx - pick(it_start)
    item_block = jnp.where(live, block, n_blocks - 1).astype(I32)
    item_lo = jnp.where(live, jnp.maximum(pick(start) - block * SLOT_BLOCK, 0), SLOT_BLOCK).astype(I32)
    item_hi = jnp.where(live, jnp.minimum(pick(end) - block * SLOT_BLOCK, SLOT_BLOCK), 0).astype(I32)
    prev_expert = jnp.concatenate([jnp.full((1,), -1, I32), expert[:-1]])
    is_new = expert != prev_expert
    slot = (jnp.cumsum(is_new.astype(I32)) - 1) % 2
    later_new = lax.cummin(jnp.where(is_new, idx, n_items)[::-1], axis=0)[::-1]
    next_new = jnp.concatenate([later_new[1:], jnp.full((1,), n_items, I32)])
    next_expert = jnp.where(next_new < n_items, expert[jnp.minimum(next_new, n_items - 1)], -1)
    return (item_block, expert, item_lo, item_hi, is_new.astype(I32), slot.astype(I32),
            next_expert.astype(I32))


def _combine_body(gstart_ref, lstart_ref, len_ref, x1_ref, p_ref, wcol_ref, ys_ref, wg_ref, bg_ref, wp_ref,
                  g2_ref, b2_ref, out_ref, buf_ref, sem, *, alpha, n_experts, tm):
    step = pl.program_id(0) * pl.num_programs(1) + pl.program_id(1)
    n_tiles = pl.num_programs(0) * pl.num_programs(1) * STEP_TILES
    tabs = (gstart_ref, lstart_ref, len_ref)
    d = x1_ref.shape[2]
    rps = d // 2 // LANES
    n = tm // COMBINE_CHAINS

    def copier(which):
        def make_copy(local, glob, n_slots):
            return pltpu.make_async_copy(_slots(ys_ref, glob, n_slots, rps),
                                         _slots(buf_ref.at[which], local, n_slots, rps), sem.at[which])
        return make_copy

    def one_tile(q, carry):
        tile = step * STEP_TILES + q
        par = lax.rem(tile, 2)

        @pl.when(tile == 0)
        def _():
            _start_run_copies(tabs, tile, n_experts, copier(par))

        @pl.when(tile + 1 < n_tiles)
        def _():
            _start_run_copies(tabs, tile + 1, n_experts, copier(1 - par))
        copier(par)(0, 0, TOP_K * tm).wait()

        hi_half, lo_half = _unpack_bf16_pairs(_load_slot_rows(buf_ref.at[par], TOP_K * tm, rps))
        ys = jnp.concatenate([hi_half.astype(BF16), lo_half.astype(BF16)], axis=1)

        def rows_of(s):
            return pl.ds(pl.multiple_of(q * tm + s * n, n), n)

        def ple_matmuls(s):
            x1 = x1_ref[0, rows_of(s)]
            gate_logits = _dot(x1.astype(BF16), wg_ref[...]) + bg_ref[...]
            return x1, gate_logits, _dot(p_ref[0, rows_of(s)].astype(BF16), wp_ref[...])

        def finish(s, x1, gate_logits, proj):
            wcol = wcol_ref[0, rows_of(s)]
            s_idx = lax.broadcasted_iota(I32, (n, TOP_K * tm), 1)
            select = jnp.zeros((n, TOP_K * tm), F32)
            for kk in range(TOP_K):
                at_kk = s_idx == wcol[:, TOP_K + kk:TOP_K + kk + 1].astype(I32)
                select = jnp.where(at_kk, wcol[:, kk:kk + 1], select)
            ffn = _dot(select.astype(BF16), ys)
            ple = jax.nn.sigmoid(gate_logits) * proj
            out_ref[0, rows_of(s)] = _layernorm(alpha * x1 + ffn + ple, g2_ref[...], b2_ref[...])

        chains = [ple_matmuls(s) for s in range(COMBINE_CHAINS)]
        for s, c in enumerate(chains):
            finish(s, *c)
        return carry
    lax.fori_loop(0, STEP_TILES, one_tile, 0)


def _combine(tabs, x1, p, wcol, ys, w_gate, b_gate, w_proj, g2, b2, *, layer, alpha, tm, n_experts):
    bsz, seq, d = x1.shape
    rows = STEP_TILES * tm
    steps = seq // rows
    d_ple = p.shape[-1]
    rps = d // 2 // LANES
    full = lambda *shape: pl.BlockSpec(shape, lambda b, i, *_: (0,) * len(shape))
    grid_spec = pltpu.PrefetchScalarGridSpec(
        num_scalar_prefetch=3,
        grid=(bsz, steps),
        in_specs=[
            pl.BlockSpec((1, rows, d), lambda b, i, *_: (b, i, 0)),
            pl.BlockSpec((None, 1, rows, d_ple), lambda b, i, *_: (layer, b, i, 0)),
            pl.BlockSpec((1, rows, LANES), lambda b, i, *_: (b, i, 0)),
            pl.BlockSpec(memory_space=pl.ANY),
            full(d, d), full(1, d), full(d_ple, d),
            full(1, d), full(1, d),
        ],
        out_specs=pl.BlockSpec((1, rows, d), lambda b, i, *_: (b, i, 0)),
        scratch_shapes=[pltpu.VMEM((2, TOP_K * tm * rps, LANES), U32), pltpu.SemaphoreType.DMA((2,))],
    )
    return pl.pallas_call(
        functools.partial(_combine_body, alpha=alpha, n_experts=n_experts, tm=tm),
        grid_spec=grid_spec,
        out_shape=jax.ShapeDtypeStruct((bsz, seq, d), F32),
        compiler_params=pltpu.CompilerParams(
            dimension_semantics=("arbitrary", "arbitrary"), vmem_limit_bytes=VMEM_LIMIT),
        name="combine",
    )(*tabs, x1, p, wcol, ys, w_gate, b_gate, w_proj, g2, b2)


def _router_weights(w_rg, b_rg, w_re, b_re):
    d, n_groups = w_rg.shape
    per_group = w_re.shape[2]
    assert n_groups <= SUBLANES and per_group <= SUBLANES and SUBLANES * (n_groups + 1) <= LANES
    w_e = jnp.pad(jnp.transpose(w_re, (1, 0, 2)), ((0, 0), (0, 0), (0, SUBLANES - per_group)))
    b_e = jnp.pad(b_re, ((0, 0), (0, SUBLANES - per_group)))
    tail = LANES - SUBLANES * (n_groups + 1)
    w = jnp.concatenate([jnp.pad(w_rg, ((0, 0), (0, SUBLANES - n_groups))), w_e.reshape(d, SUBLANES * n_groups),
                         jnp.zeros((d, tail), F32)], axis=1)
    b = jnp.concatenate([jnp.pad(b_rg, (0, SUBLANES - n_groups)), b_e.reshape(-1), jnp.zeros((tail,), F32)])
    w_hi = w.astype(BF16)
    w_lo = (w - w_hi.astype(F32)).astype(BF16)
    return jnp.concatenate([w_hi, w_lo], axis=1), b[None, :]


def kernel(x, p, ln0_g, ln0_b, w_in, b_in, conv_w, pool_w, pool_scale, w_o, ln1_g, ln1_b,
           w_router_group, b_router_group, w_router_expert, b_router_expert, w1, w3, w2,
           w_ple_gate, b_ple_gate, w_ple_proj, ln2_g, ln2_b):
    bsz, seq, d = x.shape
    depth = w_in.shape[0]
    n_groups, per_group = w_router_expert.shape[1], w_router_expert.shape[3]
    n_experts = n_groups * per_group
    alpha = (2 * depth) ** 0.25
    tm = min(SORT_ROWS, seq)
    tm_mix = min(MIX_ROWS, seq)
    n_slots = bsz * seq * TOP_K
    assert seq % tm_mix == 0 and tm_mix % tm == 0 and tm % LANES == 0 and seq % (STEP_TILES * tm) == 0
    assert n_slots % SLOT_BLOCK == 0 and n_experts <= LANES
    assert d % (2 * LANES) == 0
    row = lambda a: a[None, :]

    for i in range(depth):
        w_r, b_r = _router_weights(w_router_group[i], b_router_group[i],
                                   w_router_expert[i], b_router_expert[i])
        x1, ri, wcol, cnt = _mixer(
            x, row(ln0_g), row(ln0_b), w_in[i].astype(BF16), row(b_in[i]), conv_w[i],
            pool_w[i].astype(BF16), row(pool_scale[i]), w_o[i].astype(BF16),
            row(ln1_g[i]), row(ln1_b[i]), w_r, b_r,
            apply_ln0=(i == 0), alpha=alpha, n_groups=n_groups, per_group=per_group, tm=tm_mix,
            sort_rows=tm)
        tabs, totals = _run_tables(cnt.reshape(-1, SUBLANES, LANES))
        items = _expert_items(totals, n_experts, n_slots)
        xs = _dispatch(tabs, x1, ri, tm=tm, n_slots=n_slots, n_experts=n_experts)
        ys = _experts(items, xs, w1, w3, w2, layer=i)
        x = _combine(tabs, x1, p, wcol, ys, w_ple_gate[i].astype(BF16), row(b_ple_gate[i]),
                     w_ple_proj[i].astype(BF16), row(ln2_g[i]), row(ln2_b[i]),
                     layer=i, alpha=alpha, tm=tm, n_experts=n_experts)
    return x
```

```python
import functools

import jax
import jax.numpy as jnp
from jax import lax
from jax.experimental import pallas as pl
from jax.experimental.pallas import tpu as pltpu

POOL_WINDOWS = (2, 4, 8, 16)
CONV_WIDTH = 3
TOP_K = 2
LN_EPS = 1e-5
HALO = 8
LANES = 128
SUBLANES = 8
SORT_ROWS = 512
MIX_ROWS = 1024
MIX_CHAINS = 2
COMBINE_CHAINS = 2
STEP_TILES = 2
SLOT_BLOCK = 512
VMEM_LIMIT = 56 * 1024 * 1024

F32 = jnp.float32
BF16 = jnp.bfloat16
I32 = jnp.int32
U32 = jnp.uint32


def _layernorm(v, g, b):
    mu = jnp.mean(v, axis=-1, keepdims=True)
    c = v - mu
    var = jnp.mean(c * c, axis=-1, keepdims=True)
    return c * lax.rsqrt(var + LN_EPS) * g + b


def _sigmoid(v):
    return 0.5 * jnp.tanh(0.5 * v) + 0.5


def _dot(a, b):
    return jnp.dot(a, b, preferred_element_type=F32)


def _pack_bf16_pairs(v):
    c = v.shape[1] // 2
    hi = pltpu.bitcast(v[:, :c].astype(BF16).astype(F32), U32)
    lo = pltpu.bitcast(v[:, c:].astype(BF16).astype(F32), U32)
    return hi | (lo >> 16)


def _pack_exact_bf16_pairs(v):
    c = v.shape[1] // 2
    return pltpu.bitcast(v[:, :c], U32) | (pltpu.bitcast(v[:, c:], U32) >> 16)


def _unpack_bf16_pairs(u):
    hi = pltpu.bitcast(u & jnp.uint32(0xFFFF0000), F32)
    lo = pltpu.bitcast(u << 16, F32)
    return hi, lo


def _store_slot_rows(ref, packed, first=0):
    n, c = packed.shape
    rps = c // LANES
    for k in range(rps):
        ref[pl.ds(first * rps + k, n, stride=rps), :] = packed[:, k * LANES:(k + 1) * LANES]


def _load_slot_rows(ref, n, rps, first=0):
    return jnp.concatenate([ref[pl.ds(first * rps + k, n, stride=rps), :] for k in range(rps)], axis=1)


def _mixer_body(x_ref, xp_ref, xn_ref, g0_ref, b0_ref, win_ref, bin_ref, cw_ref, pw_ref, ps_ref,
                wo_ref, g1_ref, b1_ref, wr_ref, br_ref,
                x1_ref, ri_ref, wcol_ref, cnt_ref, *, apply_ln0, alpha, n_groups, per_group, seq,
                sort_rows):
    i = pl.program_id(1)
    nt = pl.num_programs(1)
    tm = x_ref.shape[1]
    n = tm // MIX_CHAINS
    rows = n + 2 * HALO
    d_conv = cw_ref.shape[1]
    dg = pw_ref.shape[1]

    xe = jnp.concatenate([xp_ref[0], x_ref[0], xn_ref[0]], axis=0)
    if apply_ln0:
        xe = _layernorm(xe, g0_ref[...], b0_ref[...])
    cw = cw_ref[...]
    ps = ps_ref[...]

    def in_proj(s):
        xs = xe[s * n:s * n + rows]
        return xs, _dot(xs.astype(BF16), win_ref[...]) + bin_ref[...]

    def token_mix(s, z):
        r = lax.broadcasted_iota(I32, (rows, 1), 0)
        lo_ok = jnp.logical_or(r >= HALO, i > 0) if s == 0 else True
        hi_ok = jnp.logical_or(r < n + HALO, i < nt - 1) if s == MIX_CHAINS - 1 else True
        if s == 0 or s == MIX_CHAINS - 1:
            valid = jnp.logical_and(lo_ok, hi_ok)
            mask = lambda a: jnp.where(valid, a, 0.0)
        else:
            mask = lambda a: a

        h = z[:, :d_conv]
        gate_b = z[HALO:HALO + n, d_conv:2 * d_conv]
        gate_c = z[:, 2 * d_conv:3 * d_conv]
        v = mask(gate_c * h)
        conv = (cw[0:1] * pltpu.roll(v, 1, 0)[HALO:HALO + n]
                + cw[1:2] * v[HALO:HALO + n]
                + cw[2:3] * pltpu.roll(v, rows - 1, 0)[HALO:HALO + n])
        parts = [gate_b * conv]

        pos = i * tm + s * n + r[HALO:HALO + n] - HALO
        for g, w in enumerate(POOL_WINDOWS):
            left = w // 2
            right = w - 1 - left
            c0 = 3 * d_conv + g * dg
            ug = mask(z[:, c0:c0 + dg])
            acc = ug
            k = 1
            while k < w:
                acc = acc + pltpu.roll(acc, k, 0)
                k *= 2
            if right:
                acc = pltpu.roll(acc, rows - right, 0)

            def edge(lo):
                p = pos[lo:lo + HALO]
                cnt = (jnp.minimum(p + right + 1, seq) - jnp.maximum(p - left, 0)).astype(F32)
                return acc[HALO + lo:2 * HALO + lo] / cnt - ug[HALO + lo:2 * HALO + lo]
            inner = acc[2 * HALO:n] * (1.0 / w) - ug[2 * HALO:n]
            pooled = jnp.concatenate([edge(0), inner, edge(n - HALO)], axis=0)
            parts.append(_dot(pooled.astype(BF16), pw_ref[g]) * ps[:, g * dg:(g + 1) * dg])
        return jnp.concatenate(parts, axis=1).astype(BF16)

    def out_proj(xs, cat):
        mix = _dot(cat, wo_ref[...])
        return _layernorm(alpha * xs[HALO:HALO + n] + mix, g1_ref[...], b1_ref[...])

    def route(x1s):
        xh = x1s.astype(BF16)
        xl = (x1s - xh.astype(F32)).astype(BF16)
        l_hi = _dot(xh, wr_ref[...])
        l_lo = _dot(xl, wr_ref[:, :LANES])
        return l_hi[:, :LANES] + l_hi[:, LANES:] + l_lo + br_ref[...]

    chains = [in_proj(s) for s in range(MIX_CHAINS)]
    x1s = [out_proj(xs, token_mix(s, z)) for s, (xs, z) in enumerate(chains)]
    x1 = jnp.concatenate(x1s, axis=0)
    x1_ref[0] = x1
    logits = jnp.concatenate([route(v) for v in x1s], axis=0)
    ri_ref[0] = jnp.zeros(ri_ref.shape[1:], I32)
    for q in range(tm // sort_rows):
        rows_q = slice(q * sort_rows, (q + 1) * sort_rows)
        lpos0, lpos1, gw0, gw1, counts = _route_and_sort(logits[rows_q], n_groups, per_group)
        ri_ref[0, q, 0:1, :] = lpos0.astype(I32)
        ri_ref[0, q, 1:2, :] = lpos1.astype(I32)
        rowl = lax.broadcasted_iota(I32, (LANES, sort_rows), 0)
        wrows = jnp.where(rowl == 0, gw0, jnp.where(rowl == 1, gw1,
                          jnp.where(rowl == 2, lpos0, jnp.where(rowl == 3, lpos1, 0.0))))
        wcol_ref[0, rows_q, :] = wrows.T
        cnt_ref[0, q] = counts


def _route_and_sort(logits, n_groups, per_group):
    n = logits.shape[0]
    lt = logits.T
    row8 = lax.broadcasted_iota(I32, (SUBLANES, n), 0)
    neg = jnp.float32(-jnp.inf)
    lg = jnp.where(row8 < n_groups, lt[0:SUBLANES], neg)
    mg = jnp.max(lg, axis=0, keepdims=True)
    g_w = 1.0 / jnp.sum(jnp.exp(lg - mg), axis=0, keepdims=True)
    g_sel = jnp.min(jnp.where(lg == mg, row8, SUBLANES), axis=0, keepdims=True)
    le = lt[SUBLANES:2 * SUBLANES]
    for g in range(1, n_groups):
        le = jnp.where(g_sel == g, lt[SUBLANES * (g + 1):SUBLANES * (g + 2)], le)
    le = jnp.where(row8 < per_group, le, neg)
    v1 = jnp.max(le, axis=0, keepdims=True)
    i1 = jnp.min(jnp.where(le == v1, row8, SUBLANES), axis=0, keepdims=True)
    le2 = jnp.where(row8 == i1, neg, le)
    v2 = jnp.max(le2, axis=0, keepdims=True)
    i2 = jnp.min(jnp.where(le2 == v2, row8, SUBLANES), axis=0, keepdims=True)
    e21 = jnp.exp(v2 - v1)
    w_first = 1.0 / (1.0 + e21)
    gw0 = w_first * g_w
    gw1 = (e21 * w_first) * g_w
    eid0 = g_sel * per_group + i1
    eid1 = g_sel * per_group + i2

    rowl = lax.broadcasted_iota(I32, (LANES, n), 0)
    oh0 = rowl == eid0
    oh1 = rowl == eid1
    a_idx = lax.broadcasted_iota(I32, (n, n), 0)
    b_idx = lax.broadcasted_iota(I32, (n, n), 1)
    before = (a_idx < b_idx).astype(BF16)
    r0 = _dot(oh0.astype(BF16), before)
    r1 = _dot(oh1.astype(BF16), before)
    cnt0 = jnp.sum(oh0.astype(F32), axis=1, keepdims=True)
    lower = jnp.sum((rowl > eid0).astype(F32) + (rowl > eid1).astype(F32),
                    axis=1, keepdims=True)
    lpos0 = jnp.sum(jnp.where(oh0, lower + r0, 0.0), axis=0, keepdims=True)
    lpos1 = jnp.sum(jnp.where(oh1, lower + cnt0 + r1, 0.0), axis=0, keepdims=True)

    both = jnp.logical_or(oh0, oh1).astype(BF16)
    counts = lax.dot_general(jnp.ones((SUBLANES, n), BF16), both, (((1,), (1,)), ((), ())),
                             preferred_element_type=F32)
    return lpos0, lpos1, gw0, gw1, counts


def _mixer(x, g0, b0, w_in, b_in, conv_w, pool_w, pool_scale, w_o, g1, b1, w_r, b_r, *,
           apply_ln0, alpha, n_groups, per_group, tm, sort_rows):
    bsz, seq, d = x.shape
    nt = seq // tm
    hb = tm // HALO
    nhb = seq // HALO
    d_in = w_in.shape[1]
    d_conv = conv_w.shape[1]
    full = lambda *shape: pl.BlockSpec(shape, lambda b, i: (0,) * len(shape))
    body = functools.partial(_mixer_body, apply_ln0=apply_ln0, alpha=alpha, n_groups=n_groups,
                             per_group=per_group, seq=seq, sort_rows=sort_rows)
    nq = tm // sort_rows
    return pl.pallas_call(
        body,
        grid=(bsz, nt),
        in_specs=[
            pl.BlockSpec((1, tm, d), lambda b, i: (b, i, 0)),
            pl.BlockSpec((1, HALO, d), lambda b, i: (b, jnp.maximum(i * hb - 1, 0), 0)),
            pl.BlockSpec((1, HALO, d), lambda b, i: (b, jnp.minimum((i + 1) * hb, nhb - 1), 0)),
            full(1, d), full(1, d),
            full(d, d_in), full(1, d_in),
            full(CONV_WIDTH, d_conv),
            full(*pool_w.shape), full(1, pool_scale.shape[1]),
            full(d, d), full(1, d), full(1, d),
            full(d, 2 * LANES), full(1, LANES),
        ],
        out_specs=[
            pl.BlockSpec((1, tm, d), lambda b, i: (b, i, 0)),
            pl.BlockSpec((1, nq, SUBLANES, sort_rows), lambda b, i: (b, i, 0, 0)),
            pl.BlockSpec((1, tm, LANES), lambda b, i: (b, i, 0)),
            pl.BlockSpec((1, nq, SUBLANES, LANES), lambda b, i: (b, i, 0, 0)),
        ],
        out_shape=[
            jax.ShapeDtypeStruct((bsz, seq, d), F32),
            jax.ShapeDtypeStruct((bsz, nt * nq, SUBLANES, sort_rows), I32),
            jax.ShapeDtypeStruct((bsz, seq, LANES), F32),
            jax.ShapeDtypeStruct((bsz, nt * nq, SUBLANES, LANES), F32),
        ],
        compiler_params=pltpu.CompilerParams(
            dimension_semantics=("arbitrary", "arbitrary"), vmem_limit_bytes=VMEM_LIMIT),
        name="mixer",
    )(x, x, x, g0, b0, w_in, b_in, conv_w, pool_w, pool_scale, w_o, g1, b1, w_r, b_r)


def _tables_body(cnt_ref, gstart_ref, lstart_ref, len_ref, tot_ref):
    c = cnt_ref[:, 0, :]
    n_tiles = c.shape[0]
    row = lax.broadcasted_iota(I32, c.shape, 0)
    lane = lax.broadcasted_iota(I32, c.shape, 1)

    def prefix(v, idx, size, axis):
        k = 1
        while k < size:
            v = v + jnp.where(idx >= k, pltpu.roll(v, k, axis), 0.0)
            k *= 2
        return v

    over_tiles = prefix(c, row, n_tiles, 0)
    tot = jnp.broadcast_to(over_tiles[n_tiles - 1:n_tiles], c.shape)
    expert_start = prefix(tot, lane, LANES, 1) - tot
    gstart_ref[...] = (expert_start + over_tiles - c).astype(I32)
    lstart_ref[...] = (prefix(c, lane, LANES, 1) - c).astype(I32)
    len_ref[...] = c.astype(I32)
    tot_ref[...] = jnp.broadcast_to(over_tiles[n_tiles - 1:n_tiles], tot_ref.shape).astype(I32)


def _run_tables(cnt):
    table = jax.ShapeDtypeStruct((cnt.shape[0], LANES), I32)
    *tabs, tot = pl.pallas_call(
        _tables_body, out_shape=[table, table, table, jax.ShapeDtypeStruct((SUBLANES, LANES), I32)],
        name="run_tables")(cnt)
    return tuple(tabs), tot[0]


def _slots(ref, first, n, rps):
    start = first * rps
    if not isinstance(start, int):
        start = pl.multiple_of(start, rps)
    return ref.at[pl.ds(start, n * rps)]


def _start_run_copies(tabs, tile, n_experts, make_copy):
    gstart_ref, lstart_ref, len_ref = tabs

    def run_of(e):
        e = jnp.minimum(e, n_experts - 1)
        return len_ref[tile, e], lstart_ref[tile, e], gstart_ref[tile, e]

    def per_pair(i, runs):
        nxt = (run_of(2 * i + 2), run_of(2 * i + 3))
        for prio, (n, ls, gs) in enumerate(runs):

            @pl.when(n > 0)
            def _(n=n, ls=ls, gs=gs, prio=prio):
                make_copy(ls, gs, n).start(priority=prio)
        return nxt
    assert n_experts % 2 == 0
    lax.fori_loop(0, n_experts // 2, per_pair, (run_of(0), run_of(1)))


def _dispatch_body(gstart_ref, lstart_ref, len_ref, x1_ref, ri_ref, xs_ref, buf_ref, sem, *, n_experts):
    step = pl.program_id(0) * pl.num_programs(1) + pl.program_id(1)
    n_tiles = pl.num_programs(0) * pl.num_programs(1) * STEP_TILES
    tabs = (gstart_ref, lstart_ref, len_ref)
    tm = ri_ref.shape[3]
    d = x1_ref.shape[2]
    rps = d // 2 // LANES

    def copier(which):
        def make_copy(local, glob, n):
            return pltpu.make_async_copy(_slots(buf_ref.at[which], local, n, rps),
                                         _slots(xs_ref, glob, n, rps), sem.at[which])
        return make_copy

    def wait_all(which):
        copier(which)(0, 0, TOP_K * tm).wait()

    def one_tile(q, carry):
        tile = step * STEP_TILES + q
        par = lax.rem(tile, 2)

        @pl.when(tile >= 2)
        def _():
            wait_all(par)

        lpos = ri_ref[0, q]
        s_idx = lax.broadcasted_iota(I32, (TOP_K * tm, tm), 0)
        perm = jnp.logical_or(s_idx == lpos[0:1], s_idx == lpos[1:2]).astype(BF16)
        x1 = x1_ref[0, pl.ds(pl.multiple_of(q * tm, tm), tm)]
        _store_slot_rows(buf_ref.at[par], _pack_exact_bf16_pairs(_dot(perm, x1.astype(BF16))))
        _start_run_copies(tabs, tile, n_experts, copier(par))

        @pl.when(tile == n_tiles - 1)
        def _():
            wait_all(par)

            @pl.when(tile >= 1)
            def _():
                wait_all(1 - par)
        return carry
    lax.fori_loop(0, STEP_TILES, one_tile, 0)


def _dispatch(tabs, x1, ri, *, tm, n_slots, n_experts):
    bsz, seq, d = x1.shape
    steps = seq // (tm * STEP_TILES)
    rps = d // 2 // LANES
    grid_spec = pltpu.PrefetchScalarGridSpec(
        num_scalar_prefetch=3,
        grid=(bsz, steps),
        in_specs=[
            pl.BlockSpec((1, STEP_TILES * tm, d), lambda b, i, *_: (b, i, 0)),
            pl.BlockSpec((1, STEP_TILES, SUBLANES, tm), lambda b, i, *_: (b, i, 0, 0)),
        ],
        out_specs=pl.BlockSpec(memory_space=pl.ANY),
        scratch_shapes=[pltpu.VMEM((2, TOP_K * tm * rps, LANES), U32), pltpu.SemaphoreType.DMA((2,))],
    )
    return pl.pallas_call(
        functools.partial(_dispatch_body, n_experts=n_experts),
        grid_spec=grid_spec,
        out_shape=jax.ShapeDtypeStruct((n_slots * rps, LANES), U32),
        compiler_params=pltpu.CompilerParams(
            dimension_semantics=("arbitrary", "arbitrary"), vmem_limit_bytes=VMEM_LIMIT),
        name="dispatch",
    )(*tabs, x1, ri)


def _experts_body(blk_ref, exp_ref, lo_ref, hi_ref, new_ref, par_ref, nxt_ref, xs_ref, w1_ref, w3_ref, w2_ref,
                  ys_ref, f1_ref, f3_ref, f2_ref, w1b_ref, w3b_ref, w2b_ref, sem, *, layer):
    it = pl.program_id(0)
    lo = lo_ref[it]
    rps = xs_ref.shape[0] // SLOT_BLOCK

    def weight_copies(expert, slot):
        return [pltpu.make_async_copy(w_ref.at[layer, expert], f_ref.at[slot], sem.at[slot, k])
                for k, (w_ref, f_ref) in enumerate(((w1_ref, f1_ref), (w3_ref, f3_ref), (w2_ref, f2_ref)))]

    @pl.when(it == 0)
    def _():
        for c in weight_copies(exp_ref[0], 0):
            c.start()

    @pl.when(new_ref[it] == 1)
    def _():
        slot = par_ref[it]
        for c in weight_copies(exp_ref[it], slot):
            c.wait()
        w1b_ref[...] = f1_ref[slot].astype(BF16)
        w3b_ref[...] = f3_ref[slot].astype(BF16)
        w2b_ref[...] = f2_ref[slot].astype(BF16)

        @pl.when(nxt_ref[it] >= 0)
        def _():
            for c in weight_copies(nxt_ref[it], 1 - slot):
                c.start()

    def ffn(first, n):
        hi_half, lo_half = _unpack_bf16_pairs(_load_slot_rows(xs_ref, n, rps, first))
        xb = jnp.concatenate([hi_half.astype(BF16), lo_half.astype(BF16)], axis=1)
        h1 = _dot(xb, w1b_ref[...])
        h = (h1 * _sigmoid(h1)) * _dot(xb, w3b_ref[...])
        return _pack_bf16_pairs(_dot(h.astype(BF16), w2b_ref[...]))

    def run(first, n):
        y = ffn(first, n)

        @pl.when(lo <= first)
        def _():
            _store_slot_rows(ys_ref, y, first)

        @pl.when(lo > first)
        def _():
            m = min(n, half)
            rows = first + lax.broadcasted_iota(I32, (m, y.shape[1]), 0)
            _store_slot_rows(ys_ref, jnp.where(rows >= lo, y[:m], _load_slot_rows(ys_ref, m, rps, first)), first)
            if n > m:
                _store_slot_rows(ys_ref, y[m:], first + m)

    half = SLOT_BLOCK // 2
    need_lo = lo < half
    need_hi = hi_ref[it] > half
    pl.when(jnp.logical_and(need_lo, need_hi))(lambda: run(0, SLOT_BLOCK))
    pl.when(jnp.logical_and(need_lo, jnp.logical_not(need_hi)))(lambda: run(0, half))
    pl.when(jnp.logical_and(jnp.logical_not(need_lo), need_hi))(lambda: run(half, half))


def _experts(items, xs, w1, w3, w2, *, layer):
    rows = xs.shape[0]
    _, _, d, d_e = w1.shape
    rps = d // 2 // LANES
    n_items = items[0].shape[0]
    grid_spec = pltpu.PrefetchScalarGridSpec(
        num_scalar_prefetch=len(items),
        grid=(n_items,),
        in_specs=[
            pl.BlockSpec((SLOT_BLOCK * rps, LANES), lambda i, blk, *_: (blk[i], 0)),
            pl.BlockSpec(memory_space=pl.ANY),
            pl.BlockSpec(memory_space=pl.ANY),
            pl.BlockSpec(memory_space=pl.ANY),
        ],
        out_specs=pl.BlockSpec((SLOT_BLOCK * rps, LANES), lambda i, blk, *_: (blk[i], 0)),
        scratch_shapes=[pltpu.VMEM((2, d, d_e), F32), pltpu.VMEM((2, d, d_e), F32),
                        pltpu.VMEM((2, d_e, d), F32),
                        pltpu.VMEM((d, d_e), BF16), pltpu.VMEM((d, d_e), BF16),
                        pltpu.VMEM((d_e, d), BF16), pltpu.SemaphoreType.DMA((2, 3))],
    )
    return pl.pallas_call(
        functools.partial(_experts_body, layer=layer),
        grid_spec=grid_spec,
        out_shape=jax.ShapeDtypeStruct((rows, LANES), U32),
        compiler_params=pltpu.CompilerParams(
            dimension_semantics=("arbitrary",), vmem_limit_bytes=VMEM_LIMIT),
        name="experts",
    )(*items, xs, w1, w3, w2)


def _expert_items(totals, n_experts, n_slots):
    n_blocks = n_slots // SLOT_BLOCK
    n_items = n_blocks + n_experts - 1
    tot = totals[:n_experts]
    end = jnp.cumsum(tot)
    start = end - tot
    first_blk = start // SLOT_BLOCK
    n_it = jnp.where(tot > 0, (end - 1) // SLOT_BLOCK - first_blk + 1, 0)
    it_end = jnp.cumsum(n_it)
    it_start = it_end - n_it
    idx = jnp.arange(n_items, dtype=I32)
    last_live = it_end[-1] - 1
    live = idx <= last_live
    at = jnp.minimum(idx, last_live)
    expert = jnp.sum(it_end[None, :] <= at[:, None], axis=1).astype(I32)
    onehot = expert[:, None] == jnp.arange(n_experts, dtype=I32)[None, :]
    pick = lambda v: jnp.sum(jnp.where(onehot, v[None, :], 0), axis=1)
    block = pick(first_blk) + idx - pick(it_start)
    item_block = jnp.where(live, block, n_blocks - 1).astype(I32)
    item_lo = jnp.where(live, jnp.maximum(pick(start) - block * SLOT_BLOCK, 0), SLOT_BLOCK).astype(I32)
    item_hi = jnp.where(live, jnp.minimum(pick(end) - block * SLOT_BLOCK, SLOT_BLOCK), 0).astype(I32)
    prev_expert = jnp.concatenate([jnp.full((1,), -1, I32), expert[:-1]])
    is_new = expert != prev_expert
    slot = (jnp.cumsum(is_new.astype(I32)) - 1) % 2
    later_new = lax.cummin(jnp.where(is_new, idx, n_items)[::-1], axis=0)[::-1]
    next_new = jnp.concatenate([later_new[1:], jnp.full((1,), n_items, I32)])
    next_expert = jnp.where(next_new < n_items, expert[jnp.minimum(next_new, n_items - 1)], -1)
    return (item_block, expert, item_lo, item_hi, is_new.astype(I32), slot.astype(I32),
            next_expert.astype(I32))


def _combine_body(gstart_ref, lstart_ref, len_ref, x1_ref, p_ref, wcol_ref, ys_ref, wg_ref, bg_ref, wp_ref,
                  g2_ref, b2_ref, out_ref, buf_ref, sem, *, alpha, n_experts, tm):
    step = pl.program_id(0) * pl.num_programs(1) + pl.program_id(1)
    n_tiles = pl.num_programs(0) * pl.num_programs(1) * STEP_TILES
    tabs = (gstart_ref, lstart_ref, len_ref)
    d = x1_ref.shape[2]
    rps = d // 2 // LANES
    n = tm // COMBINE_CHAINS

    def copier(which):
        def make_copy(local, glob, n_slots):
            return pltpu.make_async_copy(_slots(ys_ref, glob, n_slots, rps),
                                         _slots(buf_ref.at[which], local, n_slots, rps), sem.at[which])
        return make_copy

    def one_tile(q, carry):
        tile = step * STEP_TILES + q
        par = lax.rem(tile, 2)

        @pl.when(tile == 0)
        def _():
            _start_run_copies(tabs, tile, n_experts, copier(par))

        @pl.when(tile + 1 < n_tiles)
        def _():
            _start_run_copies(tabs, tile + 1, n_experts, copier(1 - par))
        copier(par)(0, 0, TOP_K * tm).wait()

        hi_half, lo_half = _unpack_bf16_pairs(_load_slot_rows(buf_ref.at[par], TOP_K * tm, rps))
        ys = jnp.concatenate([hi_half.astype(BF16), lo_half.astype(BF16)], axis=1)

        def rows_of(s):
            return pl.ds(pl.multiple_of(q * tm + s * n, n), n)

        def ple_matmuls(s):
            x1 = x1_ref[0, rows_of(s)]
            gate_logits = _dot(x1.astype(BF16), wg_ref[...]) + bg_ref[...]
            return x1, gate_logits, _dot(p_ref[0, rows_of(s)].astype(BF16), wp_ref[...])

        def finish(s, x1, gate_logits, proj):
            wcol = wcol_ref[0, rows_of(s)]
            s_idx = lax.broadcasted_iota(I32, (n, TOP_K * tm), 1)
            select = jnp.zeros((n, TOP_K * tm), F32)
            for kk in range(TOP_K):
                at_kk = s_idx == wcol[:, TOP_K + kk:TOP_K + kk + 1].astype(I32)
                select = jnp.where(at_kk, wcol[:, kk:kk + 1], select)
            ffn = _dot(select.astype(BF16), ys)
            ple = _sigmoid(gate_logits) * proj
            out_ref[0, rows_of(s)] = _layernorm(alpha * x1 + ffn + ple, g2_ref[...], b2_ref[...])

        chains = [ple_matmuls(s) for s in range(COMBINE_CHAINS)]
        for s, c in enumerate(chains):
            finish(s, *c)
        return carry
    lax.fori_loop(0, STEP_TILES, one_tile, 0)


def _combine(tabs, x1, p, wcol, ys, w_gate, b_gate, w_proj, g2, b2, *, layer, alpha, tm, n_experts):
    bsz, seq, d = x1.shape
    rows = STEP_TILES * tm
    steps = seq // rows
    d_ple = p.shape[-1]
    rps = d // 2 // LANES
    full = lambda *shape: pl.BlockSpec(shape, lambda b, i, *_: (0,) * len(shape))
    grid_spec = pltpu.PrefetchScalarGridSpec(
        num_scalar_prefetch=3,
        grid=(bsz, steps),
        in_specs=[
            pl.BlockSpec((1, rows, d), lambda b, i, *_: (b, i, 0)),
            pl.BlockSpec((None, 1, rows, d_ple), lambda b, i, *_: (layer, b, i, 0)),
            pl.BlockSpec((1, rows, LANES), lambda b, i, *_: (b, i, 0)),
            pl.BlockSpec(memory_space=pl.ANY),
            full(d, d), full(1, d), full(d_ple, d),
            full(1, d), full(1, d),
        ],
        out_specs=pl.BlockSpec((1, rows, d), lambda b, i, *_: (b, i, 0)),
        scratch_shapes=[pltpu.VMEM((2, TOP_K * tm * rps, LANES), U32), pltpu.SemaphoreType.DMA((2,))],
    )
    return pl.pallas_call(
        functools.partial(_combine_body, alpha=alpha, n_experts=n_experts, tm=tm),
        grid_spec=grid_spec,
        out_shape=jax.ShapeDtypeStruct((bsz, seq, d), F32),
        compiler_params=pltpu.CompilerParams(
            dimension_semantics=("arbitrary", "arbitrary"), vmem_limit_bytes=VMEM_LIMIT),
        name="combine",
    )(*tabs, x1, p, wcol, ys, w_gate, b_gate, w_proj, g2, b2)


def _router_weights(w_rg, b_rg, w_re, b_re):
    d, n_groups = w_rg.shape
    per_group = w_re.shape[2]
    assert n_groups <= SUBLANES and per_group <= SUBLANES and SUBLANES * (n_groups + 1) <= LANES
    w_e = jnp.pad(jnp.transpose(w_re, (1, 0, 2)), ((0, 0), (0, 0), (0, SUBLANES - per_group)))
    b_e = jnp.pad(b_re, ((0, 0), (0, SUBLANES - per_group)))
    tail = LANES - SUBLANES * (n_groups + 1)
    w = jnp.concatenate([jnp.pad(w_rg, ((0, 0), (0, SUBLANES - n_groups))), w_e.reshape(d, SUBLANES * n_groups),
                         jnp.zeros((d, tail), F32)], axis=1)
    b = jnp.concatenate([jnp.pad(b_rg, (0, SUBLANES - n_groups)), b_e.reshape(-1), jnp.zeros((tail,), F32)])
    w_hi = w.astype(BF16)
    w_lo = (w - w_hi.astype(F32)).astype(BF16)
    return jnp.concatenate([w_hi, w_lo], axis=1), b[None, :]


def kernel(x, p, ln0_g, ln0_b, w_in, b_in, conv_w, pool_w, pool_scale, w_o, ln1_g, ln1_b,
           w_router_group, b_router_group, w_router_expert, b_router_expert, w1, w3, w2,
           w_ple_gate, b_ple_gate, w_ple_proj, ln2_g, ln2_b):
    bsz, seq, d = x.shape
    depth = w_in.shape[0]
    n_groups, per_group = w_router_expert.shape[1], w_router_expert.shape[3]
    n_experts = n_groups * per_group
    alpha = (2 * depth) ** 0.25
    tm = min(SORT_ROWS, seq)
    tm_mix = min(MIX_ROWS, seq)
    n_slots = bsz * seq * TOP_K
    assert seq % tm_mix == 0 and tm_mix % tm == 0 and tm % LANES == 0 and seq % (STEP_TILES * tm) == 0
    assert n_slots % SLOT_BLOCK == 0 and n_experts <= LANES
    assert d % (2 * LANES) == 0
    row = lambda a: a[None, :]

    for i in range(depth):
        w_r, b_r = _router_weights(w_router_group[i], b_router_group[i],
                                   w_router_expert[i], b_router_expert[i])
        x1, ri, wcol, cnt = _mixer(
            x, row(ln0_g), row(ln0_b), w_in[i].astype(BF16), row(b_in[i]), conv_w[i],
            pool_w[i].astype(BF16), row(pool_scale[i]), w_o[i].astype(BF16),
            row(ln1_g[i]), row(ln1_b[i]), w_r, b_r,
            apply_ln0=(i == 0), alpha=alpha, n_groups=n_groups, per_group=per_group, tm=tm_mix,
            sort_rows=tm)
        tabs, totals = _run_tables(cnt.reshape(-1, SUBLANES, LANES))
        items = _expert_items(totals, n_experts, n_slots)
        xs = _dispatch(tabs, x1, ri, tm=tm, n_slots=n_slots, n_experts=n_experts)
        ys = _experts(items, xs, w1, w3, w2, layer=i)
        x = _combine(tabs, x1, p, wcol, ys, w_ple_gate[i].astype(BF16), row(b_ple_gate[i]),
                     w_ple_proj[i].astype(BF16), row(ln2_g[i]), row(ln2_b[i]),
                     layer=i, alpha=alpha, tm=tm, n_experts=n_experts)
    return x
```

```python
import functools

import jax
import jax.numpy as jnp
from jax import lax
from jax.experimental import pallas as pl
from jax.experimental.pallas import tpu as pltpu

POOL_WINDOWS = (2, 4, 8, 16)
CONV_WIDTH = 3
TOP_K = 2
LN_EPS = 1e-5
HALO = 8
LANES = 128
SUBLANES = 8
SORT_ROWS = 512
MIX_ROWS = 1024
MIX_CHAINS = 2
COMBINE_CHAINS = 2
STEP_TILES = 2
SLOT_BLOCK = 512
VMEM_LIMIT = 56 * 1024 * 1024

F32 = jnp.float32
BF16 = jnp.bfloat16
I32 = jnp.int32
U32 = jnp.uint32


def _layernorm(v, g, b):
    mu = jnp.mean(v, axis=-1, keepdims=True)
    c = v - mu
    var = jnp.mean(c * c, axis=-1, keepdims=True)
    return c * lax.rsqrt(var + LN_EPS) * g + b


def _sigmoid(v):
    return 0.5 * jnp.tanh(0.5 * v) + 0.5


def _dot(a, b):
    return jnp.dot(a, b, preferred_element_type=F32)


def _pair_groups(width):
    assert width % (2 * LANES) == 0
    return range(0, width, 2 * LANES)


def _pack_bf16_pairs(v):
    words = []
    for c0 in _pair_groups(v.shape[1]):
        hi = pltpu.bitcast(v[:, c0:c0 + LANES].astype(BF16).astype(F32), U32)
        lo = pltpu.bitcast(v[:, c0 + LANES:c0 + 2 * LANES].astype(BF16).astype(F32), U32)
        words.append(hi | (lo >> 16))
    return jnp.concatenate(words, axis=1)


def _pack_exact_bf16_pairs(v):
    words = [pltpu.bitcast(v[:, c0:c0 + LANES], U32) | (pltpu.bitcast(v[:, c0 + LANES:c0 + 2 * LANES], U32) >> 16)
             for c0 in _pair_groups(v.shape[1])]
    return jnp.concatenate(words, axis=1)


def _unpack_bf16_pairs(u):
    cols = []
    for c0 in range(0, u.shape[1], LANES):
        w = u[:, c0:c0 + LANES]
        cols.append(pltpu.bitcast(w & jnp.uint32(0xFFFF0000), F32).astype(BF16))
        cols.append(pltpu.bitcast(w << 16, F32).astype(BF16))
    return jnp.concatenate(cols, axis=1)


def _store_slot_rows(ref, packed, first=0):
    n, c = packed.shape
    rps = c // LANES
    for k in range(rps):
        ref[pl.ds(first * rps + k, n, stride=rps), :] = packed[:, k * LANES:(k + 1) * LANES]


def _load_slot_rows(ref, n, rps, first=0):
    return jnp.concatenate([ref[pl.ds(first * rps + k, n, stride=rps), :] for k in range(rps)], axis=1)


def _mixer_body(x_ref, xp_ref, xn_ref, g0_ref, b0_ref, win_ref, bin_ref, cw_ref, pw_ref, ps_ref,
                wo_ref, g1_ref, b1_ref, wr_ref, br_ref,
                x1_ref, ri_ref, wcol_ref, cnt_ref, *, apply_ln0, alpha, n_groups, per_group, seq,
                sort_rows):
    i = pl.program_id(1)
    nt = pl.num_programs(1)
    tm = x_ref.shape[1]
    n = tm // MIX_CHAINS
    rows = n + 2 * HALO
    d_conv = cw_ref.shape[1]
    dg = pw_ref.shape[1]

    xe = jnp.concatenate([xp_ref[0], x_ref[0], xn_ref[0]], axis=0)
    if apply_ln0:
        xe = _layernorm(xe, g0_ref[...], b0_ref[...])
    cw = cw_ref[...]
    ps = ps_ref[...]

    def in_proj(s):
        xs = xe[s * n:s * n + rows]
        return xs, _dot(xs.astype(BF16), win_ref[...]) + bin_ref[...]

    def token_mix(s, z):
        r = lax.broadcasted_iota(I32, (rows, 1), 0)
        lo_ok = jnp.logical_or(r >= HALO, i > 0) if s == 0 else True
        hi_ok = jnp.logical_or(r < n + HALO, i < nt - 1) if s == MIX_CHAINS - 1 else True
        if s == 0 or s == MIX_CHAINS - 1:
            valid = jnp.logical_and(lo_ok, hi_ok)
            mask = lambda a: jnp.where(valid, a, 0.0)
        else:
            mask = lambda a: a

        h = z[:, :d_conv]
        gate_b = z[HALO:HALO + n, d_conv:2 * d_conv]
        gate_c = z[:, 2 * d_conv:3 * d_conv]
        v = mask(gate_c * h)
        conv = (cw[0:1] * pltpu.roll(v, 1, 0)[HALO:HALO + n]
                + cw[1:2] * v[HALO:HALO + n]
                + cw[2:3] * pltpu.roll(v, rows - 1, 0)[HALO:HALO + n])
        parts = [gate_b * conv]

        pos = i * tm + s * n + r[HALO:HALO + n] - HALO
        for g, w in enumerate(POOL_WINDOWS):
            left = w // 2
            right = w - 1 - left
            c0 = 3 * d_conv + g * dg
            ug = mask(z[:, c0:c0 + dg])
            acc = ug
            k = 1
            while k < w:
                acc = acc + pltpu.roll(acc, k, 0)
                k *= 2
            if right:
                acc = pltpu.roll(acc, rows - right, 0)

            def edge(lo):
                p = pos[lo:lo + HALO]
                cnt = (jnp.minimum(p + right + 1, seq) - jnp.maximum(p - left, 0)).astype(F32)
                return acc[HALO + lo:2 * HALO + lo] / cnt - ug[HALO + lo:2 * HALO + lo]
            inner = acc[2 * HALO:n] * (1.0 / w) - ug[2 * HALO:n]
            pooled = jnp.concatenate([edge(0), inner, edge(n - HALO)], axis=0)
            parts.append(_dot(pooled.astype(BF16), pw_ref[g]) * ps[:, g * dg:(g + 1) * dg])
        return jnp.concatenate(parts, axis=1).astype(BF16)

    def out_proj(xs, cat):
        mix = _dot(cat, wo_ref[...])
        return _layernorm(alpha * xs[HALO:HALO + n] + mix, g1_ref[...], b1_ref[...])

    def route(x1s):
        xh = x1s.astype(BF16)
        xl = (x1s - xh.astype(F32)).astype(BF16)
        l_hi = _dot(xh, wr_ref[...])
        l_lo = _dot(xl, wr_ref[:, :LANES])
        return l_hi[:, :LANES] + l_hi[:, LANES:] + l_lo + br_ref[...]

    chains = [in_proj(s) for s in range(MIX_CHAINS)]
    x1s = [out_proj(xs, token_mix(s, z)) for s, (xs, z) in enumerate(chains)]
    x1 = jnp.concatenate(x1s, axis=0)
    x1_ref[0] = x1
    logits = jnp.concatenate([route(v) for v in x1s], axis=0)
    ri_ref[0] = jnp.zeros(ri_ref.shape[1:], I32)
    for q in range(tm // sort_rows):
        rows_q = slice(q * sort_rows, (q + 1) * sort_rows)
        lpos0, lpos1, gw0, gw1, counts = _route_and_sort(logits[rows_q], n_groups, per_group)
        ri_ref[0, q, 0:1, :] = lpos0.astype(I32)
        ri_ref[0, q, 1:2, :] = lpos1.astype(I32)
        rowl = lax.broadcasted_iota(I32, (LANES, sort_rows), 0)
        wrows = jnp.where(rowl == 0, gw0, jnp.where(rowl == 1, gw1,
                          jnp.where(rowl == 2, lpos0, jnp.where(rowl == 3, lpos1, 0.0))))
        wcol_ref[0, rows_q, :] = wrows.T
        cnt_ref[0, q] = counts


def _route_and_sort(logits, n_groups, per_group):
    n = logits.shape[0]
    lt = logits.T
    row8 = lax.broadcasted_iota(I32, (SUBLANES, n), 0)
    neg = jnp.float32(-jnp.inf)
    lg = jnp.where(row8 < n_groups, lt[0:SUBLANES], neg)
    mg = jnp.max(lg, axis=0, keepdims=True)
    g_w = 1.0 / jnp.sum(jnp.exp(lg - mg), axis=0, keepdims=True)
    g_sel = jnp.min(jnp.where(lg == mg, row8, SUBLANES), axis=0, keepdims=True)
    le = lt[SUBLANES:2 * SUBLANES]
    for g in range(1, n_groups):
        le = jnp.where(g_sel == g, lt[SUBLANES * (g + 1):SUBLANES * (g + 2)], le)
    le = jnp.where(row8 < per_group, le, neg)
    v1 = jnp.max(le, axis=0, keepdims=True)
    i1 = jnp.min(jnp.where(le == v1, row8, SUBLANES), axis=0, keepdims=True)
    le2 = jnp.where(row8 == i1, neg, le)
    v2 = jnp.max(le2, axis=0, keepdims=True)
    i2 = jnp.min(jnp.where(le2 == v2, row8, SUBLANES), axis=0, keepdims=True)
    e21 = jnp.exp(v2 - v1)
    w_first = 1.0 / (1.0 + e21)
    gw0 = w_first * g_w
    gw1 = (e21 * w_first) * g_w
    eid0 = g_sel * per_group + i1
    eid1 = g_sel * per_group + i2

    rowl = lax.broadcasted_iota(I32, (LANES, n), 0)
    oh0 = rowl == eid0
    oh1 = rowl == eid1
    a_idx = lax.broadcasted_iota(I32, (n, n), 0)
    b_idx = lax.broadcasted_iota(I32, (n, n), 1)
    before = (a_idx < b_idx).astype(BF16)
    r0 = _dot(oh0.astype(BF16), before)
    r1 = _dot(oh1.astype(BF16), before)
    cnt0 = jnp.sum(oh0.astype(F32), axis=1, keepdims=True)
    lower = jnp.sum((rowl > eid0).astype(F32) + (rowl > eid1).astype(F32),
                    axis=1, keepdims=True)
    lpos0 = jnp.sum(jnp.where(oh0, lower + r0, 0.0), axis=0, keepdims=True)
    lpos1 = jnp.sum(jnp.where(oh1, lower + cnt0 + r1, 0.0), axis=0, keepdims=True)

    both = jnp.logical_or(oh0, oh1).astype(BF16)
    counts = lax.dot_general(jnp.ones((SUBLANES, n), BF16), both, (((1,), (1,)), ((), ())),
                             preferred_element_type=F32)
    return lpos0, lpos1, gw0, gw1, counts


def _mixer(x, g0, b0, w_in, b_in, conv_w, pool_w, pool_scale, w_o, g1, b1, w_r, b_r, *,
           apply_ln0, alpha, n_groups, per_group, tm, sort_rows):
    bsz, seq, d = x.shape
    nt = seq // tm
    hb = tm // HALO
    nhb = seq // HALO
    d_in = w_in.shape[1]
    d_conv = conv_w.shape[1]
    full = lambda *shape: pl.BlockSpec(shape, lambda b, i: (0,) * len(shape))
    body = functools.partial(_mixer_body, apply_ln0=apply_ln0, alpha=alpha, n_groups=n_groups,
                             per_group=per_group, seq=seq, sort_rows=sort_rows)
    nq = tm // sort_rows
    return pl.pallas_call(
        body,
        grid=(bsz, nt),
        in_specs=[
            pl.BlockSpec((1, tm, d), lambda b, i: (b, i, 0)),
            pl.BlockSpec((1, HALO, d), lambda b, i: (b, jnp.maximum(i * hb - 1, 0), 0)),
            pl.BlockSpec((1, HALO, d), lambda b, i: (b, jnp.minimum((i + 1) * hb, nhb - 1), 0)),
            full(1, d), full(1, d),
            full(d, d_in), full(1, d_in),
            full(CONV_WIDTH, d_conv),
            full(*pool_w.shape), full(1, pool_scale.shape[1]),
            full(d, d), full(1, d), full(1, d),
            full(d, 2 * LANES), full(1, LANES),
        ],
        out_specs=[
            pl.BlockSpec((1, tm, d), lambda b, i: (b, i, 0)),
            pl.BlockSpec((1, nq, SUBLANES, sort_rows), lambda b, i: (b, i, 0, 0)),
            pl.BlockSpec((1, tm, LANES), lambda b, i: (b, i, 0)),
            pl.BlockSpec((1, nq, SUBLANES, LANES), lambda b, i: (b, i, 0, 0)),
        ],
        out_shape=[
            jax.ShapeDtypeStruct((bsz, seq, d), F32),
            jax.ShapeDtypeStruct((bsz, nt * nq, SUBLANES, sort_rows), I32),
            jax.ShapeDtypeStruct((bsz, seq, LANES), F32),
            jax.ShapeDtypeStruct((bsz, nt * nq, SUBLANES, LANES), F32),
        ],
        compiler_params=pltpu.CompilerParams(
            dimension_semantics=("arbitrary", "arbitrary"), vmem_limit_bytes=VMEM_LIMIT),
        name="mixer",
    )(x, x, x, g0, b0, w_in, b_in, conv_w, pool_w, pool_scale, w_o, g1, b1, w_r, b_r)


def _tables_body(cnt_ref, gstart_ref, lstart_ref, len_ref, tot_ref):
    c = cnt_ref[:, 0, :]
    n_tiles = c.shape[0]
    row = lax.broadcasted_iota(I32, c.shape, 0)
    lane = lax.broadcasted_iota(I32, c.shape, 1)

    def prefix(v, idx, size, axis):
        k = 1
        while k < size:
            v = v + jnp.where(idx >= k, pltpu.roll(v, k, axis), 0.0)
            k *= 2
        return v

    over_tiles = prefix(c, row, n_tiles, 0)
    tot = jnp.broadcast_to(over_tiles[n_tiles - 1:n_tiles], c.shape)
    expert_start = prefix(tot, lane, LANES, 1) - tot
    gstart_ref[...] = (expert_start + over_tiles - c).astype(I32)
    lstart_ref[...] = (prefix(c, lane, LANES, 1) - c).astype(I32)
    len_ref[...] = c.astype(I32)
    tot_ref[...] = jnp.broadcast_to(over_tiles[n_tiles - 1:n_tiles], tot_ref.shape).astype(I32)


def _run_tables(cnt):
    table = jax.ShapeDtypeStruct((cnt.shape[0], LANES), I32)
    *tabs, tot = pl.pallas_call(
        _tables_body, out_shape=[table, table, table, jax.ShapeDtypeStruct((SUBLANES, LANES), I32)],
        name="run_tables")(cnt)
    return tuple(tabs), tot[0]


def _slots(ref, first, n, rps):
    start = first * rps
    if not isinstance(start, int):
        start = pl.multiple_of(start, rps)
    return ref.at[pl.ds(start, n * rps)]


def _start_run_copies(tabs, tile, n_experts, make_copy):
    gstart_ref, lstart_ref, len_ref = tabs

    def run_of(e):
        e = jnp.minimum(e, n_experts - 1)
        return len_ref[tile, e], lstart_ref[tile, e], gstart_ref[tile, e]

    def per_pair(i, runs):
        nxt = (run_of(2 * i + 2), run_of(2 * i + 3))
        for prio, (n, ls, gs) in enumerate(runs):

            @pl.when(n > 0)
            def _(n=n, ls=ls, gs=gs, prio=prio):
                make_copy(ls, gs, n).start(priority=prio)
        return nxt
    assert n_experts % 2 == 0
    lax.fori_loop(0, n_experts // 2, per_pair, (run_of(0), run_of(1)))


def _dispatch_body(gstart_ref, lstart_ref, len_ref, x1_ref, ri_ref, xs_ref, buf_ref, sem, *, n_experts):
    step = pl.program_id(0) * pl.num_programs(1) + pl.program_id(1)
    n_tiles = pl.num_programs(0) * pl.num_programs(1) * STEP_TILES
    tabs = (gstart_ref, lstart_ref, len_ref)
    tm = ri_ref.shape[3]
    d = x1_ref.shape[2]
    rps = d // 2 // LANES

    def copier(which):
        def make_copy(local, glob, n):
            return pltpu.make_async_copy(_slots(buf_ref.at[which], local, n, rps),
                                         _slots(xs_ref, glob, n, rps), sem.at[which])
        return make_copy

    def wait_all(which):
        copier(which)(0, 0, TOP_K * tm).wait()

    def one_tile(q, carry):
        tile = step * STEP_TILES + q
        par = lax.rem(tile, 2)

        @pl.when(tile >= 2)
        def _():
            wait_all(par)

        lpos = ri_ref[0, q]
        s_idx = lax.broadcasted_iota(I32, (TOP_K * tm, tm), 0)
        perm = jnp.logical_or(s_idx == lpos[0:1], s_idx == lpos[1:2]).astype(BF16)
        x1 = x1_ref[0, pl.ds(pl.multiple_of(q * tm, tm), tm)]
        _store_slot_rows(buf_ref.at[par], _pack_exact_bf16_pairs(_dot(perm, x1.astype(BF16))))
        _start_run_copies(tabs, tile, n_experts, copier(par))

        @pl.when(tile == n_tiles - 1)
        def _():
            wait_all(par)

            @pl.when(tile >= 1)
            def _():
                wait_all(1 - par)
        return carry
    lax.fori_loop(0, STEP_TILES, one_tile, 0)


def _dispatch(tabs, x1, ri, *, tm, n_slots, n_experts):
    bsz, seq, d = x1.shape
    steps = seq // (tm * STEP_TILES)
    rps = d // 2 // LANES
    grid_spec = pltpu.PrefetchScalarGridSpec(
        num_scalar_prefetch=3,
        grid=(bsz, steps),
        in_specs=[
            pl.BlockSpec((1, STEP_TILES * tm, d), lambda b, i, *_: (b, i, 0)),
            pl.BlockSpec((1, STEP_TILES, SUBLANES, tm), lambda b, i, *_: (b, i, 0, 0)),
        ],
        out_specs=pl.BlockSpec(memory_space=pl.ANY),
        scratch_shapes=[pltpu.VMEM((2, TOP_K * tm * rps, LANES), U32), pltpu.SemaphoreType.DMA((2,))],
    )
    return pl.pallas_call(
        functools.partial(_dispatch_body, n_experts=n_experts),
        grid_spec=grid_spec,
        out_shape=jax.ShapeDtypeStruct((n_slots * rps, LANES), U32),
        compiler_params=pltpu.CompilerParams(
            dimension_semantics=("arbitrary", "arbitrary"), vmem_limit_bytes=VMEM_LIMIT),
        name="dispatch",
    )(*tabs, x1, ri)


def _experts_body(blk_ref, exp_ref, lo_ref, hi_ref, new_ref, par_ref, nxt_ref, xs_ref, w1_ref, w3_ref, w2_ref,
                  ys_ref, f1_ref, f3_ref, f2_ref, w1b_ref, w3b_ref, w2b_ref, sem, *, layer):
    it = pl.program_id(0)
    lo = lo_ref[it]
    rps = xs_ref.shape[0] // SLOT_BLOCK

    def weight_copies(expert, slot):
        return [pltpu.make_async_copy(w_ref.at[layer, expert], f_ref.at[slot], sem.at[slot, k])
                for k, (w_ref, f_ref) in enumerate(((w1_ref, f1_ref), (w3_ref, f3_ref), (w2_ref, f2_ref)))]

    @pl.when(it == 0)
    def _():
        for c in weight_copies(exp_ref[0], 0):
            c.start()

    @pl.when(new_ref[it] == 1)
    def _():
        slot = par_ref[it]
        for c in weight_copies(exp_ref[it], slot):
            c.wait()
        w1b_ref[...] = f1_ref[slot].astype(BF16)
        w3b_ref[...] = f3_ref[slot].astype(BF16)
        w2b_ref[...] = f2_ref[slot].astype(BF16)

        @pl.when(nxt_ref[it] >= 0)
        def _():
            for c in weight_copies(nxt_ref[it], 1 - slot):
                c.start()

    def ffn(first, n):
        xb = _unpack_bf16_pairs(_load_slot_rows(xs_ref, n, rps, first))
        h1 = _dot(xb, w1b_ref[...])
        h = (h1 * _sigmoid(h1)) * _dot(xb, w3b_ref[...])
        return _pack_bf16_pairs(_dot(h.astype(BF16), w2b_ref[...]))

    def run(first, n):
        y = ffn(first, n)

        @pl.when(lo <= first)
        def _():
            _store_slot_rows(ys_ref, y, first)

        @pl.when(lo > first)
        def _():
            m = min(n, half)
            rows = first + lax.broadcasted_iota(I32, (m, y.shape[1]), 0)
            _store_slot_rows(ys_ref, jnp.where(rows >= lo, y[:m], _load_slot_rows(ys_ref, m, rps, first)), first)
            if n > m:
                _store_slot_rows(ys_ref, y[m:], first + m)

    half = SLOT_BLOCK // 2
    need_lo = lo < half
    need_hi = hi_ref[it] > half
    pl.when(jnp.logical_and(need_lo, need_hi))(lambda: run(0, SLOT_BLOCK))
    pl.when(jnp.logical_and(need_lo, jnp.logical_not(need_hi)))(lambda: run(0, half))
    pl.when(jnp.logical_and(jnp.logical_not(need_lo), need_hi))(lambda: run(half, half))


def _experts(items, xs, w1, w3, w2, *, layer):
    rows = xs.shape[0]
    _, _, d, d_e = w1.shape
    rps = d // 2 // LANES
    n_items = items[0].shape[0]
    grid_spec = pltpu.PrefetchScalarGridSpec(
        num_scalar_prefetch=len(items),
        grid=(n_items,),
        in_specs=[
            pl.BlockSpec((SLOT_BLOCK * rps, LANES), lambda i, blk, *_: (blk[i], 0)),
            pl.BlockSpec(memory_space=pl.ANY),
            pl.BlockSpec(memory_space=pl.ANY),
            pl.BlockSpec(memory_space=pl.ANY),
        ],
        out_specs=pl.BlockSpec((SLOT_BLOCK * rps, LANES), lambda i, blk, *_: (blk[i], 0)),
        scratch_shapes=[pltpu.VMEM((2, d, d_e), F32), pltpu.VMEM((2, d, d_e), F32),
                        pltpu.VMEM((2, d_e, d), F32),
                        pltpu.VMEM((d, d_e), BF16), pltpu.VMEM((d, d_e), BF16),
                        pltpu.VMEM((d_e, d), BF16), pltpu.SemaphoreType.DMA((2, 3))],
    )
    return pl.pallas_call(
        functools.partial(_experts_body, layer=layer),
        grid_spec=grid_spec,
        out_shape=jax.ShapeDtypeStruct((rows, LANES), U32),
        compiler_params=pltpu.CompilerParams(
            dimension_semantics=("arbitrary",), vmem_limit_bytes=VMEM_LIMIT),
        name="experts",
    )(*items, xs, w1, w3, w2)


def _expert_items(totals, n_experts, n_slots):
    n_blocks = n_slots // SLOT_BLOCK
    n_items = n_blocks + n_experts - 1
    tot = totals[:n_experts]
    end = jnp.cumsum(tot)
    start = end - tot
    first_blk = start // SLOT_BLOCK
    n_it = jnp.where(tot > 0, (end - 1) // SLOT_BLOCK - first_blk + 1, 0)
    it_end = jnp.cumsum(n_it)
    it_start = it_end - n_it
    idx = jnp.arange(n_items, dtype=I32)
    last_live = it_end[-1] - 1
    live = idx <= last_live
    at = jnp.minimum(idx, last_live)
    expert = jnp.sum(it_end[None, :] <= at[:, None], axis=1).astype(I32)
    onehot = expert[:, None] == jnp.arange(n_experts, dtype=I32)[None, :]
    pick = lambda v: jnp.sum(jnp.where(onehot, v[None, :], 0), axis=1)
    block = pick(first_blk) + idx - pick(it_start)
    item_block = jnp.where(live, block, n_blocks - 1).astype(I32)
    item_lo = jnp.where(live, jnp.maximum(pick(start) - block * SLOT_BLOCK, 0), SLOT_BLOCK).astype(I32)
    item_hi = jnp.where(live, jnp.minimum(pick(end) - block * SLOT_BLOCK, SLOT_BLOCK), 0).astype(I32)
    prev_expert = jnp.concatenate([jnp.full((1,), -1, I32), expert[:-1]])
    is_new = expert != prev_expert
    slot = (jnp.cumsum(is_new.astype(I32)) - 1) % 2
    later_new = lax.cummin(jnp.where(is_new, idx, n_items)[::-1], axis=0)[::-1]
    next_new = jnp.concatenate([later_new[1:], jnp.full((1,), n_items, I32)])
    next_expert = jnp.where(next_new < n_items, expert[jnp.minimum(next_new, n_items - 1)], -1)
    return (item_block, expert, item_lo, item_hi, is_new.astype(I32), slot.astype(I32),
            next_expert.astype(I32))


def _combine_body(gstart_ref, lstart_ref, len_ref, x1_ref, p_ref, wcol_ref, ys_ref, wg_ref, bg_ref, wp_ref,
                  g2_ref, b2_ref, out_ref, buf_ref, sem, *, alpha, n_experts, tm):
    step = pl.program_id(0) * pl.num_programs(1) + pl.program_id(1)
    n_tiles = pl.num_programs(0) * pl.num_programs(1) * STEP_TILES
    tabs = (gstart_ref, lstart_ref, len_ref)
    d = x1_ref.shape[2]
    rps = d // 2 // LANES
    n = tm // COMBINE_CHAINS

    def copier(which):
        def make_copy(local, glob, n_slots):
            return pltpu.make_async_copy(_slots(ys_ref, glob, n_slots, rps),
                                         _slots(buf_ref.at[which], local, n_slots, rps), sem.at[which])
        return make_copy

    def one_tile(q, carry):
        tile = step * STEP_TILES + q
        par = lax.rem(tile, 2)

        @pl.when(tile == 0)
        def _():
            _start_run_copies(tabs, tile, n_experts, copier(par))

        @pl.when(tile + 1 < n_tiles)
        def _():
            _start_run_copies(tabs, tile + 1, n_experts, copier(1 - par))
        copier(par)(0, 0, TOP_K * tm).wait()

        ys = _unpack_bf16_pairs(_load_slot_rows(buf_ref.at[par], TOP_K * tm, rps))

        def rows_of(s):
            return pl.ds(pl.multiple_of(q * tm + s * n, n), n)

        def ple_matmuls(s):
            x1 = x1_ref[0, rows_of(s)]
            gate_logits = _dot(x1.astype(BF16), wg_ref[...]) + bg_ref[...]
            return x1, gate_logits, _dot(p_ref[0, rows_of(s)].astype(BF16), wp_ref[...])

        def finish(s, x1, gate_logits, proj):
            wcol = wcol_ref[0, rows_of(s)]
            s_idx = lax.broadcasted_iota(I32, (n, TOP_K * tm), 1)
            select = jnp.zeros((n, TOP_K * tm), F32)
            for kk in range(TOP_K):
                at_kk = s_idx == wcol[:, TOP_K + kk:TOP_K + kk + 1].astype(I32)
                select = jnp.where(at_kk, wcol[:, kk:kk + 1], select)
            ffn = _dot(select.astype(BF16), ys)
            ple = _sigmoid(gate_logits) * proj
            out_ref[0, rows_of(s)] = _layernorm(alpha * x1 + ffn + ple, g2_ref[...], b2_ref[...])

        chains = [ple_matmuls(s) for s in range(COMBINE_CHAINS)]
        for s, c in enumerate(chains):
            finish(s, *c)
        return carry
    lax.fori_loop(0, STEP_TILES, one_tile, 0)


def _combine(tabs, x1, p, wcol, ys, w_gate, b_gate, w_proj, g2, b2, *, layer, alpha, tm, n_experts):
    bsz, seq, d = x1.shape
    rows = STEP_TILES * tm
    steps = seq // rows
    d_ple = p.shape[-1]
    rps = d // 2 // LANES
    full = lambda *shape: pl.BlockSpec(shape, lambda b, i, *_: (0,) * len(shape))
    grid_spec = pltpu.PrefetchScalarGridSpec(
        num_scalar_prefetch=3,
        grid=(bsz, steps),
        in_specs=[
            pl.BlockSpec((1, rows, d), lambda b, i, *_: (b, i, 0)),
            pl.BlockSpec((None, 1, rows, d_ple), lambda b, i, *_: (layer, b, i, 0)),
            pl.BlockSpec((1, rows, LANES), lambda b, i, *_: (b, i, 0)),
            pl.BlockSpec(memory_space=pl.ANY),
            full(d, d), full(1, d), full(d_ple, d),
            full(1, d), full(1, d),
        ],
        out_specs=pl.BlockSpec((1, rows, d), lambda b, i, *_: (b, i, 0)),
        scratch_shapes=[pltpu.VMEM((2, TOP_K * tm * rps, LANES), U32), pltpu.SemaphoreType.DMA((2,))],
    )
    return pl.pallas_call(
        functools.partial(_combine_body, alpha=alpha, n_experts=n_experts, tm=tm),
        grid_spec=grid_spec,
        out_shape=jax.ShapeDtypeStruct((bsz, seq, d), F32),
        compiler_params=pltpu.CompilerParams(
            dimension_semantics=("arbitrary", "arbitrary"), vmem_limit_bytes=VMEM_LIMIT),
        name="combine",
    )(*tabs, x1, p, wcol, ys, w_gate, b_gate, w_proj, g2, b2)


def _router_weights(w_rg, b_rg, w_re, b_re):
    d, n_groups = w_rg.shape
    per_group = w_re.shape[2]
    assert n_groups <= SUBLANES and per_group <= SUBLANES and SUBLANES * (n_groups + 1) <= LANES
    w_e = jnp.pad(jnp.transpose(w_re, (1, 0, 2)), ((0, 0), (0, 0), (0, SUBLANES - per_group)))
    b_e = jnp.pad(b_re, ((0, 0), (0, SUBLANES - per_group)))
    tail = LANES - SUBLANES * (n_groups + 1)
    w = jnp.concatenate([jnp.pad(w_rg, ((0, 0), (0, SUBLANES - n_groups))), w_e.reshape(d, SUBLANES * n_groups),
                         jnp.zeros((d, tail), F32)], axis=1)
    b = jnp.concatenate([jnp.pad(b_rg, (0, SUBLANES - n_groups)), b_e.reshape(-1), jnp.zeros((tail,), F32)])
    w_hi = w.astype(BF16)
    w_lo = (w - w_hi.astype(F32)).astype(BF16)
    return jnp.concatenate([w_hi, w_lo], axis=1), b[None, :]


def kernel(x, p, ln0_g, ln0_b, w_in, b_in, conv_w, pool_w, pool_scale, w_o, ln1_g, ln1_b,
           w_router_group, b_router_group, w_router_expert, b_router_expert, w1, w3, w2,
           w_ple_gate, b_ple_gate, w_ple_proj, ln2_g, ln2_b):
    bsz, seq, d = x.shape
    depth = w_in.shape[0]
    n_groups, per_group = w_router_expert.shape[1], w_router_expert.shape[3]
    n_experts = n_groups * per_group
    alpha = (2 * depth) ** 0.25
    tm = min(SORT_ROWS, seq)
    tm_mix = min(MIX_ROWS, seq)
    n_slots = bsz * seq * TOP_K
    assert seq % tm_mix == 0 and tm_mix % tm == 0 and tm % LANES == 0 and seq % (STEP_TILES * tm) == 0
    assert n_slots % SLOT_BLOCK == 0 and n_experts <= LANES
    assert d % (2 * LANES) == 0
    row = lambda a: a[None, :]

    for i in range(depth):
        w_r, b_r = _router_weights(w_router_group[i], b_router_group[i],
                                   w_router_expert[i], b_router_expert[i])
        x1, ri, wcol, cnt = _mixer(
            x, row(ln0_g), row(ln0_b), w_in[i].astype(BF16), row(b_in[i]), conv_w[i],
            pool_w[i].astype(BF16), row(pool_scale[i]), w_o[i].astype(BF16),
            row(ln1_g[i]), row(ln1_b[i]), w_r, b_r,
            apply_ln0=(i == 0), alpha=alpha, n_groups=n_groups, per_group=per_group, tm=tm_mix,
            sort_rows=tm)
        tabs, totals = _run_tables(cnt.reshape(-1, SUBLANES, LANES))
        items = _expert_items(totals, n_experts, n_slots)
        xs = _dispatch(tabs, x1, ri, tm=tm, n_slots=n_slots, n_experts=n_experts)
        ys = _experts(items, xs, w1, w3, w2, layer=i)
        x = _combine(tabs, x1, p, wcol, ys, w_ple_gate[i].astype(BF16), row(b_ple_gate[i]),
                     w_ple_proj[i].astype(BF16), row(ln2_g[i]), row(ln2_b[i]),
                     layer=i, alpha=alpha, tm=tm, n_experts=n_experts)
    return x
```

```python
import functools

import jax
import jax.numpy as jnp
from jax import lax
from jax.experimental import pallas as pl
from jax.experimental.pallas import tpu as pltpu

POOL_WINDOWS = (2, 4, 8, 16)
CONV_WIDTH = 3
TOP_K = 2
LN_EPS = 1e-5
HALO = 8
LANES = 128
SUBLANES = 8
SORT_ROWS = 512
MIX_ROWS = 1024
MIX_CHAINS = 2
COMBINE_CHAINS = 2
STEP_TILES = 2
SLOT_BLOCK = 512
VMEM_LIMIT = 56 * 1024 * 1024

F32 = jnp.float32
BF16 = jnp.bfloat16
I32 = jnp.int32
U32 = jnp.uint32


def _layernorm(v, g, b):
    mu = jnp.mean(v, axis=-1, keepdims=True)
    c = v - mu
    var = jnp.mean(c * c, axis=-1, keepdims=True)
    return c * lax.rsqrt(var + LN_EPS) * g + b


def _sigmoid(v):
    return 0.5 * jnp.tanh(0.5 * v) + 0.5


def _dot(a, b):
    return jnp.dot(a, b, preferred_element_type=F32)


def _pair_groups(width):
    assert width % (2 * LANES) == 0
    return range(0, width, 2 * LANES)


def _pack_bf16_pairs(v):
    words = []
    for c0 in _pair_groups(v.shape[1]):
        hi = pltpu.bitcast(v[:, c0:c0 + LANES].astype(BF16).astype(F32), U32)
        lo = pltpu.bitcast(v[:, c0 + LANES:c0 + 2 * LANES].astype(BF16).astype(F32), U32)
        words.append(hi | (lo >> 16))
    return jnp.concatenate(words, axis=1)


def _pack_exact_bf16_pairs(v):
    words = [pltpu.bitcast(v[:, c0:c0 + LANES], U32) | (pltpu.bitcast(v[:, c0 + LANES:c0 + 2 * LANES], U32) >> 16)
             for c0 in _pair_groups(v.shape[1])]
    return jnp.concatenate(words, axis=1)


def _unpack_bf16_pairs(u):
    cols = []
    for c0 in range(0, u.shape[1], LANES):
        w = u[:, c0:c0 + LANES]
        cols.append(pltpu.bitcast(w & jnp.uint32(0xFFFF0000), F32).astype(BF16))
        cols.append(pltpu.bitcast(w << 16, F32).astype(BF16))
    return jnp.concatenate(cols, axis=1)


def _store_slot_rows(ref, packed, first=0):
    n, c = packed.shape
    rps = c // LANES
    for k in range(rps):
        ref[pl.ds(first * rps + k, n, stride=rps), :] = packed[:, k * LANES:(k + 1) * LANES]


def _load_slot_rows(ref, n, rps, first=0):
    return jnp.concatenate([ref[pl.ds(first * rps + k, n, stride=rps), :] for k in range(rps)], axis=1)


def _mixer_body(x_ref, xp_ref, xn_ref, g0_ref, b0_ref, win_ref, bin_ref, cw_ref, pw_ref, ps_ref,
                wo_ref, g1_ref, b1_ref, wr_ref, br_ref,
                x1_ref, ri_ref, wcol_ref, cnt_ref, *, apply_ln0, alpha, n_groups, per_group, seq,
                sort_rows):
    i = pl.program_id(1)
    nt = pl.num_programs(1)
    tm = x_ref.shape[1]
    n = tm // MIX_CHAINS
    rows = n + 2 * HALO
    d_conv = cw_ref.shape[1]
    dg = pw_ref.shape[1]

    xe = jnp.concatenate([xp_ref[0], x_ref[0], xn_ref[0]], axis=0)
    if apply_ln0:
        xe = _layernorm(xe, g0_ref[...], b0_ref[...])
    cw = cw_ref[...]
    ps = ps_ref[...]

    def in_proj(s):
        xs = xe[s * n:s * n + rows]
        return xs, _dot(xs.astype(BF16), win_ref[...]) + bin_ref[...]

    def token_mix(s, z):
        r = lax.broadcasted_iota(I32, (rows, 1), 0)
        lo_ok = jnp.logical_or(r >= HALO, i > 0) if s == 0 else True
        hi_ok = jnp.logical_or(r < n + HALO, i < nt - 1) if s == MIX_CHAINS - 1 else True
        if s == 0 or s == MIX_CHAINS - 1:
            valid = jnp.logical_and(lo_ok, hi_ok)
            mask = lambda a: jnp.where(valid, a, 0.0)
        else:
            mask = lambda a: a

        h = z[:, :d_conv]
        gate_b = z[HALO:HALO + n, d_conv:2 * d_conv]
        gate_c = z[:, 2 * d_conv:3 * d_conv]
        v = mask(gate_c * h)
        conv = (cw[0:1] * pltpu.roll(v, 1, 0)[HALO:HALO + n]
                + cw[1:2] * v[HALO:HALO + n]
                + cw[2:3] * pltpu.roll(v, rows - 1, 0)[HALO:HALO + n])
        parts = [gate_b * conv]

        pos = i * tm + s * n + r[HALO:HALO + n] - HALO
        for g, w in enumerate(POOL_WINDOWS):
            left = w // 2
            right = w - 1 - left
            c0 = 3 * d_conv + g * dg
            ug = mask(z[:, c0:c0 + dg])
            acc = ug
            k = 1
            while k < w:
                acc = acc + pltpu.roll(acc, k, 0)
                k *= 2
            if right:
                acc = pltpu.roll(acc, rows - right, 0)

            def edge(lo):
                p = pos[lo:lo + HALO]
                cnt = (jnp.minimum(p + right + 1, seq) - jnp.maximum(p - left, 0)).astype(F32)
                return acc[HALO + lo:2 * HALO + lo] / cnt - ug[HALO + lo:2 * HALO + lo]
            inner = acc[2 * HALO:n] * (1.0 / w) - ug[2 * HALO:n]
            pooled = jnp.concatenate([edge(0), inner, edge(n - HALO)], axis=0)
            parts.append(_dot(pooled.astype(BF16), pw_ref[g]) * ps[:, g * dg:(g + 1) * dg])
        return jnp.concatenate(parts, axis=1).astype(BF16)

    def out_proj(xs, cat):
        mix = _dot(cat, wo_ref[...])
        return _layernorm(alpha * xs[HALO:HALO + n] + mix, g1_ref[...], b1_ref[...])

    def route(x1s):
        xh = x1s.astype(BF16)
        xl = (x1s - xh.astype(F32)).astype(BF16)
        l_hi = _dot(xh, wr_ref[...])
        l_lo = _dot(xl, wr_ref[:, :LANES])
        return l_hi[:, :LANES] + l_hi[:, LANES:] + l_lo + br_ref[...]

    chains = [in_proj(s) for s in range(MIX_CHAINS)]
    x1s = [out_proj(xs, token_mix(s, z)) for s, (xs, z) in enumerate(chains)]
    x1 = jnp.concatenate(x1s, axis=0)
    x1_ref[0] = x1
    logits = jnp.concatenate([route(v) for v in x1s], axis=0)
    ri_ref[0] = jnp.zeros(ri_ref.shape[1:], I32)
    for q in range(tm // sort_rows):
        rows_q = slice(q * sort_rows, (q + 1) * sort_rows)
        lpos0, lpos1, gw0, gw1, counts = _route_and_sort(logits[rows_q], n_groups, per_group)
        ri_ref[0, q, 0:1, :] = lpos0.astype(I32)
        ri_ref[0, q, 1:2, :] = lpos1.astype(I32)
        rowl = lax.broadcasted_iota(I32, (LANES, sort_rows), 0)
        wrows = jnp.where(rowl == 0, gw0, jnp.where(rowl == 1, gw1,
                          jnp.where(rowl == 2, lpos0, jnp.where(rowl == 3, lpos1, 0.0))))
        wcol_ref[0, rows_q, :] = wrows.T
        cnt_ref[0, q] = counts


def _route_and_sort(logits, n_groups, per_group):
    n = logits.shape[0]
    lt = logits.T
    row8 = lax.broadcasted_iota(I32, (SUBLANES, n), 0)
    neg = jnp.float32(-jnp.inf)
    lg = jnp.where(row8 < n_groups, lt[0:SUBLANES], neg)
    mg = jnp.max(lg, axis=0, keepdims=True)
    g_w = 1.0 / jnp.sum(jnp.exp(lg - mg), axis=0, keepdims=True)
    g_sel = jnp.min(jnp.where(lg == mg, row8, SUBLANES), axis=0, keepdims=True)
    le = lt[SUBLANES:2 * SUBLANES]
    for g in range(1, n_groups):
        le = jnp.where(g_sel == g, lt[SUBLANES * (g + 1):SUBLANES * (g + 2)], le)
    le = jnp.where(row8 < per_group, le, neg)
    v1 = jnp.max(le, axis=0, keepdims=True)
    i1 = jnp.min(jnp.where(le == v1, row8, SUBLANES), axis=0, keepdims=True)
    le2 = jnp.where(row8 == i1, neg, le)
    v2 = jnp.max(le2, axis=0, keepdims=True)
    i2 = jnp.min(jnp.where(le2 == v2, row8, SUBLANES), axis=0, keepdims=True)
    e21 = jnp.exp(v2 - v1)
    w_first = 1.0 / (1.0 + e21)
    gw0 = w_first * g_w
    gw1 = (e21 * w_first) * g_w
    eid0 = g_sel * per_group + i1
    eid1 = g_sel * per_group + i2

    rowl = lax.broadcasted_iota(I32, (LANES, n), 0)
    oh0 = rowl == eid0
    oh1 = rowl == eid1
    a_idx = lax.broadcasted_iota(I32, (n, n), 0)
    b_idx = lax.broadcasted_iota(I32, (n, n), 1)
    before = (a_idx < b_idx).astype(BF16)
    r0 = _dot(oh0.astype(BF16), before)
    r1 = _dot(oh1.astype(BF16), before)
    cnt0 = jnp.sum(oh0.astype(F32), axis=1, keepdims=True)
    lower = jnp.sum((rowl > eid0).astype(F32) + (rowl > eid1).astype(F32),
                    axis=1, keepdims=True)
    lpos0 = jnp.sum(jnp.where(oh0, lower + r0, 0.0), axis=0, keepdims=True)
    lpos1 = jnp.sum(jnp.where(oh1, lower + cnt0 + r1, 0.0), axis=0, keepdims=True)

    both = jnp.logical_or(oh0, oh1).astype(BF16)
    counts = lax.dot_general(jnp.ones((SUBLANES, n), BF16), both, (((1,), (1,)), ((), ())),
                             preferred_element_type=F32)
    return lpos0, lpos1, gw0, gw1, counts


def _mixer(x, g0, b0, w_in, b_in, conv_w, pool_w, pool_scale, w_o, g1, b1, w_r, b_r, *,
           apply_ln0, alpha, n_groups, per_group, tm, sort_rows):
    bsz, seq, d = x.shape
    nt = seq // tm
    hb = tm // HALO
    nhb = seq // HALO
    d_in = w_in.shape[1]
    d_conv = conv_w.shape[1]
    full = lambda *shape: pl.BlockSpec(shape, lambda b, i: (0,) * len(shape))
    body = functools.partial(_mixer_body, apply_ln0=apply_ln0, alpha=alpha, n_groups=n_groups,
                             per_group=per_group, seq=seq, sort_rows=sort_rows)
    nq = tm // sort_rows
    return pl.pallas_call(
        body,
        grid=(bsz, nt),
        in_specs=[
            pl.BlockSpec((1, tm, d), lambda b, i: (b, i, 0)),
            pl.BlockSpec((1, HALO, d), lambda b, i: (b, jnp.maximum(i * hb - 1, 0), 0)),
            pl.BlockSpec((1, HALO, d), lambda b, i: (b, jnp.minimum((i + 1) * hb, nhb - 1), 0)),
            full(1, d), full(1, d),
            full(d, d_in), full(1, d_in),
            full(CONV_WIDTH, d_conv),
            full(*pool_w.shape), full(1, pool_scale.shape[1]),
            full(d, d), full(1, d), full(1, d),
            full(d, 2 * LANES), full(1, LANES),
        ],
        out_specs=[
            pl.BlockSpec((1, tm, d), lambda b, i: (b, i, 0)),
            pl.BlockSpec((1, nq, SUBLANES, sort_rows), lambda b, i: (b, i, 0, 0)),
            pl.BlockSpec((1, tm, LANES), lambda b, i: (b, i, 0)),
            pl.BlockSpec((1, nq, SUBLANES, LANES), lambda b, i: (b, i, 0, 0)),
        ],
        out_shape=[
            jax.ShapeDtypeStruct((bsz, seq, d), F32),
            jax.ShapeDtypeStruct((bsz, nt * nq, SUBLANES, sort_rows), I32),
            jax.ShapeDtypeStruct((bsz, seq, LANES), F32),
            jax.ShapeDtypeStruct((bsz, nt * nq, SUBLANES, LANES), F32),
        ],
        compiler_params=pltpu.CompilerParams(
            dimension_semantics=("arbitrary", "arbitrary"), vmem_limit_bytes=VMEM_LIMIT),
        name="mixer",
    )(x, x, x, g0, b0, w_in, b_in, conv_w, pool_w, pool_scale, w_o, g1, b1, w_r, b_r)


def _tables_body(cnt_ref, gstart_ref, lstart_ref, len_ref, tot_ref):
    c = cnt_ref[:, 0, :]
    n_tiles = c.shape[0]
    row = lax.broadcasted_iota(I32, c.shape, 0)
    lane = lax.broadcasted_iota(I32, c.shape, 1)

    def prefix(v, idx, size, axis):
        k = 1
        while k < size:
            v = v + jnp.where(idx >= k, pltpu.roll(v, k, axis), 0.0)
            k *= 2
        return v

    over_tiles = prefix(c, row, n_tiles, 0)
    tot = jnp.broadcast_to(over_tiles[n_tiles - 1:n_tiles], c.shape)
    expert_start = prefix(tot, lane, LANES, 1) - tot
    gstart_ref[...] = (expert_start + over_tiles - c).astype(I32)
    lstart_ref[...] = (prefix(c, lane, LANES, 1) - c).astype(I32)
    len_ref[...] = c.astype(I32)
    tot_ref[...] = jnp.broadcast_to(over_tiles[n_tiles - 1:n_tiles], tot_ref.shape).astype(I32)


def _run_tables(cnt):
    table = jax.ShapeDtypeStruct((cnt.shape[0], LANES), I32)
    *tabs, tot = pl.pallas_call(
        _tables_body, out_shape=[table, table, table, jax.ShapeDtypeStruct((SUBLANES, LANES), I32)],
        name="run_tables")(cnt)
    return tuple(tabs), tot[0]


def _slots(ref, first, n, rps):
    start = first * rps
    if not isinstance(start, int):
        start = pl.multiple_of(start, rps)
    return ref.at[pl.ds(start, n * rps)]


def _start_run_copies(tabs, tile, n_experts, make_copy):
    gstart_ref, lstart_ref, len_ref = tabs

    def run_of(e):
        e = jnp.minimum(e, n_experts - 1)
        return len_ref[tile, e], lstart_ref[tile, e], gstart_ref[tile, e]

    def per_pair(i, runs):
        nxt = (run_of(2 * i + 2), run_of(2 * i + 3))
        for prio, (n, ls, gs) in enumerate(runs):

            @pl.when(n > 0)
            def _(n=n, ls=ls, gs=gs, prio=prio):
                make_copy(ls, gs, n).start(priority=prio)
        return nxt
    assert n_experts % 2 == 0
    lax.fori_loop(0, n_experts // 2, per_pair, (run_of(0), run_of(1)))


def _dispatch_body(gstart_ref, lstart_ref, len_ref, x1_ref, ri_ref, xs_ref, buf_ref, sem, *, n_experts):
    step = pl.program_id(0) * pl.num_programs(1) + pl.program_id(1)
    n_tiles = pl.num_programs(0) * pl.num_programs(1) * STEP_TILES
    tabs = (gstart_ref, lstart_ref, len_ref)
    tm = ri_ref.shape[3]
    d = x1_ref.shape[2]
    rps = d // 2 // LANES

    def copier(which):
        def make_copy(local, glob, n):
            return pltpu.make_async_copy(_slots(buf_ref.at[which], local, n, rps),
                                         _slots(xs_ref, glob, n, rps), sem.at[which])
        return make_copy

    def wait_all(which):
        copier(which)(0, 0, TOP_K * tm).wait()

    def one_tile(q, carry):
        tile = step * STEP_TILES + q
        par = lax.rem(tile, 2)

        @pl.when(tile >= 2)
        def _():
            wait_all(par)

        lpos = ri_ref[0, q]
        s_idx = lax.broadcasted_iota(I32, (TOP_K * tm, tm), 0)
        perm = jnp.logical_or(s_idx == lpos[0:1], s_idx == lpos[1:2]).astype(BF16)
        x1 = x1_ref[0, pl.ds(pl.multiple_of(q * tm, tm), tm)]
        _store_slot_rows(buf_ref.at[par], _pack_exact_bf16_pairs(_dot(perm, x1.astype(BF16))))
        _start_run_copies(tabs, tile, n_experts, copier(par))

        @pl.when(tile == n_tiles - 1)
        def _():
            wait_all(par)

            @pl.when(tile >= 1)
            def _():
                wait_all(1 - par)
        return carry
    lax.fori_loop(0, STEP_TILES, one_tile, 0)


def _dispatch(tabs, x1, ri, *, tm, n_slots, n_experts):
    bsz, seq, d = x1.shape
    steps = seq // (tm * STEP_TILES)
    rps = d // 2 // LANES
    grid_spec = pltpu.PrefetchScalarGridSpec(
        num_scalar_prefetch=3,
        grid=(bsz, steps),
        in_specs=[
            pl.BlockSpec((1, STEP_TILES * tm, d), lambda b, i, *_: (b, i, 0)),
            pl.BlockSpec((1, STEP_TILES, SUBLANES, tm), lambda b, i, *_: (b, i, 0, 0)),
        ],
        out_specs=pl.BlockSpec(memory_space=pl.ANY),
        scratch_shapes=[pltpu.VMEM((2, TOP_K * tm * rps, LANES), U32), pltpu.SemaphoreType.DMA((2,))],
    )
    return pl.pallas_call(
        functools.partial(_dispatch_body, n_experts=n_experts),
        grid_spec=grid_spec,
        out_shape=jax.ShapeDtypeStruct((n_slots * rps, LANES), U32),
        compiler_params=pltpu.CompilerParams(
            dimension_semantics=("arbitrary", "arbitrary"), vmem_limit_bytes=VMEM_LIMIT),
        name="dispatch",
    )(*tabs, x1, ri)


def _experts_body(blk_ref, exp_ref, lo_ref, hi_ref, new_ref, par_ref, nxt_ref, xs_ref, w1_ref, w3_ref, w2_ref,
                  ys_ref, f1_ref, f3_ref, f2_ref, w1b_ref, w3b_ref, w2b_ref, keep_ref, sem, *, layer):
    it = pl.program_id(0)
    lo = lo_ref[it]
    rps = xs_ref.shape[0] // SLOT_BLOCK

    def weight_copies(expert, slot):
        return [pltpu.make_async_copy(w_ref.at[layer, expert], f_ref.at[slot], sem.at[slot, k])
                for k, (w_ref, f_ref) in enumerate(((w1_ref, f1_ref), (w3_ref, f3_ref), (w2_ref, f2_ref)))]

    @pl.when(it == 0)
    def _():
        for c in weight_copies(exp_ref[0], 0):
            c.start()

    @pl.when(new_ref[it] == 1)
    def _():
        slot = par_ref[it]
        for c in weight_copies(exp_ref[it], slot):
            c.wait()
        w1b_ref[...] = f1_ref[slot].astype(BF16)
        w3b_ref[...] = f3_ref[slot].astype(BF16)
        w2b_ref[...] = f2_ref[slot].astype(BF16)

        @pl.when(nxt_ref[it] >= 0)
        def _():
            for c in weight_copies(nxt_ref[it], 1 - slot):
                c.start()

    def ffn(first, n):
        xb = _unpack_bf16_pairs(_load_slot_rows(xs_ref, n, rps, first))
        h1 = _dot(xb, w1b_ref[...])
        h = (h1 * _sigmoid(h1)) * _dot(xb, w3b_ref[...])
        return _pack_bf16_pairs(_dot(h.astype(BF16), w2b_ref[...]))

    def run(first, n):
        m = min(n, half)
        cut = lo > first

        @pl.when(cut)
        def _():
            keep_ref[...] = _load_slot_rows(ys_ref, m, rps, first)

        _store_slot_rows(ys_ref, ffn(first, n), first)

        @pl.when(cut)
        def _():
            rows = first + lax.broadcasted_iota(I32, keep_ref.shape, 0)
            mine = _load_slot_rows(ys_ref, m, rps, first)
            _store_slot_rows(ys_ref, jnp.where(rows >= lo, mine, keep_ref[...]), first)

    half = SLOT_BLOCK // 2
    need_lo = lo < half
    need_hi = hi_ref[it] > half
    pl.when(jnp.logical_and(need_lo, need_hi))(lambda: run(0, SLOT_BLOCK))
    pl.when(jnp.logical_and(need_lo, jnp.logical_not(need_hi)))(lambda: run(0, half))
    pl.when(jnp.logical_and(jnp.logical_not(need_lo), need_hi))(lambda: run(half, half))


def _experts(items, xs, w1, w3, w2, *, layer):
    rows = xs.shape[0]
    _, _, d, d_e = w1.shape
    rps = d // 2 // LANES
    n_items = items[0].shape[0]
    grid_spec = pltpu.PrefetchScalarGridSpec(
        num_scalar_prefetch=len(items),
        grid=(n_items,),
        in_specs=[
            pl.BlockSpec((SLOT_BLOCK * rps, LANES), lambda i, blk, *_: (blk[i], 0)),
            pl.BlockSpec(memory_space=pl.ANY),
            pl.BlockSpec(memory_space=pl.ANY),
            pl.BlockSpec(memory_space=pl.ANY),
        ],
        out_specs=pl.BlockSpec((SLOT_BLOCK * rps, LANES), lambda i, blk, *_: (blk[i], 0)),
        scratch_shapes=[pltpu.VMEM((2, d, d_e), F32), pltpu.VMEM((2, d, d_e), F32),
                        pltpu.VMEM((2, d_e, d), F32),
                        pltpu.VMEM((d, d_e), BF16), pltpu.VMEM((d, d_e), BF16),
                        pltpu.VMEM((d_e, d), BF16),
                        pltpu.VMEM((SLOT_BLOCK // 2, d // 2), U32), pltpu.SemaphoreType.DMA((2, 3))],
    )
    return pl.pallas_call(
        functools.partial(_experts_body, layer=layer),
        grid_spec=grid_spec,
        out_shape=jax.ShapeDtypeStruct((rows, LANES), U32),
        compiler_params=pltpu.CompilerParams(
            dimension_semantics=("arbitrary",), vmem_limit_bytes=VMEM_LIMIT),
        name="experts",
    )(*items, xs, w1, w3, w2)


def _expert_items(totals, n_experts, n_slots):
    n_blocks = n_slots // SLOT_BLOCK
    n_items = n_blocks + n_experts - 1
    tot = totals[:n_experts]
    end = jnp.cumsum(tot)
    start = end - tot
    first_blk = start // SLOT_BLOCK
    n_it = jnp.where(tot > 0, (end - 1) // SLOT_BLOCK - first_blk + 1, 0)
    it_end = jnp.cumsum(n_it)
    it_start = it_end - n_it
    idx = jnp.arange(n_items, dtype=I32)
    last_live = it_end[-1] - 1
    live = idx <= last_live
    at = jnp.minimum(idx, last_live)
    expert = jnp.sum(it_end[None, :] <= at[:, None], axis=1).astype(I32)
    onehot = expert[:, None] == jnp.arange(n_experts, dtype=I32)[None, :]
    pick = lambda v: jnp.sum(jnp.where(onehot, v[None, :], 0), axis=1)
    block = pick(first_blk) + idx - pick(it_start)
    item_block = jnp.where(live, block, n_blocks - 1).astype(I32)
    item_lo = jnp.where(live, jnp.maximum(pick(start) - block * SLOT_BLOCK, 0), SLOT_BLOCK).astype(I32)
    item_hi = jnp.where(live, jnp.minimum(pick(end) - block * SLOT_BLOCK, SLOT_BLOCK), 0).astype(I32)
    prev_expert = jnp.concatenate([jnp.full((1,), -1, I32), expert[:-1]])
    is_new = expert != prev_expert
    slot = (jnp.cumsum(is_new.astype(I32)) - 1) % 2
    later_new = lax.cummin(jnp.where(is_new, idx, n_items)[::-1], axis=0)[::-1]
    next_new = jnp.concatenate([later_new[1:], jnp.full((1,), n_items, I32)])
    next_expert = jnp.where(next_new < n_items, expert[jnp.minimum(next_new, n_items - 1)], -1)
    return (item_block, expert, item_lo, item_hi, is_new.astype(I32), slot.astype(I32),
            next_expert.astype(I32))


def _combine_body(gstart_ref, lstart_ref, len_ref, x1_ref, p_ref, wcol_ref, ys_ref, wg_ref, bg_ref, wp_ref,
                  g2_ref, b2_ref, out_ref, buf_ref, sem, *, alpha, n_experts, tm):
    step = pl.program_id(0) * pl.num_programs(1) + pl.program_id(1)
    n_tiles = pl.num_programs(0) * pl.num_programs(1) * STEP_TILES
    tabs = (gstart_ref, lstart_ref, len_ref)
    d = x1_ref.shape[2]
    rps = d // 2 // LANES
    n = tm // COMBINE_CHAINS

    def copier(which):
        def make_copy(local, glob, n_slots):
            return pltpu.make_async_copy(_slots(ys_ref, glob, n_slots, rps),
                                         _slots(buf_ref.at[which], local, n_slots, rps), sem.at[which])
        return make_copy

    def one_tile(q, carry):
        tile = step * STEP_TILES + q
        par = lax.rem(tile, 2)

        @pl.when(tile == 0)
        def _():
            _start_run_copies(tabs, tile, n_experts, copier(par))

        @pl.when(tile + 1 < n_tiles)
        def _():
            _start_run_copies(tabs, tile + 1, n_experts, copier(1 - par))
        copier(par)(0, 0, TOP_K * tm).wait()

        ys = _unpack_bf16_pairs(_load_slot_rows(buf_ref.at[par], TOP_K * tm, rps))

        def rows_of(s):
            return pl.ds(pl.multiple_of(q * tm + s * n, n), n)

        def ple_matmuls(s):
            x1 = x1_ref[0, rows_of(s)]
            gate_logits = _dot(x1.astype(BF16), wg_ref[...]) + bg_ref[...]
            return x1, gate_logits, _dot(p_ref[0, rows_of(s)].astype(BF16), wp_ref[...])

        def finish(s, x1, gate_logits, proj):
            wcol = wcol_ref[0, rows_of(s)]
            s_idx = lax.broadcasted_iota(I32, (n, TOP_K * tm), 1)
            select = jnp.zeros((n, TOP_K * tm), F32)
            for kk in range(TOP_K):
                at_kk = s_idx == wcol[:, TOP_K + kk:TOP_K + kk + 1].astype(I32)
                select = jnp.where(at_kk, wcol[:, kk:kk + 1], select)
            ffn = _dot(select.astype(BF16), ys)
            ple = _sigmoid(gate_logits) * proj
            out_ref[0, rows_of(s)] = _layernorm(alpha * x1 + ffn + ple, g2_ref[...], b2_ref[...])

        chains = [ple_matmuls(s) for s in range(COMBINE_CHAINS)]
        for s, c in enumerate(chains):
            finish(s, *c)
        return carry
    lax.fori_loop(0, STEP_TILES, one_tile, 0)


def _combine(tabs, x1, p, wcol, ys, w_gate, b_gate, w_proj, g2, b2, *, layer, alpha, tm, n_experts):
    bsz, seq, d = x1.shape
    rows = STEP_TILES * tm
    steps = seq // rows
    d_ple = p.shape[-1]
    rps = d // 2 // LANES
    full = lambda *shape: pl.BlockSpec(shape, lambda b, i, *_: (0,) * len(shape))
    grid_spec = pltpu.PrefetchScalarGridSpec(
        num_scalar_prefetch=3,
        grid=(bsz, steps),
        in_specs=[
            pl.BlockSpec((1, rows, d), lambda b, i, *_: (b, i, 0)),
            pl.BlockSpec((None, 1, rows, d_ple), lambda b, i, *_: (layer, b, i, 0)),
            pl.BlockSpec((1, rows, LANES), lambda b, i, *_: (b, i, 0)),
            pl.BlockSpec(memory_space=pl.ANY),
            full(d, d), full(1, d), full(d_ple, d),
            full(1, d), full(1, d),
        ],
        out_specs=pl.BlockSpec((1, rows, d), lambda b, i, *_: (b, i, 0)),
        scratch_shapes=[pltpu.VMEM((2, TOP_K * tm * rps, LANES), U32), pltpu.SemaphoreType.DMA((2,))],
    )
    return pl.pallas_call(
        functools.partial(_combine_body, alpha=alpha, n_experts=n_experts, tm=tm),
        grid_spec=grid_spec,
        out_shape=jax.ShapeDtypeStruct((bsz, seq, d), F32),
        compiler_params=pltpu.CompilerParams(
            dimension_semantics=("arbitrary", "arbitrary"), vmem_limit_bytes=VMEM_LIMIT),
        name="combine",
    )(*tabs, x1, p, wcol, ys, w_gate, b_gate, w_proj, g2, b2)


def _router_weights(w_rg, b_rg, w_re, b_re):
    d, n_groups = w_rg.shape
    per_group = w_re.shape[2]
    assert n_groups <= SUBLANES and per_group <= SUBLANES and SUBLANES * (n_groups + 1) <= LANES
    w_e = jnp.pad(jnp.transpose(w_re, (1, 0, 2)), ((0, 0), (0, 0), (0, SUBLANES - per_group)))
    b_e = jnp.pad(b_re, ((0, 0), (0, SUBLANES - per_group)))
    tail = LANES - SUBLANES * (n_groups + 1)
    w = jnp.concatenate([jnp.pad(w_rg, ((0, 0), (0, SUBLANES - n_groups))), w_e.reshape(d, SUBLANES * n_groups),
                         jnp.zeros((d, tail), F32)], axis=1)
    b = jnp.concatenate([jnp.pad(b_rg, (0, SUBLANES - n_groups)), b_e.reshape(-1), jnp.zeros((tail,), F32)])
    w_hi = w.astype(BF16)
    w_lo = (w - w_hi.astype(F32)).astype(BF16)
    return jnp.concatenate([w_hi, w_lo], axis=1), b[None, :]


def kernel(x, p, ln0_g, ln0_b, w_in, b_in, conv_w, pool_w, pool_scale, w_o, ln1_g, ln1_b,
           w_router_group, b_router_group, w_router_expert, b_router_expert, w1, w3, w2,
           w_ple_gate, b_ple_gate, w_ple_proj, ln2_g, ln2_b):
    bsz, seq, d = x.shape
    depth = w_in.shape[0]
    n_groups, per_group = w_router_expert.shape[1], w_router_expert.shape[3]
    n_experts = n_groups * per_group
    alpha = (2 * depth) ** 0.25
    tm = min(SORT_ROWS, seq)
    tm_mix = min(MIX_ROWS, seq)
    n_slots = bsz * seq * TOP_K
    assert seq % tm_mix == 0 and tm_mix % tm == 0 and tm % LANES == 0 and seq % (STEP_TILES * tm) == 0
    assert n_slots % SLOT_BLOCK == 0 and n_experts <= LANES
    assert d % (2 * LANES) == 0
    row = lambda a: a[None, :]

    for i in range(depth):
        w_r, b_r = _router_weights(w_router_group[i], b_router_group[i],
                                   w_router_expert[i], b_router_expert[i])
        x1, ri, wcol, cnt = _mixer(
            x, row(ln0_g), row(ln0_b), w_in[i].astype(BF16), row(b_in[i]), conv_w[i],
            pool_w[i].astype(BF16), row(pool_scale[i]), w_o[i].astype(BF16),
            row(ln1_g[i]), row(ln1_b[i]), w_r, b_r,
            apply_ln0=(i == 0), alpha=alpha, n_groups=n_groups, per_group=per_group, tm=tm_mix,
            sort_rows=tm)
        tabs, totals = _run_tables(cnt.reshape(-1, SUBLANES, LANES))
        items = _expert_items(totals, n_experts, n_slots)
        xs = _dispatch(tabs, x1, ri, tm=tm, n_slots=n_slots, n_experts=n_experts)
        ys = _experts(items, xs, w1, w3, w2, layer=i)
        x = _combine(tabs, x1, p, wcol, ys, w_ple_gate[i].astype(BF16), row(b_ple_gate[i]),
                     w_ple_proj[i].astype(BF16), row(ln2_g[i]), row(ln2_b[i]),
                     layer=i, alpha=alpha, tm=tm, n_experts=n_experts)
    return x
```

```python
import functools

import jax
import jax.numpy as jnp
from jax import lax
from jax.experimental import pallas as pl
from jax.experimental.pallas import tpu as pltpu

POOL_WINDOWS = (2, 4, 8, 16)
CONV_WIDTH = 3
TOP_K = 2
LN_EPS = 1e-5
HALO = 8
LANES = 128
SUBLANES = 8
SORT_ROWS = 512
MIX_ROWS = 1024
MIX_CHAINS = 2
COMBINE_CHAINS = 2
STEP_TILES = 2
SLOT_BLOCK = 512
VMEM_LIMIT = 56 * 1024 * 1024

F32 = jnp.float32
BF16 = jnp.bfloat16
I32 = jnp.int32
U32 = jnp.uint32


def _layernorm(v, g, b):
    mu = jnp.mean(v, axis=-1, keepdims=True)
    c = v - mu
    var = jnp.mean(c * c, axis=-1, keepdims=True)
    return c * lax.rsqrt(var + LN_EPS) * g + b


def _sigmoid(v):
    return 0.5 * jnp.tanh(0.5 * v) + 0.5


def _dot(a, b):
    return jnp.dot(a, b, preferred_element_type=F32)


def _pair_groups(width):
    assert width % (2 * LANES) == 0
    return range(0, width, 2 * LANES)


def _pack_bf16_pairs(v):
    words = []
    for c0 in _pair_groups(v.shape[1]):
        hi = pltpu.bitcast(v[:, c0:c0 + LANES].astype(BF16).astype(F32), U32)
        lo = pltpu.bitcast(v[:, c0 + LANES:c0 + 2 * LANES].astype(BF16).astype(F32), U32)
        words.append(hi | (lo >> 16))
    return jnp.concatenate(words, axis=1)


def _pack_exact_bf16_pairs(v):
    words = [pltpu.bitcast(v[:, c0:c0 + LANES], U32) | (pltpu.bitcast(v[:, c0 + LANES:c0 + 2 * LANES], U32) >> 16)
             for c0 in _pair_groups(v.shape[1])]
    return jnp.concatenate(words, axis=1)


def _unpack_bf16_pairs(u):
    cols = []
    for c0 in range(0, u.shape[1], LANES):
        w = u[:, c0:c0 + LANES]
        cols.append(pltpu.bitcast(w & jnp.uint32(0xFFFF0000), F32).astype(BF16))
        cols.append(pltpu.bitcast(w << 16, F32).astype(BF16))
    return jnp.concatenate(cols, axis=1)


def _store_slot_rows(ref, packed, first=0):
    n, c = packed.shape
    rps = c // LANES
    for k in range(rps):
        ref[pl.ds(first * rps + k, n, stride=rps), :] = packed[:, k * LANES:(k + 1) * LANES]


def _load_slot_rows(ref, n, rps, first=0):
    return jnp.concatenate([ref[pl.ds(first * rps + k, n, stride=rps), :] for k in range(rps)], axis=1)


def _mixer_body(x_ref, xp_ref, xn_ref, g0_ref, b0_ref, win_ref, bin_ref, cw_ref, pw_ref, ps_ref,
                wo_ref, g1_ref, b1_ref, wr_ref, br_ref,
                x1_ref, ri_ref, wcol_ref, cnt_ref, *, apply_ln0, alpha, n_groups, per_group, seq,
                sort_rows):
    i = pl.program_id(1)
    nt = pl.num_programs(1)
    tm = x_ref.shape[1]
    n = tm // MIX_CHAINS
    rows = n + 2 * HALO
    d_conv = cw_ref.shape[1]
    dg = pw_ref.shape[1]

    xe = jnp.concatenate([xp_ref[0], x_ref[0], xn_ref[0]], axis=0)
    if apply_ln0:
        xe = _layernorm(xe, g0_ref[...], b0_ref[...])
    cw = cw_ref[...]
    ps = ps_ref[...]

    def in_proj(s):
        xs = xe[s * n:s * n + rows]
        return xs, _dot(xs.astype(BF16), win_ref[...]) + bin_ref[...]

    def token_mix(s, z):
        r = lax.broadcasted_iota(I32, (rows, 1), 0)
        lo_ok = jnp.logical_or(r >= HALO, i > 0) if s == 0 else True
        hi_ok = jnp.logical_or(r < n + HALO, i < nt - 1) if s == MIX_CHAINS - 1 else True
        if s == 0 or s == MIX_CHAINS - 1:
            valid = jnp.logical_and(lo_ok, hi_ok)
            mask = lambda a: jnp.where(valid, a, 0.0)
        else:
            mask = lambda a: a

        h = z[:, :d_conv]
        gate_b = z[HALO:HALO + n, d_conv:2 * d_conv]
        gate_c = z[:, 2 * d_conv:3 * d_conv]
        v = mask(gate_c * h)
        conv = (cw[0:1] * pltpu.roll(v, 1, 0)[HALO:HALO + n]
                + cw[1:2] * v[HALO:HALO + n]
                + cw[2:3] * pltpu.roll(v, rows - 1, 0)[HALO:HALO + n])
        parts = [gate_b * conv]

        pos = i * tm + s * n + r[HALO:HALO + n] - HALO
        for g, w in enumerate(POOL_WINDOWS):
            left = w // 2
            right = w - 1 - left
            c0 = 3 * d_conv + g * dg
            ug = mask(z[:, c0:c0 + dg])
            acc = ug
            k = 1
            while k < w:
                acc = acc + pltpu.roll(acc, k, 0)
                k *= 2
            if right:
                acc = pltpu.roll(acc, rows - right, 0)

            def edge(lo):
                p = pos[lo:lo + HALO]
                cnt = (jnp.minimum(p + right + 1, seq) - jnp.maximum(p - left, 0)).astype(F32)
                return acc[HALO + lo:2 * HALO + lo] / cnt - ug[HALO + lo:2 * HALO + lo]
            inner = acc[2 * HALO:n] * (1.0 / w) - ug[2 * HALO:n]
            pooled = jnp.concatenate([edge(0), inner, edge(n - HALO)], axis=0)
            parts.append(_dot(pooled.astype(BF16), pw_ref[g]) * ps[:, g * dg:(g + 1) * dg])
        return jnp.concatenate(parts, axis=1).astype(BF16)

    def out_proj(xs, cat):
        mix = _dot(cat, wo_ref[...])
        return _layernorm(alpha * xs[HALO:HALO + n] + mix, g1_ref[...], b1_ref[...])

    def route(x1s):
        xh = x1s.astype(BF16)
        xl = (x1s - xh.astype(F32)).astype(BF16)
        l_hi = _dot(xh, wr_ref[...])
        l_lo = _dot(xl, wr_ref[:, :LANES])
        return l_hi[:, :LANES] + l_hi[:, LANES:] + l_lo + br_ref[...]

    chains = [in_proj(s) for s in range(MIX_CHAINS)]
    x1s = [out_proj(xs, token_mix(s, z)) for s, (xs, z) in enumerate(chains)]
    x1 = jnp.concatenate(x1s, axis=0)
    x1_ref[0] = x1
    logits = jnp.concatenate([route(v) for v in x1s], axis=0)
    ri_ref[0] = jnp.zeros(ri_ref.shape[1:], I32)
    for q in range(tm // sort_rows):
        rows_q = slice(q * sort_rows, (q + 1) * sort_rows)
        lpos0, lpos1, gw0, gw1, counts = _route_and_sort(logits[rows_q], n_groups, per_group)
        ri_ref[0, q, 0:1, :] = lpos0.astype(I32)
        ri_ref[0, q, 1:2, :] = lpos1.astype(I32)
        rowl = lax.broadcasted_iota(I32, (LANES, sort_rows), 0)
        wrows = jnp.where(rowl == 0, gw0, jnp.where(rowl == 1, gw1,
                          jnp.where(rowl == 2, lpos0, jnp.where(rowl == 3, lpos1, 0.0))))
        wcol_ref[0, rows_q, :] = wrows.T
        cnt_ref[0, q] = counts


def _route_and_sort(logits, n_groups, per_group):
    n = logits.shape[0]
    lt = logits.T
    row8 = lax.broadcasted_iota(I32, (SUBLANES, n), 0)
    neg = jnp.float32(-jnp.inf)
    lg = jnp.where(row8 < n_groups, lt[0:SUBLANES], neg)
    mg = jnp.max(lg, axis=0, keepdims=True)
    g_w = 1.0 / jnp.sum(jnp.exp(lg - mg), axis=0, keepdims=True)
    g_sel = jnp.min(jnp.where(lg == mg, row8, SUBLANES), axis=0, keepdims=True)
    le = lt[SUBLANES:2 * SUBLANES]
    for g in range(1, n_groups):
        le = jnp.where(g_sel == g, lt[SUBLANES * (g + 1):SUBLANES * (g + 2)], le)
    le = jnp.where(row8 < per_group, le, neg)
    v1 = jnp.max(le, axis=0, keepdims=True)
    i1 = jnp.min(jnp.where(le == v1, row8, SUBLANES), axis=0, keepdims=True)
    le2 = jnp.where(row8 == i1, neg, le)
    v2 = jnp.max(le2, axis=0, keepdims=True)
    i2 = jnp.min(jnp.where(le2 == v2, row8, SUBLANES), axis=0, keepdims=True)
    e21 = jnp.exp(v2 - v1)
    w_first = 1.0 / (1.0 + e21)
    gw0 = w_first * g_w
    gw1 = (e21 * w_first) * g_w
    eid0 = g_sel * per_group + i1
    eid1 = g_sel * per_group + i2

    rowl = lax.broadcasted_iota(I32, (LANES, n), 0)
    oh0 = rowl == eid0
    oh1 = rowl == eid1
    a_idx = lax.broadcasted_iota(I32, (n, n), 0)
    b_idx = lax.broadcasted_iota(I32, (n, n), 1)
    before = (a_idx < b_idx).astype(BF16)
    r0 = _dot(oh0.astype(BF16), before)
    r1 = _dot(oh1.astype(BF16), before)
    cnt0 = jnp.sum(oh0.astype(F32), axis=1, keepdims=True)
    lower = jnp.sum((rowl > eid0).astype(F32) + (rowl > eid1).astype(F32),
                    axis=1, keepdims=True)
    lpos0 = jnp.sum(jnp.where(oh0, lower + r0, 0.0), axis=0, keepdims=True)
    lpos1 = jnp.sum(jnp.where(oh1, lower + cnt0 + r1, 0.0), axis=0, keepdims=True)

    both = jnp.logical_or(oh0, oh1).astype(BF16)
    counts = lax.dot_general(jnp.ones((SUBLANES, n), BF16), both, (((1,), (1,)), ((), ())),
                             preferred_element_type=F32)
    return lpos0, lpos1, gw0, gw1, counts


def _mixer(x, g0, b0, w_in, b_in, conv_w, pool_w, pool_scale, w_o, g1, b1, w_r, b_r, *,
           apply_ln0, alpha, n_groups, per_group, tm, sort_rows):
    bsz, seq, d = x.shape
    nt = seq // tm
    hb = tm // HALO
    nhb = seq // HALO
    d_in = w_in.shape[1]
    d_conv = conv_w.shape[1]
    full = lambda *shape: pl.BlockSpec(shape, lambda b, i: (0,) * len(shape))
    body = functools.partial(_mixer_body, apply_ln0=apply_ln0, alpha=alpha, n_groups=n_groups,
                             per_group=per_group, seq=seq, sort_rows=sort_rows)
    nq = tm // sort_rows
    return pl.pallas_call(
        body,
        grid=(bsz, nt),
        in_specs=[
            pl.BlockSpec((1, tm, d), lambda b, i: (b, i, 0)),
            pl.BlockSpec((1, HALO, d), lambda b, i: (b, jnp.maximum(i * hb - 1, 0), 0)),
            pl.BlockSpec((1, HALO, d), lambda b, i: (b, jnp.minimum((i + 1) * hb, nhb - 1), 0)),
            full(1, d), full(1, d),
            full(d, d_in), full(1, d_in),
            full(CONV_WIDTH, d_conv),
            full(*pool_w.shape), full(1, pool_scale.shape[1]),
            full(d, d), full(1, d), full(1, d),
            full(d, 2 * LANES), full(1, LANES),
        ],
        out_specs=[
            pl.BlockSpec((1, tm, d), lambda b, i: (b, i, 0)),
            pl.BlockSpec((1, nq, SUBLANES, sort_rows), lambda b, i: (b, i, 0, 0)),
            pl.BlockSpec((1, tm, LANES), lambda b, i: (b, i, 0)),
            pl.BlockSpec((1, nq, SUBLANES, LANES), lambda b, i: (b, i, 0, 0)),
        ],
        out_shape=[
            jax.ShapeDtypeStruct((bsz, seq, d), F32),
            jax.ShapeDtypeStruct((bsz, nt * nq, SUBLANES, sort_rows), I32),
            jax.ShapeDtypeStruct((bsz, seq, LANES), F32),
            jax.ShapeDtypeStruct((bsz, nt * nq, SUBLANES, LANES), F32),
        ],
        compiler_params=pltpu.CompilerParams(
            dimension_semantics=("arbitrary", "arbitrary"), vmem_limit_bytes=VMEM_LIMIT),
        name="mixer",
    )(x, x, x, g0, b0, w_in, b_in, conv_w, pool_w, pool_scale, w_o, g1, b1, w_r, b_r)


def _tables_body(cnt_ref, gstart_ref, lstart_ref, len_ref, tot_ref):
    c = cnt_ref[:, 0, :]
    n_tiles = c.shape[0]
    row = lax.broadcasted_iota(I32, c.shape, 0)
    lane = lax.broadcasted_iota(I32, c.shape, 1)

    def prefix(v, idx, size, axis):
        k = 1
        while k < size:
            v = v + jnp.where(idx >= k, pltpu.roll(v, k, axis), 0.0)
            k *= 2
        return v

    over_tiles = prefix(c, row, n_tiles, 0)
    tot = jnp.broadcast_to(over_tiles[n_tiles - 1:n_tiles], c.shape)
    expert_start = prefix(tot, lane, LANES, 1) - tot
    gstart_ref[...] = (expert_start + over_tiles - c).astype(I32)
    lstart_ref[...] = (prefix(c, lane, LANES, 1) - c).astype(I32)
    len_ref[...] = c.astype(I32)
    tot_ref[...] = jnp.broadcast_to(over_tiles[n_tiles - 1:n_tiles], tot_ref.shape).astype(I32)


def _run_tables(cnt):
    table = jax.ShapeDtypeStruct((cnt.shape[0], LANES), I32)
    *tabs, tot = pl.pallas_call(
        _tables_body, out_shape=[table, table, table, jax.ShapeDtypeStruct((SUBLANES, LANES), I32)],
        name="run_tables")(cnt)
    return tuple(tabs), tot[0]


def _slots(ref, first, n, rps):
    start = first * rps
    if not isinstance(start, int):
        start = pl.multiple_of(start, rps)
    return ref.at[pl.ds(start, n * rps)]


def _start_run_copies(tabs, tile, n_experts, make_copy):
    gstart_ref, lstart_ref, len_ref = tabs

    def run_of(e):
        e = jnp.minimum(e, n_experts - 1)
        return len_ref[tile, e], lstart_ref[tile, e], gstart_ref[tile, e]

    def per_pair(i, runs):
        nxt = (run_of(2 * i + 2), run_of(2 * i + 3))
        for prio, (n, ls, gs) in enumerate(runs):

            @pl.when(n > 0)
            def _(n=n, ls=ls, gs=gs, prio=prio):
                make_copy(ls, gs, n).start(priority=prio)
        return nxt
    assert n_experts % 2 == 0
    lax.fori_loop(0, n_experts // 2, per_pair, (run_of(0), run_of(1)))


def _dispatch_body(gstart_ref, lstart_ref, len_ref, x1_ref, ri_ref, xs_ref, buf_ref, sem, *, n_experts):
    step = pl.program_id(0) * pl.num_programs(1) + pl.program_id(1)
    n_tiles = pl.num_programs(0) * pl.num_programs(1) * STEP_TILES
    tabs = (gstart_ref, lstart_ref, len_ref)
    tm = ri_ref.shape[3]
    d = x1_ref.shape[2]
    rps = d // 2 // LANES

    def copier(which):
        def make_copy(local, glob, n):
            return pltpu.make_async_copy(_slots(buf_ref.at[which], local, n, rps),
                                         _slots(xs_ref, glob, n, rps), sem.at[which])
        return make_copy

    def wait_all(which):
        copier(which)(0, 0, TOP_K * tm).wait()

    def one_tile(q, carry):
        tile = step * STEP_TILES + q
        par = lax.rem(tile, 2)

        @pl.when(tile >= 2)
        def _():
            wait_all(par)

        lpos = ri_ref[0, q]
        s_idx = lax.broadcasted_iota(I32, (TOP_K * tm, tm), 0)
        perm = jnp.logical_or(s_idx == lpos[0:1], s_idx == lpos[1:2]).astype(BF16)
        x1 = x1_ref[0, pl.ds(pl.multiple_of(q * tm, tm), tm)]
        _store_slot_rows(buf_ref.at[par], _pack_exact_bf16_pairs(_dot(perm, x1.astype(BF16))))
        _start_run_copies(tabs, tile, n_experts, copier(par))

        @pl.when(tile == n_tiles - 1)
        def _():
            wait_all(par)

            @pl.when(tile >= 1)
            def _():
                wait_all(1 - par)
        return carry
    lax.fori_loop(0, STEP_TILES, one_tile, 0)


def _dispatch(tabs, x1, ri, *, tm, n_slots, n_experts):
    bsz, seq, d = x1.shape
    steps = seq // (tm * STEP_TILES)
    rps = d // 2 // LANES
    grid_spec = pltpu.PrefetchScalarGridSpec(
        num_scalar_prefetch=3,
        grid=(bsz, steps),
        in_specs=[
            pl.BlockSpec((1, STEP_TILES * tm, d), lambda b, i, *_: (b, i, 0)),
            pl.BlockSpec((1, STEP_TILES, SUBLANES, tm), lambda b, i, *_: (b, i, 0, 0)),
        ],
        out_specs=pl.BlockSpec(memory_space=pl.ANY),
        scratch_shapes=[pltpu.VMEM((2, TOP_K * tm * rps, LANES), U32), pltpu.SemaphoreType.DMA((2,))],
    )
    return pl.pallas_call(
        functools.partial(_dispatch_body, n_experts=n_experts),
        grid_spec=grid_spec,
        out_shape=jax.ShapeDtypeStruct((n_slots * rps, LANES), U32),
        compiler_params=pltpu.CompilerParams(
            dimension_semantics=("arbitrary", "arbitrary"), vmem_limit_bytes=VMEM_LIMIT),
        name="dispatch",
    )(*tabs, x1, ri)


def _experts_body(blk_ref, exp_ref, lo_ref, hi_ref, new_ref, par_ref, nxt_ref, xs_ref, w1_ref, w3_ref, w2_ref,
                  ys_ref, f1_ref, f3_ref, f2_ref, w1b_ref, w3b_ref, w2b_ref, keep_ref, sem, *, layer):
    it = pl.program_id(0)
    lo = lo_ref[it]
    rps = xs_ref.shape[0] // SLOT_BLOCK

    def weight_copies(expert, slot):
        return [pltpu.make_async_copy(w_ref.at[layer, expert], f_ref.at[slot], sem.at[slot, k])
                for k, (w_ref, f_ref) in enumerate(((w1_ref, f1_ref), (w3_ref, f3_ref), (w2_ref, f2_ref)))]

    @pl.when(it == 0)
    def _():
        for c in weight_copies(exp_ref[0], 0):
            c.start()

    @pl.when(new_ref[it] == 1)
    def _():
        slot = par_ref[it]
        for c in weight_copies(exp_ref[it], slot):
            c.wait()
        w1b_ref[...] = f1_ref[slot].astype(BF16)
        w3b_ref[...] = f3_ref[slot].astype(BF16)
        w2b_ref[...] = f2_ref[slot].astype(BF16)

        @pl.when(nxt_ref[it] >= 0)
        def _():
            for c in weight_copies(nxt_ref[it], 1 - slot):
                c.start()

    def ffn(first, n):
        xb = _unpack_bf16_pairs(_load_slot_rows(xs_ref, n, rps, first))
        h1 = _dot(xb, w1b_ref[...])
        h = (h1 * _sigmoid(h1)) * _dot(xb, w3b_ref[...])
        return _pack_bf16_pairs(_dot(h.astype(BF16), w2b_ref[...]))

    def run(first, n):
        m = min(n, half)
        cut = lo > first

        @pl.when(cut)
        def _():
            keep_ref[...] = _load_slot_rows(ys_ref, m, rps, first)

        _store_slot_rows(ys_ref, ffn(first, n), first)

        @pl.when(cut)
        def _():
            rows = first + lax.broadcasted_iota(I32, keep_ref.shape, 0)
            mine = _load_slot_rows(ys_ref, m, rps, first)
            _store_slot_rows(ys_ref, jnp.where(rows >= lo, mine, keep_ref[...]), first)

    half = SLOT_BLOCK // 2
    need_lo = lo < half
    need_hi = hi_ref[it] > half
    pl.when(jnp.logical_and(need_lo, need_hi))(lambda: run(0, SLOT_BLOCK))
    pl.when(jnp.logical_and(need_lo, jnp.logical_not(need_hi)))(lambda: run(0, half))
    pl.when(jnp.logical_and(jnp.logical_not(need_lo), need_hi))(lambda: run(half, half))


def _experts(items, xs, w1, w3, w2, *, layer):
    rows = xs.shape[0]
    _, _, d, d_e = w1.shape
    rps = d // 2 // LANES
    n_items = items[0].shape[0]
    grid_spec = pltpu.PrefetchScalarGridSpec(
        num_scalar_prefetch=len(items),
        grid=(n_items,),
        in_specs=[
            pl.BlockSpec((SLOT_BLOCK * rps, LANES), lambda i, blk, *_: (blk[i], 0)),
            pl.BlockSpec(memory_space=pl.ANY),
            pl.BlockSpec(memory_space=pl.ANY),
            pl.BlockSpec(memory_space=pl.ANY),
        ],
        out_specs=pl.BlockSpec((SLOT_BLOCK * rps, LANES), lambda i, blk, *_: (blk[i], 0)),
        scratch_shapes=[pltpu.VMEM((2, d, d_e), F32), pltpu.VMEM((2, d, d_e), F32),
                        pltpu.VMEM((2, d_e, d), F32),
                        pltpu.VMEM((d, d_e), BF16), pltpu.VMEM((d, d_e), BF16),
                        pltpu.VMEM((d_e, d), BF16),
                        pltpu.VMEM((SLOT_BLOCK // 2, d // 2), U32), pltpu.SemaphoreType.DMA((2, 3))],
    )
    return pl.pallas_call(
        functools.partial(_experts_body, layer=layer),
        grid_spec=grid_spec,
        out_shape=jax.ShapeDtypeStruct((rows, LANES), U32),
        compiler_params=pltpu.CompilerParams(
            dimension_semantics=("arbitrary",), vmem_limit_bytes=VMEM_LIMIT),
        name="experts",
    )(*items, xs, w1, w3, w2)


def _expert_items(totals, n_experts, n_slots):
    n_blocks = n_slots // SLOT_BLOCK
    n_items = n_blocks + n_experts - 1
    tot = totals[:n_experts]
    end = jnp.cumsum(tot)
    start = end - tot
    first_blk = start // SLOT_BLOCK
    n_it = jnp.where(tot > 0, (end - 1) // SLOT_BLOCK - first_blk + 1, 0)
    it_end = jnp.cumsum(n_it)
    it_start = it_end - n_it
    idx = jnp.arange(n_items, dtype=I32)
    last_live = it_end[-1] - 1
    live = idx <= last_live
    at = jnp.minimum(idx, last_live)
    expert = jnp.sum(it_end[None, :] <= at[:, None], axis=1).astype(I32)
    onehot = expert[:, None] == jnp.arange(n_experts, dtype=I32)[None, :]
    pick = lambda v: jnp.sum(jnp.where(onehot, v[None, :], 0), axis=1)
    block = pick(first_blk) + idx - pick(it_start)
    item_block = jnp.where(live, block, n_blocks - 1).astype(I32)
    item_lo = jnp.where(live, jnp.maximum(pick(start) - block * SLOT_BLOCK, 0), SLOT_BLOCK).astype(I32)
    item_hi = jnp.where(live, jnp.minimum(pick(end) - block * SLOT_BLOCK, SLOT_BLOCK), 0).astype(I32)
    prev_expert = jnp.concatenate([jnp.full((1,), -1, I32), expert[:-1]])
    is_new = expert != prev_expert
    slot = (jnp.cumsum(is_new.astype(I32)) - 1) % 2
    later_new = lax.cummin(jnp.where(is_new, idx, n_items)[::-1], axis=0)[::-1]
    next_new = jnp.concatenate([later_new[1:], jnp.full((1,), n_items, I32)])
    next_expert = jnp.where(next_new < n_items, expert[jnp.minimum(next_new, n_items - 1)], -1)
    return (item_block, expert, item_lo, item_hi, is_new.astype(I32), slot.astype(I32),
            next_expert.astype(I32))


def _combine_body(gstart_ref, lstart_ref, len_ref, x1_ref, p_ref, wcol_ref, ys_ref, wg_ref, bg_ref, wp_ref,
                  g2_ref, b2_ref, out_ref, buf_ref, sem, *, alpha, n_experts, tm):
    step = pl.program_id(0) * pl.num_programs(1) + pl.program_id(1)
    tabs = (gstart_ref, lstart_ref, len_ref)
    d = x1_ref.shape[2]
    rps = d // 2 // LANES
    n = tm // COMBINE_CHAINS

    def copier(which, q):
        def make_copy(local, glob, n_slots):
            return pltpu.make_async_copy(_slots(ys_ref, glob, n_slots, rps),
                                         _slots(buf_ref.at[which, q], local, n_slots, rps), sem.at[which, q])
        return make_copy

    n_steps = pl.num_programs(0) * pl.num_programs(1)
    cur = lax.rem(step, 2)

    @pl.when(step == 0)
    def _():
        for q in range(STEP_TILES):
            _start_run_copies(tabs, q, n_experts, copier(cur, q))

    @pl.when(step + 1 < n_steps)
    def _():
        for q in range(STEP_TILES):
            _start_run_copies(tabs, (step + 1) * STEP_TILES + q, n_experts, copier(1 - cur, q))
    for q in range(STEP_TILES):
        copier(cur, q)(0, 0, TOP_K * tm).wait()

    ys = [_unpack_bf16_pairs(_load_slot_rows(buf_ref.at[cur, q], TOP_K * tm, rps)) for q in range(STEP_TILES)]

    def rows_of(c):
        return slice(c * n, (c + 1) * n)

    def ple_matmuls(c):
        x1 = x1_ref[0, rows_of(c)]
        gate_logits = _dot(x1.astype(BF16), wg_ref[...]) + bg_ref[...]
        return x1, gate_logits, _dot(p_ref[0, rows_of(c)].astype(BF16), wp_ref[...])

    def finish(c, x1, gate_logits, proj):
        wcol = wcol_ref[0, rows_of(c)]
        s_idx = lax.broadcasted_iota(I32, (n, TOP_K * tm), 1)
        select = jnp.zeros((n, TOP_K * tm), F32)
        for kk in range(TOP_K):
            at_kk = s_idx == wcol[:, TOP_K + kk:TOP_K + kk + 1].astype(I32)
            select = jnp.where(at_kk, wcol[:, kk:kk + 1], select)
        ffn = _dot(select.astype(BF16), ys[c // COMBINE_CHAINS])
        ple = _sigmoid(gate_logits) * proj
        out_ref[0, rows_of(c)] = _layernorm(alpha * x1 + ffn + ple, g2_ref[...], b2_ref[...])

    n_chains = STEP_TILES * COMBINE_CHAINS
    ahead = ple_matmuls(0)
    for c in range(n_chains):
        now, ahead = ahead, (ple_matmuls(c + 1) if c + 1 < n_chains else None)
        finish(c, *now)


def _combine(tabs, x1, p, wcol, ys, w_gate, b_gate, w_proj, g2, b2, *, layer, alpha, tm, n_experts):
    bsz, seq, d = x1.shape
    rows = STEP_TILES * tm
    steps = seq // rows
    d_ple = p.shape[-1]
    rps = d // 2 // LANES
    full = lambda *shape: pl.BlockSpec(shape, lambda b, i, *_: (0,) * len(shape))
    grid_spec = pltpu.PrefetchScalarGridSpec(
        num_scalar_prefetch=3,
        grid=(bsz, steps),
        in_specs=[
            pl.BlockSpec((1, rows, d), lambda b, i, *_: (b, i, 0)),
            pl.BlockSpec((None, 1, rows, d_ple), lambda b, i, *_: (layer, b, i, 0)),
            pl.BlockSpec((1, rows, LANES), lambda b, i, *_: (b, i, 0)),
            pl.BlockSpec(memory_space=pl.ANY),
            full(d, d), full(1, d), full(d_ple, d),
            full(1, d), full(1, d),
        ],
        out_specs=pl.BlockSpec((1, rows, d), lambda b, i, *_: (b, i, 0)),
        scratch_shapes=[pltpu.VMEM((2, STEP_TILES, TOP_K * tm * rps, LANES), U32),
                        pltpu.SemaphoreType.DMA((2, STEP_TILES))],
    )
    return pl.pallas_call(
        functools.partial(_combine_body, alpha=alpha, n_experts=n_experts, tm=tm),
        grid_spec=grid_spec,
        out_shape=jax.ShapeDtypeStruct((bsz, seq, d), F32),
        compiler_params=pltpu.CompilerParams(
            dimension_semantics=("arbitrary", "arbitrary"), vmem_limit_bytes=VMEM_LIMIT),
        name="combine",
    )(*tabs, x1, p, wcol, ys, w_gate, b_gate, w_proj, g2, b2)


def _router_weights(w_rg, b_rg, w_re, b_re):
    d, n_groups = w_rg.shape
    per_group = w_re.shape[2]
    assert n_groups <= SUBLANES and per_group <= SUBLANES and SUBLANES * (n_groups + 1) <= LANES
    w_e = jnp.pad(jnp.transpose(w_re, (1, 0, 2)), ((0, 0), (0, 0), (0, SUBLANES - per_group)))
    b_e = jnp.pad(b_re, ((0, 0), (0, SUBLANES - per_group)))
    tail = LANES - SUBLANES * (n_groups + 1)
    w = jnp.concatenate([jnp.pad(w_rg, ((0, 0), (0, SUBLANES - n_groups))), w_e.reshape(d, SUBLANES * n_groups),
                         jnp.zeros((d, tail), F32)], axis=1)
    b = jnp.concatenate([jnp.pad(b_rg, (0, SUBLANES - n_groups)), b_e.reshape(-1), jnp.zeros((tail,), F32)])
    w_hi = w.astype(BF16)
    w_lo = (w - w_hi.astype(F32)).astype(BF16)
    return jnp.concatenate([w_hi, w_lo], axis=1), b[None, :]


def kernel(x, p, ln0_g, ln0_b, w_in, b_in, conv_w, pool_w, pool_scale, w_o, ln1_g, ln1_b,
           w_router_group, b_router_group, w_router_expert, b_router_expert, w1, w3, w2,
           w_ple_gate, b_ple_gate, w_ple_proj, ln2_g, ln2_b):
    bsz, seq, d = x.shape
    depth = w_in.shape[0]
    n_groups, per_group = w_router_expert.shape[1], w_router_expert.shape[3]
    n_experts = n_groups * per_group
    alpha = (2 * depth) ** 0.25
    tm = min(SORT_ROWS, seq)
    tm_mix = min(MIX_ROWS, seq)
    n_slots = bsz * seq * TOP_K
    assert seq % tm_mix == 0 and tm_mix % tm == 0 and tm % LANES == 0 and seq % (STEP_TILES * tm) == 0
    assert n_slots % SLOT_BLOCK == 0 and n_experts <= LANES
    assert d % (2 * LANES) == 0
    row = lambda a: a[None, :]

    for i in range(depth):
        w_r, b_r = _router_weights(w_router_group[i], b_router_group[i],
                                   w_router_expert[i], b_router_expert[i])
        x1, ri, wcol, cnt = _mixer(
            x, row(ln0_g), row(ln0_b), w_in[i].astype(BF16), row(b_in[i]), conv_w[i],
            pool_w[i].astype(BF16), row(pool_scale[i]), w_o[i].astype(BF16),
            row(ln1_g[i]), row(ln1_b[i]), w_r, b_r,
            apply_ln0=(i == 0), alpha=alpha, n_groups=n_groups, per_group=per_group, tm=tm_mix,
            sort_rows=tm)
        tabs, totals = _run_tables(cnt.reshape(-1, SUBLANES, LANES))
        items = _expert_items(totals, n_experts, n_slots)
        xs = _dispatch(tabs, x1, ri, tm=tm, n_slots=n_slots, n_experts=n_experts)
        ys = _experts(items, xs, w1, w3, w2, layer=i)
        x = _combine(tabs, x1, p, wcol, ys, w_ple_gate[i].astype(BF16), row(b_ple_gate[i]),
                     w_ple_proj[i].astype(BF16), row(ln2_g[i]), row(ln2_b[i]),
                     layer=i, alpha=alpha, tm=tm, n_experts=n_experts)
    return x
```

```python
import functools

import jax
import jax.numpy as jnp
from jax import lax
from jax.experimental import pallas as pl
from jax.experimental.pallas import tpu as pltpu

POOL_WINDOWS = (2, 4, 8, 16)
CONV_WIDTH = 3
TOP_K = 2
LN_EPS = 1e-5
HALO = 8
LANES = 128
SUBLANES = 8
SORT_ROWS = 512
MIX_ROWS = 1024
MIX_CHAINS = 2
COMBINE_CHAINS = 2
STEP_TILES = 2
SLOT_BLOCK = 512
VMEM_LIMIT = 56 * 1024 * 1024

F32 = jnp.float32
BF16 = jnp.bfloat16
I32 = jnp.int32
U32 = jnp.uint32


def _layernorm(v, g, b):
    mu = jnp.mean(v, axis=-1, keepdims=True)
    c = v - mu
    var = jnp.mean(c * c, axis=-1, keepdims=True)
    return c * lax.rsqrt(var + LN_EPS) * g + b


def _sigmoid(v):
    return 0.5 * jnp.tanh(0.5 * v) + 0.5


def _dot(a, b):
    return jnp.dot(a, b, preferred_element_type=F32)


def _pair_groups(width):
    assert width % (2 * LANES) == 0
    return range(0, width, 2 * LANES)


def _pack_bf16_pairs(v):
    words = []
    for c0 in _pair_groups(v.shape[1]):
        hi = pltpu.bitcast(v[:, c0:c0 + LANES].astype(BF16).astype(F32), U32)
        lo = pltpu.bitcast(v[:, c0 + LANES:c0 + 2 * LANES].astype(BF16).astype(F32), U32)
        words.append(hi | (lo >> 16))
    return jnp.concatenate(words, axis=1)


def _pack_exact_bf16_pairs(v):
    words = [pltpu.bitcast(v[:, c0:c0 + LANES], U32) | (pltpu.bitcast(v[:, c0 + LANES:c0 + 2 * LANES], U32) >> 16)
             for c0 in _pair_groups(v.shape[1])]
    return jnp.concatenate(words, axis=1)


def _unpack_bf16_pairs(u):
    cols = []
    for c0 in range(0, u.shape[1], LANES):
        w = u[:, c0:c0 + LANES]
        cols.append(pltpu.bitcast(w & jnp.uint32(0xFFFF0000), F32).astype(BF16))
        cols.append(pltpu.bitcast(w << 16, F32).astype(BF16))
    return jnp.concatenate(cols, axis=1)


def _store_slot_rows(ref, packed, first=0):
    n, c = packed.shape
    rps = c // LANES
    for k in range(rps):
        ref[pl.ds(first * rps + k, n, stride=rps), :] = packed[:, k * LANES:(k + 1) * LANES]


def _load_slot_rows(ref, n, rps, first=0):
    return jnp.concatenate([ref[pl.ds(first * rps + k, n, stride=rps), :] for k in range(rps)], axis=1)


def _mixer_body(x_ref, xp_ref, xn_ref, g0_ref, b0_ref, win_ref, bin_ref, cw_ref, pw_ref, ps_ref,
                wo_ref, g1_ref, b1_ref, wr_ref, br_ref,
                x1_ref, ri_ref, wcol_ref, cnt_ref, *, apply_ln0, alpha, n_groups, per_group, seq,
                sort_rows):
    i = pl.program_id(1)
    nt = pl.num_programs(1)
    tm = x_ref.shape[1]
    n = tm // MIX_CHAINS
    rows = n + 2 * HALO
    d_conv = cw_ref.shape[1]
    dg = pw_ref.shape[1]

    xe = jnp.concatenate([xp_ref[0], x_ref[0], xn_ref[0]], axis=0)
    if apply_ln0:
        xe = _layernorm(xe, g0_ref[...], b0_ref[...])
    cw = cw_ref[...]
    ps = ps_ref[...]

    def in_proj(s):
        xs = xe[s * n:s * n + rows]
        return xs, _dot(xs.astype(BF16), win_ref[...]) + bin_ref[...]

    def token_mix(s, z):
        r = lax.broadcasted_iota(I32, (rows, 1), 0)
        lo_ok = jnp.logical_or(r >= HALO, i > 0) if s == 0 else True
        hi_ok = jnp.logical_or(r < n + HALO, i < nt - 1) if s == MIX_CHAINS - 1 else True
        if s == 0 or s == MIX_CHAINS - 1:
            valid = jnp.logical_and(lo_ok, hi_ok)
            mask = lambda a: jnp.where(valid, a, 0.0)
        else:
            mask = lambda a: a

        h = z[:, :d_conv]
        gate_b = z[HALO:HALO + n, d_conv:2 * d_conv]
        gate_c = z[:, 2 * d_conv:3 * d_conv]
        v = mask(gate_c * h)
        conv = (cw[0:1] * pltpu.roll(v, 1, 0)[HALO:HALO + n]
                + cw[1:2] * v[HALO:HALO + n]
                + cw[2:3] * pltpu.roll(v, rows - 1, 0)[HALO:HALO + n])
        parts = [gate_b * conv]

        pos = i * tm + s * n + r[HALO:HALO + n] - HALO
        for g, w in enumerate(POOL_WINDOWS):
            left = w // 2
            right = w - 1 - left
            c0 = 3 * d_conv + g * dg
            ug = mask(z[:, c0:c0 + dg])
            acc = ug
            k = 1
            while k < w:
                acc = acc + pltpu.roll(acc, k, 0)
                k *= 2
            if right:
                acc = pltpu.roll(acc, rows - right, 0)

            def edge(lo):
                p = pos[lo:lo + HALO]
                cnt = (jnp.minimum(p + right + 1, seq) - jnp.maximum(p - left, 0)).astype(F32)
                return acc[HALO + lo:2 * HALO + lo] / cnt - ug[HALO + lo:2 * HALO + lo]
            inner = acc[2 * HALO:n] * (1.0 / w) - ug[2 * HALO:n]
            pooled = jnp.concatenate([edge(0), inner, edge(n - HALO)], axis=0)
            parts.append(_dot(pooled.astype(BF16), pw_ref[g]) * ps[:, g * dg:(g + 1) * dg])
        return jnp.concatenate(parts, axis=1).astype(BF16)

    def out_proj(xs, cat):
        mix = _dot(cat, wo_ref[...])
        return _layernorm(alpha * xs[HALO:HALO + n] + mix, g1_ref[...], b1_ref[...])

    def route(x1s):
        xh = x1s.astype(BF16)
        xl = (x1s - xh.astype(F32)).astype(BF16)
        l_hi = _dot(xh, wr_ref[...])
        l_lo = _dot(xl, wr_ref[:, :LANES])
        return l_hi[:, :LANES] + l_hi[:, LANES:] + l_lo + br_ref[...]

    chains = [in_proj(s) for s in range(MIX_CHAINS)]
    x1s = [out_proj(xs, token_mix(s, z)) for s, (xs, z) in enumerate(chains)]
    x1 = jnp.concatenate(x1s, axis=0)
    x1_ref[0] = x1
    logits = jnp.concatenate([route(v) for v in x1s], axis=0)
    ri_ref[0] = jnp.zeros(ri_ref.shape[1:], I32)
    for q in range(tm // sort_rows):
        rows_q = slice(q * sort_rows, (q + 1) * sort_rows)
        lpos0, lpos1, gw0, gw1, counts = _route_and_sort(logits[rows_q], n_groups, per_group)
        ri_ref[0, q, 0:1, :] = lpos0.astype(I32)
        ri_ref[0, q, 1:2, :] = lpos1.astype(I32)
        rowl = lax.broadcasted_iota(I32, (LANES, sort_rows), 0)
        wrows = jnp.where(rowl == 0, gw0, jnp.where(rowl == 1, gw1,
                          jnp.where(rowl == 2, lpos0, jnp.where(rowl == 3, lpos1, 0.0))))
        wcol_ref[0, rows_q, :] = wrows.T
        cnt_ref[0, q] = counts


def _route_and_sort(logits, n_groups, per_group):
    n = logits.shape[0]
    lt = logits.T
    row8 = lax.broadcasted_iota(I32, (SUBLANES, n), 0)
    neg = jnp.float32(-jnp.inf)
    lg = jnp.where(row8 < n_groups, lt[0:SUBLANES], neg)
    mg = jnp.max(lg, axis=0, keepdims=True)
    g_w = 1.0 / jnp.sum(jnp.exp(lg - mg), axis=0, keepdims=True)
    g_sel = jnp.min(jnp.where(lg == mg, row8, SUBLANES), axis=0, keepdims=True)
    le = lt[SUBLANES:2 * SUBLANES]
    for g in range(1, n_groups):
        le = jnp.where(g_sel == g, lt[SUBLANES * (g + 1):SUBLANES * (g + 2)], le)
    le = jnp.where(row8 < per_group, le, neg)
    v1 = jnp.max(le, axis=0, keepdims=True)
    i1 = jnp.min(jnp.where(le == v1, row8, SUBLANES), axis=0, keepdims=True)
    le2 = jnp.where(row8 == i1, neg, le)
    v2 = jnp.max(le2, axis=0, keepdims=True)
    i2 = jnp.min(jnp.where(le2 == v2, row8, SUBLANES), axis=0, keepdims=True)
    e21 = jnp.exp(v2 - v1)
    w_first = 1.0 / (1.0 + e21)
    gw0 = w_first * g_w
    gw1 = (e21 * w_first) * g_w
    eid0 = g_sel * per_group + i1
    eid1 = g_sel * per_group + i2

    rowl = lax.broadcasted_iota(I32, (LANES, n), 0)
    oh0 = rowl == eid0
    oh1 = rowl == eid1
    a_idx = lax.broadcasted_iota(I32, (n, n), 0)
    b_idx = lax.broadcasted_iota(I32, (n, n), 1)
    before = (a_idx < b_idx).astype(BF16)
    r0 = _dot(oh0.astype(BF16), before)
    r1 = _dot(oh1.astype(BF16), before)
    cnt0 = jnp.sum(oh0.astype(F32), axis=1, keepdims=True)
    lower = jnp.sum((rowl > eid0).astype(F32) + (rowl > eid1).astype(F32),
                    axis=1, keepdims=True)
    lpos0 = jnp.sum(jnp.where(oh0, lower + r0, 0.0), axis=0, keepdims=True)
    lpos1 = jnp.sum(jnp.where(oh1, lower + cnt0 + r1, 0.0), axis=0, keepdims=True)

    both = jnp.logical_or(oh0, oh1).astype(BF16)
    counts = lax.dot_general(jnp.ones((SUBLANES, n), BF16), both, (((1,), (1,)), ((), ())),
                             preferred_element_type=F32)
    return lpos0, lpos1, gw0, gw1, counts


def _mixer(x, g0, b0, w_in, b_in, conv_w, pool_w, pool_scale, w_o, g1, b1, w_r, b_r, *,
           apply_ln0, alpha, n_groups, per_group, tm, sort_rows):
    bsz, seq, d = x.shape
    nt = seq // tm
    hb = tm // HALO
    nhb = seq // HALO
    d_in = w_in.shape[1]
    d_conv = conv_w.shape[1]
    full = lambda *shape: pl.BlockSpec(shape, lambda b, i: (0,) * len(shape))
    body = functools.partial(_mixer_body, apply_ln0=apply_ln0, alpha=alpha, n_groups=n_groups,
                             per_group=per_group, seq=seq, sort_rows=sort_rows)
    nq = tm // sort_rows
    return pl.pallas_call(
        body,
        grid=(bsz, nt),
        in_specs=[
            pl.BlockSpec((1, tm, d), lambda b, i: (b, i, 0)),
            pl.BlockSpec((1, HALO, d), lambda b, i: (b, jnp.maximum(i * hb - 1, 0), 0)),
            pl.BlockSpec((1, HALO, d), lambda b, i: (b, jnp.minimum((i + 1) * hb, nhb - 1), 0)),
            full(1, d), full(1, d),
            full(d, d_in), full(1, d_in),
            full(CONV_WIDTH, d_conv),
            full(*pool_w.shape), full(1, pool_scale.shape[1]),
            full(d, d), full(1, d), full(1, d),
            full(d, 2 * LANES), full(1, LANES),
        ],
        out_specs=[
            pl.BlockSpec((1, tm, d), lambda b, i: (b, i, 0)),
            pl.BlockSpec((1, nq, SUBLANES, sort_rows), lambda b, i: (b, i, 0, 0)),
            pl.BlockSpec((1, tm, LANES), lambda b, i: (b, i, 0)),
            pl.BlockSpec((1, nq, SUBLANES, LANES), lambda b, i: (b, i, 0, 0)),
        ],
        out_shape=[
            jax.ShapeDtypeStruct((bsz, seq, d), F32),
            jax.ShapeDtypeStruct((bsz, nt * nq, SUBLANES, sort_rows), I32),
            jax.ShapeDtypeStruct((bsz, seq, LANES), F32),
            jax.ShapeDtypeStruct((bsz, nt * nq, SUBLANES, LANES), F32),
        ],
        compiler_params=pltpu.CompilerParams(
            dimension_semantics=("arbitrary", "arbitrary"), vmem_limit_bytes=VMEM_LIMIT),
        name="mixer",
    )(x, x, x, g0, b0, w_in, b_in, conv_w, pool_w, pool_scale, w_o, g1, b1, w_r, b_r)


def _tables_body(cnt_ref, gstart_ref, lstart_ref, len_ref, tot_ref):
    c = cnt_ref[:, 0, :]
    n_tiles = c.shape[0]
    row = lax.broadcasted_iota(I32, c.shape, 0)
    lane = lax.broadcasted_iota(I32, c.shape, 1)

    def prefix(v, idx, size, axis):
        k = 1
        while k < size:
            v = v + jnp.where(idx >= k, pltpu.roll(v, k, axis), 0.0)
            k *= 2
        return v

    over_tiles = prefix(c, row, n_tiles, 0)
    tot = jnp.broadcast_to(over_tiles[n_tiles - 1:n_tiles], c.shape)
    expert_start = prefix(tot, lane, LANES, 1) - tot
    gstart_ref[...] = (expert_start + over_tiles - c).astype(I32)
    lstart_ref[...] = (prefix(c, lane, LANES, 1) - c).astype(I32)
    len_ref[...] = c.astype(I32)
    tot_ref[...] = jnp.broadcast_to(over_tiles[n_tiles - 1:n_tiles], tot_ref.shape).astype(I32)


def _run_tables(cnt):
    table = jax.ShapeDtypeStruct((cnt.shape[0], LANES), I32)
    *tabs, tot = pl.pallas_call(
        _tables_body, out_shape=[table, table, table, jax.ShapeDtypeStruct((SUBLANES, LANES), I32)],
        name="run_tables")(cnt)
    return tuple(tabs), tot[0]


def _slots(ref, first, n, rps):
    start = first * rps
    if not isinstance(start, int):
        start = pl.multiple_of(start, rps)
    return ref.at[pl.ds(start, n * rps)]


def _start_run_copies(tabs, tile, n_experts, make_copy):
    gstart_ref, lstart_ref, len_ref = tabs

    def run_of(e):
        e = jnp.minimum(e, n_experts - 1)
        return len_ref[tile, e], lstart_ref[tile, e], gstart_ref[tile, e]

    def per_pair(i, runs):
        nxt = (run_of(2 * i + 2), run_of(2 * i + 3))
        for prio, (n, ls, gs) in enumerate(runs):

            @pl.when(n > 0)
            def _(n=n, ls=ls, gs=gs, prio=prio):
                make_copy(ls, gs, n).start(priority=prio)
        return nxt
    assert n_experts % 2 == 0
    lax.fori_loop(0, n_experts // 2, per_pair, (run_of(0), run_of(1)))


def _dispatch_body(gstart_ref, lstart_ref, len_ref, x1_ref, ri_ref, xs_ref, buf_ref, sem, *, n_experts):
    step = pl.program_id(0) * pl.num_programs(1) + pl.program_id(1)
    n_steps = pl.num_programs(0) * pl.num_programs(1)
    tabs = (gstart_ref, lstart_ref, len_ref)
    tm = ri_ref.shape[3]
    d = x1_ref.shape[2]
    rps = d // 2 // LANES

    def copier(which, q):
        def make_copy(local, glob, n):
            return pltpu.make_async_copy(_slots(buf_ref.at[which, q], local, n, rps),
                                         _slots(xs_ref, glob, n, rps), sem.at[which, q])
        return make_copy

    def wait_set(which):
        for q in range(STEP_TILES):
            copier(which, q)(0, 0, TOP_K * tm).wait()

    cur = lax.rem(step, 2)

    @pl.when(step >= 2)
    def _():
        wait_set(cur)

    for q in range(STEP_TILES):
        lpos = ri_ref[0, q]
        s_idx = lax.broadcasted_iota(I32, (TOP_K * tm, tm), 0)
        perm = jnp.logical_or(s_idx == lpos[0:1], s_idx == lpos[1:2]).astype(BF16)
        x1 = x1_ref[0, q * tm:(q + 1) * tm]
        _store_slot_rows(buf_ref.at[cur, q], _pack_exact_bf16_pairs(_dot(perm, x1.astype(BF16))))
    for q in range(STEP_TILES):
        _start_run_copies(tabs, step * STEP_TILES + q, n_experts, copier(cur, q))

    @pl.when(step == n_steps - 1)
    def _():
        wait_set(cur)

        @pl.when(step >= 1)
        def _():
            wait_set(1 - cur)


def _dispatch(tabs, x1, ri, *, tm, n_slots, n_experts):
    bsz, seq, d = x1.shape
    steps = seq // (tm * STEP_TILES)
    rps = d // 2 // LANES
    grid_spec = pltpu.PrefetchScalarGridSpec(
        num_scalar_prefetch=3,
        grid=(bsz, steps),
        in_specs=[
            pl.BlockSpec((1, STEP_TILES * tm, d), lambda b, i, *_: (b, i, 0)),
            pl.BlockSpec((1, STEP_TILES, SUBLANES, tm), lambda b, i, *_: (b, i, 0, 0)),
        ],
        out_specs=pl.BlockSpec(memory_space=pl.ANY),
        scratch_shapes=[pltpu.VMEM((2, STEP_TILES, TOP_K * tm * rps, LANES), U32),
                        pltpu.SemaphoreType.DMA((2, STEP_TILES))],
    )
    return pl.pallas_call(
        functools.partial(_dispatch_body, n_experts=n_experts),
        grid_spec=grid_spec,
        out_shape=jax.ShapeDtypeStruct((n_slots * rps, LANES), U32),
        compiler_params=pltpu.CompilerParams(
            dimension_semantics=("arbitrary", "arbitrary"), vmem_limit_bytes=VMEM_LIMIT),
        name="dispatch",
    )(*tabs, x1, ri)


def _experts_body(blk_ref, exp_ref, lo_ref, hi_ref, new_ref, par_ref, nxt_ref, xs_ref, w1_ref, w3_ref, w2_ref,
                  ys_ref, f1_ref, f3_ref, f2_ref, w1b_ref, w3b_ref, w2b_ref, keep_ref, sem, *, layer):
    it = pl.program_id(0)
    lo = lo_ref[it]
    rps = xs_ref.shape[0] // SLOT_BLOCK

    def weight_copies(expert, slot):
        return [pltpu.make_async_copy(w_ref.at[layer, expert], f_ref.at[slot], sem.at[slot, k])
                for k, (w_ref, f_ref) in enumerate(((w1_ref, f1_ref), (w3_ref, f3_ref), (w2_ref, f2_ref)))]

    @pl.when(it == 0)
    def _():
        for c in weight_copies(exp_ref[0], 0):
            c.start()

    @pl.when(new_ref[it] == 1)
    def _():
        slot = par_ref[it]
        for c in weight_copies(exp_ref[it], slot):
            c.wait()
        w1b_ref[...] = f1_ref[slot].astype(BF16)
        w3b_ref[...] = f3_ref[slot].astype(BF16)
        w2b_ref[...] = f2_ref[slot].astype(BF16)

        @pl.when(nxt_ref[it] >= 0)
        def _():
            for c in weight_copies(nxt_ref[it], 1 - slot):
                c.start()

    def ffn(first, n):
        xb = _unpack_bf16_pairs(_load_slot_rows(xs_ref, n, rps, first))
        h1 = _dot(xb, w1b_ref[...])
        h = (h1 * _sigmoid(h1)) * _dot(xb, w3b_ref[...])
        return _pack_bf16_pairs(_dot(h.astype(BF16), w2b_ref[...]))

    def run(first, n):
        m = min(n, half)
        cut = lo > first

        @pl.when(cut)
        def _():
            keep_ref[...] = _load_slot_rows(ys_ref, m, rps, first)

        _store_slot_rows(ys_ref, ffn(first, n), first)

        @pl.when(cut)
        def _():
            rows = first + lax.broadcasted_iota(I32, keep_ref.shape, 0)
            mine = _load_slot_rows(ys_ref, m, rps, first)
            _store_slot_rows(ys_ref, jnp.where(rows >= lo, mine, keep_ref[...]), first)

    half = SLOT_BLOCK // 2
    need_lo = lo < half
    need_hi = hi_ref[it] > half
    pl.when(jnp.logical_and(need_lo, need_hi))(lambda: run(0, SLOT_BLOCK))
    pl.when(jnp.logical_and(need_lo, jnp.logical_not(need_hi)))(lambda: run(0, half))
    pl.when(jnp.logical_and(jnp.logical_not(need_lo), need_hi))(lambda: run(half, half))


def _experts(items, xs, w1, w3, w2, *, layer):
    rows = xs.shape[0]
    _, _, d, d_e = w1.shape
    rps = d // 2 // LANES
    n_items = items[0].shape[0]
    grid_spec = pltpu.PrefetchScalarGridSpec(
        num_scalar_prefetch=len(items),
        grid=(n_items,),
        in_specs=[
            pl.BlockSpec((SLOT_BLOCK * rps, LANES), lambda i, blk, *_: (blk[i], 0)),
            pl.BlockSpec(memory_space=pl.ANY),
            pl.BlockSpec(memory_space=pl.ANY),
            pl.BlockSpec(memory_space=pl.ANY),
        ],
        out_specs=pl.BlockSpec((SLOT_BLOCK * rps, LANES), lambda i, blk, *_: (blk[i], 0)),
        scratch_shapes=[pltpu.VMEM((2, d, d_e), F32), pltpu.VMEM((2, d, d_e), F32),
                        pltpu.VMEM((2, d_e, d), F32),
                        pltpu.VMEM((d, d_e), BF16), pltpu.VMEM((d, d_e), BF16),
                        pltpu.VMEM((d_e, d), BF16),
                        pltpu.VMEM((SLOT_BLOCK // 2, d // 2), U32), pltpu.SemaphoreType.DMA((2, 3))],
    )
    return pl.pallas_call(
        functools.partial(_experts_body, layer=layer),
        grid_spec=grid_spec,
        out_shape=jax.ShapeDtypeStruct((rows, LANES), U32),
        compiler_params=pltpu.CompilerParams(
            dimension_semantics=("arbitrary",), vmem_limit_bytes=VMEM_LIMIT),
        name="experts",
    )(*items, xs, w1, w3, w2)


def _expert_items(totals, n_experts, n_slots):
    n_blocks = n_slots // SLOT_BLOCK
    n_items = n_blocks + n_experts - 1
    tot = totals[:n_experts]
    end = jnp.cumsum(tot)
    start = end - tot
    first_blk = start // SLOT_BLOCK
    n_it = jnp.where(tot > 0, (end - 1) // SLOT_BLOCK - first_blk + 1, 0)
    it_end = jnp.cumsum(n_it)
    it_start = it_end - n_it
    idx = jnp.arange(n_items, dtype=I32)
    last_live = it_end[-1] - 1
    live = idx <= last_live
    at = jnp.minimum(idx, last_live)
    expert = jnp.sum(it_end[None, :] <= at[:, None], axis=1).astype(I32)
    onehot = expert[:, None] == jnp.arange(n_experts, dtype=I32)[None, :]
    pick = lambda v: jnp.sum(jnp.where(onehot, v[None, :], 0), axis=1)
    block = pick(first_blk) + idx - pick(it_start)
    item_block = jnp.where(live, block, n_blocks - 1).astype(I32)
    item_lo = jnp.where(live, jnp.maximum(pick(start) - block * SLOT_BLOCK, 0), SLOT_BLOCK).astype(I32)
    item_hi = jnp.where(live, jnp.minimum(pick(end) - block * SLOT_BLOCK, SLOT_BLOCK), 0).astype(I32)
    prev_expert = jnp.concatenate([jnp.full((1,), -1, I32), expert[:-1]])
    is_new = expert != prev_expert
    slot = (jnp.cumsum(is_new.astype(I32)) - 1) % 2
    later_new = lax.cummin(jnp.where(is_new, idx, n_items)[::-1], axis=0)[::-1]
    next_new = jnp.concatenate([later_new[1:], jnp.full((1,), n_items, I32)])
    next_expert = jnp.where(next_new < n_items, expert[jnp.minimum(next_new, n_items - 1)], -1)
    return (item_block, expert, item_lo, item_hi, is_new.astype(I32), slot.astype(I32),
            next_expert.astype(I32))


def _combine_body(gstart_ref, lstart_ref, len_ref, x1_ref, p_ref, wcol_ref, ys_ref, wg_ref, bg_ref, wp_ref,
                  g2_ref, b2_ref, out_ref, buf_ref, sem, *, alpha, n_experts, tm):
    step = pl.program_id(0) * pl.num_programs(1) + pl.program_id(1)
    tabs = (gstart_ref, lstart_ref, len_ref)
    d = x1_ref.shape[2]
    rps = d // 2 // LANES
    n = tm // COMBINE_CHAINS

    def copier(which, q):
        def make_copy(local, glob, n_slots):
            return pltpu.make_async_copy(_slots(ys_ref, glob, n_slots, rps),
                                         _slots(buf_ref.at[which, q], local, n_slots, rps), sem.at[which, q])
        return make_copy

    n_steps = pl.num_programs(0) * pl.num_programs(1)
    cur = lax.rem(step, 2)

    @pl.when(step == 0)
    def _():
        for q in range(STEP_TILES):
            _start_run_copies(tabs, q, n_experts, copier(cur, q))

    @pl.when(step + 1 < n_steps)
    def _():
        for q in range(STEP_TILES):
            _start_run_copies(tabs, (step + 1) * STEP_TILES + q, n_experts, copier(1 - cur, q))
    for q in range(STEP_TILES):
        copier(cur, q)(0, 0, TOP_K * tm).wait()

    ys = [_unpack_bf16_pairs(_load_slot_rows(buf_ref.at[cur, q], TOP_K * tm, rps)) for q in range(STEP_TILES)]

    def rows_of(c):
        return slice(c * n, (c + 1) * n)

    def ple_matmuls(c):
        x1 = x1_ref[0, rows_of(c)]
        gate_logits = _dot(x1.astype(BF16), wg_ref[...]) + bg_ref[...]
        return x1, gate_logits, _dot(p_ref[0, rows_of(c)].astype(BF16), wp_ref[...])

    def finish(c, x1, gate_logits, proj):
        wcol = wcol_ref[0, rows_of(c)]
        s_idx = lax.broadcasted_iota(I32, (n, TOP_K * tm), 1)
        select = jnp.zeros((n, TOP_K * tm), F32)
        for kk in range(TOP_K):
            at_kk = s_idx == wcol[:, TOP_K + kk:TOP_K + kk + 1].astype(I32)
            select = jnp.where(at_kk, wcol[:, kk:kk + 1], select)
        ffn = _dot(select.astype(BF16), ys[c // COMBINE_CHAINS])
        ple = _sigmoid(gate_logits) * proj
        out_ref[0, rows_of(c)] = _layernorm(alpha * x1 + ffn + ple, g2_ref[...], b2_ref[...])

    n_chains = STEP_TILES * COMBINE_CHAINS
    ahead = ple_matmuls(0)
    for c in range(n_chains):
        now, ahead = ahead, (ple_matmuls(c + 1) if c + 1 < n_chains else None)
        finish(c, *now)


def _combine(tabs, x1, p, wcol, ys, w_gate, b_gate, w_proj, g2, b2, *, layer, alpha, tm, n_experts):
    bsz, seq, d = x1.shape
    rows = STEP_TILES * tm
    steps = seq // rows
    d_ple = p.shape[-1]
    rps = d // 2 // LANES
    full = lambda *shape: pl.BlockSpec(shape, lambda b, i, *_: (0,) * len(shape))
    grid_spec = pltpu.PrefetchScalarGridSpec(
        num_scalar_prefetch=3,
        grid=(bsz, steps),
        in_specs=[
            pl.BlockSpec((1, rows, d), lambda b, i, *_: (b, i, 0)),
            pl.BlockSpec((None, 1, rows, d_ple), lambda b, i, *_: (layer, b, i, 0)),
            pl.BlockSpec((1, rows, LANES), lambda b, i, *_: (b, i, 0)),
            pl.BlockSpec(memory_space=pl.ANY),
            full(d, d), full(1, d), full(d_ple, d),
            full(1, d), full(1, d),
        ],
        out_specs=pl.BlockSpec((1, rows, d), lambda b, i, *_: (b, i, 0)),
        scratch_shapes=[pltpu.VMEM((2, STEP_TILES, TOP_K * tm * rps, LANES), U32),
                        pltpu.SemaphoreType.DMA((2, STEP_TILES))],
    )
    return pl.pallas_call(
        functools.partial(_combine_body, alpha=alpha, n_experts=n_experts, tm=tm),
        grid_spec=grid_spec,
        out_shape=jax.ShapeDtypeStruct((bsz, seq, d), F32),
        compiler_params=pltpu.CompilerParams(
            dimension_semantics=("arbitrary", "arbitrary"), vmem_limit_bytes=VMEM_LIMIT),
        name="combine",
    )(*tabs, x1, p, wcol, ys, w_gate, b_gate, w_proj, g2, b2)


def _router_weights(w_rg, b_rg, w_re, b_re):
    d, n_groups = w_rg.shape
    per_group = w_re.shape[2]
    assert n_groups <= SUBLANES and per_group <= SUBLANES and SUBLANES * (n_groups + 1) <= LANES
    w_e = jnp.pad(jnp.transpose(w_re, (1, 0, 2)), ((0, 0), (0, 0), (0, SUBLANES - per_group)))
    b_e = jnp.pad(b_re, ((0, 0), (0, SUBLANES - per_group)))
    tail = LANES - SUBLANES * (n_groups + 1)
    w = jnp.concatenate([jnp.pad(w_rg, ((0, 0), (0, SUBLANES - n_groups))), w_e.reshape(d, SUBLANES * n_groups),
                         jnp.zeros((d, tail), F32)], axis=1)
    b = jnp.concatenate([jnp.pad(b_rg, (0, SUBLANES - n_groups)), b_e.reshape(-1), jnp.zeros((tail,), F32)])
    w_hi = w.astype(BF16)
    w_lo = (w - w_hi.astype(F32)).astype(BF16)
    return jnp.concatenate([w_hi, w_lo], axis=1), b[None, :]


def kernel(x, p, ln0_g, ln0_b, w_in, b_in, conv_w, pool_w, pool_scale, w_o, ln1_g, ln1_b,
           w_router_group, b_router_group, w_router_expert, b_router_expert, w1, w3, w2,
           w_ple_gate, b_ple_gate, w_ple_proj, ln2_g, ln2_b):
    bsz, seq, d = x.shape
    depth = w_in.shape[0]
    n_groups, per_group = w_router_expert.shape[1], w_router_expert.shape[3]
    n_experts = n_groups * per_group
    alpha = (2 * depth) ** 0.25
    tm = min(SORT_ROWS, seq)
    tm_mix = min(MIX_ROWS, seq)
    n_slots = bsz * seq * TOP_K
    assert seq % tm_mix == 0 and tm_mix % tm == 0 and tm % LANES == 0 and seq % (STEP_TILES * tm) == 0
    assert n_slots % SLOT_BLOCK == 0 and n_experts <= LANES
    assert d % (2 * LANES) == 0
    row = lambda a: a[None, :]

    for i in range(depth):
        w_r, b_r = _router_weights(w_router_group[i], b_router_group[i],
                                   w_router_expert[i], b_router_expert[i])
        x1, ri, wcol, cnt = _mixer(
            x, row(ln0_g), row(ln0_b), w_in[i].astype(BF16), row(b_in[i]), conv_w[i],
            pool_w[i].astype(BF16), row(pool_scale[i]), w_o[i].astype(BF16),
            row(ln1_g[i]), row(ln1_b[i]), w_r, b_r,
            apply_ln0=(i == 0), alpha=alpha, n_groups=n_groups, per_group=per_group, tm=tm_mix,
            sort_rows=tm)
        tabs, totals = _run_tables(cnt.reshape(-1, SUBLANES, LANES))
        items = _expert_items(totals, n_experts, n_slots)
        xs = _dispatch(tabs, x1, ri, tm=tm, n_slots=n_slots, n_experts=n_experts)
        ys = _experts(items, xs, w1, w3, w2, layer=i)
        x = _combine(tabs, x1, p, wcol, ys, w_ple_gate[i].astype(BF16), row(b_ple_gate[i]),
                     w_ple_proj[i].astype(BF16), row(ln2_g[i]), row(ln2_b[i]),
                     layer=i, alpha=alpha, tm=tm, n_experts=n_experts)
    return x
```

```python
import functools

import jax
import jax.numpy as jnp
from jax import lax
from jax.experimental import pallas as pl
from jax.experimental.pallas import tpu as pltpu

POOL_WINDOWS = (2, 4, 8, 16)
CONV_WIDTH = 3
TOP_K = 2
LN_EPS = 1e-5
HALO = 8
LANES = 128
SUBLANES = 8
SORT_ROWS = 512
MIX_ROWS = 1024
MIX_CHAINS = 2
COMBINE_CHAINS = 2
STEP_TILES = 2
SLOT_BLOCK = 512
VMEM_LIMIT = 56 * 1024 * 1024

F32 = jnp.float32
BF16 = jnp.bfloat16
I32 = jnp.int32
U32 = jnp.uint32


def _layernorm(v, g, b):
    mu = jnp.mean(v, axis=-1, keepdims=True)
    c = v - mu
    var = jnp.mean(c * c, axis=-1, keepdims=True)
    return c * lax.rsqrt(var + LN_EPS) * g + b


def _sigmoid(v):
    return 0.5 * jnp.tanh(0.5 * v) + 0.5


def _dot(a, b):
    return jnp.dot(a, b, preferred_element_type=F32)


def _pair_groups(width):
    assert width % (2 * LANES) == 0
    return range(0, width, 2 * LANES)


def _pack_bf16_pairs(v):
    words = []
    for c0 in _pair_groups(v.shape[1]):
        hi = pltpu.bitcast(v[:, c0:c0 + LANES].astype(BF16).astype(F32), U32)
        lo = pltpu.bitcast(v[:, c0 + LANES:c0 + 2 * LANES].astype(BF16).astype(F32), U32)
        words.append(hi | (lo >> 16))
    return jnp.concatenate(words, axis=1)


def _pack_exact_bf16_pairs(v):
    words = [pltpu.bitcast(v[:, c0:c0 + LANES], U32) | (pltpu.bitcast(v[:, c0 + LANES:c0 + 2 * LANES], U32) >> 16)
             for c0 in _pair_groups(v.shape[1])]
    return jnp.concatenate(words, axis=1)


def _unpack_bf16_pairs(u):
    cols = []
    for c0 in range(0, u.shape[1], LANES):
        w = u[:, c0:c0 + LANES]
        cols.append(pltpu.bitcast(w & jnp.uint32(0xFFFF0000), F32).astype(BF16))
        cols.append(pltpu.bitcast(w << 16, F32).astype(BF16))
    return jnp.concatenate(cols, axis=1)


def _store_slot_rows(ref, packed, first=0):
    n, c = packed.shape
    rps = c // LANES
    for k in range(rps):
        ref[pl.ds(first * rps + k, n, stride=rps), :] = packed[:, k * LANES:(k + 1) * LANES]


def _load_slot_rows(ref, n, rps, first=0):
    return jnp.concatenate([ref[pl.ds(first * rps + k, n, stride=rps), :] for k in range(rps)], axis=1)


def _mixer_body(x_ref, xp_ref, xn_ref, g0_ref, b0_ref, win_ref, bin_ref, cw_ref, pw_ref, ps_ref,
                wo_ref, g1_ref, b1_ref, wr_ref, br_ref,
                x1_ref, ri_ref, wcol_ref, cnt_ref, *, apply_ln0, alpha, n_groups, per_group, seq,
                sort_rows):
    i = pl.program_id(1)
    nt = pl.num_programs(1)
    tm = x_ref.shape[1]
    n = tm // MIX_CHAINS
    rows = n + 2 * HALO
    d_conv = cw_ref.shape[1]
    dg = pw_ref.shape[1]

    xe = jnp.concatenate([xp_ref[0], x_ref[0], xn_ref[0]], axis=0)
    if apply_ln0:
        xe = _layernorm(xe, g0_ref[...], b0_ref[...])
    cw = cw_ref[...]
    ps = ps_ref[...]

    def in_proj(s):
        xs = xe[s * n:s * n + rows]
        return xs, _dot(xs.astype(BF16), win_ref[...]) + bin_ref[...]

    def token_mix(s, z):
        r = lax.broadcasted_iota(I32, (rows, 1), 0)
        lo_ok = jnp.logical_or(r >= HALO, i > 0) if s == 0 else True
        hi_ok = jnp.logical_or(r < n + HALO, i < nt - 1) if s == MIX_CHAINS - 1 else True
        if s == 0 or s == MIX_CHAINS - 1:
            valid = jnp.logical_and(lo_ok, hi_ok)
            mask = lambda a: jnp.where(valid, a, 0.0)
        else:
            mask = lambda a: a

        h = z[:, :d_conv]
        gate_b = z[HALO:HALO + n, d_conv:2 * d_conv]
        gate_c = z[:, 2 * d_conv:3 * d_conv]
        v = mask(gate_c * h)
        conv = (cw[0:1] * pltpu.roll(v, 1, 0)[HALO:HALO + n]
                + cw[1:2] * v[HALO:HALO + n]
                + cw[2:3] * pltpu.roll(v, rows - 1, 0)[HALO:HALO + n])
        parts = [gate_b * conv]

        pos = i * tm + s * n + r[HALO:HALO + n] - HALO
        for g, w in enumerate(POOL_WINDOWS):
            left = w // 2
            right = w - 1 - left
            c0 = 3 * d_conv + g * dg
            ug = mask(z[:, c0:c0 + dg])
            acc = ug
            k = 1
            while k < w:
                acc = acc + pltpu.roll(acc, k, 0)
                k *= 2
            if right:
                acc = pltpu.roll(acc, rows - right, 0)

            def edge(lo):
                p = pos[lo:lo + HALO]
                cnt = (jnp.minimum(p + right + 1, seq) - jnp.maximum(p - left, 0)).astype(F32)
                return acc[HALO + lo:2 * HALO + lo] / cnt - ug[HALO + lo:2 * HALO + lo]
            inner = acc[2 * HALO:n] * (1.0 / w) - ug[2 * HALO:n]
            pooled = jnp.concatenate([edge(0), inner, edge(n - HALO)], axis=0)
            parts.append(_dot(pooled.astype(BF16), pw_ref[g]) * ps[:, g * dg:(g + 1) * dg])
        return jnp.concatenate(parts, axis=1).astype(BF16)

    def out_proj(xs, cat):
        mix = _dot(cat, wo_ref[...])
        return _layernorm(alpha * xs[HALO:HALO + n] + mix, g1_ref[...], b1_ref[...])

    def route(x1s):
        xh = x1s.astype(BF16)
        xl = (x1s - xh.astype(F32)).astype(BF16)
        l_hi = _dot(xh, wr_ref[...])
        l_lo = _dot(xl, wr_ref[:, :LANES])
        return l_hi[:, :LANES] + l_hi[:, LANES:] + l_lo + br_ref[...]

    chains = [in_proj(s) for s in range(MIX_CHAINS)]
    x1s = [out_proj(xs, token_mix(s, z)) for s, (xs, z) in enumerate(chains)]
    x1 = jnp.concatenate(x1s, axis=0)
    x1_ref[0] = x1
    logits = jnp.concatenate([route(v) for v in x1s], axis=0)
    ri_ref[0] = jnp.zeros(ri_ref.shape[1:], I32)
    for q in range(tm // sort_rows):
        rows_q = slice(q * sort_rows, (q + 1) * sort_rows)
        lpos0, lpos1, gw0, gw1, counts = _route_and_sort(logits[rows_q], n_groups, per_group)
        ri_ref[0, q, 0:1, :] = lpos0.astype(I32)
        ri_ref[0, q, 1:2, :] = lpos1.astype(I32)
        rowl = lax.broadcasted_iota(I32, (LANES, sort_rows), 0)
        wrows = jnp.where(rowl == 0, gw0, jnp.where(rowl == 1, gw1,
                          jnp.where(rowl == 2, lpos0, jnp.where(rowl == 3, lpos1, 0.0))))
        wcol_ref[0, rows_q, :] = wrows.T
        cnt_ref[0, q] = counts


def _route_and_sort(logits, n_groups, per_group):
    n = logits.shape[0]
    lt = logits.T
    row8 = lax.broadcasted_iota(I32, (SUBLANES, n), 0)
    neg = jnp.float32(-jnp.inf)
    lg = jnp.where(row8 < n_groups, lt[0:SUBLANES], neg)
    mg = jnp.max(lg, axis=0, keepdims=True)
    g_w = 1.0 / jnp.sum(jnp.exp(lg - mg), axis=0, keepdims=True)
    g_sel = jnp.min(jnp.where(lg == mg, row8, SUBLANES), axis=0, keepdims=True)
    le = lt[SUBLANES:2 * SUBLANES]
    for g in range(1, n_groups):
        le = jnp.where(g_sel == g, lt[SUBLANES * (g + 1):SUBLANES * (g + 2)], le)
    le = jnp.where(row8 < per_group, le, neg)
    v1 = jnp.max(le, axis=0, keepdims=True)
    i1 = jnp.min(jnp.where(le == v1, row8, SUBLANES), axis=0, keepdims=True)
    le2 = jnp.where(row8 == i1, neg, le)
    v2 = jnp.max(le2, axis=0, keepdims=True)
    i2 = jnp.min(jnp.where(le2 == v2, row8, SUBLANES), axis=0, keepdims=True)
    e21 = jnp.exp(v2 - v1)
    w_first = 1.0 / (1.0 + e21)
    gw0 = w_first * g_w
    gw1 = (e21 * w_first) * g_w
    eid0 = g_sel * per_group + i1
    eid1 = g_sel * per_group + i2

    rowl = lax.broadcasted_iota(I32, (LANES, n), 0)
    oh0 = rowl == eid0
    oh1 = rowl == eid1
    a_idx = lax.broadcasted_iota(I32, (n, n), 0)
    b_idx = lax.broadcasted_iota(I32, (n, n), 1)
    before = (a_idx < b_idx).astype(BF16)
    r0 = _dot(oh0.astype(BF16), before)
    r1 = _dot(oh1.astype(BF16), before)
    cnt0 = jnp.sum(oh0.astype(F32), axis=1, keepdims=True)
    lower = jnp.sum((rowl > eid0).astype(F32) + (rowl > eid1).astype(F32),
                    axis=1, keepdims=True)
    lpos0 = jnp.sum(jnp.where(oh0, lower + r0, 0.0), axis=0, keepdims=True)
    lpos1 = jnp.sum(jnp.where(oh1, lower + cnt0 + r1, 0.0), axis=0, keepdims=True)

    both = jnp.logical_or(oh0, oh1).astype(BF16)
    counts = lax.dot_general(jnp.ones((SUBLANES, n), BF16), both, (((1,), (1,)), ((), ())),
                             preferred_element_type=F32)
    return lpos0, lpos1, gw0, gw1, counts


def _mixer(x, g0, b0, w_in, b_in, conv_w, pool_w, pool_scale, w_o, g1, b1, w_r, b_r, *,
           apply_ln0, alpha, n_groups, per_group, tm, sort_rows):
    bsz, seq, d = x.shape
    nt = seq // tm
    hb = tm // HALO
    nhb = seq // HALO
    d_in = w_in.shape[1]
    d_conv = conv_w.shape[1]
    full = lambda *shape: pl.BlockSpec(shape, lambda b, i: (0,) * len(shape))
    body = functools.partial(_mixer_body, apply_ln0=apply_ln0, alpha=alpha, n_groups=n_groups,
                             per_group=per_group, seq=seq, sort_rows=sort_rows)
    nq = tm // sort_rows
    return pl.pallas_call(
        body,
        grid=(bsz, nt),
        in_specs=[
            pl.BlockSpec((1, tm, d), lambda b, i: (b, i, 0)),
            pl.BlockSpec((1, HALO, d), lambda b, i: (b, jnp.maximum(i * hb - 1, 0), 0)),
            pl.BlockSpec((1, HALO, d), lambda b, i: (b, jnp.minimum((i + 1) * hb, nhb - 1), 0)),
            full(1, d), full(1, d),
            full(d, d_in), full(1, d_in),
            full(CONV_WIDTH, d_conv),
            full(*pool_w.shape), full(1, pool_scale.shape[1]),
            full(d, d), full(1, d), full(1, d),
            full(d, 2 * LANES), full(1, LANES),
        ],
        out_specs=[
            pl.BlockSpec((1, tm, d), lambda b, i: (b, i, 0)),
            pl.BlockSpec((1, nq, SUBLANES, sort_rows), lambda b, i: (b, i, 0, 0)),
            pl.BlockSpec((1, tm, LANES), lambda b, i: (b, i, 0)),
            pl.BlockSpec((1, nq, SUBLANES, LANES), lambda b, i: (b, i, 0, 0)),
        ],
        out_shape=[
            jax.ShapeDtypeStruct((bsz, seq, d), F32),
            jax.ShapeDtypeStruct((bsz, nt * nq, SUBLANES, sort_rows), I32),
            jax.ShapeDtypeStruct((bsz, seq, LANES), F32),
            jax.ShapeDtypeStruct((bsz, nt * nq, SUBLANES, LANES), F32),
        ],
        compiler_params=pltpu.CompilerParams(
            dimension_semantics=("arbitrary", "arbitrary"), vmem_limit_bytes=VMEM_LIMIT),
        name="mixer",
    )(x, x, x, g0, b0, w_in, b_in, conv_w, pool_w, pool_scale, w_o, g1, b1, w_r, b_r)


def _tables_body(cnt_ref, gstart_ref, lstart_ref, len_ref, tot_ref):
    c = cnt_ref[:, 0, :]
    n_tiles = c.shape[0]
    row = lax.broadcasted_iota(I32, c.shape, 0)
    lane = lax.broadcasted_iota(I32, c.shape, 1)

    def prefix(v, idx, size, axis):
        k = 1
        while k < size:
            v = v + jnp.where(idx >= k, pltpu.roll(v, k, axis), 0.0)
            k *= 2
        return v

    over_tiles = prefix(c, row, n_tiles, 0)
    tot = jnp.broadcast_to(over_tiles[n_tiles - 1:n_tiles], c.shape)
    expert_start = prefix(tot, lane, LANES, 1) - tot
    gstart_ref[...] = (expert_start + over_tiles - c).astype(I32)
    lstart_ref[...] = (prefix(c, lane, LANES, 1) - c).astype(I32)
    len_ref[...] = c.astype(I32)
    tot_ref[...] = jnp.broadcast_to(over_tiles[n_tiles - 1:n_tiles], tot_ref.shape).astype(I32)


def _run_tables(cnt):
    table = jax.ShapeDtypeStruct((cnt.shape[0], LANES), I32)
    *tabs, tot = pl.pallas_call(
        _tables_body, out_shape=[table, table, table, jax.ShapeDtypeStruct((SUBLANES, LANES), I32)],
        name="run_tables")(cnt)
    return tuple(tabs), tot[0]


def _slots(ref, first, n, rps):
    start = first * rps
    if not isinstance(start, int):
        start = pl.multiple_of(start, rps)
    return ref.at[pl.ds(start, n * rps)]


def _start_run_copies(tabs, tile, n_experts, make_copy):
    gstart_ref, lstart_ref, len_ref = tabs

    def run_of(e):
        e = jnp.minimum(e, n_experts - 1)
        return len_ref[tile, e], lstart_ref[tile, e], gstart_ref[tile, e]

    def per_pair(i, runs):
        nxt = (run_of(2 * i + 2), run_of(2 * i + 3))
        for prio, (n, ls, gs) in enumerate(runs):

            @pl.when(n > 0)
            def _(n=n, ls=ls, gs=gs, prio=prio):
                make_copy(ls, gs, n).start(priority=prio)
        return nxt
    assert n_experts % 2 == 0
    lax.fori_loop(0, n_experts // 2, per_pair, (run_of(0), run_of(1)))


def _dispatch_body(gstart_ref, lstart_ref, len_ref, x1_ref, ri_ref, xs_ref, buf_ref, sem, *, n_experts):
    step = pl.program_id(0) * pl.num_programs(1) + pl.program_id(1)
    n_steps = pl.num_programs(0) * pl.num_programs(1)
    tabs = (gstart_ref, lstart_ref, len_ref)
    tm = ri_ref.shape[3]
    d = x1_ref.shape[2]
    rps = d // 2 // LANES

    def copier(which, q):
        def make_copy(local, glob, n):
            return pltpu.make_async_copy(_slots(buf_ref.at[which, q], local, n, rps),
                                         _slots(xs_ref, glob, n, rps), sem.at[which, q])
        return make_copy

    def wait_set(which):
        for q in range(STEP_TILES):
            copier(which, q)(0, 0, TOP_K * tm).wait()

    cur = lax.rem(step, 2)

    @pl.when(step >= 2)
    def _():
        wait_set(cur)

    for q in range(STEP_TILES):
        lpos = ri_ref[0, q]
        s_idx = lax.broadcasted_iota(I32, (TOP_K * tm, tm), 0)
        perm = jnp.logical_or(s_idx == lpos[0:1], s_idx == lpos[1:2]).astype(BF16)
        x1 = x1_ref[0, q * tm:(q + 1) * tm]
        _store_slot_rows(buf_ref.at[cur, q], _pack_exact_bf16_pairs(_dot(perm, x1.astype(BF16))))
    for q in range(STEP_TILES):
        _start_run_copies(tabs, step * STEP_TILES + q, n_experts, copier(cur, q))

    @pl.when(step == n_steps - 1)
    def _():
        wait_set(cur)

        @pl.when(step >= 1)
        def _():
            wait_set(1 - cur)


def _dispatch(tabs, x1, ri, *, tm, n_slots, n_experts):
    bsz, seq, d = x1.shape
    steps = seq // (tm * STEP_TILES)
    rps = d // 2 // LANES
    grid_spec = pltpu.PrefetchScalarGridSpec(
        num_scalar_prefetch=3,
        grid=(bsz, steps),
        in_specs=[
            pl.BlockSpec((1, STEP_TILES * tm, d), lambda b, i, *_: (b, i, 0)),
            pl.BlockSpec((1, STEP_TILES, SUBLANES, tm), lambda b, i, *_: (b, i, 0, 0)),
        ],
        out_specs=pl.BlockSpec(memory_space=pl.ANY),
        scratch_shapes=[pltpu.VMEM((2, STEP_TILES, TOP_K * tm * rps, LANES), U32),
                        pltpu.SemaphoreType.DMA((2, STEP_TILES))],
    )
    return pl.pallas_call(
        functools.partial(_dispatch_body, n_experts=n_experts),
        grid_spec=grid_spec,
        out_shape=jax.ShapeDtypeStruct((n_slots * rps, LANES), U32),
        compiler_params=pltpu.CompilerParams(
            dimension_semantics=("arbitrary", "arbitrary"), vmem_limit_bytes=VMEM_LIMIT),
        name="dispatch",
    )(*tabs, x1, ri)


def _experts_body(blk_ref, exp_ref, lo_ref, hi_ref, new_ref, par_ref, nxt_ref, xs_ref, w1_ref, w3_ref, w2_ref,
                  ys_ref, f1_ref, f3_ref, f2_ref, w1b_ref, w3b_ref, w2b_ref, keep_ref, sem, *, layer):
    it = pl.program_id(0)
    lo = lo_ref[it]
    rps = xs_ref.shape[0] // SLOT_BLOCK

    def weight_copies(expert, slot):
        return [pltpu.make_async_copy(w_ref.at[layer, expert], f_ref.at[slot], sem.at[slot, k])
                for k, (w_ref, f_ref) in enumerate(((w1_ref, f1_ref), (w3_ref, f3_ref), (w2_ref, f2_ref)))]

    @pl.when(it == 0)
    def _():
        for c in weight_copies(exp_ref[0], 0):
            c.start(priority=1)

    @pl.when(new_ref[it] == 1)
    def _():
        slot = par_ref[it]
        for c in weight_copies(exp_ref[it], slot):
            c.wait()
        w1b_ref[...] = f1_ref[slot].astype(BF16)
        w3b_ref[...] = f3_ref[slot].astype(BF16)
        w2b_ref[...] = f2_ref[slot].astype(BF16)

        @pl.when(nxt_ref[it] >= 0)
        def _():
            for c in weight_copies(nxt_ref[it], 1 - slot):
                c.start(priority=1)

    def ffn(first, n):
        xb = _unpack_bf16_pairs(_load_slot_rows(xs_ref, n, rps, first))
        h1 = _dot(xb, w1b_ref[...])
        h = (h1 * _sigmoid(h1)) * _dot(xb, w3b_ref[...])
        return _pack_bf16_pairs(_dot(h.astype(BF16), w2b_ref[...]))

    def run(first, n):
        m = min(n, half)
        cut = lo > first

        @pl.when(cut)
        def _():
            keep_ref[...] = _load_slot_rows(ys_ref, m, rps, first)

        _store_slot_rows(ys_ref, ffn(first, n), first)

        @pl.when(cut)
        def _():
            rows = first + lax.broadcasted_iota(I32, keep_ref.shape, 0)
            mine = _load_slot_rows(ys_ref, m, rps, first)
            _store_slot_rows(ys_ref, jnp.where(rows >= lo, mine, keep_ref[...]), first)

    half = SLOT_BLOCK // 2
    need_lo = lo < half
    need_hi = hi_ref[it] > half
    pl.when(jnp.logical_and(need_lo, need_hi))(lambda: run(0, SLOT_BLOCK))
    pl.when(jnp.logical_and(need_lo, jnp.logical_not(need_hi)))(lambda: run(0, half))
    pl.when(jnp.logical_and(jnp.logical_not(need_lo), need_hi))(lambda: run(half, half))


def _experts(items, xs, w1, w3, w2, *, layer):
    rows = xs.shape[0]
    _, _, d, d_e = w1.shape
    rps = d // 2 // LANES
    n_items = items[0].shape[0]
    grid_spec = pltpu.PrefetchScalarGridSpec(
        num_scalar_prefetch=len(items),
        grid=(n_items,),
        in_specs=[
            pl.BlockSpec((SLOT_BLOCK * rps, LANES), lambda i, blk, *_: (blk[i], 0)),
            pl.BlockSpec(memory_space=pl.ANY),
            pl.BlockSpec(memory_space=pl.ANY),
            pl.BlockSpec(memory_space=pl.ANY),
        ],
        out_specs=pl.BlockSpec((SLOT_BLOCK * rps, LANES), lambda i, blk, *_: (blk[i], 0)),
        scratch_shapes=[pltpu.VMEM((2, d, d_e), F32), pltpu.VMEM((2, d, d_e), F32),
                        pltpu.VMEM((2, d_e, d), F32),
                        pltpu.VMEM((d, d_e), BF16), pltpu.VMEM((d, d_e), BF16),
                        pltpu.VMEM((d_e, d), BF16),
                        pltpu.VMEM((SLOT_BLOCK // 2, d // 2), U32), pltpu.SemaphoreType.DMA((2, 3))],
    )
    return pl.pallas_call(
        functools.partial(_experts_body, layer=layer),
        grid_spec=grid_spec,
        out_shape=jax.ShapeDtypeStruct((rows, LANES), U32),
        compiler_params=pltpu.CompilerParams(
            dimension_semantics=("arbitrary",), vmem_limit_bytes=VMEM_LIMIT),
        name="experts",
    )(*items, xs, w1, w3, w2)


def _expert_items(totals, n_experts, n_slots):
    n_blocks = n_slots // SLOT_BLOCK
    n_items = n_blocks + n_experts - 1
    tot = totals[:n_experts]
    end = jnp.cumsum(tot)
    start = end - tot
    first_blk = start // SLOT_BLOCK
    n_it = jnp.where(tot > 0, (end - 1) // SLOT_BLOCK - first_blk + 1, 0)
    it_end = jnp.cumsum(n_it)
    it_start = it_end - n_it
    idx = jnp.arange(n_items, dtype=I32)
    last_live = it_end[-1] - 1
    live = idx <= last_live
    at = jnp.minimum(idx, last_live)
    expert = jnp.sum(it_end[None, :] <= at[:, None], axis=1).astype(I32)
    onehot = expert[:, None] == jnp.arange(n_experts, dtype=I32)[None, :]
    pick = lambda v: jnp.sum(jnp.where(onehot, v[None, :], 0), axis=1)
    block = pick(first_blk) + idx - pick(it_start)
    item_block = jnp.where(live, block, n_blocks - 1).astype(I32)
    item_lo = jnp.where(live, jnp.maximum(pick(start) - block * SLOT_BLOCK, 0), SLOT_BLOCK).astype(I32)
    item_hi = jnp.where(live, jnp.minimum(pick(end) - block * SLOT_BLOCK, SLOT_BLOCK), 0).astype(I32)
    prev_expert = jnp.concatenate([jnp.full((1,), -1, I32), expert[:-1]])
    is_new = expert != prev_expert
    slot = (jnp.cumsum(is_new.astype(I32)) - 1) % 2
    later_new = lax.cummin(jnp.where(is_new, idx, n_items)[::-1], axis=0)[::-1]
    next_new = jnp.concatenate([later_new[1:], jnp.full((1,), n_items, I32)])
    next_expert = jnp.where(next_new < n_items, expert[jnp.minimum(next_new, n_items - 1)], -1)
    return (item_block, expert, item_lo, item_hi, is_new.astype(I32), slot.astype(I32),
            next_expert.astype(I32))


def _combine_body(gstart_ref, lstart_ref, len_ref, x1_ref, p_ref, wcol_ref, ys_ref, wg_ref, bg_ref, wp_ref,
                  g2_ref, b2_ref, out_ref, buf_ref, sem, *, alpha, n_experts, tm):
    step = pl.program_id(0) * pl.num_programs(1) + pl.program_id(1)
    tabs = (gstart_ref, lstart_ref, len_ref)
    d = x1_ref.shape[2]
    rps = d // 2 // LANES
    n = tm // COMBINE_CHAINS

    def copier(which, q):
        def make_copy(local, glob, n_slots):
            return pltpu.make_async_copy(_slots(ys_ref, glob, n_slots, rps),
                                         _slots(buf_ref.at[which, q], local, n_slots, rps), sem.at[which, q])
        return make_copy

    n_steps = pl.num_programs(0) * pl.num_programs(1)
    cur = lax.rem(step, 2)

    @pl.when(step == 0)
    def _():
        for q in range(STEP_TILES):
            _start_run_copies(tabs, q, n_experts, copier(cur, q))

    @pl.when(step + 1 < n_steps)
    def _():
        for q in range(STEP_TILES):
            _start_run_copies(tabs, (step + 1) * STEP_TILES + q, n_experts, copier(1 - cur, q))
    for q in range(STEP_TILES):
        copier(cur, q)(0, 0, TOP_K * tm).wait()

    ys = [_unpack_bf16_pairs(_load_slot_rows(buf_ref.at[cur, q], TOP_K * tm, rps)) for q in range(STEP_TILES)]

    def rows_of(c):
        return slice(c * n, (c + 1) * n)

    def ple_matmuls(c):
        x1 = x1_ref[0, rows_of(c)]
        gate_logits = _dot(x1.astype(BF16), wg_ref[...]) + bg_ref[...]
        return x1, gate_logits, _dot(p_ref[0, rows_of(c)].astype(BF16), wp_ref[...])

    def finish(c, x1, gate_logits, proj):
        wcol = wcol_ref[0, rows_of(c)]
        s_idx = lax.broadcasted_iota(I32, (n, TOP_K * tm), 1)
        select = jnp.zeros((n, TOP_K * tm), F32)
        for kk in range(TOP_K):
            at_kk = s_idx == wcol[:, TOP_K + kk:TOP_K + kk + 1].astype(I32)
            select = jnp.where(at_kk, wcol[:, kk:kk + 1], select)
        ffn = _dot(select.astype(BF16), ys[c // COMBINE_CHAINS])
        ple = _sigmoid(gate_logits) * proj
        out_ref[0, rows_of(c)] = _layernorm(alpha * x1 + ffn + ple, g2_ref[...], b2_ref[...])

    n_chains = STEP_TILES * COMBINE_CHAINS
    ahead = ple_matmuls(0)
    for c in range(n_chains):
        now, ahead = ahead, (ple_matmuls(c + 1) if c + 1 < n_chains else None)
        finish(c, *now)


def _combine(tabs, x1, p, wcol, ys, w_gate, b_gate, w_proj, g2, b2, *, layer, alpha, tm, n_experts):
    bsz, seq, d = x1.shape
    rows = STEP_TILES * tm
    steps = seq // rows
    d_ple = p.shape[-1]
    rps = d // 2 // LANES
    full = lambda *shape: pl.BlockSpec(shape, lambda b, i, *_: (0,) * len(shape))
    grid_spec = pltpu.PrefetchScalarGridSpec(
        num_scalar_prefetch=3,
        grid=(bsz, steps),
        in_specs=[
            pl.BlockSpec((1, rows, d), lambda b, i, *_: (b, i, 0)),
            pl.BlockSpec((None, 1, rows, d_ple), lambda b, i, *_: (layer, b, i, 0)),
            pl.BlockSpec((1, rows, LANES), lambda b, i, *_: (b, i, 0)),
            pl.BlockSpec(memory_space=pl.ANY),
            full(d, d), full(1, d), full(d_ple, d),
            full(1, d), full(1, d),
        ],
        out_specs=pl.BlockSpec((1, rows, d), lambda b, i, *_: (b, i, 0)),
        scratch_shapes=[pltpu.VMEM((2, STEP_TILES, TOP_K * tm * rps, LANES), U32),
                        pltpu.SemaphoreType.DMA((2, STEP_TILES))],
    )
    return pl.pallas_call(
        functools.partial(_combine_body, alpha=alpha, n_experts=n_experts, tm=tm),
        grid_spec=grid_spec,
        out_shape=jax.ShapeDtypeStruct((bsz, seq, d), F32),
        compiler_params=pltpu.CompilerParams(
            dimension_semantics=("arbitrary", "arbitrary"), vmem_limit_bytes=VMEM_LIMIT),
        name="combine",
    )(*tabs, x1, p, wcol, ys, w_gate, b_gate, w_proj, g2, b2)


def _router_weights(w_rg, b_rg, w_re, b_re):
    d, n_groups = w_rg.shape
    per_group = w_re.shape[2]
    assert n_groups <= SUBLANES and per_group <= SUBLANES and SUBLANES * (n_groups + 1) <= LANES
    w_e = jnp.pad(jnp.transpose(w_re, (1, 0, 2)), ((0, 0), (0, 0), (0, SUBLANES - per_group)))
    b_e = jnp.pad(b_re, ((0, 0), (0, SUBLANES - per_group)))
    tail = LANES - SUBLANES * (n_groups + 1)
    w = jnp.concatenate([jnp.pad(w_rg, ((0, 0), (0, SUBLANES - n_groups))), w_e.reshape(d, SUBLANES * n_groups),
                         jnp.zeros((d, tail), F32)], axis=1)
    b = jnp.concatenate([jnp.pad(b_rg, (0, SUBLANES - n_groups)), b_e.reshape(-1), jnp.zeros((tail,), F32)])
    w_hi = w.astype(BF16)
    w_lo = (w - w_hi.astype(F32)).astype(BF16)
    return jnp.concatenate([w_hi, w_lo], axis=1), b[None, :]


def kernel(x, p, ln0_g, ln0_b, w_in, b_in, conv_w, pool_w, pool_scale, w_o, ln1_g, ln1_b,
           w_router_group, b_router_group, w_router_expert, b_router_expert, w1, w3, w2,
           w_ple_gate, b_ple_gate, w_ple_proj, ln2_g, ln2_b):
    bsz, seq, d = x.shape
    depth = w_in.shape[0]
    n_groups, per_group = w_router_expert.shape[1], w_router_expert.shape[3]
    n_experts = n_groups * per_group
    alpha = (2 * depth) ** 0.25
    tm = min(SORT_ROWS, seq)
    tm_mix = min(MIX_ROWS, seq)
    n_slots = bsz * seq * TOP_K
    assert seq % tm_mix == 0 and tm_mix % tm == 0 and tm % LANES == 0 and seq % (STEP_TILES * tm) == 0
    assert n_slots % SLOT_BLOCK == 0 and n_experts <= LANES
    assert d % (2 * LANES) == 0
    row = lambda a: a[None, :]

    for i in range(depth):
        w_r, b_r = _router_weights(w_router_group[i], b_router_group[i],
                                   w_router_expert[i], b_router_expert[i])
        x1, ri, wcol, cnt = _mixer(
            x, row(ln0_g), row(ln0_b), w_in[i].astype(BF16), row(b_in[i]), conv_w[i],
            pool_w[i].astype(BF16), row(pool_scale[i]), w_o[i].astype(BF16),
            row(ln1_g[i]), row(ln1_b[i]), w_r, b_r,
            apply_ln0=(i == 0), alpha=alpha, n_groups=n_groups, per_group=per_group, tm=tm_mix,
            sort_rows=tm)
        tabs, totals = _run_tables(cnt.reshape(-1, SUBLANES, LANES))
        items = _expert_items(totals, n_experts, n_slots)
        xs = _dispatch(tabs, x1, ri, tm=tm, n_slots=n_slots, n_experts=n_experts)
        ys = _experts(items, xs, w1, w3, w2, layer=i)
        x = _combine(tabs, x1, p, wcol, ys, w_ple_gate[i].astype(BF16), row(b_ple_gate[i]),
                     w_ple_proj[i].astype(BF16), row(ln2_g[i]), row(ln2_b[i]),
                     layer=i, alpha=alpha, tm=tm, n_experts=n_experts)
    return x
```

```python
import functools

import jax
import jax.numpy as jnp
from jax import lax
from jax.experimental import pallas as pl
from jax.experimental.pallas import tpu as pltpu

POOL_WINDOWS = (2, 4, 8, 16)
CONV_WIDTH = 3
TOP_K = 2
LN_EPS = 1e-5
HALO = 8
LANES = 128
SUBLANES = 8
SORT_ROWS = 512
MIX_ROWS = 1024
MIX_CHAINS = 2
COMBINE_CHAINS = 2
STEP_TILES = 2
SLOT_BLOCK = 512
VMEM_LIMIT = 56 * 1024 * 1024

F32 = jnp.float32
BF16 = jnp.bfloat16
I32 = jnp.int32
U32 = jnp.uint32


def _layernorm(v, g, b):
    mu = jnp.mean(v, axis=-1, keepdims=True)
    c = v - mu
    var = jnp.mean(c * c, axis=-1, keepdims=True)
    return c * lax.rsqrt(var + LN_EPS) * g + b


def _sigmoid(v):
    return 0.5 * jnp.tanh(0.5 * v) + 0.5


def _dot(a, b):
    return jnp.dot(a, b, preferred_element_type=F32)


def _pair_groups(width):
    assert width % (2 * LANES) == 0
    return range(0, width, 2 * LANES)


def _pack_bf16_pairs(v):
    words = []
    for c0 in _pair_groups(v.shape[1]):
        hi = pltpu.bitcast(v[:, c0:c0 + LANES].astype(BF16).astype(F32), U32)
        lo = pltpu.bitcast(v[:, c0 + LANES:c0 + 2 * LANES].astype(BF16).astype(F32), U32)
        words.append(hi | (lo >> 16))
    return jnp.concatenate(words, axis=1)


def _pack_exact_bf16_pairs(v):
    words = [pltpu.bitcast(v[:, c0:c0 + LANES], U32) | (pltpu.bitcast(v[:, c0 + LANES:c0 + 2 * LANES], U32) >> 16)
             for c0 in _pair_groups(v.shape[1])]
    return jnp.concatenate(words, axis=1)


def _unpack_bf16_pairs(u):
    cols = []
    for c0 in range(0, u.shape[1], LANES):
        w = u[:, c0:c0 + LANES]
        cols.append(pltpu.bitcast(w & jnp.uint32(0xFFFF0000), F32).astype(BF16))
        cols.append(pltpu.bitcast(w << 16, F32).astype(BF16))
    return jnp.concatenate(cols, axis=1)


def _store_slot_rows(ref, packed, first=0):
    n, c = packed.shape
    rps = c // LANES
    for k in range(rps):
        ref[pl.ds(first * rps + k, n, stride=rps), :] = packed[:, k * LANES:(k + 1) * LANES]


def _load_slot_rows(ref, n, rps, first=0):
    return jnp.concatenate([ref[pl.ds(first * rps + k, n, stride=rps), :] for k in range(rps)], axis=1)


def _mixer_body(x_ref, xp_ref, xn_ref, g0_ref, b0_ref, win_ref, bin_ref, cw_ref, pw_ref, ps_ref,
                wo_ref, g1_ref, b1_ref, wr_ref, br_ref,
                x1_ref, ri_ref, wcol_ref, cnt_ref, *, apply_ln0, alpha, n_groups, per_group, seq,
                sort_rows):
    i = pl.program_id(1)
    nt = pl.num_programs(1)
    tm = x_ref.shape[1]
    n_chains = 2 * MIX_CHAINS if apply_ln0 else MIX_CHAINS
    n = tm // n_chains
    rows = n + 2 * HALO
    d_conv = cw_ref.shape[1]
    dg = pw_ref.shape[1]

    xe = jnp.concatenate([xp_ref[0], x_ref[0], xn_ref[0]], axis=0)
    if apply_ln0:
        xe = _layernorm(xe, g0_ref[...], b0_ref[...])
    cw = cw_ref[...]
    ps = ps_ref[...]

    def in_proj(s):
        xs = xe[s * n:s * n + rows]
        return xs, _dot(xs.astype(BF16), win_ref[...]) + bin_ref[...]

    def token_mix(s, z):
        r = lax.broadcasted_iota(I32, (rows, 1), 0)
        lo_ok = jnp.logical_or(r >= HALO, i > 0) if s == 0 else True
        hi_ok = jnp.logical_or(r < n + HALO, i < nt - 1) if s == n_chains - 1 else True
        if s == 0 or s == n_chains - 1:
            valid = jnp.logical_and(lo_ok, hi_ok)
            mask = lambda a: jnp.where(valid, a, 0.0)
        else:
            mask = lambda a: a

        h = z[:, :d_conv]
        gate_b = z[HALO:HALO + n, d_conv:2 * d_conv]
        gate_c = z[:, 2 * d_conv:3 * d_conv]
        v = mask(gate_c * h)
        conv = (cw[0:1] * pltpu.roll(v, 1, 0)[HALO:HALO + n]
                + cw[1:2] * v[HALO:HALO + n]
                + cw[2:3] * pltpu.roll(v, rows - 1, 0)[HALO:HALO + n])
        parts = [gate_b * conv]

        pos = i * tm + s * n + r[HALO:HALO + n] - HALO
        for g, w in enumerate(POOL_WINDOWS):
            left = w // 2
            right = w - 1 - left
            c0 = 3 * d_conv + g * dg
            ug = mask(z[:, c0:c0 + dg])
            acc = ug
            k = 1
            while k < w:
                acc = acc + pltpu.roll(acc, k, 0)
                k *= 2
            if right:
                acc = pltpu.roll(acc, rows - right, 0)

            def edge(lo):
                p = pos[lo:lo + HALO]
                cnt = (jnp.minimum(p + right + 1, seq) - jnp.maximum(p - left, 0)).astype(F32)
                return acc[HALO + lo:2 * HALO + lo] / cnt - ug[HALO + lo:2 * HALO + lo]
            inner = acc[2 * HALO:n] * (1.0 / w) - ug[2 * HALO:n]
            pooled = jnp.concatenate([edge(0), inner, edge(n - HALO)], axis=0)
            parts.append(_dot(pooled.astype(BF16), pw_ref[g]) * ps[:, g * dg:(g + 1) * dg])
        return jnp.concatenate(parts, axis=1).astype(BF16)

    def out_proj(xs, cat):
        mix = _dot(cat, wo_ref[...])
        return _layernorm(alpha * xs[HALO:HALO + n] + mix, g1_ref[...], b1_ref[...])

    def route(x1s):
        xh = x1s.astype(BF16)
        xl = (x1s - xh.astype(F32)).astype(BF16)
        l_hi = _dot(xh, wr_ref[...])
        l_lo = _dot(xl, wr_ref[:, :LANES])
        return l_hi[:, :LANES] + l_hi[:, LANES:] + l_lo + br_ref[...]

    chains = [in_proj(s) for s in range(n_chains)]
    x1s = [out_proj(xs, token_mix(s, z)) for s, (xs, z) in enumerate(chains)]
    x1 = jnp.concatenate(x1s, axis=0)
    x1_ref[0] = x1
    logits = jnp.concatenate([route(v) for v in x1s], axis=0)
    ri_ref[0] = jnp.zeros(ri_ref.shape[1:], I32)
    for q in range(tm // sort_rows):
        rows_q = slice(q * sort_rows, (q + 1) * sort_rows)
        lpos0, lpos1, gw0, gw1, counts = _route_and_sort(logits[rows_q], n_groups, per_group)
        ri_ref[0, q, 0:1, :] = lpos0.astype(I32)
        ri_ref[0, q, 1:2, :] = lpos1.astype(I32)
        rowl = lax.broadcasted_iota(I32, (LANES, sort_rows), 0)
        wrows = jnp.where(rowl == 0, gw0, jnp.where(rowl == 1, gw1,
                          jnp.where(rowl == 2, lpos0, jnp.where(rowl == 3, lpos1, 0.0))))
        wcol_ref[0, rows_q, :] = wrows.T
        cnt_ref[0, q] = counts


def _route_and_sort(logits, n_groups, per_group):
    n = logits.shape[0]
    lt = logits.T
    row8 = lax.broadcasted_iota(I32, (SUBLANES, n), 0)
    neg = jnp.float32(-jnp.inf)
    lg = jnp.where(row8 < n_groups, lt[0:SUBLANES], neg)
    mg = jnp.max(lg, axis=0, keepdims=True)
    g_w = 1.0 / jnp.sum(jnp.exp(lg - mg), axis=0, keepdims=True)
    g_sel = jnp.min(jnp.where(lg == mg, row8, SUBLANES), axis=0, keepdims=True)
    le = lt[SUBLANES:2 * SUBLANES]
    for g in range(1, n_groups):
        le = jnp.where(g_sel == g, lt[SUBLANES * (g + 1):SUBLANES * (g + 2)], le)
    le = jnp.where(row8 < per_group, le, neg)
    v1 = jnp.max(le, axis=0, keepdims=True)
    i1 = jnp.min(jnp.where(le == v1, row8, SUBLANES), axis=0, keepdims=True)
    le2 = jnp.where(row8 == i1, neg, le)
    v2 = jnp.max(le2, axis=0, keepdims=True)
    i2 = jnp.min(jnp.where(le2 == v2, row8, SUBLANES), axis=0, keepdims=True)
    e21 = jnp.exp(v2 - v1)
    w_first = 1.0 / (1.0 + e21)
    gw0 = w_first * g_w
    gw1 = (e21 * w_first) * g_w
    eid0 = g_sel * per_group + i1
    eid1 = g_sel * per_group + i2

    rowl = lax.broadcasted_iota(I32, (LANES, n), 0)
    oh0 = rowl == eid0
    oh1 = rowl == eid1
    a_idx = lax.broadcasted_iota(I32, (n, n), 0)
    b_idx = lax.broadcasted_iota(I32, (n, n), 1)
    before = (a_idx < b_idx).astype(BF16)
    r0 = _dot(oh0.astype(BF16), before)
    r1 = _dot(oh1.astype(BF16), before)
    cnt0 = jnp.sum(oh0.astype(F32), axis=1, keepdims=True)
    lower = jnp.sum((rowl > eid0).astype(F32) + (rowl > eid1).astype(F32),
                    axis=1, keepdims=True)
    lpos0 = jnp.sum(jnp.where(oh0, lower + r0, 0.0), axis=0, keepdims=True)
    lpos1 = jnp.sum(jnp.where(oh1, lower + cnt0 + r1, 0.0), axis=0, keepdims=True)

    both = jnp.logical_or(oh0, oh1).astype(BF16)
    counts = lax.dot_general(jnp.ones((SUBLANES, n), BF16), both, (((1,), (1,)), ((), ())),
                             preferred_element_type=F32)
    return lpos0, lpos1, gw0, gw1, counts


def _mixer(x, g0, b0, w_in, b_in, conv_w, pool_w, pool_scale, w_o, g1, b1, w_r, b_r, *,
           apply_ln0, alpha, n_groups, per_group, tm, sort_rows):
    bsz, seq, d = x.shape
    nt = seq // tm
    hb = tm // HALO
    nhb = seq // HALO
    d_in = w_in.shape[1]
    d_conv = conv_w.shape[1]
    full = lambda *shape: pl.BlockSpec(shape, lambda b, i: (0,) * len(shape))
    body = functools.partial(_mixer_body, apply_ln0=apply_ln0, alpha=alpha, n_groups=n_groups,
                             per_group=per_group, seq=seq, sort_rows=sort_rows)
    nq = tm // sort_rows
    return pl.pallas_call(
        body,
        grid=(bsz, nt),
        in_specs=[
            pl.BlockSpec((1, tm, d), lambda b, i: (b, i, 0)),
            pl.BlockSpec((1, HALO, d), lambda b, i: (b, jnp.maximum(i * hb - 1, 0), 0)),
            pl.BlockSpec((1, HALO, d), lambda b, i: (b, jnp.minimum((i + 1) * hb, nhb - 1), 0)),
            full(1, d), full(1, d),
            full(d, d_in), full(1, d_in),
            full(CONV_WIDTH, d_conv),
            full(*pool_w.shape), full(1, pool_scale.shape[1]),
            full(d, d), full(1, d), full(1, d),
            full(d, 2 * LANES), full(1, LANES),
        ],
        out_specs=[
            pl.BlockSpec((1, tm, d), lambda b, i: (b, i, 0)),
            pl.BlockSpec((1, nq, SUBLANES, sort_rows), lambda b, i: (b, i, 0, 0)),
            pl.BlockSpec((1, tm, LANES), lambda b, i: (b, i, 0)),
            pl.BlockSpec((1, nq, SUBLANES, LANES), lambda b, i: (b, i, 0, 0)),
        ],
        out_shape=[
            jax.ShapeDtypeStruct((bsz, seq, d), F32),
            jax.ShapeDtypeStruct((bsz, nt * nq, SUBLANES, sort_rows), I32),
            jax.ShapeDtypeStruct((bsz, seq, LANES), F32),
            jax.ShapeDtypeStruct((bsz, nt * nq, SUBLANES, LANES), F32),
        ],
        compiler_params=pltpu.CompilerParams(
            dimension_semantics=("arbitrary", "arbitrary"), vmem_limit_bytes=VMEM_LIMIT),
        name="mixer",
    )(x, x, x, g0, b0, w_in, b_in, conv_w, pool_w, pool_scale, w_o, g1, b1, w_r, b_r)


def _tables_body(cnt_ref, gstart_ref, lstart_ref, len_ref, tot_ref):
    c = cnt_ref[:, 0, :]
    n_tiles = c.shape[0]
    row = lax.broadcasted_iota(I32, c.shape, 0)
    lane = lax.broadcasted_iota(I32, c.shape, 1)

    def prefix(v, idx, size, axis):
        k = 1
        while k < size:
            v = v + jnp.where(idx >= k, pltpu.roll(v, k, axis), 0.0)
            k *= 2
        return v

    over_tiles = prefix(c, row, n_tiles, 0)
    tot = jnp.broadcast_to(over_tiles[n_tiles - 1:n_tiles], c.shape)
    expert_start = prefix(tot, lane, LANES, 1) - tot
    gstart_ref[...] = (expert_start + over_tiles - c).astype(I32)
    lstart_ref[...] = (prefix(c, lane, LANES, 1) - c).astype(I32)
    len_ref[...] = c.astype(I32)
    tot_ref[...] = jnp.broadcast_to(over_tiles[n_tiles - 1:n_tiles], tot_ref.shape).astype(I32)


def _run_tables(cnt):
    table = jax.ShapeDtypeStruct((cnt.shape[0], LANES), I32)
    *tabs, tot = pl.pallas_call(
        _tables_body, out_shape=[table, table, table, jax.ShapeDtypeStruct((SUBLANES, LANES), I32)],
        name="run_tables")(cnt)
    return tuple(tabs), tot[0]


def _slots(ref, first, n, rps):
    start = first * rps
    if not isinstance(start, int):
        start = pl.multiple_of(start, rps)
    return ref.at[pl.ds(start, n * rps)]


def _start_run_copies(tabs, tile, n_experts, make_copy):
    gstart_ref, lstart_ref, len_ref = tabs

    def run_of(e):
        e = jnp.minimum(e, n_experts - 1)
        return len_ref[tile, e], lstart_ref[tile, e], gstart_ref[tile, e]

    def per_pair(i, runs):
        nxt = (run_of(2 * i + 2), run_of(2 * i + 3))
        for prio, (n, ls, gs) in enumerate(runs):

            @pl.when(n > 0)
            def _(n=n, ls=ls, gs=gs, prio=prio):
                make_copy(ls, gs, n).start(priority=prio)
        return nxt
    assert n_experts % 2 == 0
    lax.fori_loop(0, n_experts // 2, per_pair, (run_of(0), run_of(1)))


def _dispatch_body(gstart_ref, lstart_ref, len_ref, x1_ref, ri_ref, xs_ref, buf_ref, sem, *, n_experts):
    step = pl.program_id(0) * pl.num_programs(1) + pl.program_id(1)
    n_steps = pl.num_programs(0) * pl.num_programs(1)
    tabs = (gstart_ref, lstart_ref, len_ref)
    tm = ri_ref.shape[3]
    d = x1_ref.shape[2]
    rps = d // 2 // LANES

    def copier(which, q):
        def make_copy(local, glob, n):
            return pltpu.make_async_copy(_slots(buf_ref.at[which, q], local, n, rps),
                                         _slots(xs_ref, glob, n, rps), sem.at[which, q])
        return make_copy

    def wait_set(which):
        for q in range(STEP_TILES):
            copier(which, q)(0, 0, TOP_K * tm).wait()

    cur = lax.rem(step, 2)

    @pl.when(step >= 2)
    def _():
        wait_set(cur)

    for q in range(STEP_TILES):
        lpos = ri_ref[0, q]
        s_idx = lax.broadcasted_iota(I32, (TOP_K * tm, tm), 0)
        perm = jnp.logical_or(s_idx == lpos[0:1], s_idx == lpos[1:2]).astype(BF16)
        x1 = x1_ref[0, q * tm:(q + 1) * tm]
        _store_slot_rows(buf_ref.at[cur, q], _pack_exact_bf16_pairs(_dot(perm, x1.astype(BF16))))
    for q in range(STEP_TILES):
        _start_run_copies(tabs, step * STEP_TILES + q, n_experts, copier(cur, q))

    @pl.when(step == n_steps - 1)
    def _():
        wait_set(cur)

        @pl.when(step >= 1)
        def _():
            wait_set(1 - cur)


def _dispatch(tabs, x1, ri, *, tm, n_slots, n_experts):
    bsz, seq, d = x1.shape
    steps = seq // (tm * STEP_TILES)
    rps = d // 2 // LANES
    grid_spec = pltpu.PrefetchScalarGridSpec(
        num_scalar_prefetch=3,
        grid=(bsz, steps),
        in_specs=[
            pl.BlockSpec((1, STEP_TILES * tm, d), lambda b, i, *_: (b, i, 0)),
            pl.BlockSpec((1, STEP_TILES, SUBLANES, tm), lambda b, i, *_: (b, i, 0, 0)),
        ],
        out_specs=pl.BlockSpec(memory_space=pl.ANY),
        scratch_shapes=[pltpu.VMEM((2, STEP_TILES, TOP_K * tm * rps, LANES), U32),
                        pltpu.SemaphoreType.DMA((2, STEP_TILES))],
    )
    return pl.pallas_call(
        functools.partial(_dispatch_body, n_experts=n_experts),
        grid_spec=grid_spec,
        out_shape=jax.ShapeDtypeStruct((n_slots * rps, LANES), U32),
        compiler_params=pltpu.CompilerParams(
            dimension_semantics=("arbitrary", "arbitrary"), vmem_limit_bytes=VMEM_LIMIT),
        name="dispatch",
    )(*tabs, x1, ri)


def _experts_body(blk_ref, exp_ref, lo_ref, hi_ref, new_ref, par_ref, nxt_ref, xs_ref, w1_ref, w3_ref, w2_ref,
                  ys_ref, f1_ref, f3_ref, f2_ref, w1b_ref, w3b_ref, w2b_ref, keep_ref, sem, *, layer):
    it = pl.program_id(0)
    lo = lo_ref[it]
    rps = xs_ref.shape[0] // SLOT_BLOCK

    def weight_copies(expert, slot):
        return [pltpu.make_async_copy(w_ref.at[layer, expert], f_ref.at[slot], sem.at[slot, k])
                for k, (w_ref, f_ref) in enumerate(((w1_ref, f1_ref), (w3_ref, f3_ref), (w2_ref, f2_ref)))]

    @pl.when(it == 0)
    def _():
        for c in weight_copies(exp_ref[0], 0):
            c.start(priority=1)

    @pl.when(new_ref[it] == 1)
    def _():
        slot = par_ref[it]
        for c in weight_copies(exp_ref[it], slot):
            c.wait()
        w1b_ref[...] = f1_ref[slot].astype(BF16)
        w3b_ref[...] = f3_ref[slot].astype(BF16)
        w2b_ref[...] = f2_ref[slot].astype(BF16)

        @pl.when(nxt_ref[it] >= 0)
        def _():
            for c in weight_copies(nxt_ref[it], 1 - slot):
                c.start(priority=1)

    def ffn(first, n):
        xb = _unpack_bf16_pairs(_load_slot_rows(xs_ref, n, rps, first))
        h1 = _dot(xb, w1b_ref[...])
        h = (h1 * _sigmoid(h1)) * _dot(xb, w3b_ref[...])
        return _pack_bf16_pairs(_dot(h.astype(BF16), w2b_ref[...]))

    def run(first, n):
        m = min(n, half)
        cut = lo > first

        @pl.when(cut)
        def _():
            keep_ref[...] = _load_slot_rows(ys_ref, m, rps, first)

        _store_slot_rows(ys_ref, ffn(first, n), first)

        @pl.when(cut)
        def _():
            rows = first + lax.broadcasted_iota(I32, keep_ref.shape, 0)
            mine = _load_slot_rows(ys_ref, m, rps, first)
            _store_slot_rows(ys_ref, jnp.where(rows >= lo, mine, keep_ref[...]), first)

    half = SLOT_BLOCK // 2
    need_lo = lo < half
    need_hi = hi_ref[it] > half
    pl.when(jnp.logical_and(need_lo, need_hi))(lambda: run(0, SLOT_BLOCK))
    pl.when(jnp.logical_and(need_lo, jnp.logical_not(need_hi)))(lambda: run(0, half))
    pl.when(jnp.logical_and(jnp.logical_not(need_lo), need_hi))(lambda: run(half, half))


def _experts(items, xs, w1, w3, w2, *, layer):
    rows = xs.shape[0]
    _, _, d, d_e = w1.shape
    rps = d // 2 // LANES
    n_items = items[0].shape[0]
    grid_spec = pltpu.PrefetchScalarGridSpec(
        num_scalar_prefetch=len(items),
        grid=(n_items,),
        in_specs=[
            pl.BlockSpec((SLOT_BLOCK * rps, LANES), lambda i, blk, *_: (blk[i], 0)),
            pl.BlockSpec(memory_space=pl.ANY),
            pl.BlockSpec(memory_space=pl.ANY),
            pl.BlockSpec(memory_space=pl.ANY),
        ],
        out_specs=pl.BlockSpec((SLOT_BLOCK * rps, LANES), lambda i, blk, *_: (blk[i], 0)),
        scratch_shapes=[pltpu.VMEM((2, d, d_e), F32), pltpu.VMEM((2, d, d_e), F32),
                        pltpu.VMEM((2, d_e, d), F32),
                        pltpu.VMEM((d, d_e), BF16), pltpu.VMEM((d, d_e), BF16),
                        pltpu.VMEM((d_e, d), BF16),
                        pltpu.VMEM((SLOT_BLOCK // 2, d // 2), U32), pltpu.SemaphoreType.DMA((2, 3))],
    )
    return pl.pallas_call(
        functools.partial(_experts_body, layer=layer),
        grid_spec=grid_spec,
        out_shape=jax.ShapeDtypeStruct((rows, LANES), U32),
        compiler_params=pltpu.CompilerParams(
            dimension_semantics=("arbitrary",), vmem_limit_bytes=VMEM_LIMIT),
        name="experts",
    )(*items, xs, w1, w3, w2)


def _expert_items(totals, n_experts, n_slots):
    n_blocks = n_slots // SLOT_BLOCK
    n_items = n_blocks + n_experts - 1
    tot = totals[:n_experts]
    end = jnp.cumsum(tot)
    start = end - tot
    first_blk = start // SLOT_BLOCK
    n_it = jnp.where(tot > 0, (end - 1) // SLOT_BLOCK - first_blk + 1, 0)
    it_end = jnp.cumsum(n_it)
    it_start = it_end - n_it
    idx = jnp.arange(n_items, dtype=I32)
    last_live = it_end[-1] - 1
    live = idx <= last_live
    at = jnp.minimum(idx, last_live)
    expert = jnp.sum(it_end[None, :] <= at[:, None], axis=1).astype(I32)
    onehot = expert[:, None] == jnp.arange(n_experts, dtype=I32)[None, :]
    pick = lambda v: jnp.sum(jnp.where(onehot, v[None, :], 0), axis=1)
    block = pick(first_blk) + idx - pick(it_start)
    item_block = jnp.where(live, block, n_blocks - 1).astype(I32)
    item_lo = jnp.where(live, jnp.maximum(pick(start) - block * SLOT_BLOCK, 0), SLOT_BLOCK).astype(I32)
    item_hi = jnp.where(live, jnp.minimum(pick(end) - block * SLOT_BLOCK, SLOT_BLOCK), 0).astype(I32)
    prev_expert = jnp.concatenate([jnp.full((1,), -1, I32), expert[:-1]])
    is_new = expert != prev_expert
    slot = (jnp.cumsum(is_new.astype(I32)) - 1) % 2
    later_new = lax.cummin(jnp.where(is_new, idx, n_items)[::-1], axis=0)[::-1]
    next_new = jnp.concatenate([later_new[1:], jnp.full((1,), n_items, I32)])
    next_expert = jnp.where(next_new < n_items, expert[jnp.minimum(next_new, n_items - 1)], -1)
    return (item_block, expert, item_lo, item_hi, is_new.astype(I32), slot.astype(I32),
            next_expert.astype(I32))


def _combine_body(gstart_ref, lstart_ref, len_ref, x1_ref, p_ref, wcol_ref, ys_ref, wg_ref, bg_ref, wp_ref,
                  g2_ref, b2_ref, out_ref, buf_ref, sem, *, alpha, n_experts, tm):
    step = pl.program_id(0) * pl.num_programs(1) + pl.program_id(1)
    tabs = (gstart_ref, lstart_ref, len_ref)
    d = x1_ref.shape[2]
    rps = d // 2 // LANES
    n = tm // COMBINE_CHAINS

    def copier(which, q):
        def make_copy(local, glob, n_slots):
            return pltpu.make_async_copy(_slots(ys_ref, glob, n_slots, rps),
                                         _slots(buf_ref.at[which, q], local, n_slots, rps), sem.at[which, q])
        return make_copy

    n_steps = pl.num_programs(0) * pl.num_programs(1)
    cur = lax.rem(step, 2)

    @pl.when(step == 0)
    def _():
        for q in range(STEP_TILES):
            _start_run_copies(tabs, q, n_experts, copier(cur, q))

    @pl.when(step + 1 < n_steps)
    def _():
        for q in range(STEP_TILES):
            _start_run_copies(tabs, (step + 1) * STEP_TILES + q, n_experts, copier(1 - cur, q))
    for q in range(STEP_TILES):
        copier(cur, q)(0, 0, TOP_K * tm).wait()

    ys = [_unpack_bf16_pairs(_load_slot_rows(buf_ref.at[cur, q], TOP_K * tm, rps)) for q in range(STEP_TILES)]

    def rows_of(c):
        return slice(c * n, (c + 1) * n)

    def ple_matmuls(c):
        x1 = x1_ref[0, rows_of(c)]
        gate_logits = _dot(x1.astype(BF16), wg_ref[...]) + bg_ref[...]
        return x1, gate_logits, _dot(p_ref[0, rows_of(c)].astype(BF16), wp_ref[...])

    def finish(c, x1, gate_logits, proj):
        wcol = wcol_ref[0, rows_of(c)]
        s_idx = lax.broadcasted_iota(I32, (n, TOP_K * tm), 1)
        select = jnp.zeros((n, TOP_K * tm), F32)
        for kk in range(TOP_K):
            at_kk = s_idx == wcol[:, TOP_K + kk:TOP_K + kk + 1].astype(I32)
            select = jnp.where(at_kk, wcol[:, kk:kk + 1], select)
        ffn = _dot(select.astype(BF16), ys[c // COMBINE_CHAINS])
        ple = _sigmoid(gate_logits) * proj
        out_ref[0, rows_of(c)] = _layernorm(alpha * x1 + ffn + ple, g2_ref[...], b2_ref[...])

    n_chains = STEP_TILES * COMBINE_CHAINS
    ahead = ple_matmuls(0)
    for c in range(n_chains):
        now, ahead = ahead, (ple_matmuls(c + 1) if c + 1 < n_chains else None)
        finish(c, *now)


def _combine(tabs, x1, p, wcol, ys, w_gate, b_gate, w_proj, g2, b2, *, layer, alpha, tm, n_experts):
    bsz, seq, d = x1.shape
    rows = STEP_TILES * tm
    steps = seq // rows
    d_ple = p.shape[-1]
    rps = d // 2 // LANES
    full = lambda *shape: pl.BlockSpec(shape, lambda b, i, *_: (0,) * len(shape))
    grid_spec = pltpu.PrefetchScalarGridSpec(
        num_scalar_prefetch=3,
        grid=(bsz, steps),
        in_specs=[
            pl.BlockSpec((1, rows, d), lambda b, i, *_: (b, i, 0)),
            pl.BlockSpec((None, 1, rows, d_ple), lambda b, i, *_: (layer, b, i, 0)),
            pl.BlockSpec((1, rows, LANES), lambda b, i, *_: (b, i, 0)),
            pl.BlockSpec(memory_space=pl.ANY),
            full(d, d), full(1, d), full(d_ple, d),
            full(1, d), full(1, d),
        ],
        out_specs=pl.BlockSpec((1, rows, d), lambda b, i, *_: (b, i, 0)),
        scratch_shapes=[pltpu.VMEM((2, STEP_TILES, TOP_K * tm * rps, LANES), U32),
                        pltpu.SemaphoreType.DMA((2, STEP_TILES))],
    )
    return pl.pallas_call(
        functools.partial(_combine_body, alpha=alpha, n_experts=n_experts, tm=tm),
        grid_spec=grid_spec,
        out_shape=jax.ShapeDtypeStruct((bsz, seq, d), F32),
        compiler_params=pltpu.CompilerParams(
            dimension_semantics=("arbitrary", "arbitrary"), vmem_limit_bytes=VMEM_LIMIT),
        name="combine",
    )(*tabs, x1, p, wcol, ys, w_gate, b_gate, w_proj, g2, b2)


def _router_weights(w_rg, b_rg, w_re, b_re):
    d, n_groups = w_rg.shape
    per_group = w_re.shape[2]
    assert n_groups <= SUBLANES and per_group <= SUBLANES and SUBLANES * (n_groups + 1) <= LANES
    w_e = jnp.pad(jnp.transpose(w_re, (1, 0, 2)), ((0, 0), (0, 0), (0, SUBLANES - per_group)))
    b_e = jnp.pad(b_re, ((0, 0), (0, SUBLANES - per_group)))
    tail = LANES - SUBLANES * (n_groups + 1)
    w = jnp.concatenate([jnp.pad(w_rg, ((0, 0), (0, SUBLANES - n_groups))), w_e.reshape(d, SUBLANES * n_groups),
                         jnp.zeros((d, tail), F32)], axis=1)
    b = jnp.concatenate([jnp.pad(b_rg, (0, SUBLANES - n_groups)), b_e.reshape(-1), jnp.zeros((tail,), F32)])
    w_hi = w.astype(BF16)
    w_lo = (w - w_hi.astype(F32)).astype(BF16)
    return jnp.concatenate([w_hi, w_lo], axis=1), b[None, :]


def kernel(x, p, ln0_g, ln0_b, w_in, b_in, conv_w, pool_w, pool_scale, w_o, ln1_g, ln1_b,
           w_router_group, b_router_group, w_router_expert, b_router_expert, w1, w3, w2,
           w_ple_gate, b_ple_gate, w_ple_proj, ln2_g, ln2_b):
    bsz, seq, d = x.shape
    depth = w_in.shape[0]
    n_groups, per_group = w_router_expert.shape[1], w_router_expert.shape[3]
    n_experts = n_groups * per_group
    alpha = (2 * depth) ** 0.25
    tm = min(SORT_ROWS, seq)
    tm_mix = min(MIX_ROWS, seq)
    n_slots = bsz * seq * TOP_K
    assert seq % tm_mix == 0 and tm_mix % tm == 0 and tm % LANES == 0 and seq % (STEP_TILES * tm) == 0
    assert n_slots % SLOT_BLOCK == 0 and n_experts <= LANES
    assert d % (2 * LANES) == 0
    row = lambda a: a[None, :]

    for i in range(depth):
        w_r, b_r = _router_weights(w_router_group[i], b_router_group[i],
                                   w_router_expert[i], b_router_expert[i])
        x1, ri, wcol, cnt = _mixer(
            x, row(ln0_g), row(ln0_b), w_in[i].astype(BF16), row(b_in[i]), conv_w[i],
            pool_w[i].astype(BF16), row(pool_scale[i]), w_o[i].astype(BF16),
            row(ln1_g[i]), row(ln1_b[i]), w_r, b_r,
            apply_ln0=(i == 0), alpha=alpha, n_groups=n_groups, per_group=per_group, tm=tm_mix,
            sort_rows=tm)
        tabs, totals = _run_tables(cnt.reshape(-1, SUBLANES, LANES))
        items = _expert_items(totals, n_experts, n_slots)
        xs = _dispatch(tabs, x1, ri, tm=tm, n_slots=n_slots, n_experts=n_experts)
        ys = _experts(items, xs, w1, w3, w2, layer=i)
        x = _combine(tabs, x1, p, wcol, ys, w_ple_gate[i].astype(BF16), row(b_ple_gate[i]),
                     w_ple_proj[i].astype(BF16), row(ln2_g[i]), row(ln2_b[i]),
                     layer=i, alpha=alpha, tm=tm, n_experts=n_experts)
    return x
```
